```python
import math
import jax, jax.numpy as jnp
from jax import lax
import numpy as np

D_MODEL = 1024
BATCH = 8
SEQ = 2048
DEPTH = 4

GRID_W = 64
CTX_LEN = 256
Q_BLOCK = 128
HEAD_DIM = 64
ROPE_BASE = 10000.0
GQA_Q_HEADS = 8
GQA_KV_HEADS = 2
GQA_GROUP = GQA_Q_HEADS // GQA_KV_HEADS
DIFF_HEADS = 4
DIFF_V_DIM = 2 * HEAD_DIM
GQA_Q_W = GQA_Q_HEADS * HEAD_DIM
GQA_KV_W = GQA_KV_HEADS * HEAD_DIM
DIFF_QK_W = DIFF_HEADS * 2 * HEAD_DIM
DIFF_V_W = DIFF_HEADS * DIFF_V_DIM
ATTN_IN_W = GQA_Q_W + 2 * GQA_KV_W + 2 * DIFF_QK_W + DIFF_V_W
ATTN_OUT_W = GQA_Q_W + DIFF_V_W
ATTN_SPLITS = (GQA_Q_W, GQA_Q_W + GQA_KV_W, GQA_Q_W + 2 * GQA_KV_W,
               GQA_Q_W + 2 * GQA_KV_W + DIFF_QK_W, GQA_Q_W + 2 * GQA_KV_W + 2 * DIFF_QK_W)
SSM_GROUP_CH = 16
SSM_GROUPS = D_MODEL // SSM_GROUP_CH
SSM_STATE = 64
SSM_DT_MIN = 0.001
SSM_DT_MAX = 0.1
MOE_GROUPS = 4
MOE_EXPERTS_PER_GROUP = 8
MOE_EXPERTS = MOE_GROUPS * MOE_EXPERTS_PER_GROUP
MOE_TOP_K = 2
MOE_HIDDEN = D_MODEL // 4
RMS_EPS = 1e-6

kernel_name = 'hybrid_gqa_diffattn_s5_hmoe_dit'


def rms_norm(x, g):
    xf = x.astype(jnp.float32)
    y = xf * lax.rsqrt(jnp.mean(xf * xf, axis=-1, keepdims=True) + RMS_EPS)
    return (y * g.astype(jnp.float32)).astype(x.dtype)


def modulate(x, shift, scale):
    return x * (1 + scale) + shift


def axial_rope_tables(seq_len, dtype):
    n_rows = seq_len // GRID_W
    rows = jnp.repeat(jnp.arange(n_rows, dtype=jnp.float32), GRID_W)
    cols = jnp.tile(jnp.arange(GRID_W, dtype=jnp.float32), n_rows)
    half = HEAD_DIM // 2
    inv = 1.0 / (ROPE_BASE ** (jnp.arange(0, half, 2, dtype=jnp.float32) / half))
    ang_r = rows[:, None] * inv
    ang_c = cols[:, None] * inv
    ang = jnp.concatenate([ang_r, ang_r, ang_c, ang_c], axis=-1)
    return jnp.cos(ang).astype(dtype)[None, :, None, :], jnp.sin(ang).astype(dtype)[None, :, None, :]


def apply_rope(x, cos, sin):
    xs = x.reshape(x.shape[:-1] + (2, 2, HEAD_DIM // 4))
    rot = jnp.stack([-xs[..., 1, :], xs[..., 0, :]], axis=-2).reshape(x.shape)
    return x * cos + rot * sin


def sweep_query_blocks(fn, qs):
    b, lq = qs[0].shape[:2]
    nb = lq // Q_BLOCK
    blocks = tuple(jnp.moveaxis(q.reshape((b, nb, Q_BLOCK) + q.shape[2:]), 1, 0) for q in qs)
    out = lax.map(fn, blocks)
    return jnp.moveaxis(out, 0, 1).reshape((b, lq) + out.shape[3:])


def gqa_attend(q, k, v):
    scale = HEAD_DIM ** -0.5

    def blk(qs):
        (qb,) = qs
        s = jnp.einsum('bqhgd,bkhd->bhgqk', qb, k, preferred_element_type=jnp.float32) * scale
        p = jax.nn.softmax(s, axis=-1).astype(v.dtype)
        return jnp.einsum('bhgqk,bkhd->bqhgd', p, v)

    return sweep_query_blocks(blk, (q,))


def diff_attend(q1, q2, k1, k2, v, lam):
    scale = HEAD_DIM ** -0.5

    def blk(qs):
        q1b, q2b = qs
        s1 = jnp.einsum('bqhd,bkhd->bhqk', q1b, k1, preferred_element_type=jnp.float32) * scale
        s2 = jnp.einsum('bqhd,bkhd->bhqk', q2b, k2, preferred_element_type=jnp.float32) * scale
        p = jax.nn.softmax(s1, axis=-1) - lam * jax.nn.softmax(s2, axis=-1)
        return jnp.einsum('bhqk,bkhe->bqhe', p.astype(v.dtype), v)

    return sweep_query_blocks(blk, (q1, q2))


def attention_mixer(xn, cn, w_in, w_out, q_g, k_g, lq1, lk1, lq2, lk2, subln_g, lambda_init, with_ctx_out):
    bsz = xn.shape[0]

    def project(t, with_pos):
        n = t.shape[1]
        hp = t @ w_in
        qa, ka, va, qb, kb, vb = jnp.split(hp, ATTN_SPLITS, axis=-1)
        qa = rms_norm(qa.reshape(bsz, n, GQA_Q_HEADS, HEAD_DIM), q_g)
        ka = rms_norm(ka.reshape(bsz, n, GQA_KV_HEADS, HEAD_DIM), k_g)
        va = va.reshape(bsz, n, GQA_KV_HEADS, HEAD_DIM)
        qb = qb.reshape(bsz, n, 2 * DIFF_HEADS, HEAD_DIM)
        kb = kb.reshape(bsz, n, 2 * DIFF_HEADS, HEAD_DIM)
        vb = vb.reshape(bsz, n, DIFF_HEADS, DIFF_V_DIM)
        if with_pos:
            cos, sin = axial_rope_tables(n, t.dtype)
            qa, ka, qb, kb = (apply_rope(a, cos, sin) for a in (qa, ka, qb, kb))
        return qa, ka, va, qb, kb, vb

    lam = (jnp.exp(jnp.sum(lq1.astype(jnp.float32) * lk1.astype(jnp.float32)))
           - jnp.exp(jnp.sum(lq2.astype(jnp.float32) * lk2.astype(jnp.float32))) + lambda_init)

    def mix(qa, qb, ka, va, kb, vb):
        n = qa.shape[1]
        oa = gqa_attend(qa.reshape(bsz, n, GQA_KV_HEADS, GQA_GROUP, HEAD_DIM), ka, va)
        qb5 = qb.reshape(bsz, n, DIFF_HEADS, 2, HEAD_DIM)
        kb5 = kb.reshape(bsz, kb.shape[1], DIFF_HEADS, 2, HEAD_DIM)
        ob = diff_attend(qb5[..., 0, :], qb5[..., 1, :], kb5[..., 0, :], kb5[..., 1, :], vb, lam)
        ob = rms_norm(ob, subln_g) * (1.0 - lambda_init)
        merged = jnp.concatenate([oa.reshape(bsz, n, GQA_Q_W), ob.reshape(bsz, n, DIFF_V_W)], axis=-1)
        return merged @ w_out

    lqa, lka, lva, lqb, lkb, lvb = project(xn, True)
    cqa, cka, cva, cqb, ckb, cvb = project(cn, False)
    cat = lambda a, b: jnp.concatenate([a, b], axis=1)
    y_lat = mix(lqa, lqb, cat(cka, lka), cat(cva, lva), cat(ckb, lkb), cat(cvb, lvb))
    y_ctx = mix(cqa, cqb, cka, cva, ckb, cvb) if with_ctx_out else None
    return y_lat, y_ctx


def cmul(ar, ai, br, bi):
    return ar * br - ai * bi, ar * bi + ai * br


def lti_combine(e1, e2):
    a1r, a1i, b1r, b1i = e1
    a2r, a2i, b2r, b2i = e2
    ar, ai = cmul(a2r, a2i, a1r, a1i)
    br, bi = cmul(a2r, a2i, b1r, b1i)
    return ar, ai, br + b2r, bi + b2i


def diag_scan(lbar_re, lbar_im, bu_re, bu_im, h0, reverse):
    l = bu_re.shape[1]
    shape = (1, l) + lbar_re.shape
    elems = (jnp.broadcast_to(lbar_re, shape), jnp.broadcast_to(lbar_im, shape), bu_re, bu_im)
    a_re, a_im, s_re, s_im = lax.associative_scan(lti_combine, elems, reverse=reverse, axis=1)
    if h0 is not None:
        dr, di = cmul(a_re, a_im, h0[0], h0[1])
        s_re = s_re + dr
        s_im = s_im + di
    return s_re, s_im


def s5_direction(u_lat, u_ctx, a_re, a_im, log_dt, b_re, b_im, c_re, c_im, reverse, with_ctx_out):
    f32 = jnp.float32
    lre = jnp.minimum(a_re.astype(f32), -1e-4)
    lim = a_im.astype(f32)
    dt = jnp.exp(log_dt.astype(f32))[:, None]
    mag = jnp.exp(lre * dt)
    lbar_re = mag * jnp.cos(lim * dt)
    lbar_im = mag * jnp.sin(lim * dt)
    nr = lbar_re - 1.0
    ni = lbar_im
    den = lre * lre + lim * lim
    coef_re = (nr * lre + ni * lim) / den
    coef_im = (ni * lre - nr * lim) / den
    bb_re, bb_im = cmul(coef_re[..., None], coef_im[..., None], b_re.astype(f32), b_im.astype(f32))
    cr = c_re.astype(f32)
    ci = c_im.astype(f32)

    def drive(u):
        ug = u.reshape(u.shape[:2] + (SSM_GROUPS, SSM_GROUP_CH))
        return jnp.einsum('gpc,blgc->blgp', bb_re, ug), jnp.einsum('gpc,blgc->blgp', bb_im, ug)

    def readout(s_re, s_im):
        y = jnp.einsum('gcp,blgp->blgc', cr, s_re) - jnp.einsum('gcp,blgp->blgc', ci, s_im)
        return y.reshape(y.shape[:2] + (D_MODEL,))

    cs_re, cs_im = diag_scan(lbar_re, lbar_im, *drive(u_ctx), None, reverse)
    idx = 0 if reverse else cs_re.shape[1] - 1
    h0 = (cs_re[:, idx:idx + 1], cs_im[:, idx:idx + 1])
    ls_re, ls_im = diag_scan(lbar_re, lbar_im, *drive(u_lat), h0, reverse)
    y_lat = readout(ls_re, ls_im)
    y_ctx = readout(cs_re, cs_im) if with_ctx_out else None
    return y_lat, y_ctx


def s5_mixer(xn, cn, a_re, a_im, log_dt, b_re, b_im, c_re, c_im, d_skip, w_a, w_b, with_ctx_out):
    u = xn.astype(jnp.float32)
    uc = cn.astype(jnp.float32)
    yf, ycf = s5_direction(u, uc, a_re[0], a_im[0], log_dt[0], b_re[0], b_im[0], c_re[0], c_im[0], False, with_ctx_out)
    yb, ycb = s5_direction(u, uc, a_re[1], a_im[1], log_dt[1], b_re[1], b_im[1], c_re[1], c_im[1], True, with_ctx_out)
    d = d_skip.astype(jnp.float32)

    def out(y, uu):
        y = jax.nn.gelu(y + d * uu).astype(xn.dtype)
        return (y @ w_a) * jax.nn.sigmoid(y @ w_b)

    y_lat = out(yf + yb, u)
    y_ctx = out(ycf + ycb, uc) if with_ctx_out else None
    return y_lat, y_ctx


def moe_tokens(xt, gw, gb, rw, rb, w_gate, w_up, w_down):
    f32 = jnp.float32
    xf = xt.astype(f32)
    g_logits = xf @ gw.astype(f32) + gb.astype(f32)
    g_prob = jax.nn.softmax(g_logits, axis=-1)
    _, g_idx = lax.top_k(g_logits, 1)
    g_onehot = jax.nn.one_hot(g_idx[:, 0], MOE_GROUPS, dtype=f32)
    p_group = jnp.sum(g_prob * g_onehot, axis=-1, keepdims=True)
    e_logits = jnp.einsum('sd,gde->sge', xf, rw.astype(f32)) + rb.astype(f32)
    e_sel = jnp.einsum('sge,sg->se', e_logits, g_onehot)
    e_val, e_idx = lax.top_k(e_sel, MOE_TOP_K)
    w_top = jax.nn.softmax(e_val, axis=-1) * p_group
    expert = g_idx * MOE_EXPERTS_PER_GROUP + e_idx
    combine = jnp.sum(jax.nn.one_hot(expert, MOE_EXPERTS, dtype=f32) * w_top[..., None], axis=1)
    hid = jax.nn.silu(jnp.einsum('sd,edf->sef', xt, w_gate)) * jnp.einsum('sd,edf->sef', xt, w_up)
    hid = hid * combine[..., None].astype(hid.dtype)
    return jnp.einsum('sef,efd->sd', hid, w_down)


def hier_moe(x, gw, gb, rw, rb, w_gate, w_up, w_down):
    return lax.map(lambda xb: moe_tokens(xb, gw, gb, rw, rb, w_gate, w_up, w_down), x)


def setup_inputs(seed: int = 0) -> dict:
    key = jax.random.key(seed)
    ks = iter(jax.random.split(key, 48))
    f32 = jnp.float32
    D = D_MODEL
    n_attn = (DEPTH + 1) // 2
    n_ssm = DEPTH // 2

    def nrm(shape, std):
        return jax.random.normal(next(ks), shape, f32) * std

    def gain(shape):
        return 1.0 + nrm(shape, 0.02)

    return {
        'x': nrm((BATCH, SEQ, D), 1.0),
        'c': nrm((BATCH, D), 1.0),
        'ctx': nrm((BATCH, CTX_LEN, D), 1.0),
        'c_ctx': nrm((D,), 1.0),
        'mod_w': nrm((DEPTH, D, 6 * D), 0.5 * D ** -0.5),
        'mod_b': nrm((DEPTH, 6 * D), 0.02),
        'norm1_g': gain((DEPTH, D)),
        'norm2_g': gain((DEPTH, D)),
        'final_g': gain((D,)),
        'attn_w_in': nrm((n_attn, D, ATTN_IN_W), D ** -0.5),
        'attn_w_out': nrm((n_attn, ATTN_OUT_W, D), ATTN_OUT_W ** -0.5),
        'attn_q_norm_g': gain((n_attn, HEAD_DIM)),
        'attn_k_norm_g': gain((n_attn, HEAD_DIM)),
        'diff_lambda_q1': nrm((n_attn, HEAD_DIM), 0.1),
        'diff_lambda_k1': nrm((n_attn, HEAD_DIM), 0.1),
        'diff_lambda_q2': nrm((n_attn, HEAD_DIM), 0.1),
        'diff_lambda_k2': nrm((n_attn, HEAD_DIM), 0.1),
        'diff_subln_g': gain((n_attn, DIFF_V_DIM)),
        'ssm_a_re': -0.5 + nrm((n_ssm, 2, SSM_GROUPS, SSM_STATE), 0.01),
        'ssm_a_im': math.pi * jnp.arange(SSM_STATE, dtype=f32) + nrm((n_ssm, 2, SSM_GROUPS, SSM_STATE), 0.01),
        'ssm_log_dt': jax.random.uniform(next(ks), (n_ssm, 2, SSM_GROUPS), f32,
                                         math.log(SSM_DT_MIN), math.log(SSM_DT_MAX)),
        'ssm_b_re': nrm((n_ssm, 2, SSM_GROUPS, SSM_STATE, SSM_GROUP_CH), (2 * SSM_GROUP_CH) ** -0.5),
        'ssm_b_im': nrm((n_ssm, 2, SSM_GROUPS, SSM_STATE, SSM_GROUP_CH), (2 * SSM_GROUP_CH) ** -0.5),
        'ssm_c_re': nrm((n_ssm, 2, SSM_GROUPS, SSM_GROUP_CH, SSM_STATE), (2 * SSM_STATE) ** -0.5),
        'ssm_c_im': nrm((n_ssm, 2, SSM_GROUPS, SSM_GROUP_CH, SSM_STATE), (2 * SSM_STATE) ** -0.5),
        'ssm_d': nrm((n_ssm, D), 1.0),
        'ssm_glu_w_a': nrm((n_ssm, D, D), D ** -0.5),
        'ssm_glu_w_b': nrm((n_ssm, D, D), D ** -0.5),
        'moe_group_w': nrm((DEPTH, D, MOE_GROUPS), D ** -0.5),
        'moe_group_b': nrm((DEPTH, MOE_GROUPS), 0.01),
        'moe_router_w': nrm((DEPTH, MOE_GROUPS, D, MOE_EXPERTS_PER_GROUP), D ** -0.5),
        'moe_router_b': nrm((DEPTH, MOE_GROUPS, MOE_EXPERTS_PER_GROUP), 0.01),
        'moe_w_gate': nrm((DEPTH, MOE_EXPERTS, D, MOE_HIDDEN), D ** -0.5),
        'moe_w_up': nrm((DEPTH, MOE_EXPERTS, D, MOE_HIDDEN), D ** -0.5),
        'moe_w_down': nrm((DEPTH, MOE_EXPERTS, MOE_HIDDEN, D), MOE_HIDDEN ** -0.5),
    }


def reference(x, c, ctx, c_ctx, mod_w, mod_b, norm1_g, norm2_g, final_g,
              attn_w_in, attn_w_out, attn_q_norm_g, attn_k_norm_g,
              diff_lambda_q1, diff_lambda_k1, diff_lambda_q2, diff_lambda_k2, diff_subln_g,
              ssm_a_re, ssm_a_im, ssm_log_dt, ssm_b_re, ssm_b_im, ssm_c_re, ssm_c_im, ssm_d,
              ssm_glu_w_a, ssm_glu_w_b,
              moe_group_w, moe_group_b, moe_router_w, moe_router_b, moe_w_gate, moe_w_up, moe_w_down):
    h, hc = x, ctx
    for layer in range(DEPTH):
        last = layer == DEPTH - 1
        m = [t[:, None, :] for t in jnp.split(jax.nn.silu(c) @ mod_w[layer] + mod_b[layer], 6, axis=-1)]
        mc = [t[None, None, :] for t in jnp.split(jax.nn.silu(c_ctx) @ mod_w[layer] + mod_b[layer], 6, axis=-1)]
        xn = modulate(rms_norm(h, norm1_g[layer]), m[0], m[1])
        cn = modulate(rms_norm(hc, norm1_g[layer]), mc[0], mc[1])
        i = layer // 2
        if layer % 2 == 0:
            lambda_init = 0.8 - 0.6 * math.exp(-0.3 * layer)
            y, yc = attention_mixer(xn, cn, attn_w_in[i], attn_w_out[i], attn_q_norm_g[i], attn_k_norm_g[i],
                                    diff_lambda_q1[i], diff_lambda_k1[i], diff_lambda_q2[i], diff_lambda_k2[i],
                                    diff_subln_g[i], lambda_init, not last)
        else:
            y, yc = s5_mixer(xn, cn, ssm_a_re[i], ssm_a_im[i], ssm_log_dt[i], ssm_b_re[i], ssm_b_im[i],
                             ssm_c_re[i], ssm_c_im[i], ssm_d[i], ssm_glu_w_a[i], ssm_glu_w_b[i], not last)
        moe_p = (moe_group_w[layer], moe_group_b[layer], moe_router_w[layer], moe_router_b[layer],
                 moe_w_gate[layer], moe_w_up[layer], moe_w_down[layer])
        h = h + m[2] * y
        h = h + m[5] * hier_moe(modulate(rms_norm(h, norm2_g[layer]), m[3], m[4]), *moe_p)
        if not last:
            hc = hc + mc[2] * yc
            hc = hc + mc[5] * hier_moe(modulate(rms_norm(hc, norm2_g[layer]), mc[3], mc[4]), *moe_p)
    return rms_norm(h, final_g)
```

```python
import functools
import math

import jax
import jax.numpy as jnp
from jax import lax
from jax.experimental import pallas as pl
from jax.experimental.pallas import tpu as pltpu

F32 = jnp.float32
_MXU_DTYPE = jnp.bfloat16
_HIGHEST = lax.Precision.HIGHEST

LANES = 128
HEAD_DIM = 64
GRID_W = 64
ROPE_BASE = 10000.0
GQA_Q_HEADS = 8
GQA_GROUP = 4
DIFF_HEADS = 4
GQA_Q_W = 512
GQA_KV_W = 128
DIFF_QK_W = 512
DIFF_V_W = 512
SSM_GROUP_CH = 16
SSM_STATE = 64
MOE_GROUPS = 4
MOE_EPG = 8
MOE_EXPERTS = 32
RMS_EPS = 1e-6
SSM_CHUNK = 8
ROW_TILE = 256
VMEM_LIMIT = 56 * 1024 * 1024


def _params(*sem):
    return pltpu.CompilerParams(dimension_semantics=sem, vmem_limit_bytes=VMEM_LIMIT)


def _norm_mod(h, g, shift, scale):
    y = h * lax.rsqrt(jnp.mean(h * h, axis=-1, keepdims=True) + RMS_EPS) * g
    return y * (1.0 + scale) + shift


def _mm(a, b):
    return jnp.dot(a.astype(_MXU_DTYPE), b.astype(_MXU_DTYPE), preferred_element_type=F32)


def _mod_kernel(c_ref, w_ref, b_ref, o_ref):
    c = c_ref[...]
    a = c / (1.0 + jnp.exp(-c))
    o_ref[...] = jnp.dot(a, w_ref[...], preferred_element_type=F32, precision=_HIGHEST) + b_ref[...]


def _modulation(c_all, mod_w, mod_b):
    depth, d, n = mod_w.shape
    rows = c_all.shape[0]
    tn = 1536
    return pl.pallas_call(
        _mod_kernel,
        grid=(depth, n // tn),
        in_specs=[pl.BlockSpec((rows, d), lambda l, j: (0, 0)),
                  pl.BlockSpec((None, d, tn), lambda l, j: (l, 0, j)),
                  pl.BlockSpec((None, 1, tn), lambda l, j: (l, 0, j))],
        out_specs=pl.BlockSpec((None, rows, tn), lambda l, j: (l, 0, j)),
        out_shape=jax.ShapeDtypeStruct((depth, rows, n), F32),
        compiler_params=_params("parallel", "parallel"),
        name="modulation",
    )(c_all, mod_w, mod_b.reshape(depth, 1, n))


def _mod_spec(d):
    return pl.BlockSpec((None, None, 8, d), lambda b, i: (b, jnp.minimum(i, 1), 0, 0))


def _attn_proj_kernel(h_ref, mod_ref, g_ref, w_ref, gq_ref, gk_ref, cos_ref, sin_ref,
                      qa_ref, ka_ref, va_ref, qb_ref, kb_ref, vb_ref):
    xn = _norm_mod(h_ref[...], g_ref[...], mod_ref[0:1, :], mod_ref[1:2, :])
    hp = _mm(xn, w_ref[...])
    cos = cos_ref[...]
    sin = sin_ref[...]
    lane = lax.broadcasted_iota(jnp.int32, (1, LANES), 1)
    first_half = (lane % 32) < 16
    r = lax.broadcasted_iota(jnp.int32, (LANES, LANES), 0) // HEAD_DIM
    c = lax.broadcasted_iota(jnp.int32, (LANES, LANES), 1) // HEAD_DIM
    same_head = (r == c).astype(_MXU_DTYPE)

    def rope(x):
        rot = jnp.where(first_half, -pltpu.roll(x, LANES - 16, 1), pltpu.roll(x, 16, 1))
        return x * cos + rot * sin

    def head_norm(x, g):
        ss = jnp.dot((x * x).astype(_MXU_DTYPE), same_head, preferred_element_type=F32)
        return x * lax.rsqrt(ss * (1.0 / HEAD_DIM) + RMS_EPS) * g

    scale = HEAD_DIM ** -0.5
    o = 0
    for s in range(GQA_Q_W // LANES):
        x = hp[:, o + s * LANES:o + (s + 1) * LANES]
        qa_ref[:, s * LANES:(s + 1) * LANES] = (rope(head_norm(x, gq_ref[...])) * scale).astype(qa_ref.dtype)
    o += GQA_Q_W
    ka_ref[...] = rope(head_norm(hp[:, o:o + LANES], gk_ref[...])).astype(ka_ref.dtype)
    o += GQA_KV_W
    va_ref[...] = hp[:, o:o + LANES].astype(va_ref.dtype)
    o += GQA_KV_W
    for s in range(DIFF_QK_W // LANES):
        x = hp[:, o + s * LANES:o + (s + 1) * LANES]
        qb_ref[:, s * LANES:(s + 1) * LANES] = (rope(x) * scale).astype(qb_ref.dtype)
    o += DIFF_QK_W
    for s in range(DIFF_QK_W // LANES):
        x = hp[:, o + s * LANES:o + (s + 1) * LANES]
        kb_ref[:, s * LANES:(s + 1) * LANES] = rope(x).astype(kb_ref.dtype)
    o += DIFF_QK_W
    vb_ref[...] = hp[:, o:o + DIFF_V_W].astype(vb_ref.dtype)


def _attn_proj(h, mod_tab, g, w_in, gq, gk, cos, sin):
    b, nt, d = h.shape
    tm = ROW_TILE
    widths = (GQA_Q_W, GQA_KV_W, GQA_KV_W, DIFF_QK_W, DIFF_QK_W, DIFF_V_W)
    full = lambda shape: pl.BlockSpec(shape, lambda bb, i: (0,) * len(shape))
    return pl.pallas_call(
        _attn_proj_kernel,
        grid=(b, nt // tm),
        in_specs=[pl.BlockSpec((None, tm, d), lambda bb, i: (bb, i, 0)),
                  _mod_spec(d),
                  full((1, d)),
                  full(w_in.shape),
                  full((1, LANES)),
                  full((1, LANES)),
                  pl.BlockSpec((tm, LANES), lambda bb, i: (i, 0)),
                  pl.BlockSpec((tm, LANES), lambda bb, i: (i, 0))],
        out_specs=[pl.BlockSpec((None, tm, w), lambda bb, i: (bb, i, 0)) for w in widths],
        out_shape=[jax.ShapeDtypeStruct((b, nt, w), _MXU_DTYPE) for w in widths],
        compiler_params=_params("parallel", "parallel"),
        name="attn_proj",
    )(h, mod_tab, g, w_in, gq, gk, cos, sin)


def _softmax_pv(q, k, v):
    s = lax.dot_general(q, k, (((1,), (1,)), ((), ())), preferred_element_type=F32)
    e = jnp.exp(s - jnp.max(s, axis=-1, keepdims=True))
    l = jnp.sum(e, axis=-1, keepdims=True)
    return jnp.dot(e.astype(v.dtype), v, preferred_element_type=F32) / l


def _attn_kernel(qa_ref, qb_ref, ka_ref, va_ref, kb_ref, vb_ref, lam_ref, sg_ref, wo_ref, h_ref, mod_ref,
                 o_ref, mrg_ref, *, lambda_init, ctx_len):
    lv = lam_ref[...]
    lam = (jnp.exp(jnp.sum(lv[0:1] * lv[1:2], axis=-1, keepdims=True))
           - jnp.exp(jnp.sum(lv[2:3] * lv[3:4], axis=-1, keepdims=True)) + lambda_init)

    def run(nk):
        for h in range(GQA_Q_HEADS):
            g = h // GQA_GROUP
            o = _softmax_pv(qa_ref[:, h * HEAD_DIM:(h + 1) * HEAD_DIM],
                            ka_ref[0:nk, g * HEAD_DIM:(g + 1) * HEAD_DIM],
                            va_ref[0:nk, g * HEAD_DIM:(g + 1) * HEAD_DIM])
            mrg_ref[:, h * HEAD_DIM:(h + 1) * HEAD_DIM] = o.astype(mrg_ref.dtype)
        for h in range(DIFF_HEADS):
            c0 = h * 2 * HEAD_DIM
            v = vb_ref[0:nk, c0:c0 + 2 * HEAD_DIM]
            o1 = _softmax_pv(qb_ref[:, c0:c0 + HEAD_DIM], kb_ref[0:nk, c0:c0 + HEAD_DIM], v)
            o2 = _softmax_pv(qb_ref[:, c0 + HEAD_DIM:c0 + 2 * HEAD_DIM],
                             kb_ref[0:nk, c0 + HEAD_DIM:c0 + 2 * HEAD_DIM], v)
            o = o1 - lam * o2
            o = o * lax.rsqrt(jnp.mean(o * o, axis=-1, keepdims=True) + RMS_EPS) * sg_ref[...]
            o = o * (1.0 - lambda_init)
            mrg_ref[:, GQA_Q_W + c0:GQA_Q_W + c0 + 2 * HEAD_DIM] = o.astype(mrg_ref.dtype)

    i = pl.program_id(1)

    @pl.when(i == 0)
    def _():
        run(ctx_len)

    @pl.when(i > 0)
    def _():
        run(ka_ref.shape[0])

    y = jnp.dot(mrg_ref[...], wo_ref[...], preferred_element_type=F32)
    o_ref[...] = h_ref[...] + mod_ref[2:3, :] * y


def _attention(qkv, lam_rows, subln_g, w_out, h, mod_tab, lambda_init, ctx_len):
    qa, ka, va, qb, kb, vb = qkv
    b, nt, d = h.shape
    tq = ROW_TILE
    assert ctx_len == tq
    blk = lambda w: pl.BlockSpec((None, tq, w), lambda bb, i: (bb, i, 0))
    per_batch = lambda w: pl.BlockSpec((None, nt, w), lambda bb, i: (bb, 0, 0))
    full = lambda shape: pl.BlockSpec(shape, lambda bb, i: (0,) * len(shape))
    return pl.pallas_call(
        functools.partial(_attn_kernel, lambda_init=lambda_init, ctx_len=ctx_len),
        grid=(b, nt // tq),
        in_specs=[blk(GQA_Q_W), blk(DIFF_QK_W), per_batch(GQA_KV_W), per_batch(GQA_KV_W),
                  per_batch(DIFF_QK_W), per_batch(DIFF_V_W),
                  full((8, LANES)), full((1, LANES)), full(w_out.shape), blk(d), _mod_spec(d)],
        out_specs=blk(d),
        out_shape=jax.ShapeDtypeStruct((b, nt, d), F32),
        scratch_shapes=[pltpu.VMEM((tq, GQA_Q_W + DIFF_V_W), _MXU_DTYPE)],
        compiler_params=_params("parallel", "parallel"),
        name="attention",
    )(qa, qb, ka, va, kb, vb, lam_rows, subln_g, w_out, h, mod_tab)


def _norm1_kernel(h_ref, mod_ref, g_ref, o_ref):
    o_ref[...] = _norm_mod(h_ref[...], g_ref[...], mod_ref[0:1, :], mod_ref[1:2, :])


def _norm1(h, mod_tab, g):
    b, nt, d = h.shape
    tm = ROW_TILE
    blk = pl.BlockSpec((None, tm, d), lambda bb, i: (bb, i, 0))
    return pl.pallas_call(
        _norm1_kernel,
        grid=(b, nt // tm),
        in_specs=[blk, _mod_spec(d), pl.BlockSpec((1, d), lambda bb, i: (0, 0))],
        out_specs=blk,
        out_shape=jax.ShapeDtypeStruct((b, nt, d), F32),
        compiler_params=_params("parallel", "parallel"),
        name="ssm_norm",
    )(h, mod_tab, g)


def _ssm_kernel(u_ref, win_ref, m_ref, wout_ref, lam_ref, y_ref, bd_ref, *, chunk, n_ctx_chunks):
    nb, nt, _ = u_ref.shape
    nc = nt // chunk
    n_state_slabs = bd_ref.shape[0]
    q = n_state_slabs // 4

    def chunk_rows(bi):
        parts = [u_ref[bi, pl.ds(s, nc, stride=chunk), :] for s in range(chunk)]
        return jnp.concatenate(parts, axis=1).astype(_MXU_DTYPE)

    for bi in range(nb):
        drive = jnp.dot(chunk_rows(bi), win_ref[...], preferred_element_type=F32)
        for c in range(n_state_slabs):
            bd_ref[c, pl.ds(bi, nc, stride=nb), :] = drive[:, c * LANES:(c + 1) * LANES]

    lam = lam_ref[...]

    def make_step(base):
        a_re = [lam[:, (base + c) * LANES:(base + c + 1) * LANES] for c in range(q)]
        a_im = [lam[:, (base + q + c) * LANES:(base + q + c + 1) * LANES] for c in range(q)]

        def step(k, carry):
            row = pl.multiple_of(k * nb, nb)
            out = []
            for c in range(q):
                s_re, s_im = carry[2 * c], carry[2 * c + 1]
                d_re = bd_ref[base + c, pl.ds(row, nb), :]
                d_im = bd_ref[base + q + c, pl.ds(row, nb), :]
                bd_ref[base + c, pl.ds(row, nb), :] = s_re
                bd_ref[base + q + c, pl.ds(row, nb), :] = s_im
                out.append(a_re[c] * s_re - a_im[c] * s_im + d_re)
                out.append(a_re[c] * s_im + a_im[c] * s_re + d_im)
            return tuple(out)

        return step

    zero = tuple(jnp.zeros((nb, LANES), F32) for _ in range(2 * q))
    fwd = make_step(0)
    lax.fori_loop(0, nc, fwd, zero)
    rev = make_step(2 * q)
    carry = lax.fori_loop(0, n_ctx_chunks, lambda i, cr: rev(n_ctx_chunks - 1 - i, cr), zero)
    lax.fori_loop(0, nc - n_ctx_chunks, lambda i, cr: rev(nc - 1 - i, cr), carry)

    for bi in range(nb):
        states = jnp.concatenate([bd_ref[c, pl.ds(bi, nc, stride=nb), :] for c in range(n_state_slabs)], axis=1)
        y = (jnp.dot(chunk_rows(bi), m_ref[...], preferred_element_type=F32)
             + jnp.dot(states.astype(_MXU_DTYPE), wout_ref[...], preferred_element_type=F32))
        for t in range(chunk):
            y_ref[bi, pl.ds(t, nc, stride=chunk), :] = y[:, t * LANES:(t + 1) * LANES]


def _ssm_scan(u, win, m, wout, lam_t, ctx_len):
    b, nt, d = u.shape
    chunk = SSM_CHUNK
    nb = 4 if b % 4 == 0 else b
    n_slabs = d // LANES
    state_w = win.shape[-1]
    nc = nt // chunk
    blk = pl.BlockSpec((nb, nt, LANES), lambda j, bb: (bb, 0, j))
    return pl.pallas_call(
        functools.partial(_ssm_kernel, chunk=chunk, n_ctx_chunks=ctx_len // chunk),
        grid=(n_slabs, b // nb),
        in_specs=[blk,
                  pl.BlockSpec((None,) + win.shape[1:], lambda j, bb: (j, 0, 0)),
                  pl.BlockSpec((None,) + m.shape[1:], lambda j, bb: (j, 0, 0)),
                  pl.BlockSpec((None,) + wout.shape[1:], lambda j, bb: (j, 0, 0)),
                  pl.BlockSpec((None, 1, state_w), lambda j, bb: (j, 0, 0))],
        out_specs=blk,
        out_shape=jax.ShapeDtypeStruct((b, nt, d), F32),
        scratch_shapes=[pltpu.VMEM((state_w // LANES, nc * nb, LANES), F32)],
        compiler_params=_params("parallel", "parallel"),
        name="ssm_scan",
    )(u, win, m, wout, lam_t)


def _ssm_out_kernel(y_ref, u_ref, d_ref, wa_ref, wb_ref, h_ref, mod_ref, o_ref):
    x = y_ref[...] + d_ref[...] * u_ref[...]
    z = 0.5 * x * (1.0 + jnp.tanh(math.sqrt(2.0 / math.pi) * (x + 0.044715 * (x * x * x))))
    z = z.astype(_MXU_DTYPE)
    a = jnp.dot(z, wa_ref[...], preferred_element_type=F32)
    g = jnp.dot(z, wb_ref[...], preferred_element_type=F32)
    o_ref[...] = h_ref[...] + mod_ref[2:3, :] * (a / (1.0 + jnp.exp(-g)))


def _ssm_out(y, u, d_skip, wa, wb, h, mod_tab):
    b, nt, d = h.shape
    tm = ROW_TILE
    blk = pl.BlockSpec((None, tm, d), lambda bb, i: (bb, i, 0))
    full = lambda shape: pl.BlockSpec(shape, lambda bb, i: (0,) * len(shape))
    return pl.pallas_call(
        _ssm_out_kernel,
        grid=(b, nt // tm),
        in_specs=[blk, blk, full((1, d)), full(wa.shape), full(wb.shape), blk, _mod_spec(d)],
        out_specs=blk,
        out_shape=jax.ShapeDtypeStruct((b, nt, d), F32),
        compiler_params=_params("parallel", "parallel"),
        name="ssm_out",
    )(y, u, d_skip, wa, wb, h, mod_tab)


def _ssm_tables(a_re, a_im, log_dt, b_re, b_im, c_re, c_im, chunk):
    g_total = a_re.shape[1]
    gpt = LANES // SSM_GROUP_CH
    n_slabs = g_total // gpt
    p = SSM_STATE
    lre = jnp.minimum(a_re, -1e-4)
    lim = a_im
    dt = jnp.exp(log_dt)[..., None]
    steps = jnp.arange(chunk + 1, dtype=F32)[:, None, None, None]
    mag = jnp.exp(steps * (lre * dt))
    pw_re = mag * jnp.cos(steps * (lim * dt))
    pw_im = mag * jnp.sin(steps * (lim * dt))
    nr = pw_re[1] - 1.0
    ni = pw_im[1]
    den = lre * lre + lim * lim
    coef_re = (nr * lre + ni * lim) / den
    coef_im = (ni * lre - nr * lim) / den
    bb_re = coef_re[..., None] * b_re - coef_im[..., None] * b_im
    bb_im = coef_re[..., None] * b_im + coef_im[..., None] * b_re

    ein = functools.partial(jnp.einsum, precision=_HIGHEST)
    cp_re = ein('jxgp,xgcp->jxgcp', pw_re, c_re) - ein('jxgp,xgcp->jxgcp', pw_im, c_im)
    cp_im = ein('jxgp,xgcp->jxgcp', pw_re, c_im) + ein('jxgp,xgcp->jxgcp', pw_im, c_re)
    taps = ein('jxgcp,xgpd->jxgcd', cp_re, bb_re) - ein('jxgcp,xgpd->jxgcd', cp_im, bb_im)
    s_idx = jnp.arange(chunk)[:, None]
    t_idx = jnp.arange(chunk)[None, :]
    lag = t_idx - s_idx
    kf = taps[jnp.clip(lag, 0, chunk), 0]
    kr = taps[jnp.clip(-lag, 0, chunk), 1]
    sel = lambda cond: cond[:, :, None, None, None]
    kst = jnp.where(sel(lag >= 0), kf, 0.0) + jnp.where(sel(lag <= 0), kr, 0.0)
    eye = jnp.eye(gpt, dtype=F32)
    kst = kst.reshape(chunk, chunk, n_slabs, gpt, SSM_GROUP_CH, SSM_GROUP_CH)
    m = jnp.einsum('stjacd,ab->jsadtbc', kst, eye).reshape(n_slabs, chunk * LANES, chunk * LANES)

    def drive(pw_r, pw_i, br, bi):
        re = ein('sgp,gpd->sgdp', pw_r, br) - ein('sgp,gpd->sgdp', pw_i, bi)
        im = ein('sgp,gpd->sgdp', pw_r, bi) + ein('sgp,gpd->sgdp', pw_i, br)
        return re, im

    f_re, f_im = drive(pw_re[:chunk, 0][::-1], pw_im[:chunk, 0][::-1], bb_re[0], bb_im[0])
    r_re, r_im = drive(pw_re[:chunk, 1], pw_im[:chunk, 1], bb_re[1], bb_im[1])
    win = jnp.stack([f_re, f_im, r_re, r_im], axis=0)
    win = win.reshape(4, chunk, n_slabs, gpt, SSM_GROUP_CH, p)
    win = jnp.einsum('qsjadp,ab->jsadqbp', win, eye).reshape(n_slabs, chunk * LANES, 4 * gpt * p)

    of_re, of_im = cp_re[1:, 0], cp_im[1:, 0]
    or_re, or_im = cp_re[1:, 1][::-1], cp_im[1:, 1][::-1]
    wout = jnp.stack([of_re, -of_im, or_re, -or_im], axis=0)
    wout = wout.reshape(4, chunk, n_slabs, gpt, SSM_GROUP_CH, p)
    wout = jnp.einsum('qtjacp,ab->jqaptbc', wout, eye).reshape(n_slabs, 4 * gpt * p, chunk * LANES)

    lam_t = jnp.stack([pw_re[chunk, 0], pw_im[chunk, 0], pw_re[chunk, 1], pw_im[chunk, 1]], axis=0)
    lam_t = lam_t.reshape(4, n_slabs, gpt * p).transpose(1, 0, 2).reshape(n_slabs, 1, 4 * gpt * p)
    return win.astype(_MXU_DTYPE), m.astype(_MXU_DTYPE), wout.astype(_MXU_DTYPE), lam_t


def _router_kernel(h_ref, mod_ref, g_ref, wr_ref, br_ref, xt_ref, cmb_ref):
    xt = _norm_mod(h_ref[...], g_ref[...], mod_ref[3:4, :], mod_ref[4:5, :])
    xt_ref[...] = xt.astype(xt_ref.dtype)
    logits = jnp.dot(xt, wr_ref[...], preferred_element_type=F32, precision=_HIGHEST) + br_ref[...]
    lane = lax.broadcasted_iota(jnp.int32, (1, LANES), 1)
    lane_f = lane.astype(F32)
    neg = -jnp.inf
    big = 1e9
    gmask = (lane >= MOE_EXPERTS) & (lane < MOE_EXPERTS + MOE_GROUPS)
    gl = jnp.where(gmask, logits, neg)
    gmax = jnp.max(gl, axis=-1, keepdims=True)
    gidx = jnp.min(jnp.where(gl == gmax, lane_f, big), axis=-1, keepdims=True) - MOE_EXPERTS
    p_group = 1.0 / jnp.sum(jnp.where(gmask, jnp.exp(gl - gmax), 0.0), axis=-1, keepdims=True)
    in_group = (lane < MOE_EXPERTS) & ((lane // MOE_EPG).astype(F32) == gidx)
    el = jnp.where(in_group, logits, neg)
    v1 = jnp.max(el, axis=-1, keepdims=True)
    i1 = jnp.min(jnp.where(el == v1, lane_f, big), axis=-1, keepdims=True)
    el2 = jnp.where(lane_f == i1, neg, el)
    v2 = jnp.max(el2, axis=-1, keepdims=True)
    i2 = jnp.min(jnp.where(el2 == v2, lane_f, big), axis=-1, keepdims=True)
    t = jnp.exp(v2 - v1)
    w1 = p_group / (1.0 + t)
    w2 = p_group * t / (1.0 + t)
    cmb_ref[...] = jnp.where(lane_f == i1, w1, 0.0) + jnp.where(lane_f == i2, w2, 0.0)


def _router(h, mod_tab, g, wr, br):
    b, nt, d = h.shape
    tm = ROW_TILE
    full = lambda shape: pl.BlockSpec(shape, lambda bb, i: (0,) * len(shape))
    return pl.pallas_call(
        _router_kernel,
        grid=(b, nt // tm),
        in_specs=[pl.BlockSpec((None, tm, d), lambda bb, i: (bb, i, 0)), _mod_spec(d),
                  full((1, d)), full(wr.shape), full(br.shape)],
        out_specs=[pl.BlockSpec((None, tm, d), lambda bb, i: (bb, i, 0)),
                   pl.BlockSpec((None, tm, LANES), lambda bb, i: (bb, i, 0))],
        out_shape=[jax.ShapeDtypeStruct((b, nt, d), _MXU_DTYPE),
                   jax.ShapeDtypeStruct((b, nt, LANES), F32)],
        compiler_params=_params("parallel", "parallel"),
        name="moe_router",
    )(h, mod_tab, g, wr, br)


def _experts_kernel(xt_ref, cmb_ref, wg_ref, wu_ref, wd_ref, h_ref, mod_ref, o_ref, acc_ref, *, ctx_len, nt):
    i = pl.program_id(0)
    e = pl.program_id(1)
    tm = xt_ref.shape[0]

    @pl.when(e == 0)
    def _():
        acc_ref[...] = jnp.zeros_like(acc_ref)

    x = xt_ref[...]
    gate = jnp.dot(x, wg_ref[...], preferred_element_type=F32)
    up = jnp.dot(x, wu_ref[...], preferred_element_type=F32)
    lane = lax.broadcasted_iota(jnp.int32, (1, LANES), 1)
    w = jnp.sum(jnp.where(lane == e, cmb_ref[...], 0.0), axis=-1, keepdims=True)
    hid = (gate / (1.0 + jnp.exp(-gate))) * up * w
    acc_ref[...] += jnp.dot(hid.astype(_MXU_DTYPE), wd_ref[...], preferred_element_type=F32)

    @pl.when(e == pl.num_programs(1) - 1)
    def _():
        row = (i * tm) % nt + lax.broadcasted_iota(jnp.int32, (tm, 1), 0)
        gate_row = jnp.where(row < ctx_len, mod_ref[0, 5:6, :], mod_ref[1, 5:6, :])
        o_ref[...] = h_ref[...] + gate_row * acc_ref[...]


def _experts(xt, cmb, wg, wu, wd, h, mod_tab, ctx_len):
    b, nt, d = h.shape
    tm = 768 if nt % 768 == 0 else ROW_TILE
    per_b = nt // tm
    n_exp, _, hid = wg.shape
    rows = b * nt
    blk = lambda w: pl.BlockSpec((tm, w), lambda i, e: (i, 0))
    out = pl.pallas_call(
        functools.partial(_experts_kernel, ctx_len=ctx_len, nt=nt),
        grid=(rows // tm, n_exp),
        in_specs=[blk(d), blk(LANES),
                  pl.BlockSpec((None, d, hid), lambda i, e: (e, 0, 0)),
                  pl.BlockSpec((None, d, hid), lambda i, e: (e, 0, 0)),
                  pl.BlockSpec((None, hid, d), lambda i, e: (e, 0, 0)),
                  blk(d),
                  pl.BlockSpec((None, 2, 8, d), lambda i, e: (i // per_b, 0, 0, 0))],
        out_specs=blk(d),
        out_shape=jax.ShapeDtypeStruct((rows, d), F32),
        scratch_shapes=[pltpu.VMEM((tm, d), F32)],
        compiler_params=_params("parallel", "arbitrary"),
        name="moe_experts",
    )(xt.reshape(rows, d), cmb.reshape(rows, LANES), wg, wu, wd, h.reshape(rows, d), mod_tab)
    return out.reshape(b, nt, d)


def _final_kernel(h_ref, g_ref, o_ref):
    h = h_ref[...]
    o_ref[...] = h * lax.rsqrt(jnp.mean(h * h, axis=-1, keepdims=True) + RMS_EPS) * g_ref[...]


def _final_norm(h, g, ctx_len):
    b, nt, d = h.shape
    tm = ROW_TILE
    skip = ctx_len // tm
    return pl.pallas_call(
        _final_kernel,
        grid=(b, (nt - ctx_len) // tm),
        in_specs=[pl.BlockSpec((None, tm, d), lambda bb, i: (bb, i + skip, 0)),
                  pl.BlockSpec((1, d), lambda bb, i: (0, 0))],
        out_specs=pl.BlockSpec((None, tm, d), lambda bb, i: (bb, i, 0)),
        out_shape=jax.ShapeDtypeStruct((b, nt - ctx_len, d), F32),
        compiler_params=_params("parallel", "parallel"),
        name="final_norm",
    )(h, g)


def _rope_tables(seq_len, ctx_len):
    n_rows = seq_len // GRID_W
    rows = jnp.repeat(jnp.arange(n_rows, dtype=F32), GRID_W)
    cols = jnp.tile(jnp.arange(GRID_W, dtype=F32), n_rows)
    half = HEAD_DIM // 2
    inv = 1.0 / (ROPE_BASE ** (jnp.arange(0, half, 2, dtype=F32) / half))
    ang_r = rows[:, None] * inv
    ang_c = cols[:, None] * inv
    ang = jnp.concatenate([ang_r, ang_r, ang_c, ang_c], axis=-1)
    ang = jnp.concatenate([jnp.zeros((ctx_len, HEAD_DIM), F32), ang], axis=0)
    ang = jnp.tile(ang, (1, LANES // HEAD_DIM))
    return jnp.cos(ang), jnp.sin(ang)


def _pad_row(v, width=LANES):
    return jnp.pad(v, (0, width - v.shape[0]))[None, :]


def kernel(x, c, ctx, c_ctx, mod_w, mod_b, norm1_g, norm2_g, final_g, attn_w_in, attn_w_out, attn_q_norm_g, attn_k_norm_g, diff_lambda_q1, diff_lambda_k1, diff_lambda_q2, diff_lambda_k2, diff_subln_g, ssm_a_re, ssm_a_im, ssm_log_dt, ssm_b_re, ssm_b_im, ssm_c_re, ssm_c_im, ssm_d, ssm_glu_w_a, ssm_glu_w_b, moe_group_w, moe_group_b, moe_router_w, moe_router_b, moe_w_gate, moe_w_up, moe_w_down):
    bsz, seq, d = x.shape
    ctx_len = ctx.shape[1]
    depth = mod_w.shape[0]
    assert ctx_len == ROW_TILE and seq % ROW_TILE == 0 and seq % GRID_W == 0

    h = jnp.concatenate([ctx, x], axis=1)

    mod_rows = 16
    c_all = jnp.concatenate([c, c_ctx[None, :], jnp.zeros((mod_rows - bsz - 1, d), F32)], axis=0)
    mods = _modulation(c_all, mod_w, mod_b).reshape(depth, mod_rows, 6, d)
    mods = jnp.pad(mods, ((0, 0), (0, 0), (0, 2), (0, 0)))
    mod_tabs = jnp.stack([jnp.broadcast_to(mods[:, bsz:bsz + 1], (depth, bsz, 8, d)), mods[:, :bsz]], axis=2)

    cos, sin = _rope_tables(seq, ctx_len)
    cast = lambda w: w.astype(_MXU_DTYPE)

    for layer in range(depth):
        mod_tab = mod_tabs[layer]
        i = layer // 2
        if layer % 2 == 0:
            lambda_init = 0.8 - 0.6 * math.exp(-0.3 * layer)
            qkv = _attn_proj(h, mod_tab, norm1_g[layer][None, :], cast(attn_w_in[i]),
                             jnp.tile(attn_q_norm_g[i], 2)[None, :], jnp.tile(attn_k_norm_g[i], 2)[None, :],
                             cos, sin)
            lam_rows = jnp.concatenate([_pad_row(diff_lambda_q1[i]), _pad_row(diff_lambda_k1[i]),
                                        _pad_row(diff_lambda_q2[i]), _pad_row(diff_lambda_k2[i]),
                                        jnp.zeros((4, LANES), F32)], axis=0)
            h = _attention(qkv, lam_rows, diff_subln_g[i][None, :], cast(attn_w_out[i]), h, mod_tab,
                           lambda_init, ctx_len)
        else:
            u = _norm1(h, mod_tab, norm1_g[layer][None, :])
            win, m, wout, lam_t = _ssm_tables(ssm_a_re[i], ssm_a_im[i], ssm_log_dt[i], ssm_b_re[i], ssm_b_im[i],
                                              ssm_c_re[i], ssm_c_im[i], SSM_CHUNK)
            y = _ssm_scan(u, win, m, wout, lam_t, ctx_len)
            h = _ssm_out(y, u, ssm_d[i][None, :], cast(ssm_glu_w_a[i]), cast(ssm_glu_w_b[i]), h, mod_tab)

        wr = jnp.concatenate([jnp.transpose(moe_router_w[layer], (1, 0, 2)).reshape(d, MOE_EXPERTS),
                              moe_group_w[layer],
                              jnp.zeros((d, LANES - MOE_EXPERTS - MOE_GROUPS), F32)], axis=1)
        br = _pad_row(jnp.concatenate([moe_router_b[layer].reshape(-1), moe_group_b[layer]]))
        xt, cmb = _router(h, mod_tab, norm2_g[layer][None, :], wr, br)
        h = _experts(xt, cmb, cast(moe_w_gate[layer]), cast(moe_w_up[layer]), cast(moe_w_down[layer]),
                     h, mod_tab, ctx_len)

    return _final_norm(h, final_g[None, :], ctx_len)
```

```python
import functools
import math

import jax
import jax.numpy as jnp
from jax import lax
from jax.experimental import pallas as pl
from jax.experimental.pallas import tpu as pltpu

F32 = jnp.float32
_MXU_DTYPE = jnp.bfloat16
_HIGHEST = lax.Precision.HIGHEST

LANES = 128
HEAD_DIM = 64
GRID_W = 64
ROPE_BASE = 10000.0
GQA_Q_HEADS = 8
GQA_GROUP = 4
DIFF_HEADS = 4
GQA_Q_W = 512
GQA_KV_W = 128
DIFF_QK_W = 512
DIFF_V_W = 512
SSM_GROUP_CH = 16
SSM_STATE = 64
MOE_GROUPS = 4
MOE_EPG = 8
MOE_EXPERTS = 32
RMS_EPS = 1e-6
SSM_CHUNK = 8
ROW_TILE = 256
VMEM_LIMIT = 56 * 1024 * 1024


def _params(*sem):
    return pltpu.CompilerParams(dimension_semantics=sem, vmem_limit_bytes=VMEM_LIMIT)


def _norm_mod(h, g, shift, scale):
    y = h * lax.rsqrt(jnp.mean(h * h, axis=-1, keepdims=True) + RMS_EPS) * g
    return y * (1.0 + scale) + shift


def _mm(a, b):
    return jnp.dot(a.astype(_MXU_DTYPE), b.astype(_MXU_DTYPE), preferred_element_type=F32)


def _mod_kernel(c_ref, w_ref, b_ref, o_ref):
    c = c_ref[...]
    a = c / (1.0 + jnp.exp(-c))
    o_ref[...] = jnp.dot(a, w_ref[...], preferred_element_type=F32, precision=_HIGHEST) + b_ref[...]


def _modulation(c_all, mod_w, mod_b):
    depth, d, n = mod_w.shape
    rows = c_all.shape[0]
    tn = 1536
    return pl.pallas_call(
        _mod_kernel,
        grid=(depth, n // tn),
        in_specs=[pl.BlockSpec((rows, d), lambda l, j: (0, 0)),
                  pl.BlockSpec((None, d, tn), lambda l, j: (l, 0, j)),
                  pl.BlockSpec((None, 1, tn), lambda l, j: (l, 0, j))],
        out_specs=pl.BlockSpec((None, rows, tn), lambda l, j: (l, 0, j)),
        out_shape=jax.ShapeDtypeStruct((depth, rows, n), F32),
        compiler_params=_params("parallel", "parallel"),
        name="modulation",
    )(c_all, mod_w, mod_b.reshape(depth, 1, n))


def _mod_spec(d):
    return pl.BlockSpec((None, None, 8, d), lambda b, i: (b, jnp.minimum(i, 1), 0, 0))


def _attn_proj_kernel(h_ref, mod_ref, g_ref, w_ref, gq_ref, gk_ref, cos_ref, sin_ref,
                      qa_ref, ka_ref, va_ref, qb_ref, kb_ref, vb_ref):
    xn = _norm_mod(h_ref[...], g_ref[...], mod_ref[0:1, :], mod_ref[1:2, :])
    hp = _mm(xn, w_ref[...])
    cos = cos_ref[...]
    sin = sin_ref[...]
    lane = lax.broadcasted_iota(jnp.int32, (1, LANES), 1)
    first_half = (lane % 32) < 16
    r = lax.broadcasted_iota(jnp.int32, (LANES, LANES), 0) // HEAD_DIM
    c = lax.broadcasted_iota(jnp.int32, (LANES, LANES), 1) // HEAD_DIM
    same_head = (r == c).astype(_MXU_DTYPE)

    def rope(x):
        rot = jnp.where(first_half, -pltpu.roll(x, LANES - 16, 1), pltpu.roll(x, 16, 1))
        return x * cos + rot * sin

    def head_norm(x, g):
        ss = jnp.dot((x * x).astype(_MXU_DTYPE), same_head, preferred_element_type=F32)
        return x * lax.rsqrt(ss * (1.0 / HEAD_DIM) + RMS_EPS) * g

    scale = HEAD_DIM ** -0.5 * math.log2(math.e)
    o = 0
    for s in range(GQA_Q_W // LANES):
        x = hp[:, o + s * LANES:o + (s + 1) * LANES]
        qa_ref[:, s * LANES:(s + 1) * LANES] = (rope(head_norm(x, gq_ref[...])) * scale).astype(qa_ref.dtype)
    o += GQA_Q_W
    ka_ref[...] = rope(head_norm(hp[:, o:o + LANES], gk_ref[...])).astype(ka_ref.dtype)
    o += GQA_KV_W
    va_ref[...] = hp[:, o:o + LANES].astype(va_ref.dtype)
    o += GQA_KV_W
    for s in range(DIFF_QK_W // LANES):
        x = hp[:, o + s * LANES:o + (s + 1) * LANES]
        qb_ref[:, s * LANES:(s + 1) * LANES] = (rope(x) * scale).astype(qb_ref.dtype)
    o += DIFF_QK_W
    for s in range(DIFF_QK_W // LANES):
        x = hp[:, o + s * LANES:o + (s + 1) * LANES]
        kb_ref[:, s * LANES:(s + 1) * LANES] = rope(x).astype(kb_ref.dtype)
    o += DIFF_QK_W
    vb_ref[...] = hp[:, o:o + DIFF_V_W].astype(vb_ref.dtype)


def _attn_proj(h, mod_tab, g, w_in, gq, gk, cos, sin):
    b, nt, d = h.shape
    tm = ROW_TILE
    widths = (GQA_Q_W, GQA_KV_W, GQA_KV_W, DIFF_QK_W, DIFF_QK_W, DIFF_V_W)
    full = lambda shape: pl.BlockSpec(shape, lambda bb, i: (0,) * len(shape))
    return pl.pallas_call(
        _attn_proj_kernel,
        grid=(b, nt // tm),
        in_specs=[pl.BlockSpec((None, tm, d), lambda bb, i: (bb, i, 0)),
                  _mod_spec(d),
                  full((1, d)),
                  full(w_in.shape),
                  full((1, LANES)),
                  full((1, LANES)),
                  pl.BlockSpec((tm, LANES), lambda bb, i: (i, 0)),
                  pl.BlockSpec((tm, LANES), lambda bb, i: (i, 0))],
        out_specs=[pl.BlockSpec((None, tm, w), lambda bb, i: (bb, i, 0)) for w in widths],
        out_shape=[jax.ShapeDtypeStruct((b, nt, w), _MXU_DTYPE) for w in widths],
        compiler_params=_params("parallel", "parallel"),
        name="attn_proj",
    )(h, mod_tab, g, w_in, gq, gk, cos, sin)


def _softmax_pv(q, k, v):
    s = lax.dot_general(q, k, (((1,), (1,)), ((), ())), preferred_element_type=F32)
    e = jnp.exp2(s - jnp.max(s, axis=-1, keepdims=True))
    l = jnp.sum(e, axis=-1, keepdims=True)
    return jnp.dot(e.astype(v.dtype), v, preferred_element_type=F32) / l


def _attn_kernel(qa_ref, qb_ref, ka_ref, va_ref, kb_ref, vb_ref, lam_ref, sg_ref, wo_ref, h_ref, mod_ref,
                 o_ref, mrg_ref, *, lambda_init, ctx_len):
    lv = lam_ref[...]
    lam = (jnp.exp(jnp.sum(lv[0:1] * lv[1:2], axis=-1, keepdims=True))
           - jnp.exp(jnp.sum(lv[2:3] * lv[3:4], axis=-1, keepdims=True)) + lambda_init)

    def run(nk):
        for h in range(GQA_Q_HEADS):
            g = h // GQA_GROUP
            o = _softmax_pv(qa_ref[:, h * HEAD_DIM:(h + 1) * HEAD_DIM],
                            ka_ref[0:nk, g * HEAD_DIM:(g + 1) * HEAD_DIM],
                            va_ref[0:nk, g * HEAD_DIM:(g + 1) * HEAD_DIM])
            mrg_ref[:, h * HEAD_DIM:(h + 1) * HEAD_DIM] = o.astype(mrg_ref.dtype)
        for h in range(DIFF_HEADS):
            c0 = h * 2 * HEAD_DIM
            v = vb_ref[0:nk, c0:c0 + 2 * HEAD_DIM]
            o1 = _softmax_pv(qb_ref[:, c0:c0 + HEAD_DIM], kb_ref[0:nk, c0:c0 + HEAD_DIM], v)
            o2 = _softmax_pv(qb_ref[:, c0 + HEAD_DIM:c0 + 2 * HEAD_DIM],
                             kb_ref[0:nk, c0 + HEAD_DIM:c0 + 2 * HEAD_DIM], v)
            o = o1 - lam * o2
            o = o * lax.rsqrt(jnp.mean(o * o, axis=-1, keepdims=True) + RMS_EPS) * sg_ref[...]
            o = o * (1.0 - lambda_init)
            mrg_ref[:, GQA_Q_W + c0:GQA_Q_W + c0 + 2 * HEAD_DIM] = o.astype(mrg_ref.dtype)

    i = pl.program_id(1)

    @pl.when(i == 0)
    def _():
        run(ctx_len)

    @pl.when(i > 0)
    def _():
        run(ka_ref.shape[0])

    y = jnp.dot(mrg_ref[...], wo_ref[...], preferred_element_type=F32)
    o_ref[...] = h_ref[...] + mod_ref[2:3, :] * y


def _attention(qkv, lam_rows, subln_g, w_out, h, mod_tab, lambda_init, ctx_len):
    qa, ka, va, qb, kb, vb = qkv
    b, nt, d = h.shape
    tq = ROW_TILE
    assert ctx_len == tq
    blk = lambda w: pl.BlockSpec((None, tq, w), lambda bb, i: (bb, i, 0))
    per_batch = lambda w: pl.BlockSpec((None, nt, w), lambda bb, i: (bb, 0, 0))
    full = lambda shape: pl.BlockSpec(shape, lambda bb, i: (0,) * len(shape))
    return pl.pallas_call(
        functools.partial(_attn_kernel, lambda_init=lambda_init, ctx_len=ctx_len),
        grid=(b, nt // tq),
        in_specs=[blk(GQA_Q_W), blk(DIFF_QK_W), per_batch(GQA_KV_W), per_batch(GQA_KV_W),
                  per_batch(DIFF_QK_W), per_batch(DIFF_V_W),
                  full((8, LANES)), full((1, LANES)), full(w_out.shape), blk(d), _mod_spec(d)],
        out_specs=blk(d),
        out_shape=jax.ShapeDtypeStruct((b, nt, d), F32),
        scratch_shapes=[pltpu.VMEM((tq, GQA_Q_W + DIFF_V_W), _MXU_DTYPE)],
        compiler_params=_params("parallel", "parallel"),
        name="attention",
    )(qa, qb, ka, va, kb, vb, lam_rows, subln_g, w_out, h, mod_tab)


def _norm1_kernel(h_ref, mod_ref, g_ref, o_ref):
    o_ref[...] = _norm_mod(h_ref[...], g_ref[...], mod_ref[0:1, :], mod_ref[1:2, :])


def _norm1(h, mod_tab, g):
    b, nt, d = h.shape
    tm = ROW_TILE
    blk = pl.BlockSpec((None, tm, d), lambda bb, i: (bb, i, 0))
    return pl.pallas_call(
        _norm1_kernel,
        grid=(b, nt // tm),
        in_specs=[blk, _mod_spec(d), pl.BlockSpec((1, d), lambda bb, i: (0, 0))],
        out_specs=blk,
        out_shape=jax.ShapeDtypeStruct((b, nt, d), F32),
        compiler_params=_params("parallel", "parallel"),
        name="ssm_norm",
    )(h, mod_tab, g)


def _ssm_kernel(u_ref, win_ref, m_ref, wout_ref, lam_ref, y_ref, bd_ref, *, chunk, n_ctx_chunks):
    nb, nt, _ = u_ref.shape
    nc = nt // chunk
    n_state_slabs = bd_ref.shape[0]
    q = n_state_slabs // 4

    def chunk_rows(bi):
        parts = [u_ref[bi, pl.ds(s, nc, stride=chunk), :] for s in range(chunk)]
        return jnp.concatenate(parts, axis=1).astype(_MXU_DTYPE)

    for bi in range(nb):
        drive = jnp.dot(chunk_rows(bi), win_ref[...], preferred_element_type=F32)
        for c in range(n_state_slabs):
            bd_ref[c, pl.ds(bi, nc, stride=nb), :] = drive[:, c * LANES:(c + 1) * LANES]

    lam = lam_ref[...]

    def make_step(base):
        a_re = [lam[:, (base + c) * LANES:(base + c + 1) * LANES] for c in range(q)]
        a_im = [lam[:, (base + q + c) * LANES:(base + q + c + 1) * LANES] for c in range(q)]

        def step(k, carry):
            row = pl.multiple_of(k * nb, nb)
            out = []
            for c in range(q):
                s_re, s_im = carry[2 * c], carry[2 * c + 1]
                d_re = bd_ref[base + c, pl.ds(row, nb), :]
                d_im = bd_ref[base + q + c, pl.ds(row, nb), :]
                bd_ref[base + c, pl.ds(row, nb), :] = s_re
                bd_ref[base + q + c, pl.ds(row, nb), :] = s_im
                out.append(a_re[c] * s_re - a_im[c] * s_im + d_re)
                out.append(a_re[c] * s_im + a_im[c] * s_re + d_im)
            return tuple(out)

        return step

    zero = tuple(jnp.zeros((nb, LANES), F32) for _ in range(2 * q))
    fwd = make_step(0)
    lax.fori_loop(0, nc, fwd, zero)
    rev = make_step(2 * q)
    carry = lax.fori_loop(0, n_ctx_chunks, lambda i, cr: rev(n_ctx_chunks - 1 - i, cr), zero)
    lax.fori_loop(0, nc - n_ctx_chunks, lambda i, cr: rev(nc - 1 - i, cr), carry)

    for bi in range(nb):
        states = jnp.concatenate([bd_ref[c, pl.ds(bi, nc, stride=nb), :] for c in range(n_state_slabs)], axis=1)
        y = (jnp.dot(chunk_rows(bi), m_ref[...], preferred_element_type=F32)
             + jnp.dot(states.astype(_MXU_DTYPE), wout_ref[...], preferred_element_type=F32))
        for t in range(chunk):
            y_ref[bi, pl.ds(t, nc, stride=chunk), :] = y[:, t * LANES:(t + 1) * LANES]


def _ssm_scan(u, win, m, wout, lam_t, ctx_len):
    b, nt, d = u.shape
    chunk = SSM_CHUNK
    nb = 4 if b % 4 == 0 else b
    n_slabs = d // LANES
    state_w = win.shape[-1]
    nc = nt // chunk
    blk = pl.BlockSpec((nb, nt, LANES), lambda j, bb: (bb, 0, j))
    return pl.pallas_call(
        functools.partial(_ssm_kernel, chunk=chunk, n_ctx_chunks=ctx_len // chunk),
        grid=(n_slabs, b // nb),
        in_specs=[blk,
                  pl.BlockSpec((None,) + win.shape[1:], lambda j, bb: (j, 0, 0)),
                  pl.BlockSpec((None,) + m.shape[1:], lambda j, bb: (j, 0, 0)),
                  pl.BlockSpec((None,) + wout.shape[1:], lambda j, bb: (j, 0, 0)),
                  pl.BlockSpec((None, 1, state_w), lambda j, bb: (j, 0, 0))],
        out_specs=blk,
        out_shape=jax.ShapeDtypeStruct((b, nt, d), F32),
        scratch_shapes=[pltpu.VMEM((state_w // LANES, nc * nb, LANES), F32)],
        compiler_params=_params("parallel", "parallel"),
        name="ssm_scan",
    )(u, win, m, wout, lam_t)


def _ssm_out_kernel(y_ref, u_ref, d_ref, wa_ref, wb_ref, h_ref, mod_ref, o_ref):
    x = y_ref[...] + d_ref[...] * u_ref[...]
    z = 0.5 * x * (1.0 + jnp.tanh(math.sqrt(2.0 / math.pi) * (x + 0.044715 * (x * x * x))))
    z = z.astype(_MXU_DTYPE)
    a = jnp.dot(z, wa_ref[...], preferred_element_type=F32)
    g = jnp.dot(z, wb_ref[...], preferred_element_type=F32)
    o_ref[...] = h_ref[...] + mod_ref[2:3, :] * (a / (1.0 + jnp.exp(-g)))


def _ssm_out(y, u, d_skip, wa, wb, h, mod_tab):
    b, nt, d = h.shape
    tm = ROW_TILE
    blk = pl.BlockSpec((None, tm, d), lambda bb, i: (bb, i, 0))
    full = lambda shape: pl.BlockSpec(shape, lambda bb, i: (0,) * len(shape))
    return pl.pallas_call(
        _ssm_out_kernel,
        grid=(b, nt // tm),
        in_specs=[blk, blk, full((1, d)), full(wa.shape), full(wb.shape), blk, _mod_spec(d)],
        out_specs=blk,
        out_shape=jax.ShapeDtypeStruct((b, nt, d), F32),
        compiler_params=_params("parallel", "parallel"),
        name="ssm_out",
    )(y, u, d_skip, wa, wb, h, mod_tab)


def _ssm_tables(a_re, a_im, log_dt, b_re, b_im, c_re, c_im, chunk):
    g_total = a_re.shape[1]
    gpt = LANES // SSM_GROUP_CH
    n_slabs = g_total // gpt
    p = SSM_STATE
    lre = jnp.minimum(a_re, -1e-4)
    lim = a_im
    dt = jnp.exp(log_dt)[..., None]
    steps = jnp.arange(chunk + 1, dtype=F32)[:, None, None, None]
    mag = jnp.exp(steps * (lre * dt))
    pw_re = mag * jnp.cos(steps * (lim * dt))
    pw_im = mag * jnp.sin(steps * (lim * dt))
    nr = pw_re[1] - 1.0
    ni = pw_im[1]
    den = lre * lre + lim * lim
    coef_re = (nr * lre + ni * lim) / den
    coef_im = (ni * lre - nr * lim) / den
    bb_re = coef_re[..., None] * b_re - coef_im[..., None] * b_im
    bb_im = coef_re[..., None] * b_im + coef_im[..., None] * b_re

    ein = functools.partial(jnp.einsum, precision=_HIGHEST)
    cp_re = ein('jxgp,xgcp->jxgcp', pw_re, c_re) - ein('jxgp,xgcp->jxgcp', pw_im, c_im)
    cp_im = ein('jxgp,xgcp->jxgcp', pw_re, c_im) + ein('jxgp,xgcp->jxgcp', pw_im, c_re)
    taps = ein('jxgcp,xgpd->jxgcd', cp_re, bb_re) - ein('jxgcp,xgpd->jxgcd', cp_im, bb_im)
    def tap(s, t):
        if t > s:
            return taps[t - s, 0]
        if t < s:
            return taps[s - t, 1]
        return taps[0, 0] + taps[0, 1]

    kst = jnp.stack([jnp.stack([tap(s, t) for t in range(chunk)], axis=0) for s in range(chunk)], axis=0)
    def block_diag(x, a_axis, b_axis):
        x = jnp.expand_dims(x, b_axis)
        shape_a = [1] * x.ndim
        shape_a[a_axis] = gpt
        shape_b = [1] * x.ndim
        shape_b[b_axis] = gpt
        on_diag = jnp.arange(gpt).reshape(shape_a) == jnp.arange(gpt).reshape(shape_b)
        return jnp.where(on_diag, x, 0.0).astype(_MXU_DTYPE)

    kst = kst.reshape(chunk, chunk, n_slabs, gpt, SSM_GROUP_CH, SSM_GROUP_CH)
    kst = jnp.transpose(kst, (2, 0, 3, 5, 1, 4))
    m = block_diag(kst, 2, 5).reshape(n_slabs, chunk * LANES, chunk * LANES)

    def drive(pw_r, pw_i, br, bi):
        re = ein('sgp,gpd->sgdp', pw_r, br) - ein('sgp,gpd->sgdp', pw_i, bi)
        im = ein('sgp,gpd->sgdp', pw_r, bi) + ein('sgp,gpd->sgdp', pw_i, br)
        return re, im

    f_re, f_im = drive(pw_re[:chunk, 0][::-1], pw_im[:chunk, 0][::-1], bb_re[0], bb_im[0])
    r_re, r_im = drive(pw_re[:chunk, 1], pw_im[:chunk, 1], bb_re[1], bb_im[1])
    win = jnp.stack([f_re, f_im, r_re, r_im], axis=0)
    win = win.reshape(4, chunk, n_slabs, gpt, SSM_GROUP_CH, p)
    win = jnp.transpose(win, (2, 1, 3, 4, 0, 5))
    win = block_diag(win, 2, 5).reshape(n_slabs, chunk * LANES, 4 * gpt * p)

    of_re, of_im = cp_re[1:, 0], cp_im[1:, 0]
    or_re, or_im = cp_re[1:, 1][::-1], cp_im[1:, 1][::-1]
    wout = jnp.stack([of_re, -of_im, or_re, -or_im], axis=0)
    wout = wout.reshape(4, chunk, n_slabs, gpt, SSM_GROUP_CH, p)
    wout = jnp.transpose(wout, (2, 0, 3, 5, 1, 4))
    wout = block_diag(wout, 2, 5).reshape(n_slabs, 4 * gpt * p, chunk * LANES)

    lam_t = jnp.stack([pw_re[chunk, 0], pw_im[chunk, 0], pw_re[chunk, 1], pw_im[chunk, 1]], axis=0)
    lam_t = lam_t.reshape(4, n_slabs, gpt * p).transpose(1, 0, 2).reshape(n_slabs, 1, 4 * gpt * p)
    return win.astype(_MXU_DTYPE), m.astype(_MXU_DTYPE), wout.astype(_MXU_DTYPE), lam_t


def _router_kernel(h_ref, mod_ref, g_ref, wr_ref, br_ref, xt_ref, cmb_ref):
    xt = _norm_mod(h_ref[...], g_ref[...], mod_ref[3:4, :], mod_ref[4:5, :])
    xt_ref[...] = xt.astype(xt_ref.dtype)
    logits = jnp.dot(xt, wr_ref[...], preferred_element_type=F32, precision=_HIGHEST) + br_ref[...]
    lane = lax.broadcasted_iota(jnp.int32, (1, LANES), 1)
    lane_f = lane.astype(F32)
    neg = -jnp.inf
    big = 1e9
    gmask = (lane >= MOE_EXPERTS) & (lane < MOE_EXPERTS + MOE_GROUPS)
    gl = jnp.where(gmask, logits, neg)
    gmax = jnp.max(gl, axis=-1, keepdims=True)
    gidx = jnp.min(jnp.where(gl == gmax, lane_f, big), axis=-1, keepdims=True) - MOE_EXPERTS
    p_group = 1.0 / jnp.sum(jnp.where(gmask, jnp.exp(gl - gmax), 0.0), axis=-1, keepdims=True)
    in_group = (lane < MOE_EXPERTS) & ((lane // MOE_EPG).astype(F32) == gidx)
    el = jnp.where(in_group, logits, neg)
    v1 = jnp.max(el, axis=-1, keepdims=True)
    i1 = jnp.min(jnp.where(el == v1, lane_f, big), axis=-1, keepdims=True)
    el2 = jnp.where(lane_f == i1, neg, el)
    v2 = jnp.max(el2, axis=-1, keepdims=True)
    i2 = jnp.min(jnp.where(el2 == v2, lane_f, big), axis=-1, keepdims=True)
    t = jnp.exp(v2 - v1)
    w1 = p_group / (1.0 + t)
    w2 = p_group * t / (1.0 + t)
    cmb_ref[...] = jnp.where(lane_f == i1, w1, 0.0) + jnp.where(lane_f == i2, w2, 0.0)


def _router(h, mod_tab, g, wr, br):
    b, nt, d = h.shape
    tm = ROW_TILE
    full = lambda shape: pl.BlockSpec(shape, lambda bb, i: (0,) * len(shape))
    return pl.pallas_call(
        _router_kernel,
        grid=(b, nt // tm),
        in_specs=[pl.BlockSpec((None, tm, d), lambda bb, i: (bb, i, 0)), _mod_spec(d),
                  full((1, d)), full(wr.shape), full(br.shape)],
        out_specs=[pl.BlockSpec((None, tm, d), lambda bb, i: (bb, i, 0)),
                   pl.BlockSpec((None, tm, LANES), lambda bb, i: (bb, i, 0))],
        out_shape=[jax.ShapeDtypeStruct((b, nt, d), _MXU_DTYPE),
                   jax.ShapeDtypeStruct((b, nt, LANES), F32)],
        compiler_params=_params("parallel", "parallel"),
        name="moe_router",
    )(h, mod_tab, g, wr, br)


def _experts_kernel(xt_ref, cmb_ref, wg_ref, wu_ref, wd_ref, h_ref, mod_ref, o_ref, acc_ref, *, ctx_len, nt):
    i = pl.program_id(0)
    e = pl.program_id(1)
    tm = xt_ref.shape[0]

    @pl.when(e == 0)
    def _():
        acc_ref[...] = jnp.zeros_like(acc_ref)

    x = xt_ref[...]
    gate = jnp.dot(x, wg_ref[...].astype(_MXU_DTYPE), preferred_element_type=F32)
    up = jnp.dot(x, wu_ref[...].astype(_MXU_DTYPE), preferred_element_type=F32)
    lane = lax.broadcasted_iota(jnp.int32, (1, LANES), 1)
    w = jnp.sum(jnp.where(lane == e, cmb_ref[...], 0.0), axis=-1, keepdims=True)
    hid = (gate / (1.0 + jnp.exp(-gate))) * up * w
    acc_ref[...] += jnp.dot(hid.astype(_MXU_DTYPE), wd_ref[...].astype(_MXU_DTYPE), preferred_element_type=F32)

    @pl.when(e == pl.num_programs(1) - 1)
    def _():
        row = (i * tm) % nt + lax.broadcasted_iota(jnp.int32, (tm, 1), 0)
        gate_row = jnp.where(row < ctx_len, mod_ref[0, 5:6, :], mod_ref[1, 5:6, :])
        o_ref[...] = h_ref[...] + gate_row * acc_ref[...]


def _experts(xt, cmb, wg, wu, wd, h, mod_tab, ctx_len):
    b, nt, d = h.shape
    tm = 1152 if nt % 1152 == 0 else ROW_TILE
    per_b = nt // tm
    n_exp, _, hid = wg.shape
    rows = b * nt
    blk = lambda w: pl.BlockSpec((tm, w), lambda i, e: (i, 0))
    out = pl.pallas_call(
        functools.partial(_experts_kernel, ctx_len=ctx_len, nt=nt),
        grid=(rows // tm, n_exp),
        in_specs=[blk(d), blk(LANES),
                  pl.BlockSpec((None, d, hid), lambda i, e: (e, 0, 0)),
                  pl.BlockSpec((None, d, hid), lambda i, e: (e, 0, 0)),
                  pl.BlockSpec((None, hid, d), lambda i, e: (e, 0, 0)),
                  blk(d),
                  pl.BlockSpec((None, 2, 8, d), lambda i, e: (i // per_b, 0, 0, 0))],
        out_specs=blk(d),
        out_shape=jax.ShapeDtypeStruct((rows, d), F32),
        scratch_shapes=[pltpu.VMEM((tm, d), F32)],
        compiler_params=_params("parallel", "arbitrary"),
        name="moe_experts",
    )(xt.reshape(rows, d), cmb.reshape(rows, LANES), wg, wu, wd, h.reshape(rows, d), mod_tab)
    return out.reshape(b, nt, d)


def _final_kernel(h_ref, g_ref, o_ref):
    h = h_ref[...]
    o_ref[...] = h * lax.rsqrt(jnp.mean(h * h, axis=-1, keepdims=True) + RMS_EPS) * g_ref[...]


def _final_norm(h, g, ctx_len):
    b, nt, d = h.shape
    tm = ROW_TILE
    skip = ctx_len // tm
    return pl.pallas_call(
        _final_kernel,
        grid=(b, (nt - ctx_len) // tm),
        in_specs=[pl.BlockSpec((None, tm, d), lambda bb, i: (bb, i + skip, 0)),
                  pl.BlockSpec((1, d), lambda bb, i: (0, 0))],
        out_specs=pl.BlockSpec((None, tm, d), lambda bb, i: (bb, i, 0)),
        out_shape=jax.ShapeDtypeStruct((b, nt - ctx_len, d), F32),
        compiler_params=_params("parallel", "parallel"),
        name="final_norm",
    )(h, g)


def _rope_tables(seq_len, ctx_len):
    n_rows = seq_len // GRID_W
    rows = jnp.repeat(jnp.arange(n_rows, dtype=F32), GRID_W)
    cols = jnp.tile(jnp.arange(GRID_W, dtype=F32), n_rows)
    half = HEAD_DIM // 2
    inv = 1.0 / (ROPE_BASE ** (jnp.arange(0, half, 2, dtype=F32) / half))
    ang_r = rows[:, None] * inv
    ang_c = cols[:, None] * inv
    ang = jnp.concatenate([ang_r, ang_r, ang_c, ang_c], axis=-1)
    ang = jnp.concatenate([jnp.zeros((ctx_len, HEAD_DIM), F32), ang], axis=0)
    ang = jnp.tile(ang, (1, LANES // HEAD_DIM))
    return jnp.cos(ang), jnp.sin(ang)


def _pad_row(v, width=LANES):
    return jnp.pad(v, (0, width - v.shape[0]))[None, :]


def kernel(x, c, ctx, c_ctx, mod_w, mod_b, norm1_g, norm2_g, final_g, attn_w_in, attn_w_out, attn_q_norm_g, attn_k_norm_g, diff_lambda_q1, diff_lambda_k1, diff_lambda_q2, diff_lambda_k2, diff_subln_g, ssm_a_re, ssm_a_im, ssm_log_dt, ssm_b_re, ssm_b_im, ssm_c_re, ssm_c_im, ssm_d, ssm_glu_w_a, ssm_glu_w_b, moe_group_w, moe_group_b, moe_router_w, moe_router_b, moe_w_gate, moe_w_up, moe_w_down):
    bsz, seq, d = x.shape
    ctx_len = ctx.shape[1]
    depth = mod_w.shape[0]
    assert ctx_len == ROW_TILE and seq % ROW_TILE == 0 and seq % GRID_W == 0

    h = jnp.concatenate([ctx, x], axis=1)

    mod_rows = 16
    c_all = jnp.concatenate([c, c_ctx[None, :], jnp.zeros((mod_rows - bsz - 1, d), F32)], axis=0)
    mods = _modulation(c_all, mod_w, mod_b).reshape(depth, mod_rows, 6, d)
    mods = jnp.pad(mods, ((0, 0), (0, 0), (0, 2), (0, 0)))
    mod_tabs = jnp.stack([jnp.broadcast_to(mods[:, bsz:bsz + 1], (depth, bsz, 8, d)), mods[:, :bsz]], axis=2)

    cos, sin = _rope_tables(seq, ctx_len)
    cast = lambda w: w.astype(_MXU_DTYPE)

    for layer in range(depth):
        mod_tab = mod_tabs[layer]
        i = layer // 2
        if layer % 2 == 0:
            lambda_init = 0.8 - 0.6 * math.exp(-0.3 * layer)
            qkv = _attn_proj(h, mod_tab, norm1_g[layer][None, :], cast(attn_w_in[i]),
                             jnp.tile(attn_q_norm_g[i], 2)[None, :], jnp.tile(attn_k_norm_g[i], 2)[None, :],
                             cos, sin)
            lam_rows = jnp.concatenate([_pad_row(diff_lambda_q1[i]), _pad_row(diff_lambda_k1[i]),
                                        _pad_row(diff_lambda_q2[i]), _pad_row(diff_lambda_k2[i]),
                                        jnp.zeros((4, LANES), F32)], axis=0)
            h = _attention(qkv, lam_rows, diff_subln_g[i][None, :], cast(attn_w_out[i]), h, mod_tab,
                           lambda_init, ctx_len)
        else:
            u = _norm1(h, mod_tab, norm1_g[layer][None, :])
            win, m, wout, lam_t = _ssm_tables(ssm_a_re[i], ssm_a_im[i], ssm_log_dt[i], ssm_b_re[i], ssm_b_im[i],
                                              ssm_c_re[i], ssm_c_im[i], SSM_CHUNK)
            y = _ssm_scan(u, win, m, wout, lam_t, ctx_len)
            h = _ssm_out(y, u, ssm_d[i][None, :], cast(ssm_glu_w_a[i]), cast(ssm_glu_w_b[i]), h, mod_tab)

        wr = jnp.concatenate([jnp.transpose(moe_router_w[layer], (1, 0, 2)).reshape(d, MOE_EXPERTS),
                              moe_group_w[layer],
                              jnp.zeros((d, LANES - MOE_EXPERTS - MOE_GROUPS), F32)], axis=1)
        br = _pad_row(jnp.concatenate([moe_router_b[layer].reshape(-1), moe_group_b[layer]]))
        xt, cmb = _router(h, mod_tab, norm2_g[layer][None, :], wr, br)
        h = _experts(xt, cmb, moe_w_gate[layer], moe_w_up[layer], moe_w_down[layer], h, mod_tab, ctx_len)

    return _final_norm(h, final_g[None, :], ctx_len)
```

```python
import functools
import math

import jax
import jax.numpy as jnp
from jax import lax
from jax.experimental import pallas as pl
from jax.experimental.pallas import tpu as pltpu

F32 = jnp.float32
_MXU_DTYPE = jnp.bfloat16
_HIGHEST = lax.Precision.HIGHEST

LANES = 128
HEAD_DIM = 64
GRID_W = 64
ROPE_BASE = 10000.0
GQA_Q_HEADS = 8
GQA_GROUP = 4
DIFF_HEADS = 4
GQA_Q_W = 512
GQA_KV_W = 128
DIFF_QK_W = 512
DIFF_V_W = 512
SSM_GROUP_CH = 16
SSM_STATE = 64
MOE_GROUPS = 4
MOE_EPG = 8
MOE_EXPERTS = 32
RMS_EPS = 1e-6
SSM_CHUNK = 8
ROW_TILE = 256
VMEM_LIMIT = 56 * 1024 * 1024


def _params(*sem):
    return pltpu.CompilerParams(dimension_semantics=sem, vmem_limit_bytes=VMEM_LIMIT)


def _norm_mod(h, g, shift, scale):
    y = h * lax.rsqrt(jnp.mean(h * h, axis=-1, keepdims=True) + RMS_EPS) * g
    return y * (1.0 + scale) + shift


def _mm(a, b):
    return jnp.dot(a.astype(_MXU_DTYPE), b.astype(_MXU_DTYPE), preferred_element_type=F32)


def _mod_kernel(c_ref, w_ref, b_ref, o_ref):
    c = c_ref[...]
    a = c / (1.0 + jnp.exp(-c))
    o_ref[...] = jnp.dot(a, w_ref[...], preferred_element_type=F32, precision=_HIGHEST) + b_ref[...]


def _modulation(c_all, mod_w, mod_b):
    depth, d, n = mod_w.shape
    rows = c_all.shape[0]
    tn = 1536
    return pl.pallas_call(
        _mod_kernel,
        grid=(depth, n // tn),
        in_specs=[pl.BlockSpec((rows, d), lambda l, j: (0, 0)),
                  pl.BlockSpec((None, d, tn), lambda l, j: (l, 0, j)),
                  pl.BlockSpec((None, 1, tn), lambda l, j: (l, 0, j))],
        out_specs=pl.BlockSpec((None, rows, tn), lambda l, j: (l, 0, j)),
        out_shape=jax.ShapeDtypeStruct((depth, rows, n), F32),
        compiler_params=_params("parallel", "parallel"),
        name="modulation",
    )(c_all, mod_w, mod_b.reshape(depth, 1, n))


def _mod_spec(d):
    return pl.BlockSpec((None, None, 8, d), lambda b, i: (b, jnp.minimum(i, 1), 0, 0))


def _attn_proj_kernel(h_ref, mod_ref, g_ref, w_ref, gq_ref, gk_ref, cos_ref, sin_ref,
                      qa_ref, ka_ref, va_ref, qb_ref, kb_ref, vb_ref):
    xn = _norm_mod(h_ref[...], g_ref[...], mod_ref[0:1, :], mod_ref[1:2, :])
    hp = _mm(xn, w_ref[...])
    cos = cos_ref[...]
    sin = sin_ref[...]
    lane = lax.broadcasted_iota(jnp.int32, (1, LANES), 1)
    first_half = (lane % 32) < 16
    r = lax.broadcasted_iota(jnp.int32, (LANES, LANES), 0) // HEAD_DIM
    c = lax.broadcasted_iota(jnp.int32, (LANES, LANES), 1) // HEAD_DIM
    same_head = (r == c).astype(_MXU_DTYPE)

    def rope(x):
        rot = jnp.where(first_half, -pltpu.roll(x, LANES - 16, 1), pltpu.roll(x, 16, 1))
        return x * cos + rot * sin

    def head_norm(x, g):
        ss = jnp.dot((x * x).astype(_MXU_DTYPE), same_head, preferred_element_type=F32)
        return x * lax.rsqrt(ss * (1.0 / HEAD_DIM) + RMS_EPS) * g

    scale = HEAD_DIM ** -0.5 * math.log2(math.e)
    o = 0
    for s in range(GQA_Q_W // LANES):
        x = hp[:, o + s * LANES:o + (s + 1) * LANES]
        qa_ref[:, s * LANES:(s + 1) * LANES] = (rope(head_norm(x, gq_ref[...])) * scale).astype(qa_ref.dtype)
    o += GQA_Q_W
    ka_ref[...] = rope(head_norm(hp[:, o:o + LANES], gk_ref[...])).astype(ka_ref.dtype)
    o += GQA_KV_W
    va_ref[...] = hp[:, o:o + LANES].astype(va_ref.dtype)
    o += GQA_KV_W
    for s in range(DIFF_QK_W // LANES):
        x = hp[:, o + s * LANES:o + (s + 1) * LANES]
        qb_ref[:, s * LANES:(s + 1) * LANES] = (rope(x) * scale).astype(qb_ref.dtype)
    o += DIFF_QK_W
    for s in range(DIFF_QK_W // LANES):
        x = hp[:, o + s * LANES:o + (s + 1) * LANES]
        kb_ref[:, s * LANES:(s + 1) * LANES] = rope(x).astype(kb_ref.dtype)
    o += DIFF_QK_W
    vb_ref[...] = hp[:, o:o + DIFF_V_W].astype(vb_ref.dtype)


def _attn_proj(h, mod_tab, g, w_in, gq, gk, cos, sin):
    b, nt, d = h.shape
    tm = ROW_TILE
    widths = (GQA_Q_W, GQA_KV_W, GQA_KV_W, DIFF_QK_W, DIFF_QK_W, DIFF_V_W)
    full = lambda shape: pl.BlockSpec(shape, lambda bb, i: (0,) * len(shape))
    return pl.pallas_call(
        _attn_proj_kernel,
        grid=(b, nt // tm),
        in_specs=[pl.BlockSpec((None, tm, d), lambda bb, i: (bb, i, 0)),
                  _mod_spec(d),
                  full((1, d)),
                  full(w_in.shape),
                  full((1, LANES)),
                  full((1, LANES)),
                  pl.BlockSpec((tm, LANES), lambda bb, i: (i, 0)),
                  pl.BlockSpec((tm, LANES), lambda bb, i: (i, 0))],
        out_specs=[pl.BlockSpec((None, tm, w), lambda bb, i: (bb, i, 0)) for w in widths],
        out_shape=[jax.ShapeDtypeStruct((b, nt, w), _MXU_DTYPE) for w in widths],
        compiler_params=_params("parallel", "parallel"),
        name="attn_proj",
    )(h, mod_tab, g, w_in, gq, gk, cos, sin)


def _softmax_pv(q, k, v):
    s = lax.dot_general(q, k, (((1,), (1,)), ((), ())), preferred_element_type=F32)
    e = jnp.exp2(s - jnp.max(s, axis=-1, keepdims=True))
    l = jnp.sum(e, axis=-1, keepdims=True)
    return jnp.dot(e.astype(v.dtype), v, preferred_element_type=F32) / l


def _attn_kernel(qa_ref, qb_ref, ka_ref, va_ref, kb_ref, vb_ref, lam_ref, sg_ref, wo_ref, h_ref, mod_ref,
                 o_ref, mrg_ref, *, lambda_init, ctx_len):
    lv = lam_ref[...]
    lam = (jnp.exp(jnp.sum(lv[0:1] * lv[1:2], axis=-1, keepdims=True))
           - jnp.exp(jnp.sum(lv[2:3] * lv[3:4], axis=-1, keepdims=True)) + lambda_init)

    def run(nk):
        for h in range(GQA_Q_HEADS):
            g = h // GQA_GROUP
            o = _softmax_pv(qa_ref[:, h * HEAD_DIM:(h + 1) * HEAD_DIM],
                            ka_ref[0:nk, g * HEAD_DIM:(g + 1) * HEAD_DIM],
                            va_ref[0:nk, g * HEAD_DIM:(g + 1) * HEAD_DIM])
            mrg_ref[:, h * HEAD_DIM:(h + 1) * HEAD_DIM] = o.astype(mrg_ref.dtype)
        for h in range(DIFF_HEADS):
            c0 = h * 2 * HEAD_DIM
            v = vb_ref[0:nk, c0:c0 + 2 * HEAD_DIM]
            o1 = _softmax_pv(qb_ref[:, c0:c0 + HEAD_DIM], kb_ref[0:nk, c0:c0 + HEAD_DIM], v)
            o2 = _softmax_pv(qb_ref[:, c0 + HEAD_DIM:c0 + 2 * HEAD_DIM],
                             kb_ref[0:nk, c0 + HEAD_DIM:c0 + 2 * HEAD_DIM], v)
            o = o1 - lam * o2
            o = o * lax.rsqrt(jnp.mean(o * o, axis=-1, keepdims=True) + RMS_EPS) * sg_ref[...]
            o = o * (1.0 - lambda_init)
            mrg_ref[:, GQA_Q_W + c0:GQA_Q_W + c0 + 2 * HEAD_DIM] = o.astype(mrg_ref.dtype)

    i = pl.program_id(1)

    @pl.when(i == 0)
    def _():
        run(ctx_len)

    @pl.when(i > 0)
    def _():
        run(ka_ref.shape[0])

    y = jnp.dot(mrg_ref[...], wo_ref[...], preferred_element_type=F32)
    o_ref[...] = h_ref[...] + mod_ref[2:3, :] * y


def _attention(qkv, lam_rows, subln_g, w_out, h, mod_tab, lambda_init, ctx_len):
    qa, ka, va, qb, kb, vb = qkv
    b, nt, d = h.shape
    tq = ROW_TILE
    assert ctx_len == tq
    blk = lambda w: pl.BlockSpec((None, tq, w), lambda bb, i: (bb, i, 0))
    per_batch = lambda w: pl.BlockSpec((None, nt, w), lambda bb, i: (bb, 0, 0))
    full = lambda shape: pl.BlockSpec(shape, lambda bb, i: (0,) * len(shape))
    return pl.pallas_call(
        functools.partial(_attn_kernel, lambda_init=lambda_init, ctx_len=ctx_len),
        grid=(b, nt // tq),
        in_specs=[blk(GQA_Q_W), blk(DIFF_QK_W), per_batch(GQA_KV_W), per_batch(GQA_KV_W),
                  per_batch(DIFF_QK_W), per_batch(DIFF_V_W),
                  full((8, LANES)), full((1, LANES)), full(w_out.shape), blk(d), _mod_spec(d)],
        out_specs=blk(d),
        out_shape=jax.ShapeDtypeStruct((b, nt, d), F32),
        scratch_shapes=[pltpu.VMEM((tq, GQA_Q_W + DIFF_V_W), _MXU_DTYPE)],
        compiler_params=_params("parallel", "parallel"),
        name="attention",
    )(qa, qb, ka, va, kb, vb, lam_rows, subln_g, w_out, h, mod_tab)


def _norm1_kernel(h_ref, mod_ref, g_ref, o_ref):
    o_ref[...] = _norm_mod(h_ref[...], g_ref[...], mod_ref[0:1, :], mod_ref[1:2, :])


def _norm1(h, mod_tab, g):
    b, nt, d = h.shape
    tm = ROW_TILE
    blk = pl.BlockSpec((None, tm, d), lambda bb, i: (bb, i, 0))
    return pl.pallas_call(
        _norm1_kernel,
        grid=(b, nt // tm),
        in_specs=[blk, _mod_spec(d), pl.BlockSpec((1, d), lambda bb, i: (0, 0))],
        out_specs=blk,
        out_shape=jax.ShapeDtypeStruct((b, nt, d), F32),
        compiler_params=_params("parallel", "parallel"),
        name="ssm_norm",
    )(h, mod_tab, g)


def _ssm_kernel(u_ref, win_ref, m_ref, wout_ref, lam_ref, y_ref, bd_ref, *, chunk, n_ctx_chunks):
    nb, nt, _ = u_ref.shape
    nc = nt // chunk
    n_state_slabs = bd_ref.shape[0]
    q = n_state_slabs // 4

    def chunk_rows(bi):
        parts = [u_ref[bi, pl.ds(s, nc, stride=chunk), :] for s in range(chunk)]
        return jnp.concatenate(parts, axis=1).astype(_MXU_DTYPE)

    for bi in range(nb):
        drive = jnp.dot(chunk_rows(bi), win_ref[...], preferred_element_type=F32)
        for c in range(n_state_slabs):
            bd_ref[c, pl.ds(bi, nc, stride=nb), :] = drive[:, c * LANES:(c + 1) * LANES]

    lam = lam_ref[...]

    def make_step(base):
        a_re = [lam[:, (base + c) * LANES:(base + c + 1) * LANES] for c in range(q)]
        a_im = [lam[:, (base + q + c) * LANES:(base + q + c + 1) * LANES] for c in range(q)]

        def step(k, carry):
            row = pl.multiple_of(k * nb, nb)
            out = []
            for c in range(q):
                s_re, s_im = carry[2 * c], carry[2 * c + 1]
                d_re = bd_ref[base + c, pl.ds(row, nb), :]
                d_im = bd_ref[base + q + c, pl.ds(row, nb), :]
                bd_ref[base + c, pl.ds(row, nb), :] = s_re
                bd_ref[base + q + c, pl.ds(row, nb), :] = s_im
                out.append(a_re[c] * s_re - a_im[c] * s_im + d_re)
                out.append(a_re[c] * s_im + a_im[c] * s_re + d_im)
            return tuple(out)

        return step

    zero = tuple(jnp.zeros((nb, LANES), F32) for _ in range(2 * q))
    fwd = make_step(0)
    lax.fori_loop(0, nc, fwd, zero)
    rev = make_step(2 * q)
    carry = lax.fori_loop(0, n_ctx_chunks, lambda i, cr: rev(n_ctx_chunks - 1 - i, cr), zero)
    lax.fori_loop(0, nc - n_ctx_chunks, lambda i, cr: rev(nc - 1 - i, cr), carry)

    for bi in range(nb):
        states = jnp.concatenate([bd_ref[c, pl.ds(bi, nc, stride=nb), :] for c in range(n_state_slabs)], axis=1)
        y = (jnp.dot(chunk_rows(bi), m_ref[...], preferred_element_type=F32)
             + jnp.dot(states.astype(_MXU_DTYPE), wout_ref[...], preferred_element_type=F32))
        for t in range(chunk):
            y_ref[bi, pl.ds(t, nc, stride=chunk), :] = y[:, t * LANES:(t + 1) * LANES]


def _ssm_scan(u, win, m, wout, lam_t, ctx_len, first_slab):
    b, nt, d = u.shape
    chunk = SSM_CHUNK
    nb = 4 if b % 4 == 0 else b
    n_slabs = d // LANES
    state_w = win.shape[-1]
    nc = nt // chunk
    blk = pl.BlockSpec((nb, nt, LANES), lambda j, bb: (bb, 0, j))
    table = lambda j, bb: (first_slab + j, 0, 0)
    return pl.pallas_call(
        functools.partial(_ssm_kernel, chunk=chunk, n_ctx_chunks=ctx_len // chunk),
        grid=(n_slabs, b // nb),
        in_specs=[blk,
                  pl.BlockSpec((None,) + win.shape[1:], table),
                  pl.BlockSpec((None,) + m.shape[1:], table),
                  pl.BlockSpec((None,) + wout.shape[1:], table),
                  pl.BlockSpec((None, 1, state_w), table)],
        out_specs=blk,
        out_shape=jax.ShapeDtypeStruct((b, nt, d), F32),
        scratch_shapes=[pltpu.VMEM((state_w // LANES, nc * nb, LANES), F32)],
        compiler_params=_params("parallel", "parallel"),
        name="ssm_scan",
    )(u, win, m, wout, lam_t)


def _ssm_out_kernel(y_ref, u_ref, d_ref, wa_ref, wb_ref, h_ref, mod_ref, o_ref):
    x = y_ref[...] + d_ref[...] * u_ref[...]
    z = 0.5 * x * (1.0 + jnp.tanh(math.sqrt(2.0 / math.pi) * (x + 0.044715 * (x * x * x))))
    z = z.astype(_MXU_DTYPE)
    a = jnp.dot(z, wa_ref[...], preferred_element_type=F32)
    g = jnp.dot(z, wb_ref[...], preferred_element_type=F32)
    o_ref[...] = h_ref[...] + mod_ref[2:3, :] * (a / (1.0 + jnp.exp(-g)))


def _ssm_out(y, u, d_skip, wa, wb, h, mod_tab):
    b, nt, d = h.shape
    tm = ROW_TILE
    blk = pl.BlockSpec((None, tm, d), lambda bb, i: (bb, i, 0))
    full = lambda shape: pl.BlockSpec(shape, lambda bb, i: (0,) * len(shape))
    return pl.pallas_call(
        _ssm_out_kernel,
        grid=(b, nt // tm),
        in_specs=[blk, blk, full((1, d)), full(wa.shape), full(wb.shape), blk, _mod_spec(d)],
        out_specs=blk,
        out_shape=jax.ShapeDtypeStruct((b, nt, d), F32),
        compiler_params=_params("parallel", "parallel"),
        name="ssm_out",
    )(y, u, d_skip, wa, wb, h, mod_tab)


def _expand_kernel(x_ref, o_ref, *, lo, row_div, gpt):
    rows, cc = x_ref.shape
    oc = o_ref.shape[1]
    ci = lax.broadcasted_iota(jnp.int32, (cc, oc), 0)
    oi = lax.broadcasted_iota(jnp.int32, (cc, oc), 1)
    pick = (ci == (oi // (gpt * lo)) * lo + oi % lo).astype(_MXU_DTYPE)
    y = jnp.dot(x_ref[...].astype(_MXU_DTYPE), pick, preferred_element_type=F32)
    ri = lax.broadcasted_iota(jnp.int32, (rows, oc), 0) + pl.program_id(1) * rows
    oj = lax.broadcasted_iota(jnp.int32, (rows, oc), 1)
    keep = ((ri // row_div) % gpt) == ((oj // lo) % gpt)
    o_ref[...] = jnp.where(keep, y, 0.0).astype(o_ref.dtype)


def _expand_block_diag(x, lo, row_div, gpt):
    n, rows, cc = x.shape
    tr = ROW_TILE
    return pl.pallas_call(
        functools.partial(_expand_kernel, lo=lo, row_div=row_div, gpt=gpt),
        grid=(n, rows // tr),
        in_specs=[pl.BlockSpec((None, tr, cc), lambda i, r: (i, r, 0))],
        out_specs=pl.BlockSpec((None, tr, gpt * cc), lambda i, r: (i, r, 0)),
        out_shape=jax.ShapeDtypeStruct((n, rows, gpt * cc), _MXU_DTYPE),
        compiler_params=_params("parallel", "parallel"),
        name="ssm_expand",
    )(x)


def _ssm_tables(a_re, a_im, log_dt, b_re, b_im, c_re, c_im, chunk):
    n_layers, _, g_total, p = a_re.shape
    gpt = LANES // SSM_GROUP_CH
    n_slabs = g_total // gpt
    n = n_layers * n_slabs
    gc = SSM_GROUP_CH
    lre = jnp.minimum(a_re, -1e-4)
    lim = a_im
    dt = jnp.exp(log_dt)[..., None]
    steps = jnp.arange(chunk + 1, dtype=F32)[:, None, None, None, None]
    mag = jnp.exp(steps * (lre * dt))
    pw_re = mag * jnp.cos(steps * (lim * dt))
    pw_im = mag * jnp.sin(steps * (lim * dt))
    nr = pw_re[1] - 1.0
    ni = pw_im[1]
    den = lre * lre + lim * lim
    coef_re = (nr * lre + ni * lim) / den
    coef_im = (ni * lre - nr * lim) / den
    bb_re = coef_re[..., None] * b_re - coef_im[..., None] * b_im
    bb_im = coef_re[..., None] * b_im + coef_im[..., None] * b_re

    ein = functools.partial(jnp.einsum, precision=_HIGHEST)
    cp_re = ein('jlxgp,lxgcp->jlxgcp', pw_re, c_re) - ein('jlxgp,lxgcp->jlxgcp', pw_im, c_im)
    cp_im = ein('jlxgp,lxgcp->jlxgcp', pw_re, c_im) + ein('jlxgp,lxgcp->jlxgcp', pw_im, c_re)
    taps = ein('jlxgcp,lxgpd->jlxgdc', cp_re, bb_re) - ein('jlxgcp,lxgpd->jlxgdc', cp_im, bb_im)

    def tap(s, t):
        if t > s:
            return taps[t - s, :, 0]
        if t < s:
            return taps[s - t, :, 1]
        return taps[0, :, 0] + taps[0, :, 1]

    kst = jnp.stack([jnp.stack([tap(s, t) for t in range(chunk)], axis=1) for s in range(chunk)], axis=1)
    kst = kst.reshape(n_layers, chunk, chunk, n_slabs, gpt, gc, gc)
    kst = jnp.transpose(kst, (0, 3, 1, 4, 5, 2, 6)).reshape(n, chunk * LANES, chunk * gc)
    m = _expand_block_diag(kst, gc, gc, gpt)

    def drive(pw_r, pw_i, br, bi):
        re = ein('slgp,lgpd->slgdp', pw_r, br) - ein('slgp,lgpd->slgdp', pw_i, bi)
        im = ein('slgp,lgpd->slgdp', pw_r, bi) + ein('slgp,lgpd->slgdp', pw_i, br)
        return re, im

    f_re, f_im = drive(pw_re[:chunk, :, 0][::-1], pw_im[:chunk, :, 0][::-1], bb_re[:, 0], bb_im[:, 0])
    r_re, r_im = drive(pw_re[:chunk, :, 1], pw_im[:chunk, :, 1], bb_re[:, 1], bb_im[:, 1])
    win = jnp.stack([f_re, f_im, r_re, r_im], axis=0)
    win = win.reshape(4, chunk, n_layers, n_slabs, gpt, gc, p)
    win = jnp.transpose(win, (2, 3, 1, 4, 5, 0, 6)).reshape(n, chunk * LANES, 4 * p)
    win = _expand_block_diag(win, p, gc, gpt)

    of_re, of_im = cp_re[1:, :, 0], cp_im[1:, :, 0]
    or_re, or_im = cp_re[1:, :, 1][::-1], cp_im[1:, :, 1][::-1]
    wout = jnp.stack([of_re, -of_im, or_re, -or_im], axis=0)
    wout = wout.reshape(4, chunk, n_layers, n_slabs, gpt, gc, p)
    wout = jnp.transpose(wout, (2, 3, 0, 4, 6, 1, 5)).reshape(n, 4 * gpt * p, chunk * gc)
    wout = _expand_block_diag(wout, gc, p, gpt)

    lam_t = jnp.stack([pw_re[chunk, :, 0], pw_im[chunk, :, 0], pw_re[chunk, :, 1], pw_im[chunk, :, 1]], axis=0)
    lam_t = lam_t.reshape(4, n_layers, n_slabs, gpt * p)
    lam_t = jnp.transpose(lam_t, (1, 2, 0, 3)).reshape(n, 1, 4 * gpt * p)
    return win, m, wout, lam_t


def _router_kernel(h_ref, mod_ref, g_ref, wr_ref, br_ref, xt_ref, cmb_ref):
    xt = _norm_mod(h_ref[...], g_ref[...], mod_ref[3:4, :], mod_ref[4:5, :])
    xt_ref[...] = xt.astype(xt_ref.dtype)
    logits = jnp.dot(xt, wr_ref[...], preferred_element_type=F32, precision=_HIGHEST) + br_ref[...]
    lane = lax.broadcasted_iota(jnp.int32, (1, LANES), 1)
    lane_f = lane.astype(F32)
    neg = -jnp.inf
    big = 1e9
    gmask = (lane >= MOE_EXPERTS) & (lane < MOE_EXPERTS + MOE_GROUPS)
    gl = jnp.where(gmask, logits, neg)
    gmax = jnp.max(gl, axis=-1, keepdims=True)
    gidx = jnp.min(jnp.where(gl == gmax, lane_f, big), axis=-1, keepdims=True) - MOE_EXPERTS
    p_group = 1.0 / jnp.sum(jnp.where(gmask, jnp.exp(gl - gmax), 0.0), axis=-1, keepdims=True)
    in_group = (lane < MOE_EXPERTS) & ((lane // MOE_EPG).astype(F32) == gidx)
    el = jnp.where(in_group, logits, neg)
    v1 = jnp.max(el, axis=-1, keepdims=True)
    i1 = jnp.min(jnp.where(el == v1, lane_f, big), axis=-1, keepdims=True)
    el2 = jnp.where(lane_f == i1, neg, el)
    v2 = jnp.max(el2, axis=-1, keepdims=True)
    i2 = jnp.min(jnp.where(el2 == v2, lane_f, big), axis=-1, keepdims=True)
    t = jnp.exp(v2 - v1)
    w1 = p_group / (1.0 + t)
    w2 = p_group * t / (1.0 + t)
    cmb_ref[...] = jnp.where(lane_f == i1, w1, 0.0) + jnp.where(lane_f == i2, w2, 0.0)


def _router(h, mod_tab, g, wr, br):
    b, nt, d = h.shape
    tm = ROW_TILE
    full = lambda shape: pl.BlockSpec(shape, lambda bb, i: (0,) * len(shape))
    return pl.pallas_call(
        _router_kernel,
        grid=(b, nt // tm),
        in_specs=[pl.BlockSpec((None, tm, d), lambda bb, i: (bb, i, 0)), _mod_spec(d),
                  full((1, d)), full(wr.shape), full(br.shape)],
        out_specs=[pl.BlockSpec((None, tm, d), lambda bb, i: (bb, i, 0)),
                   pl.BlockSpec((None, tm, LANES), lambda bb, i: (bb, i, 0))],
        out_shape=[jax.ShapeDtypeStruct((b, nt, d), _MXU_DTYPE),
                   jax.ShapeDtypeStruct((b, nt, LANES), F32)],
        compiler_params=_params("parallel", "parallel"),
        name="moe_router",
    )(h, mod_tab, g, wr, br)


def _experts_kernel(xt_ref, cmb_ref, wg_ref, wu_ref, wd_ref, h_ref, mod_ref, o_ref, acc_ref, *, ctx_len, nt):
    i = pl.program_id(0)
    e = pl.program_id(1)
    tm = xt_ref.shape[0]

    @pl.when(e == 0)
    def _():
        acc_ref[...] = jnp.zeros_like(acc_ref)

    x = xt_ref[...]
    gate = jnp.dot(x, wg_ref[...].astype(_MXU_DTYPE), preferred_element_type=F32)
    up = jnp.dot(x, wu_ref[...].astype(_MXU_DTYPE), preferred_element_type=F32)
    lane = lax.broadcasted_iota(jnp.int32, (1, LANES), 1)
    w = jnp.sum(jnp.where(lane == e, cmb_ref[...], 0.0), axis=-1, keepdims=True)
    hid = (gate / (1.0 + jnp.exp(-gate))) * up * w
    acc_ref[...] += jnp.dot(hid.astype(_MXU_DTYPE), wd_ref[...].astype(_MXU_DTYPE), preferred_element_type=F32)

    @pl.when(e == pl.num_programs(1) - 1)
    def _():
        row = (i * tm) % nt + lax.broadcasted_iota(jnp.int32, (tm, 1), 0)
        gate_row = jnp.where(row < ctx_len, mod_ref[0, 5:6, :], mod_ref[1, 5:6, :])
        o_ref[...] = h_ref[...] + gate_row * acc_ref[...]


def _experts(xt, cmb, wg, wu, wd, h, mod_tab, ctx_len):
    b, nt, d = h.shape
    tm = 1152 if nt % 1152 == 0 else ROW_TILE
    per_b = nt // tm
    n_exp, _, hid = wg.shape
    rows = b * nt
    blk = lambda w: pl.BlockSpec((tm, w), lambda i, e: (i, 0))
    out = pl.pallas_call(
        functools.partial(_experts_kernel, ctx_len=ctx_len, nt=nt),
        grid=(rows // tm, n_exp),
        in_specs=[blk(d), blk(LANES),
                  pl.BlockSpec((None, d, hid), lambda i, e: (e, 0, 0)),
                  pl.BlockSpec((None, d, hid), lambda i, e: (e, 0, 0)),
                  pl.BlockSpec((None, hid, d), lambda i, e: (e, 0, 0)),
                  blk(d),
                  pl.BlockSpec((None, 2, 8, d), lambda i, e: (i // per_b, 0, 0, 0))],
        out_specs=blk(d),
        out_shape=jax.ShapeDtypeStruct((rows, d), F32),
        scratch_shapes=[pltpu.VMEM((tm, d), F32)],
        compiler_params=_params("parallel", "arbitrary"),
        name="moe_experts",
    )(xt.reshape(rows, d), cmb.reshape(rows, LANES), wg, wu, wd, h.reshape(rows, d), mod_tab)
    return out.reshape(b, nt, d)


def _final_kernel(h_ref, g_ref, o_ref):
    h = h_ref[...]
    o_ref[...] = h * lax.rsqrt(jnp.mean(h * h, axis=-1, keepdims=True) + RMS_EPS) * g_ref[...]


def _final_norm(h, g, ctx_len):
    b, nt, d = h.shape
    tm = ROW_TILE
    skip = ctx_len // tm
    return pl.pallas_call(
        _final_kernel,
        grid=(b, (nt - ctx_len) // tm),
        in_specs=[pl.BlockSpec((None, tm, d), lambda bb, i: (bb, i + skip, 0)),
                  pl.BlockSpec((1, d), lambda bb, i: (0, 0))],
        out_specs=pl.BlockSpec((None, tm, d), lambda bb, i: (bb, i, 0)),
        out_shape=jax.ShapeDtypeStruct((b, nt - ctx_len, d), F32),
        compiler_params=_params("parallel", "parallel"),
        name="final_norm",
    )(h, g)


def _rope_tables(seq_len, ctx_len):
    n_rows = seq_len // GRID_W
    rows = jnp.repeat(jnp.arange(n_rows, dtype=F32), GRID_W)
    cols = jnp.tile(jnp.arange(GRID_W, dtype=F32), n_rows)
    half = HEAD_DIM // 2
    inv = 1.0 / (ROPE_BASE ** (jnp.arange(0, half, 2, dtype=F32) / half))
    ang_r = rows[:, None] * inv
    ang_c = cols[:, None] * inv
    ang = jnp.concatenate([ang_r, ang_r, ang_c, ang_c], axis=-1)
    ang = jnp.concatenate([jnp.zeros((ctx_len, HEAD_DIM), F32), ang], axis=0)
    ang = jnp.tile(ang, (1, LANES // HEAD_DIM))
    return jnp.cos(ang), jnp.sin(ang)


def _pad_row(v, width=LANES):
    return jnp.pad(v, (0, width - v.shape[0]))[None, :]


def kernel(x, c, ctx, c_ctx, mod_w, mod_b, norm1_g, norm2_g, final_g, attn_w_in, attn_w_out, attn_q_norm_g, attn_k_norm_g, diff_lambda_q1, diff_lambda_k1, diff_lambda_q2, diff_lambda_k2, diff_subln_g, ssm_a_re, ssm_a_im, ssm_log_dt, ssm_b_re, ssm_b_im, ssm_c_re, ssm_c_im, ssm_d, ssm_glu_w_a, ssm_glu_w_b, moe_group_w, moe_group_b, moe_router_w, moe_router_b, moe_w_gate, moe_w_up, moe_w_down):
    bsz, seq, d = x.shape
    ctx_len = ctx.shape[1]
    depth = mod_w.shape[0]
    assert ctx_len == ROW_TILE and seq % ROW_TILE == 0 and seq % GRID_W == 0

    h = jnp.concatenate([ctx, x], axis=1)

    mod_rows = 16
    c_all = jnp.concatenate([c, c_ctx[None, :], jnp.zeros((mod_rows - bsz - 1, d), F32)], axis=0)
    mods = _modulation(c_all, mod_w, mod_b).reshape(depth, mod_rows, 6, d)
    mods = jnp.pad(mods, ((0, 0), (0, 0), (0, 2), (0, 0)))
    mod_tabs = jnp.stack([jnp.broadcast_to(mods[:, bsz:bsz + 1], (depth, bsz, 8, d)), mods[:, :bsz]], axis=2)

    cos, sin = _rope_tables(seq, ctx_len)
    cast = lambda w: w.astype(_MXU_DTYPE)
    ssm_tabs = _ssm_tables(ssm_a_re, ssm_a_im, ssm_log_dt, ssm_b_re, ssm_b_im, ssm_c_re, ssm_c_im, SSM_CHUNK)

    for layer in range(depth):
        mod_tab = mod_tabs[layer]
        i = layer // 2
        if layer % 2 == 0:
            lambda_init = 0.8 - 0.6 * math.exp(-0.3 * layer)
            qkv = _attn_proj(h, mod_tab, norm1_g[layer][None, :], cast(attn_w_in[i]),
                             jnp.tile(attn_q_norm_g[i], 2)[None, :], jnp.tile(attn_k_norm_g[i], 2)[None, :],
                             cos, sin)
            lam_rows = jnp.concatenate([_pad_row(diff_lambda_q1[i]), _pad_row(diff_lambda_k1[i]),
                                        _pad_row(diff_lambda_q2[i]), _pad_row(diff_lambda_k2[i]),
                                        jnp.zeros((4, LANES), F32)], axis=0)
            h = _attention(qkv, lam_rows, diff_subln_g[i][None, :], cast(attn_w_out[i]), h, mod_tab,
                           lambda_init, ctx_len)
        else:
            u = _norm1(h, mod_tab, norm1_g[layer][None, :])
            y = _ssm_scan(u, *ssm_tabs, ctx_len, i * (d // LANES))
            h = _ssm_out(y, u, ssm_d[i][None, :], cast(ssm_glu_w_a[i]), cast(ssm_glu_w_b[i]), h, mod_tab)

        wr = jnp.concatenate([jnp.transpose(moe_router_w[layer], (1, 0, 2)).reshape(d, MOE_EXPERTS),
                              moe_group_w[layer],
                              jnp.zeros((d, LANES - MOE_EXPERTS - MOE_GROUPS), F32)], axis=1)
        br = _pad_row(jnp.concatenate([moe_router_b[layer].reshape(-1), moe_group_b[layer]]))
        xt, cmb = _router(h, mod_tab, norm2_g[layer][None, :], wr, br)
        h = _experts(xt, cmb, moe_w_gate[layer], moe_w_up[layer], moe_w_down[layer], h, mod_tab, ctx_len)

    return _final_norm(h, final_g[None, :], ctx_len)
```

```python
import functools
import math

import jax
import jax.numpy as jnp
from jax import lax
from jax.experimental import pallas as pl
from jax.experimental.pallas import tpu as pltpu

F32 = jnp.float32
_MXU_DTYPE = jnp.bfloat16
_HIGHEST = lax.Precision.HIGHEST

LANES = 128
HEAD_DIM = 64
GRID_W = 64
ROPE_BASE = 10000.0
GQA_Q_HEADS = 8
GQA_GROUP = 4
DIFF_HEADS = 4
GQA_Q_W = 512
GQA_KV_W = 128
DIFF_QK_W = 512
DIFF_V_W = 512
SSM_GROUP_CH = 16
SSM_STATE = 64
MOE_GROUPS = 4
MOE_EPG = 8
MOE_EXPERTS = 32
RMS_EPS = 1e-6
SSM_CHUNK = 8
ROW_TILE = 256
ROW_ALIGN = 16
VMEM_LIMIT = 56 * 1024 * 1024


def _params(*sem):
    return pltpu.CompilerParams(dimension_semantics=sem, vmem_limit_bytes=VMEM_LIMIT)


def _norm_mod(h, g, shift, scale):
    y = h * lax.rsqrt(jnp.mean(h * h, axis=-1, keepdims=True) + RMS_EPS) * g
    return y * (1.0 + scale) + shift


def _mm(a, b):
    return jnp.dot(a.astype(_MXU_DTYPE), b.astype(_MXU_DTYPE), preferred_element_type=F32)


def _mod_kernel(c_ref, w_ref, b_ref, o_ref):
    c = c_ref[...]
    a = c / (1.0 + jnp.exp(-c))
    o_ref[...] = jnp.dot(a, w_ref[...], preferred_element_type=F32, precision=_HIGHEST) + b_ref[...]


def _modulation(c_all, mod_w, mod_b):
    depth, d, n = mod_w.shape
    rows = c_all.shape[0]
    tn = 1536
    return pl.pallas_call(
        _mod_kernel,
        grid=(depth, n // tn),
        in_specs=[pl.BlockSpec((rows, d), lambda l, j: (0, 0)),
                  pl.BlockSpec((None, d, tn), lambda l, j: (l, 0, j)),
                  pl.BlockSpec((None, 1, tn), lambda l, j: (l, 0, j))],
        out_specs=pl.BlockSpec((None, rows, tn), lambda l, j: (l, 0, j)),
        out_shape=jax.ShapeDtypeStruct((depth, rows, n), F32),
        compiler_params=_params("parallel", "parallel"),
        name="modulation",
    )(c_all, mod_w, mod_b.reshape(depth, 1, n))


def _mod_spec(d):
    return pl.BlockSpec((None, None, 8, d), lambda b, i: (b, jnp.minimum(i, 1), 0, 0))


def _attn_proj_kernel(h_ref, mod_ref, g_ref, w_ref, gq_ref, gk_ref, cos_ref, sin_ref,
                      qa_ref, ka_ref, va_ref, qb_ref, kb_ref, vb_ref):
    xn = _norm_mod(h_ref[...], g_ref[...], mod_ref[0:1, :], mod_ref[1:2, :])
    hp = _mm(xn, w_ref[...])
    cos = cos_ref[...]
    sin = sin_ref[...]
    lane = lax.broadcasted_iota(jnp.int32, (1, LANES), 1)
    first_half = (lane % 32) < 16
    r = lax.broadcasted_iota(jnp.int32, (LANES, LANES), 0) // HEAD_DIM
    c = lax.broadcasted_iota(jnp.int32, (LANES, LANES), 1) // HEAD_DIM
    same_head = (r == c).astype(_MXU_DTYPE)

    def rope(x):
        rot = jnp.where(first_half, -pltpu.roll(x, LANES - 16, 1), pltpu.roll(x, 16, 1))
        return x * cos + rot * sin

    def head_norm(x, g):
        ss = jnp.dot((x * x).astype(_MXU_DTYPE), same_head, preferred_element_type=F32)
        return x * lax.rsqrt(ss * (1.0 / HEAD_DIM) + RMS_EPS) * g

    scale = HEAD_DIM ** -0.5 * math.log2(math.e)
    o = 0
    for s in range(GQA_Q_W // LANES):
        x = hp[:, o + s * LANES:o + (s + 1) * LANES]
        qa_ref[:, s * LANES:(s + 1) * LANES] = (rope(head_norm(x, gq_ref[...])) * scale).astype(qa_ref.dtype)
    o += GQA_Q_W
    ka_ref[...] = rope(head_norm(hp[:, o:o + LANES], gk_ref[...])).astype(ka_ref.dtype)
    o += GQA_KV_W
    va_ref[...] = hp[:, o:o + LANES].astype(va_ref.dtype)
    o += GQA_KV_W
    for s in range(DIFF_QK_W // LANES):
        x = hp[:, o + s * LANES:o + (s + 1) * LANES]
        qb_ref[:, s * LANES:(s + 1) * LANES] = (rope(x) * scale).astype(qb_ref.dtype)
    o += DIFF_QK_W
    for s in range(DIFF_QK_W // LANES):
        x = hp[:, o + s * LANES:o + (s + 1) * LANES]
        kb_ref[:, s * LANES:(s + 1) * LANES] = rope(x).astype(kb_ref.dtype)
    o += DIFF_QK_W
    vb_ref[...] = hp[:, o:o + DIFF_V_W].astype(vb_ref.dtype)


def _attn_proj(h, mod_tab, g, w_in, gq, gk, cos, sin):
    b, nt, d = h.shape
    tm = ROW_TILE
    widths = (GQA_Q_W, GQA_KV_W, GQA_KV_W, DIFF_QK_W, DIFF_QK_W, DIFF_V_W)
    full = lambda shape: pl.BlockSpec(shape, lambda bb, i: (0,) * len(shape))
    return pl.pallas_call(
        _attn_proj_kernel,
        grid=(b, nt // tm),
        in_specs=[pl.BlockSpec((None, tm, d), lambda bb, i: (bb, i, 0)),
                  _mod_spec(d),
                  full((1, d)),
                  full(w_in.shape),
                  full((1, LANES)),
                  full((1, LANES)),
                  pl.BlockSpec((tm, LANES), lambda bb, i: (i, 0)),
                  pl.BlockSpec((tm, LANES), lambda bb, i: (i, 0))],
        out_specs=[pl.BlockSpec((None, tm, w), lambda bb, i: (bb, i, 0)) for w in widths],
        out_shape=[jax.ShapeDtypeStruct((b, nt, w), _MXU_DTYPE) for w in widths],
        compiler_params=_params("parallel", "parallel"),
        name="attn_proj",
    )(h, mod_tab, g, w_in, gq, gk, cos, sin)


def _softmax_pv(q, k, v):
    s = lax.dot_general(q, k, (((1,), (1,)), ((), ())), preferred_element_type=F32)
    e = jnp.exp2(s - jnp.max(s, axis=-1, keepdims=True))
    l = jnp.sum(e, axis=-1, keepdims=True)
    return jnp.dot(e.astype(v.dtype), v, preferred_element_type=F32) / l


def _attn_kernel(qa_ref, qb_ref, ka_ref, va_ref, kb_ref, vb_ref, lam_ref, sg_ref, wo_ref, h_ref, mod_ref,
                 o_ref, mrg_ref, *, lambda_init, ctx_len):
    lv = lam_ref[...]
    lam = (jnp.exp(jnp.sum(lv[0:1] * lv[1:2], axis=-1, keepdims=True))
           - jnp.exp(jnp.sum(lv[2:3] * lv[3:4], axis=-1, keepdims=True)) + lambda_init)

    def run(nk):
        for h in range(GQA_Q_HEADS):
            g = h // GQA_GROUP
            o = _softmax_pv(qa_ref[:, h * HEAD_DIM:(h + 1) * HEAD_DIM],
                            ka_ref[0:nk, g * HEAD_DIM:(g + 1) * HEAD_DIM],
                            va_ref[0:nk, g * HEAD_DIM:(g + 1) * HEAD_DIM])
            mrg_ref[:, h * HEAD_DIM:(h + 1) * HEAD_DIM] = o.astype(mrg_ref.dtype)
        for h in range(DIFF_HEADS):
            c0 = h * 2 * HEAD_DIM
            v = vb_ref[0:nk, c0:c0 + 2 * HEAD_DIM]
            o1 = _softmax_pv(qb_ref[:, c0:c0 + HEAD_DIM], kb_ref[0:nk, c0:c0 + HEAD_DIM], v)
            o2 = _softmax_pv(qb_ref[:, c0 + HEAD_DIM:c0 + 2 * HEAD_DIM],
                             kb_ref[0:nk, c0 + HEAD_DIM:c0 + 2 * HEAD_DIM], v)
            o = o1 - lam * o2
            o = o * lax.rsqrt(jnp.mean(o * o, axis=-1, keepdims=True) + RMS_EPS) * sg_ref[...]
            o = o * (1.0 - lambda_init)
            mrg_ref[:, GQA_Q_W + c0:GQA_Q_W + c0 + 2 * HEAD_DIM] = o.astype(mrg_ref.dtype)

    i = pl.program_id(1)

    @pl.when(i == 0)
    def _():
        run(ctx_len)

    @pl.when(i > 0)
    def _():
        run(ka_ref.shape[0])

    y = jnp.dot(mrg_ref[...], wo_ref[...], preferred_element_type=F32)
    o_ref[...] = h_ref[...] + mod_ref[2:3, :] * y


def _attention(qkv, lam_rows, subln_g, w_out, h, mod_tab, lambda_init, ctx_len):
    qa, ka, va, qb, kb, vb = qkv
    b, nt, d = h.shape
    tq = ROW_TILE
    assert ctx_len == tq
    blk = lambda w: pl.BlockSpec((None, tq, w), lambda bb, i: (bb, i, 0))
    per_batch = lambda w: pl.BlockSpec((None, nt, w), lambda bb, i: (bb, 0, 0))
    full = lambda shape: pl.BlockSpec(shape, lambda bb, i: (0,) * len(shape))
    return pl.pallas_call(
        functools.partial(_attn_kernel, lambda_init=lambda_init, ctx_len=ctx_len),
        grid=(b, nt // tq),
        in_specs=[blk(GQA_Q_W), blk(DIFF_QK_W), per_batch(GQA_KV_W), per_batch(GQA_KV_W),
                  per_batch(DIFF_QK_W), per_batch(DIFF_V_W),
                  full((8, LANES)), full((1, LANES)), full(w_out.shape), blk(d), _mod_spec(d)],
        out_specs=blk(d),
        out_shape=jax.ShapeDtypeStruct((b, nt, d), F32),
        scratch_shapes=[pltpu.VMEM((tq, GQA_Q_W + DIFF_V_W), _MXU_DTYPE)],
        compiler_params=_params("parallel", "parallel"),
        name="attention",
    )(qa, qb, ka, va, kb, vb, lam_rows, subln_g, w_out, h, mod_tab)


def _norm1_kernel(h_ref, mod_ref, g_ref, o_ref):
    o_ref[...] = _norm_mod(h_ref[...], g_ref[...], mod_ref[0:1, :], mod_ref[1:2, :])


def _norm1(h, mod_tab, g):
    b, nt, d = h.shape
    tm = ROW_TILE
    blk = pl.BlockSpec((None, tm, d), lambda bb, i: (bb, i, 0))
    return pl.pallas_call(
        _norm1_kernel,
        grid=(b, nt // tm),
        in_specs=[blk, _mod_spec(d), pl.BlockSpec((1, d), lambda bb, i: (0, 0))],
        out_specs=blk,
        out_shape=jax.ShapeDtypeStruct((b, nt, d), F32),
        compiler_params=_params("parallel", "parallel"),
        name="ssm_norm",
    )(h, mod_tab, g)


def _ssm_kernel(u_ref, win_ref, m_ref, wout_ref, lam_ref, y_ref, bd_ref, *, chunk, n_ctx_chunks):
    nb, nt, _ = u_ref.shape
    nc = nt // chunk
    n_state_slabs = bd_ref.shape[0]
    q = n_state_slabs // 4

    def chunk_rows(bi):
        parts = [u_ref[bi, pl.ds(s, nc, stride=chunk), :] for s in range(chunk)]
        return jnp.concatenate(parts, axis=1).astype(_MXU_DTYPE)

    for bi in range(nb):
        drive = jnp.dot(chunk_rows(bi), win_ref[...], preferred_element_type=F32)
        for c in range(n_state_slabs):
            bd_ref[c, pl.ds(bi, nc, stride=nb), :] = drive[:, c * LANES:(c + 1) * LANES]

    lam = lam_ref[...]

    def make_step(base):
        a_re = [lam[:, (base + c) * LANES:(base + c + 1) * LANES] for c in range(q)]
        a_im = [lam[:, (base + q + c) * LANES:(base + q + c + 1) * LANES] for c in range(q)]

        def step(k, carry):
            row = pl.multiple_of(k * nb, nb)
            out = []
            for c in range(q):
                s_re, s_im = carry[2 * c], carry[2 * c + 1]
                d_re = bd_ref[base + c, pl.ds(row, nb), :]
                d_im = bd_ref[base + q + c, pl.ds(row, nb), :]
                bd_ref[base + c, pl.ds(row, nb), :] = s_re
                bd_ref[base + q + c, pl.ds(row, nb), :] = s_im
                out.append(a_re[c] * s_re - a_im[c] * s_im + d_re)
                out.append(a_re[c] * s_im + a_im[c] * s_re + d_im)
            return tuple(out)

        return step

    zero = tuple(jnp.zeros((nb, LANES), F32) for _ in range(2 * q))
    fwd = make_step(0)
    lax.fori_loop(0, nc, fwd, zero)
    rev = make_step(2 * q)
    carry = lax.fori_loop(0, n_ctx_chunks, lambda i, cr: rev(n_ctx_chunks - 1 - i, cr), zero)
    lax.fori_loop(0, nc - n_ctx_chunks, lambda i, cr: rev(nc - 1 - i, cr), carry)

    for bi in range(nb):
        states = jnp.concatenate([bd_ref[c, pl.ds(bi, nc, stride=nb), :] for c in range(n_state_slabs)], axis=1)
        y = (jnp.dot(chunk_rows(bi), m_ref[...], preferred_element_type=F32)
             + jnp.dot(states.astype(_MXU_DTYPE), wout_ref[...], preferred_element_type=F32))
        for t in range(chunk):
            y_ref[bi, pl.ds(t, nc, stride=chunk), :] = y[:, t * LANES:(t + 1) * LANES]


def _ssm_scan(u, win, m, wout, lam_t, ctx_len, first_slab):
    b, nt, d = u.shape
    chunk = SSM_CHUNK
    nb = 4 if b % 4 == 0 else b
    n_slabs = d // LANES
    state_w = win.shape[-1]
    nc = nt // chunk
    blk = pl.BlockSpec((nb, nt, LANES), lambda j, bb: (bb, 0, j))
    table = lambda j, bb: (first_slab + j, 0, 0)
    return pl.pallas_call(
        functools.partial(_ssm_kernel, chunk=chunk, n_ctx_chunks=ctx_len // chunk),
        grid=(n_slabs, b // nb),
        in_specs=[blk,
                  pl.BlockSpec((None,) + win.shape[1:], table),
                  pl.BlockSpec((None,) + m.shape[1:], table),
                  pl.BlockSpec((None,) + wout.shape[1:], table),
                  pl.BlockSpec((None, 1, state_w), table)],
        out_specs=blk,
        out_shape=jax.ShapeDtypeStruct((b, nt, d), F32),
        scratch_shapes=[pltpu.VMEM((state_w // LANES, nc * nb, LANES), F32)],
        compiler_params=_params("parallel", "parallel"),
        name="ssm_scan",
    )(u, win, m, wout, lam_t)


def _ssm_out_kernel(y_ref, u_ref, d_ref, wa_ref, wb_ref, h_ref, mod_ref, o_ref):
    x = y_ref[...] + d_ref[...] * u_ref[...]
    z = 0.5 * x * (1.0 + jnp.tanh(math.sqrt(2.0 / math.pi) * (x + 0.044715 * (x * x * x))))
    z = z.astype(_MXU_DTYPE)
    a = jnp.dot(z, wa_ref[...], preferred_element_type=F32)
    g = jnp.dot(z, wb_ref[...], preferred_element_type=F32)
    o_ref[...] = h_ref[...] + mod_ref[2:3, :] * (a / (1.0 + jnp.exp(-g)))


def _ssm_out(y, u, d_skip, wa, wb, h, mod_tab):
    b, nt, d = h.shape
    tm = ROW_TILE
    blk = pl.BlockSpec((None, tm, d), lambda bb, i: (bb, i, 0))
    full = lambda shape: pl.BlockSpec(shape, lambda bb, i: (0,) * len(shape))
    return pl.pallas_call(
        _ssm_out_kernel,
        grid=(b, nt // tm),
        in_specs=[blk, blk, full((1, d)), full(wa.shape), full(wb.shape), blk, _mod_spec(d)],
        out_specs=blk,
        out_shape=jax.ShapeDtypeStruct((b, nt, d), F32),
        compiler_params=_params("parallel", "parallel"),
        name="ssm_out",
    )(y, u, d_skip, wa, wb, h, mod_tab)


def _expand_kernel(x_ref, o_ref, *, lo, row_div, gpt):
    rows, cc = x_ref.shape
    oc = o_ref.shape[1]
    ci = lax.broadcasted_iota(jnp.int32, (cc, oc), 0)
    oi = lax.broadcasted_iota(jnp.int32, (cc, oc), 1)
    pick = (ci == (oi // (gpt * lo)) * lo + oi % lo).astype(_MXU_DTYPE)
    y = jnp.dot(x_ref[...].astype(_MXU_DTYPE), pick, preferred_element_type=F32)
    ri = lax.broadcasted_iota(jnp.int32, (rows, oc), 0) + pl.program_id(1) * rows
    oj = lax.broadcasted_iota(jnp.int32, (rows, oc), 1)
    keep = ((ri // row_div) % gpt) == ((oj // lo) % gpt)
    o_ref[...] = jnp.where(keep, y, 0.0).astype(o_ref.dtype)


def _expand_block_diag(x, lo, row_div, gpt):
    n, rows, cc = x.shape
    tr = ROW_TILE
    return pl.pallas_call(
        functools.partial(_expand_kernel, lo=lo, row_div=row_div, gpt=gpt),
        grid=(n, rows // tr),
        in_specs=[pl.BlockSpec((None, tr, cc), lambda i, r: (i, r, 0))],
        out_specs=pl.BlockSpec((None, tr, gpt * cc), lambda i, r: (i, r, 0)),
        out_shape=jax.ShapeDtypeStruct((n, rows, gpt * cc), _MXU_DTYPE),
        compiler_params=_params("parallel", "parallel"),
        name="ssm_expand",
    )(x)


def _ssm_tables(a_re, a_im, log_dt, b_re, b_im, c_re, c_im, chunk):
    n_layers, _, g_total, p = a_re.shape
    gpt = LANES // SSM_GROUP_CH
    n_slabs = g_total // gpt
    n = n_layers * n_slabs
    gc = SSM_GROUP_CH
    lre = jnp.minimum(a_re, -1e-4)
    lim = a_im
    dt = jnp.exp(log_dt)[..., None]
    steps = jnp.arange(chunk + 1, dtype=F32)[:, None, None, None, None]
    mag = jnp.exp(steps * (lre * dt))
    pw_re = mag * jnp.cos(steps * (lim * dt))
    pw_im = mag * jnp.sin(steps * (lim * dt))
    nr = pw_re[1] - 1.0
    ni = pw_im[1]
    den = lre * lre + lim * lim
    coef_re = (nr * lre + ni * lim) / den
    coef_im = (ni * lre - nr * lim) / den
    bb_re = coef_re[..., None] * b_re - coef_im[..., None] * b_im
    bb_im = coef_re[..., None] * b_im + coef_im[..., None] * b_re

    ein = functools.partial(jnp.einsum, precision=_HIGHEST)
    cp_re = ein('jlxgp,lxgcp->jlxgcp', pw_re, c_re) - ein('jlxgp,lxgcp->jlxgcp', pw_im, c_im)
    cp_im = ein('jlxgp,lxgcp->jlxgcp', pw_re, c_im) + ein('jlxgp,lxgcp->jlxgcp', pw_im, c_re)
    taps = ein('jlxgcp,lxgpd->jlxgdc', cp_re, bb_re) - ein('jlxgcp,lxgpd->jlxgdc', cp_im, bb_im)

    def tap(s, t):
        if t > s:
            return taps[t - s, :, 0]
        if t < s:
            return taps[s - t, :, 1]
        return taps[0, :, 0] + taps[0, :, 1]

    kst = jnp.stack([jnp.stack([tap(s, t) for t in range(chunk)], axis=1) for s in range(chunk)], axis=1)
    kst = kst.reshape(n_layers, chunk, chunk, n_slabs, gpt, gc, gc)
    kst = jnp.transpose(kst, (0, 3, 1, 4, 5, 2, 6)).reshape(n, chunk * LANES, chunk * gc)
    m = _expand_block_diag(kst, gc, gc, gpt)

    def drive(pw_r, pw_i, br, bi):
        re = ein('slgp,lgpd->slgdp', pw_r, br) - ein('slgp,lgpd->slgdp', pw_i, bi)
        im = ein('slgp,lgpd->slgdp', pw_r, bi) + ein('slgp,lgpd->slgdp', pw_i, br)
        return re, im

    f_re, f_im = drive(pw_re[:chunk, :, 0][::-1], pw_im[:chunk, :, 0][::-1], bb_re[:, 0], bb_im[:, 0])
    r_re, r_im = drive(pw_re[:chunk, :, 1], pw_im[:chunk, :, 1], bb_re[:, 1], bb_im[:, 1])
    win = jnp.stack([f_re, f_im, r_re, r_im], axis=0)
    win = win.reshape(4, chunk, n_layers, n_slabs, gpt, gc, p)
    win = jnp.transpose(win, (2, 3, 1, 4, 5, 0, 6)).reshape(n, chunk * LANES, 4 * p)
    win = _expand_block_diag(win, p, gc, gpt)

    of_re, of_im = cp_re[1:, :, 0], cp_im[1:, :, 0]
    or_re, or_im = cp_re[1:, :, 1][::-1], cp_im[1:, :, 1][::-1]
    wout = jnp.stack([of_re, -of_im, or_re, -or_im], axis=0)
    wout = wout.reshape(4, chunk, n_layers, n_slabs, gpt, gc, p)
    wout = jnp.transpose(wout, (2, 3, 0, 4, 6, 1, 5)).reshape(n, 4 * gpt * p, chunk * gc)
    wout = _expand_block_diag(wout, gc, p, gpt)

    lam_t = jnp.stack([pw_re[chunk, :, 0], pw_im[chunk, :, 0], pw_re[chunk, :, 1], pw_im[chunk, :, 1]], axis=0)
    lam_t = lam_t.reshape(4, n_layers, n_slabs, gpt * p)
    lam_t = jnp.transpose(lam_t, (1, 2, 0, 3)).reshape(n, 1, 4 * gpt * p)
    return win, m, wout, lam_t


def _router_kernel(h_ref, mod_ref, g_ref, wr_ref, br_ref, xt_ref, cmb_ref):
    xt = _norm_mod(h_ref[...], g_ref[...], mod_ref[3:4, :], mod_ref[4:5, :])
    xt_ref[...] = xt.astype(xt_ref.dtype)
    logits = jnp.dot(xt, wr_ref[...], preferred_element_type=F32, precision=_HIGHEST) + br_ref[...]
    lane = lax.broadcasted_iota(jnp.int32, (1, LANES), 1)
    lane_f = lane.astype(F32)
    neg = -jnp.inf
    big = 1e9
    gmask = (lane >= MOE_EXPERTS) & (lane < MOE_EXPERTS + MOE_GROUPS)
    gl = jnp.where(gmask, logits, neg)
    gmax = jnp.max(gl, axis=-1, keepdims=True)
    gidx = jnp.min(jnp.where(gl == gmax, lane_f, big), axis=-1, keepdims=True) - MOE_EXPERTS
    p_group = 1.0 / jnp.sum(jnp.where(gmask, jnp.exp(gl - gmax), 0.0), axis=-1, keepdims=True)
    in_group = (lane < MOE_EXPERTS) & ((lane // MOE_EPG).astype(F32) == gidx)
    el = jnp.where(in_group, logits, neg)
    v1 = jnp.max(el, axis=-1, keepdims=True)
    i1 = jnp.min(jnp.where(el == v1, lane_f, big), axis=-1, keepdims=True)
    el2 = jnp.where(lane_f == i1, neg, el)
    v2 = jnp.max(el2, axis=-1, keepdims=True)
    i2 = jnp.min(jnp.where(el2 == v2, lane_f, big), axis=-1, keepdims=True)
    t = jnp.exp(v2 - v1)
    w1 = p_group / (1.0 + t)
    w2 = p_group * t / (1.0 + t)
    cmb_ref[...] = jnp.where(lane_f == i1, w1, 0.0) + jnp.where(lane_f == i2, w2, 0.0)


def _router(h, mod_tab, g, wr, br):
    b, nt, d = h.shape
    tm = ROW_TILE
    full = lambda shape: pl.BlockSpec(shape, lambda bb, i: (0,) * len(shape))
    return pl.pallas_call(
        _router_kernel,
        grid=(b, nt // tm),
        in_specs=[pl.BlockSpec((None, tm, d), lambda bb, i: (bb, i, 0)), _mod_spec(d),
                  full((1, d)), full(wr.shape), full(br.shape)],
        out_specs=[pl.BlockSpec((None, tm, d), lambda bb, i: (bb, i, 0)),
                   pl.BlockSpec((None, tm, LANES), lambda bb, i: (bb, i, 0))],
        out_shape=[jax.ShapeDtypeStruct((b, nt, d), _MXU_DTYPE),
                   jax.ShapeDtypeStruct((b, nt, LANES), F32)],
        compiler_params=_params("parallel", "parallel"),
        name="moe_router",
    )(h, mod_tab, g, wr, br)


def _split_terms(x, n):
    terms = []
    for _ in range(n - 1):
        t = x.astype(_MXU_DTYPE)
        terms.append(t)
        x = x - t.astype(F32)
    terms.append(x.astype(_MXU_DTYPE))
    return terms


def _experts_kernel(xt_ref, cmb_ref, wg_ref, wu_ref, wd_ref, h_ref, mod_ref, o_ref,
                    xs_ref, cs_ref, acc_ref, pos_ref, seg_ref, *, ctx_len, nt, window):
    i = pl.program_id(0)
    e = pl.program_id(1)
    sb = xt_ref.shape[0]
    d = xt_ref.shape[1]

    @pl.when(e == 0)
    def _():
        cmb = cmb_ref[...]
        lane = lax.broadcasted_iota(jnp.int32, (1, LANES), 1)
        routed = cmb != 0.0
        goh = jnp.zeros((sb, LANES), F32)
        for g in range(MOE_GROUPS):
            in_g = routed & (lane >= g * MOE_EPG) & (lane < (g + 1) * MOE_EPG)
            hit = jnp.max(jnp.where(in_g, 1.0, 0.0), axis=-1, keepdims=True)
            goh = goh + jnp.where(lane == g, hit, 0.0)
        r_i = lax.broadcasted_iota(jnp.int32, (sb, sb), 0)
        c_i = lax.broadcasted_iota(jnp.int32, (sb, sb), 1)
        earlier = (c_i < r_i).astype(_MXU_DTYPE)
        before = jnp.dot(earlier, goh.astype(_MXU_DTYPE), preferred_element_type=F32)
        cnt = jnp.sum(goh, axis=0, keepdims=True)
        off = jnp.zeros((1, LANES), F32)
        run = jnp.zeros((1, 1), F32)
        for g in range(MOE_GROUPS):
            off = off + jnp.where(lane == g, run, 0.0)
            run = run + jnp.sum(jnp.where(lane == g, cnt, 0.0), axis=-1, keepdims=True)
        pos = jnp.sum(goh * (off + before), axis=-1, keepdims=True)
        pos_b = jnp.broadcast_to(pos, (sb, LANES))
        pos_ref[...] = pos_b
        pos_row = pos_b.T[0:1, :].astype(jnp.int32)
        perm = (r_i == pos_row).astype(_MXU_DTYPE)
        xs_ref[0:sb, :] = jnp.dot(perm, xt_ref[...], preferred_element_type=F32).astype(xs_ref.dtype)
        cs = jnp.zeros((sb, LANES), F32)
        for term in _split_terms(cmb, 3):
            cs = cs + jnp.dot(perm, term, preferred_element_type=F32)
        cs_ref[0:sb, :] = cs
        xs_ref[sb:sb + window, :] = jnp.zeros((window, d), xs_ref.dtype)
        cs_ref[sb:sb + window, :] = jnp.zeros((window, LANES), F32)
        acc_ref[...] = jnp.zeros_like(acc_ref)
        off_i = off.astype(jnp.int32)
        cnt_i = cnt.astype(jnp.int32)
        for g in range(MOE_GROUPS):
            seg_ref[g] = off_i[0, g]
            seg_ref[MOE_GROUPS + g] = cnt_i[0, g]

    g = e // MOE_EPG
    start = seg_ref[g]
    count = seg_ref[MOE_GROUPS + g]
    first = (start // ROW_ALIGN) * ROW_ALIGN
    n_win = (start - first + count + window - 1) // window
    n_win = jnp.where(count > 0, n_win, 0)
    wg = wg_ref[...].astype(_MXU_DTYPE)
    wu = wu_ref[...].astype(_MXU_DTYPE)
    wd = wd_ref[...].astype(_MXU_DTYPE)
    lane = lax.broadcasted_iota(jnp.int32, (1, LANES), 1)

    def window_step(k, carry):
        r0 = pl.multiple_of(first + k * window, ROW_ALIGN)
        x = xs_ref[pl.ds(r0, window), :]
        gate = jnp.dot(x, wg, preferred_element_type=F32)
        up = jnp.dot(x, wu, preferred_element_type=F32)
        w = jnp.sum(jnp.where(lane == e, cs_ref[pl.ds(r0, window), :], 0.0), axis=-1, keepdims=True)
        hid = (gate / (1.0 + jnp.exp(-gate))) * up * w
        acc_ref[pl.ds(r0, window), :] += jnp.dot(hid.astype(_MXU_DTYPE), wd, preferred_element_type=F32)
        return carry

    lax.fori_loop(0, n_win, window_step, 0)

    @pl.when(e == pl.num_programs(1) - 1)
    def _():
        c_i = lax.broadcasted_iota(jnp.int32, (sb, sb), 1)
        unperm = (c_i == pos_ref[:, 0:1].astype(jnp.int32)).astype(_MXU_DTYPE)
        y = jnp.zeros((sb, d), F32)
        for term in _split_terms(acc_ref[0:sb, :], 2):
            y = y + jnp.dot(unperm, term, preferred_element_type=F32)
        row = (i * sb) % nt + lax.broadcasted_iota(jnp.int32, (sb, 1), 0)
        gate_row = jnp.where(row < ctx_len, mod_ref[0, 5:6, :], mod_ref[1, 5:6, :])
        o_ref[...] = h_ref[...] + gate_row * y


def _experts(xt, cmb, wg, wu, wd, h, mod_tab, ctx_len):
    b, nt, d = h.shape
    sb = 1152 if nt % 1152 == 0 else ROW_TILE
    window = 320 if sb == 1152 else 96
    per_b = nt // sb
    n_exp, _, hid = wg.shape
    rows = b * nt
    blk = lambda w: pl.BlockSpec((sb, w), lambda i, e: (i, 0))
    out = pl.pallas_call(
        functools.partial(_experts_kernel, ctx_len=ctx_len, nt=nt, window=window),
        grid=(rows // sb, n_exp),
        in_specs=[blk(d), blk(LANES),
                  pl.BlockSpec((None, d, hid), lambda i, e: (e, 0, 0)),
                  pl.BlockSpec((None, d, hid), lambda i, e: (e, 0, 0)),
                  pl.BlockSpec((None, hid, d), lambda i, e: (e, 0, 0)),
                  blk(d),
                  pl.BlockSpec((None, 2, 8, d), lambda i, e: (i // per_b, 0, 0, 0))],
        out_specs=blk(d),
        out_shape=jax.ShapeDtypeStruct((rows, d), F32),
        scratch_shapes=[pltpu.VMEM((sb + window, d), _MXU_DTYPE), pltpu.VMEM((sb + window, LANES), F32),
                        pltpu.VMEM((sb + window, d), F32), pltpu.VMEM((sb, LANES), F32),
                        pltpu.SMEM((2 * MOE_GROUPS,), jnp.int32)],
        compiler_params=_params("parallel", "arbitrary"),
        name="moe_experts",
    )(xt.reshape(rows, d), cmb.reshape(rows, LANES), wg, wu, wd, h.reshape(rows, d), mod_tab)
    return out.reshape(b, nt, d)


def _final_kernel(h_ref, g_ref, o_ref):
    h = h_ref[...]
    o_ref[...] = h * lax.rsqrt(jnp.mean(h * h, axis=-1, keepdims=True) + RMS_EPS) * g_ref[...]


def _final_norm(h, g, ctx_len):
    b, nt, d = h.shape
    tm = ROW_TILE
    skip = ctx_len // tm
    return pl.pallas_call(
        _final_kernel,
        grid=(b, (nt - ctx_len) // tm),
        in_specs=[pl.BlockSpec((None, tm, d), lambda bb, i: (bb, i + skip, 0)),
                  pl.BlockSpec((1, d), lambda bb, i: (0, 0))],
        out_specs=pl.BlockSpec((None, tm, d), lambda bb, i: (bb, i, 0)),
        out_shape=jax.ShapeDtypeStruct((b, nt - ctx_len, d), F32),
        compiler_params=_params("parallel", "parallel"),
        name="final_norm",
    )(h, g)


def _rope_tables(seq_len, ctx_len):
    n_rows = seq_len // GRID_W
    rows = jnp.repeat(jnp.arange(n_rows, dtype=F32), GRID_W)
    cols = jnp.tile(jnp.arange(GRID_W, dtype=F32), n_rows)
    half = HEAD_DIM // 2
    inv = 1.0 / (ROPE_BASE ** (jnp.arange(0, half, 2, dtype=F32) / half))
    ang_r = rows[:, None] * inv
    ang_c = cols[:, None] * inv
    ang = jnp.concatenate([ang_r, ang_r, ang_c, ang_c], axis=-1)
    ang = jnp.concatenate([jnp.zeros((ctx_len, HEAD_DIM), F32), ang], axis=0)
    ang = jnp.tile(ang, (1, LANES // HEAD_DIM))
    return jnp.cos(ang), jnp.sin(ang)


def _pad_row(v, width=LANES):
    return jnp.pad(v, (0, width - v.shape[0]))[None, :]


def kernel(x, c, ctx, c_ctx, mod_w, mod_b, norm1_g, norm2_g, final_g, attn_w_in, attn_w_out, attn_q_norm_g, attn_k_norm_g, diff_lambda_q1, diff_lambda_k1, diff_lambda_q2, diff_lambda_k2, diff_subln_g, ssm_a_re, ssm_a_im, ssm_log_dt, ssm_b_re, ssm_b_im, ssm_c_re, ssm_c_im, ssm_d, ssm_glu_w_a, ssm_glu_w_b, moe_group_w, moe_group_b, moe_router_w, moe_router_b, moe_w_gate, moe_w_up, moe_w_down):
    bsz, seq, d = x.shape
    ctx_len = ctx.shape[1]
    depth = mod_w.shape[0]
    assert ctx_len == ROW_TILE and seq % ROW_TILE == 0 and seq % GRID_W == 0

    h = jnp.concatenate([ctx, x], axis=1)

    mod_rows = 16
    c_all = jnp.concatenate([c, c_ctx[None, :], jnp.zeros((mod_rows - bsz - 1, d), F32)], axis=0)
    mods = _modulation(c_all, mod_w, mod_b).reshape(depth, mod_rows, 6, d)
    mods = jnp.pad(mods, ((0, 0), (0, 0), (0, 2), (0, 0)))
    mod_tabs = jnp.stack([jnp.broadcast_to(mods[:, bsz:bsz + 1], (depth, bsz, 8, d)), mods[:, :bsz]], axis=2)

    cos, sin = _rope_tables(seq, ctx_len)
    cast = lambda w: w.astype(_MXU_DTYPE)
    ssm_tabs = _ssm_tables(ssm_a_re, ssm_a_im, ssm_log_dt, ssm_b_re, ssm_b_im, ssm_c_re, ssm_c_im, SSM_CHUNK)

    for layer in range(depth):
        mod_tab = mod_tabs[layer]
        i = layer // 2
        if layer % 2 == 0:
            lambda_init = 0.8 - 0.6 * math.exp(-0.3 * layer)
            qkv = _attn_proj(h, mod_tab, norm1_g[layer][None, :], cast(attn_w_in[i]),
                             jnp.tile(attn_q_norm_g[i], 2)[None, :], jnp.tile(attn_k_norm_g[i], 2)[None, :],
                             cos, sin)
            lam_rows = jnp.concatenate([_pad_row(diff_lambda_q1[i]), _pad_row(diff_lambda_k1[i]),
                                        _pad_row(diff_lambda_q2[i]), _pad_row(diff_lambda_k2[i]),
                                        jnp.zeros((4, LANES), F32)], axis=0)
            h = _attention(qkv, lam_rows, diff_subln_g[i][None, :], cast(attn_w_out[i]), h, mod_tab,
                           lambda_init, ctx_len)
        else:
            u = _norm1(h, mod_tab, norm1_g[layer][None, :])
            y = _ssm_scan(u, *ssm_tabs, ctx_len, i * (d // LANES))
            h = _ssm_out(y, u, ssm_d[i][None, :], cast(ssm_glu_w_a[i]), cast(ssm_glu_w_b[i]), h, mod_tab)

        wr = jnp.concatenate([jnp.transpose(moe_router_w[layer], (1, 0, 2)).reshape(d, MOE_EXPERTS),
                              moe_group_w[layer],
                              jnp.zeros((d, LANES - MOE_EXPERTS - MOE_GROUPS), F32)], axis=1)
        br = _pad_row(jnp.concatenate([moe_router_b[layer].reshape(-1), moe_group_b[layer]]))
        xt, cmb = _router(h, mod_tab, norm2_g[layer][None, :], wr, br)
        h = _experts(xt, cmb, moe_w_gate[layer], moe_w_up[layer], moe_w_down[layer], h, mod_tab, ctx_len)

    return _final_norm(h, final_g[None, :], ctx_len)
```

```python
import functools
import math

import jax
import jax.numpy as jnp
from jax import lax
from jax.experimental import pallas as pl
from jax.experimental.pallas import tpu as pltpu

F32 = jnp.float32
_MXU_DTYPE = jnp.bfloat16
_HIGHEST = lax.Precision.HIGHEST

LANES = 128
HEAD_DIM = 64
GRID_W = 64
ROPE_BASE = 10000.0
GQA_Q_HEADS = 8
GQA_GROUP = 4
DIFF_HEADS = 4
GQA_Q_W = 512
GQA_KV_W = 128
DIFF_QK_W = 512
DIFF_V_W = 512
SSM_GROUP_CH = 16
SSM_STATE = 64
MOE_GROUPS = 4
MOE_EPG = 8
MOE_EXPERTS = 32
RMS_EPS = 1e-6
SSM_CHUNK = 8
ROW_TILE = 256
ROW_ALIGN = 16
VMEM_LIMIT = 56 * 1024 * 1024


def _params(*sem):
    return pltpu.CompilerParams(dimension_semantics=sem, vmem_limit_bytes=VMEM_LIMIT)


def _norm_mod(h, g, shift, scale):
    y = h * lax.rsqrt(jnp.mean(h * h, axis=-1, keepdims=True) + RMS_EPS) * g
    return y * (1.0 + scale) + shift


def _mm(a, b):
    return jnp.dot(a.astype(_MXU_DTYPE), b.astype(_MXU_DTYPE), preferred_element_type=F32)


def _mod_kernel(c_ref, w_ref, b_ref, o_ref):
    c = c_ref[...]
    a = c / (1.0 + jnp.exp(-c))
    o_ref[...] = jnp.dot(a, w_ref[...], preferred_element_type=F32, precision=_HIGHEST) + b_ref[...]


def _modulation(c_all, mod_w, mod_b):
    depth, d, n = mod_w.shape
    rows = c_all.shape[0]
    tn = 1536
    return pl.pallas_call(
        _mod_kernel,
        grid=(depth, n // tn),
        in_specs=[pl.BlockSpec((rows, d), lambda l, j: (0, 0)),
                  pl.BlockSpec((None, d, tn), lambda l, j: (l, 0, j)),
                  pl.BlockSpec((None, 1, tn), lambda l, j: (l, 0, j))],
        out_specs=pl.BlockSpec((None, rows, tn), lambda l, j: (l, 0, j)),
        out_shape=jax.ShapeDtypeStruct((depth, rows, n), F32),
        compiler_params=_params("parallel", "parallel"),
        name="modulation",
    )(c_all, mod_w, mod_b.reshape(depth, 1, n))


def _mod_spec(d):
    return pl.BlockSpec((None, None, 8, d), lambda b, i: (b, jnp.minimum(i, 1), 0, 0))


def _attn_proj_kernel(h_ref, mod_ref, g_ref, w_ref, gq_ref, gk_ref, cos_ref, sin_ref,
                      qa_ref, ka_ref, va_ref, qb_ref, kb_ref, vb_ref):
    xn = _norm_mod(h_ref[...], g_ref[...], mod_ref[0:1, :], mod_ref[1:2, :])
    hp = _mm(xn, w_ref[...])
    cos = cos_ref[...]
    sin = sin_ref[...]
    lane = lax.broadcasted_iota(jnp.int32, (1, LANES), 1)
    first_half = (lane % 32) < 16
    r = lax.broadcasted_iota(jnp.int32, (LANES, LANES), 0) // HEAD_DIM
    c = lax.broadcasted_iota(jnp.int32, (LANES, LANES), 1) // HEAD_DIM
    same_head = (r == c).astype(_MXU_DTYPE)

    def rope(x):
        rot = jnp.where(first_half, -pltpu.roll(x, LANES - 16, 1), pltpu.roll(x, 16, 1))
        return x * cos + rot * sin

    def head_norm(x, g):
        ss = jnp.dot((x * x).astype(_MXU_DTYPE), same_head, preferred_element_type=F32)
        return x * lax.rsqrt(ss * (1.0 / HEAD_DIM) + RMS_EPS) * g

    scale = HEAD_DIM ** -0.5 * math.log2(math.e)
    o = 0
    for s in range(GQA_Q_W // LANES):
        x = hp[:, o + s * LANES:o + (s + 1) * LANES]
        qa_ref[:, s * LANES:(s + 1) * LANES] = (rope(head_norm(x, gq_ref[...])) * scale).astype(qa_ref.dtype)
    o += GQA_Q_W
    ka_ref[...] = rope(head_norm(hp[:, o:o + LANES], gk_ref[...])).astype(ka_ref.dtype)
    o += GQA_KV_W
    va_ref[...] = hp[:, o:o + LANES].astype(va_ref.dtype)
    o += GQA_KV_W
    for s in range(DIFF_QK_W // LANES):
        x = hp[:, o + s * LANES:o + (s + 1) * LANES]
        qb_ref[:, s * LANES:(s + 1) * LANES] = (rope(x) * scale).astype(qb_ref.dtype)
    o += DIFF_QK_W
    for s in range(DIFF_QK_W // LANES):
        x = hp[:, o + s * LANES:o + (s + 1) * LANES]
        kb_ref[:, s * LANES:(s + 1) * LANES] = rope(x).astype(kb_ref.dtype)
    o += DIFF_QK_W
    vb_ref[...] = hp[:, o:o + DIFF_V_W].astype(vb_ref.dtype)


def _attn_proj(h, mod_tab, g, w_in, gq, gk, cos, sin):
    b, nt, d = h.shape
    tm = ROW_TILE
    widths = (GQA_Q_W, GQA_KV_W, GQA_KV_W, DIFF_QK_W, DIFF_QK_W, DIFF_V_W)
    full = lambda shape: pl.BlockSpec(shape, lambda bb, i: (0,) * len(shape))
    return pl.pallas_call(
        _attn_proj_kernel,
        grid=(b, nt // tm),
        in_specs=[pl.BlockSpec((None, tm, d), lambda bb, i: (bb, i, 0)),
                  _mod_spec(d),
                  full((1, d)),
                  full(w_in.shape),
                  full((1, LANES)),
                  full((1, LANES)),
                  pl.BlockSpec((tm, LANES), lambda bb, i: (i, 0)),
                  pl.BlockSpec((tm, LANES), lambda bb, i: (i, 0))],
        out_specs=[pl.BlockSpec((None, tm, w), lambda bb, i: (bb, i, 0)) for w in widths],
        out_shape=[jax.ShapeDtypeStruct((b, nt, w), _MXU_DTYPE) for w in widths],
        compiler_params=_params("parallel", "parallel"),
        name="attn_proj",
    )(h, mod_tab, g, w_in, gq, gk, cos, sin)


def _softmax_pv(q, k, v):
    s = lax.dot_general(q, k, (((1,), (1,)), ((), ())), preferred_element_type=F32)
    e = jnp.exp2(s - jnp.max(s, axis=-1, keepdims=True))
    l = jnp.sum(e, axis=-1, keepdims=True)
    return jnp.dot(e.astype(v.dtype), v, preferred_element_type=F32) / l


def _attn_kernel(qa_ref, qb_ref, ka_ref, va_ref, kb_ref, vb_ref, lam_ref, sg_ref, wo_ref, h_ref, mod_ref,
                 o_ref, mrg_ref, *, lambda_init, ctx_len):
    lv = lam_ref[...]
    lam = (jnp.exp(jnp.sum(lv[0:1] * lv[1:2], axis=-1, keepdims=True))
           - jnp.exp(jnp.sum(lv[2:3] * lv[3:4], axis=-1, keepdims=True)) + lambda_init)

    def run(nk):
        for h in range(GQA_Q_HEADS):
            g = h // GQA_GROUP
            o = _softmax_pv(qa_ref[:, h * HEAD_DIM:(h + 1) * HEAD_DIM],
                            ka_ref[0:nk, g * HEAD_DIM:(g + 1) * HEAD_DIM],
                            va_ref[0:nk, g * HEAD_DIM:(g + 1) * HEAD_DIM])
            mrg_ref[:, h * HEAD_DIM:(h + 1) * HEAD_DIM] = o.astype(mrg_ref.dtype)
        for h in range(DIFF_HEADS):
            c0 = h * 2 * HEAD_DIM
            v = vb_ref[0:nk, c0:c0 + 2 * HEAD_DIM]
            o1 = _softmax_pv(qb_ref[:, c0:c0 + HEAD_DIM], kb_ref[0:nk, c0:c0 + HEAD_DIM], v)
            o2 = _softmax_pv(qb_ref[:, c0 + HEAD_DIM:c0 + 2 * HEAD_DIM],
                             kb_ref[0:nk, c0 + HEAD_DIM:c0 + 2 * HEAD_DIM], v)
            o = o1 - lam * o2
            o = o * lax.rsqrt(jnp.mean(o * o, axis=-1, keepdims=True) + RMS_EPS) * sg_ref[...]
            o = o * (1.0 - lambda_init)
            mrg_ref[:, GQA_Q_W + c0:GQA_Q_W + c0 + 2 * HEAD_DIM] = o.astype(mrg_ref.dtype)

    i = pl.program_id(1)

    @pl.when(i == 0)
    def _():
        run(ctx_len)

    @pl.when(i > 0)
    def _():
        run(ka_ref.shape[0])

    y = jnp.dot(mrg_ref[...], wo_ref[...], preferred_element_type=F32)
    o_ref[...] = h_ref[...] + mod_ref[2:3, :] * y


def _attention(qkv, lam_rows, subln_g, w_out, h, mod_tab, lambda_init, ctx_len):
    qa, ka, va, qb, kb, vb = qkv
    b, nt, d = h.shape
    tq = ROW_TILE
    assert ctx_len == tq
    blk = lambda w: pl.BlockSpec((None, tq, w), lambda bb, i: (bb, i, 0))
    per_batch = lambda w: pl.BlockSpec((None, nt, w), lambda bb, i: (bb, 0, 0))
    full = lambda shape: pl.BlockSpec(shape, lambda bb, i: (0,) * len(shape))
    return pl.pallas_call(
        functools.partial(_attn_kernel, lambda_init=lambda_init, ctx_len=ctx_len),
        grid=(b, nt // tq),
        in_specs=[blk(GQA_Q_W), blk(DIFF_QK_W), per_batch(GQA_KV_W), per_batch(GQA_KV_W),
                  per_batch(DIFF_QK_W), per_batch(DIFF_V_W),
                  full((8, LANES)), full((1, LANES)), full(w_out.shape), blk(d), _mod_spec(d)],
        out_specs=blk(d),
        out_shape=jax.ShapeDtypeStruct((b, nt, d), F32),
        scratch_shapes=[pltpu.VMEM((tq, GQA_Q_W + DIFF_V_W), _MXU_DTYPE)],
        compiler_params=_params("parallel", "parallel"),
        name="attention",
    )(qa, qb, ka, va, kb, vb, lam_rows, subln_g, w_out, h, mod_tab)


def _norm1_kernel(h_ref, mod_ref, g_ref, o_ref):
    o_ref[...] = _norm_mod(h_ref[...], g_ref[...], mod_ref[0:1, :], mod_ref[1:2, :])


def _norm1(h, mod_tab, g):
    b, nt, d = h.shape
    tm = ROW_TILE
    blk = pl.BlockSpec((None, tm, d), lambda bb, i: (bb, i, 0))
    return pl.pallas_call(
        _norm1_kernel,
        grid=(b, nt // tm),
        in_specs=[blk, _mod_spec(d), pl.BlockSpec((1, d), lambda bb, i: (0, 0))],
        out_specs=blk,
        out_shape=jax.ShapeDtypeStruct((b, nt, d), F32),
        compiler_params=_params("parallel", "parallel"),
        name="ssm_norm",
    )(h, mod_tab, g)


def _ssm_kernel(u_ref, win_ref, m_ref, wout_ref, lam_ref, y_ref, bd_ref, *, chunk, n_ctx_chunks):
    nb, nt, _ = u_ref.shape
    nc = nt // chunk
    n_state_slabs = bd_ref.shape[0]
    q = n_state_slabs // 4

    def chunk_rows(bi):
        parts = [u_ref[bi, pl.ds(s, nc, stride=chunk), :] for s in range(chunk)]
        return jnp.concatenate(parts, axis=1).astype(_MXU_DTYPE)

    for bi in range(nb):
        drive = jnp.dot(chunk_rows(bi), win_ref[...], preferred_element_type=F32)
        for c in range(n_state_slabs):
            bd_ref[c, pl.ds(bi, nc, stride=nb), :] = drive[:, c * LANES:(c + 1) * LANES]

    lam = lam_ref[...]

    def make_step(base):
        a_re = [lam[:, (base + c) * LANES:(base + c + 1) * LANES] for c in range(q)]
        a_im = [lam[:, (base + q + c) * LANES:(base + q + c + 1) * LANES] for c in range(q)]

        def step(k, carry):
            row = pl.multiple_of(k * nb, nb)
            out = []
            for c in range(q):
                s_re, s_im = carry[2 * c], carry[2 * c + 1]
                d_re = bd_ref[base + c, pl.ds(row, nb), :]
                d_im = bd_ref[base + q + c, pl.ds(row, nb), :]
                bd_ref[base + c, pl.ds(row, nb), :] = s_re
                bd_ref[base + q + c, pl.ds(row, nb), :] = s_im
                out.append(a_re[c] * s_re - a_im[c] * s_im + d_re)
                out.append(a_re[c] * s_im + a_im[c] * s_re + d_im)
            return tuple(out)

        return step

    zero = tuple(jnp.zeros((nb, LANES), F32) for _ in range(2 * q))
    fwd = make_step(0)
    lax.fori_loop(0, nc, fwd, zero)
    rev = make_step(2 * q)
    carry = lax.fori_loop(0, n_ctx_chunks, lambda i, cr: rev(n_ctx_chunks - 1 - i, cr), zero)
    lax.fori_loop(0, nc - n_ctx_chunks, lambda i, cr: rev(nc - 1 - i, cr), carry)

    for bi in range(nb):
        states = jnp.concatenate([bd_ref[c, pl.ds(bi, nc, stride=nb), :] for c in range(n_state_slabs)], axis=1)
        y = (jnp.dot(chunk_rows(bi), m_ref[...], preferred_element_type=F32)
             + jnp.dot(states.astype(_MXU_DTYPE), wout_ref[...], preferred_element_type=F32))
        for t in range(chunk):
            y_ref[bi, pl.ds(t, nc, stride=chunk), :] = y[:, t * LANES:(t + 1) * LANES]


def _ssm_scan(u, win, m, wout, lam_t, ctx_len, first_slab):
    b, nt, d = u.shape
    chunk = SSM_CHUNK
    nb = 4 if b % 4 == 0 else b
    n_slabs = d // LANES
    state_w = win.shape[-1]
    nc = nt // chunk
    blk = pl.BlockSpec((nb, nt, LANES), lambda j, bb: (bb, 0, j))
    table = lambda j, bb: (first_slab + j, 0, 0)
    return pl.pallas_call(
        functools.partial(_ssm_kernel, chunk=chunk, n_ctx_chunks=ctx_len // chunk),
        grid=(n_slabs, b // nb),
        in_specs=[blk,
                  pl.BlockSpec((None,) + win.shape[1:], table),
                  pl.BlockSpec((None,) + m.shape[1:], table),
                  pl.BlockSpec((None,) + wout.shape[1:], table),
                  pl.BlockSpec((None, 1, state_w), table)],
        out_specs=blk,
        out_shape=jax.ShapeDtypeStruct((b, nt, d), F32),
        scratch_shapes=[pltpu.VMEM((state_w // LANES, nc * nb, LANES), F32)],
        compiler_params=_params("parallel", "parallel"),
        name="ssm_scan",
    )(u, win, m, wout, lam_t)


def _ssm_out_kernel(y_ref, u_ref, d_ref, wa_ref, wb_ref, h_ref, mod_ref, o_ref):
    x = y_ref[...] + d_ref[...] * u_ref[...]
    z = 0.5 * x * (1.0 + jnp.tanh(math.sqrt(2.0 / math.pi) * (x + 0.044715 * (x * x * x))))
    z = z.astype(_MXU_DTYPE)
    a = jnp.dot(z, wa_ref[...], preferred_element_type=F32)
    g = jnp.dot(z, wb_ref[...], preferred_element_type=F32)
    o_ref[...] = h_ref[...] + mod_ref[2:3, :] * (a / (1.0 + jnp.exp(-g)))


def _ssm_out(y, u, d_skip, wa, wb, h, mod_tab):
    b, nt, d = h.shape
    tm = ROW_TILE
    blk = pl.BlockSpec((None, tm, d), lambda bb, i: (bb, i, 0))
    full = lambda shape: pl.BlockSpec(shape, lambda bb, i: (0,) * len(shape))
    return pl.pallas_call(
        _ssm_out_kernel,
        grid=(b, nt // tm),
        in_specs=[blk, blk, full((1, d)), full(wa.shape), full(wb.shape), blk, _mod_spec(d)],
        out_specs=blk,
        out_shape=jax.ShapeDtypeStruct((b, nt, d), F32),
        compiler_params=_params("parallel", "parallel"),
        name="ssm_out",
    )(y, u, d_skip, wa, wb, h, mod_tab)


def _expand_kernel(x_ref, o_ref, *, lo, row_div, gpt):
    rows, cc = x_ref.shape
    oc = o_ref.shape[1]
    ci = lax.broadcasted_iota(jnp.int32, (cc, oc), 0)
    oi = lax.broadcasted_iota(jnp.int32, (cc, oc), 1)
    pick = (ci == (oi // (gpt * lo)) * lo + oi % lo).astype(_MXU_DTYPE)
    y = jnp.dot(x_ref[...].astype(_MXU_DTYPE), pick, preferred_element_type=F32)
    ri = lax.broadcasted_iota(jnp.int32, (rows, oc), 0) + pl.program_id(1) * rows
    oj = lax.broadcasted_iota(jnp.int32, (rows, oc), 1)
    keep = ((ri // row_div) % gpt) == ((oj // lo) % gpt)
    o_ref[...] = jnp.where(keep, y, 0.0).astype(o_ref.dtype)


def _expand_block_diag(x, lo, row_div, gpt):
    n, rows, cc = x.shape
    tr = ROW_TILE
    return pl.pallas_call(
        functools.partial(_expand_kernel, lo=lo, row_div=row_div, gpt=gpt),
        grid=(n, rows // tr),
        in_specs=[pl.BlockSpec((None, tr, cc), lambda i, r: (i, r, 0))],
        out_specs=pl.BlockSpec((None, tr, gpt * cc), lambda i, r: (i, r, 0)),
        out_shape=jax.ShapeDtypeStruct((n, rows, gpt * cc), _MXU_DTYPE),
        compiler_params=_params("parallel", "parallel"),
        name="ssm_expand",
    )(x)


def _ssm_tables(a_re, a_im, log_dt, b_re, b_im, c_re, c_im, chunk):
    n_layers, _, g_total, p = a_re.shape
    gpt = LANES // SSM_GROUP_CH
    n_slabs = g_total // gpt
    n = n_layers * n_slabs
    gc = SSM_GROUP_CH
    lre = jnp.minimum(a_re, -1e-4)
    lim = a_im
    dt = jnp.exp(log_dt)[..., None]
    steps = jnp.arange(chunk + 1, dtype=F32)[:, None, None, None, None]
    mag = jnp.exp(steps * (lre * dt))
    pw_re = mag * jnp.cos(steps * (lim * dt))
    pw_im = mag * jnp.sin(steps * (lim * dt))
    nr = pw_re[1] - 1.0
    ni = pw_im[1]
    den = lre * lre + lim * lim
    coef_re = (nr * lre + ni * lim) / den
    coef_im = (ni * lre - nr * lim) / den
    bb_re = coef_re[..., None] * b_re - coef_im[..., None] * b_im
    bb_im = coef_re[..., None] * b_im + coef_im[..., None] * b_re

    ein = functools.partial(jnp.einsum, precision=_HIGHEST)
    cp_re = ein('jlxgp,lxgcp->jlxgcp', pw_re, c_re) - ein('jlxgp,lxgcp->jlxgcp', pw_im, c_im)
    cp_im = ein('jlxgp,lxgcp->jlxgcp', pw_re, c_im) + ein('jlxgp,lxgcp->jlxgcp', pw_im, c_re)
    taps = ein('jlxgcp,lxgpd->jlxgdc', cp_re, bb_re) - ein('jlxgcp,lxgpd->jlxgdc', cp_im, bb_im)

    def tap(s, t):
        if t > s:
            return taps[t - s, :, 0]
        if t < s:
            return taps[s - t, :, 1]
        return taps[0, :, 0] + taps[0, :, 1]

    kst = jnp.stack([jnp.stack([tap(s, t) for t in range(chunk)], axis=1) for s in range(chunk)], axis=1)
    kst = kst.reshape(n_layers, chunk, chunk, n_slabs, gpt, gc, gc)
    kst = jnp.transpose(kst, (0, 3, 1, 4, 5, 2, 6)).reshape(n, chunk * LANES, chunk * gc)
    m = _expand_block_diag(kst, gc, gc, gpt)

    def drive(pw_r, pw_i, br, bi):
        re = ein('slgp,lgpd->slgdp', pw_r, br) - ein('slgp,lgpd->slgdp', pw_i, bi)
        im = ein('slgp,lgpd->slgdp', pw_r, bi) + ein('slgp,lgpd->slgdp', pw_i, br)
        return re, im

    f_re, f_im = drive(pw_re[:chunk, :, 0][::-1], pw_im[:chunk, :, 0][::-1], bb_re[:, 0], bb_im[:, 0])
    r_re, r_im = drive(pw_re[:chunk, :, 1], pw_im[:chunk, :, 1], bb_re[:, 1], bb_im[:, 1])
    win = jnp.stack([f_re, f_im, r_re, r_im], axis=0)
    win = win.reshape(4, chunk, n_layers, n_slabs, gpt, gc, p)
    win = jnp.transpose(win, (2, 3, 1, 4, 5, 0, 6)).reshape(n, chunk * LANES, 4 * p)
    win = _expand_block_diag(win, p, gc, gpt)

    of_re, of_im = cp_re[1:, :, 0], cp_im[1:, :, 0]
    or_re, or_im = cp_re[1:, :, 1][::-1], cp_im[1:, :, 1][::-1]
    wout = jnp.stack([of_re, -of_im, or_re, -or_im], axis=0)
    wout = wout.reshape(4, chunk, n_layers, n_slabs, gpt, gc, p)
    wout = jnp.transpose(wout, (2, 3, 0, 4, 6, 1, 5)).reshape(n, 4 * gpt * p, chunk * gc)
    wout = _expand_block_diag(wout, gc, p, gpt)

    lam_t = jnp.stack([pw_re[chunk, :, 0], pw_im[chunk, :, 0], pw_re[chunk, :, 1], pw_im[chunk, :, 1]], axis=0)
    lam_t = lam_t.reshape(4, n_layers, n_slabs, gpt * p)
    lam_t = jnp.transpose(lam_t, (1, 2, 0, 3)).reshape(n, 1, 4 * gpt * p)
    return win, m, wout, lam_t


def _router_kernel(h_ref, mod_ref, g_ref, wr_ref, br_ref, xt_ref, cmb_ref):
    xt = _norm_mod(h_ref[...], g_ref[...], mod_ref[3:4, :], mod_ref[4:5, :])
    xt_ref[...] = xt.astype(xt_ref.dtype)
    logits = jnp.dot(xt, wr_ref[...], preferred_element_type=F32, precision=_HIGHEST) + br_ref[...]
    lane = lax.broadcasted_iota(jnp.int32, (1, LANES), 1)
    lane_f = lane.astype(F32)
    neg = -jnp.inf
    big = 1e9
    gmask = (lane >= MOE_EXPERTS) & (lane < MOE_EXPERTS + MOE_GROUPS)
    gl = jnp.where(gmask, logits, neg)
    gmax = jnp.max(gl, axis=-1, keepdims=True)
    gidx = jnp.min(jnp.where(gl == gmax, lane_f, big), axis=-1, keepdims=True) - MOE_EXPERTS
    p_group = 1.0 / jnp.sum(jnp.where(gmask, jnp.exp(gl - gmax), 0.0), axis=-1, keepdims=True)
    in_group = (lane < MOE_EXPERTS) & ((lane // MOE_EPG).astype(F32) == gidx)
    el = jnp.where(in_group, logits, neg)
    v1 = jnp.max(el, axis=-1, keepdims=True)
    i1 = jnp.min(jnp.where(el == v1, lane_f, big), axis=-1, keepdims=True)
    el2 = jnp.where(lane_f == i1, neg, el)
    v2 = jnp.max(el2, axis=-1, keepdims=True)
    i2 = jnp.min(jnp.where(el2 == v2, lane_f, big), axis=-1, keepdims=True)
    t = jnp.exp(v2 - v1)
    w1 = p_group / (1.0 + t)
    w2 = p_group * t / (1.0 + t)
    cmb_ref[...] = jnp.where(lane_f == i1, w1, 0.0) + jnp.where(lane_f == i2, w2, 0.0)


def _router(h, mod_tab, g, wr, br):
    b, nt, d = h.shape
    tm = ROW_TILE
    full = lambda shape: pl.BlockSpec(shape, lambda bb, i: (0,) * len(shape))
    return pl.pallas_call(
        _router_kernel,
        grid=(b, nt // tm),
        in_specs=[pl.BlockSpec((None, tm, d), lambda bb, i: (bb, i, 0)), _mod_spec(d),
                  full((1, d)), full(wr.shape), full(br.shape)],
        out_specs=[pl.BlockSpec((None, tm, d), lambda bb, i: (bb, i, 0)),
                   pl.BlockSpec((None, tm, LANES), lambda bb, i: (bb, i, 0))],
        out_shape=[jax.ShapeDtypeStruct((b, nt, d), _MXU_DTYPE),
                   jax.ShapeDtypeStruct((b, nt, LANES), F32)],
        compiler_params=_params("parallel", "parallel"),
        name="moe_router",
    )(h, mod_tab, g, wr, br)


def _split_terms(x, n):
    terms = []
    for _ in range(n - 1):
        t = x.astype(_MXU_DTYPE)
        terms.append(t)
        x = x - t.astype(F32)
    terms.append(x.astype(_MXU_DTYPE))
    return terms


def _experts_kernel(xt_ref, cmb_ref, wg_ref, wu_ref, wd_ref, h_ref, mod_ref, o_ref,
                    xs_ref, cs_ref, acc_ref, pos_ref, seg_ref, *, ctx_len, nt, window):
    i = pl.program_id(0)
    e = pl.program_id(1)
    sb = xt_ref.shape[0]
    d = xt_ref.shape[1]

    @pl.when(e == 0)
    def _():
        cmb = cmb_ref[...]
        lane = lax.broadcasted_iota(jnp.int32, (1, LANES), 1)
        routed = cmb != 0.0
        goh = jnp.zeros((sb, LANES), F32)
        for g in range(MOE_GROUPS):
            in_g = routed & (lane >= g * MOE_EPG) & (lane < (g + 1) * MOE_EPG)
            hit = jnp.max(jnp.where(in_g, 1.0, 0.0), axis=-1, keepdims=True)
            goh = goh + jnp.where(lane == g, hit, 0.0)
        r_i = lax.broadcasted_iota(jnp.int32, (sb, sb), 0)
        c_i = lax.broadcasted_iota(jnp.int32, (sb, sb), 1)
        earlier = (c_i < r_i).astype(_MXU_DTYPE)
        before = jnp.dot(earlier, goh.astype(_MXU_DTYPE), preferred_element_type=F32)
        cnt = jnp.sum(goh, axis=0, keepdims=True)
        off = jnp.zeros((1, LANES), F32)
        run = jnp.zeros((1, 1), F32)
        for g in range(MOE_GROUPS):
            off = off + jnp.where(lane == g, run, 0.0)
            run = run + jnp.sum(jnp.where(lane == g, cnt, 0.0), axis=-1, keepdims=True)
        pos = jnp.sum(goh * (off + before), axis=-1, keepdims=True)
        pos_b = jnp.broadcast_to(pos, (sb, LANES))
        pos_ref[...] = pos_b
        pos_row = pos_b.T[0:1, :].astype(jnp.int32)
        perm = (r_i == pos_row).astype(_MXU_DTYPE)
        xs_ref[0:sb, :] = jnp.dot(perm, xt_ref[...], preferred_element_type=F32).astype(xs_ref.dtype)
        cs = jnp.zeros((sb, LANES), F32)
        for term in _split_terms(cmb, 3):
            cs = cs + jnp.dot(perm, term, preferred_element_type=F32)
        cs_ref[0:sb, :] = cs
        xs_ref[sb:sb + window, :] = jnp.zeros((window, d), xs_ref.dtype)
        cs_ref[sb:sb + window, :] = jnp.zeros((window, LANES), F32)
        acc_ref[...] = jnp.zeros_like(acc_ref)
        off_i = off.astype(jnp.int32)
        cnt_i = cnt.astype(jnp.int32)
        for g in range(MOE_GROUPS):
            seg_ref[g] = off_i[0, g]
            seg_ref[MOE_GROUPS + g] = cnt_i[0, g]

    g = e // MOE_EPG
    start = seg_ref[g]
    count = seg_ref[MOE_GROUPS + g]
    first = (start // ROW_ALIGN) * ROW_ALIGN
    n_win = (start - first + count + window - 1) // window
    n_win = jnp.where(count > 0, n_win, 0)
    wg = wg_ref[...]
    wu = wu_ref[...]
    wd = wd_ref[...]
    lane = lax.broadcasted_iota(jnp.int32, (1, LANES), 1)

    def window_step(k, carry):
        r0 = pl.multiple_of(first + k * window, ROW_ALIGN)
        x = xs_ref[pl.ds(r0, window), :]
        gate = jnp.dot(x, wg, preferred_element_type=F32)
        up = jnp.dot(x, wu, preferred_element_type=F32)
        w = jnp.sum(jnp.where(lane == e, cs_ref[pl.ds(r0, window), :], 0.0), axis=-1, keepdims=True)
        hid = (gate / (1.0 + jnp.exp(-gate))) * up * w
        acc_ref[pl.ds(r0, window), :] += jnp.dot(hid.astype(_MXU_DTYPE), wd, preferred_element_type=F32)
        return carry

    lax.fori_loop(0, n_win, window_step, 0)

    @pl.when(e == pl.num_programs(1) - 1)
    def _():
        c_i = lax.broadcasted_iota(jnp.int32, (sb, sb), 1)
        unperm = (c_i == pos_ref[:, 0:1].astype(jnp.int32)).astype(_MXU_DTYPE)
        y = jnp.zeros((sb, d), F32)
        for term in _split_terms(acc_ref[0:sb, :], 2):
            y = y + jnp.dot(unperm, term, preferred_element_type=F32)
        row = (i * sb) % nt + lax.broadcasted_iota(jnp.int32, (sb, 1), 0)
        gate_row = jnp.where(row < ctx_len, mod_ref[0, 5:6, :], mod_ref[1, 5:6, :])
        o_ref[...] = h_ref[...] + gate_row * y


def _experts(xt, cmb, wg, wu, wd, h, mod_tab, ctx_len):
    b, nt, d = h.shape
    sb = 1152 if nt % 1152 == 0 else ROW_TILE
    window = 320 if sb == 1152 else 96
    per_b = nt // sb
    n_exp, _, hid = wg.shape
    rows = b * nt
    blk = lambda w: pl.BlockSpec((sb, w), lambda i, e: (i, 0))
    out = pl.pallas_call(
        functools.partial(_experts_kernel, ctx_len=ctx_len, nt=nt, window=window),
        grid=(rows // sb, n_exp),
        in_specs=[blk(d), blk(LANES),
                  pl.BlockSpec((None, d, hid), lambda i, e: (e, 0, 0)),
                  pl.BlockSpec((None, d, hid), lambda i, e: (e, 0, 0)),
                  pl.BlockSpec((None, hid, d), lambda i, e: (e, 0, 0)),
                  blk(d),
                  pl.BlockSpec((None, 2, 8, d), lambda i, e: (i // per_b, 0, 0, 0))],
        out_specs=blk(d),
        out_shape=jax.ShapeDtypeStruct((rows, d), F32),
        scratch_shapes=[pltpu.VMEM((sb + window, d), _MXU_DTYPE), pltpu.VMEM((sb + window, LANES), F32),
                        pltpu.VMEM((sb + window, d), F32), pltpu.VMEM((sb, LANES), F32),
                        pltpu.SMEM((2 * MOE_GROUPS,), jnp.int32)],
        compiler_params=_params("parallel", "arbitrary"),
        name="moe_experts",
    )(xt.reshape(rows, d), cmb.reshape(rows, LANES), wg, wu, wd, h.reshape(rows, d), mod_tab)
    return out.reshape(b, nt, d)


def _final_kernel(h_ref, g_ref, o_ref):
    h = h_ref[...]
    o_ref[...] = h * lax.rsqrt(jnp.mean(h * h, axis=-1, keepdims=True) + RMS_EPS) * g_ref[...]


def _final_norm(h, g, ctx_len):
    b, nt, d = h.shape
    tm = ROW_TILE
    skip = ctx_len // tm
    return pl.pallas_call(
        _final_kernel,
        grid=(b, (nt - ctx_len) // tm),
        in_specs=[pl.BlockSpec((None, tm, d), lambda bb, i: (bb, i + skip, 0)),
                  pl.BlockSpec((1, d), lambda bb, i: (0, 0))],
        out_specs=pl.BlockSpec((None, tm, d), lambda bb, i: (bb, i, 0)),
        out_shape=jax.ShapeDtypeStruct((b, nt - ctx_len, d), F32),
        compiler_params=_params("parallel", "parallel"),
        name="final_norm",
    )(h, g)


def _rope_tables(seq_len, ctx_len):
    n_rows = seq_len // GRID_W
    rows = jnp.repeat(jnp.arange(n_rows, dtype=F32), GRID_W)
    cols = jnp.tile(jnp.arange(GRID_W, dtype=F32), n_rows)
    half = HEAD_DIM // 2
    inv = 1.0 / (ROPE_BASE ** (jnp.arange(0, half, 2, dtype=F32) / half))
    ang_r = rows[:, None] * inv
    ang_c = cols[:, None] * inv
    ang = jnp.concatenate([ang_r, ang_r, ang_c, ang_c], axis=-1)
    ang = jnp.concatenate([jnp.zeros((ctx_len, HEAD_DIM), F32), ang], axis=0)
    ang = jnp.tile(ang, (1, LANES // HEAD_DIM))
    return jnp.cos(ang), jnp.sin(ang)


def _pad_row(v, width=LANES):
    return jnp.pad(v, (0, width - v.shape[0]))[None, :]


def kernel(x, c, ctx, c_ctx, mod_w, mod_b, norm1_g, norm2_g, final_g, attn_w_in, attn_w_out, attn_q_norm_g, attn_k_norm_g, diff_lambda_q1, diff_lambda_k1, diff_lambda_q2, diff_lambda_k2, diff_subln_g, ssm_a_re, ssm_a_im, ssm_log_dt, ssm_b_re, ssm_b_im, ssm_c_re, ssm_c_im, ssm_d, ssm_glu_w_a, ssm_glu_w_b, moe_group_w, moe_group_b, moe_router_w, moe_router_b, moe_w_gate, moe_w_up, moe_w_down):
    bsz, seq, d = x.shape
    ctx_len = ctx.shape[1]
    depth = mod_w.shape[0]
    assert ctx_len == ROW_TILE and seq % ROW_TILE == 0 and seq % GRID_W == 0

    h = jnp.concatenate([ctx, x], axis=1)

    mod_rows = 16
    c_all = jnp.concatenate([c, c_ctx[None, :], jnp.zeros((mod_rows - bsz - 1, d), F32)], axis=0)
    mods = _modulation(c_all, mod_w, mod_b).reshape(depth, mod_rows, 6, d)
    mods = jnp.pad(mods, ((0, 0), (0, 0), (0, 2), (0, 0)))
    mod_tabs = jnp.stack([jnp.broadcast_to(mods[:, bsz:bsz + 1], (depth, bsz, 8, d)), mods[:, :bsz]], axis=2)

    cos, sin = _rope_tables(seq, ctx_len)
    cast = lambda w: w.astype(_MXU_DTYPE)
    ssm_tabs = _ssm_tables(ssm_a_re, ssm_a_im, ssm_log_dt, ssm_b_re, ssm_b_im, ssm_c_re, ssm_c_im, SSM_CHUNK)

    for layer in range(depth):
        mod_tab = mod_tabs[layer]
        i = layer // 2
        if layer % 2 == 0:
            lambda_init = 0.8 - 0.6 * math.exp(-0.3 * layer)
            qkv = _attn_proj(h, mod_tab, norm1_g[layer][None, :], cast(attn_w_in[i]),
                             jnp.tile(attn_q_norm_g[i], 2)[None, :], jnp.tile(attn_k_norm_g[i], 2)[None, :],
                             cos, sin)
            lam_rows = jnp.concatenate([_pad_row(diff_lambda_q1[i]), _pad_row(diff_lambda_k1[i]),
                                        _pad_row(diff_lambda_q2[i]), _pad_row(diff_lambda_k2[i]),
                                        jnp.zeros((4, LANES), F32)], axis=0)
            h = _attention(qkv, lam_rows, diff_subln_g[i][None, :], cast(attn_w_out[i]), h, mod_tab,
                           lambda_init, ctx_len)
        else:
            u = _norm1(h, mod_tab, norm1_g[layer][None, :])
            y = _ssm_scan(u, *ssm_tabs, ctx_len, i * (d // LANES))
            h = _ssm_out(y, u, ssm_d[i][None, :], cast(ssm_glu_w_a[i]), cast(ssm_glu_w_b[i]), h, mod_tab)

        wr = jnp.concatenate([jnp.transpose(moe_router_w[layer], (1, 0, 2)).reshape(d, MOE_EXPERTS),
                              moe_group_w[layer],
                              jnp.zeros((d, LANES - MOE_EXPERTS - MOE_GROUPS), F32)], axis=1)
        br = _pad_row(jnp.concatenate([moe_router_b[layer].reshape(-1), moe_group_b[layer]]))
        xt, cmb = _router(h, mod_tab, norm2_g[layer][None, :], wr, br)
        h = _experts(xt, cmb, cast(moe_w_gate[layer]), cast(moe_w_up[layer]), cast(moe_w_down[layer]),
                     h, mod_tab, ctx_len)

    return _final_norm(h, final_g[None, :], ctx_len)
```

```python
import functools
import math

import jax
import jax.numpy as jnp
from jax import lax
from jax.experimental import pallas as pl
from jax.experimental.pallas import tpu as pltpu

F32 = jnp.float32
_MXU_DTYPE = jnp.bfloat16
_HIGHEST = lax.Precision.HIGHEST

LANES = 128
HEAD_DIM = 64
GRID_W = 64
ROPE_BASE = 10000.0
GQA_Q_HEADS = 8
GQA_GROUP = 4
DIFF_HEADS = 4
GQA_Q_W = 512
GQA_KV_W = 128
DIFF_QK_W = 512
DIFF_V_W = 512
SSM_GROUP_CH = 16
SSM_STATE = 64
MOE_GROUPS = 4
MOE_EPG = 8
MOE_EXPERTS = 32
RMS_EPS = 1e-6
SSM_CHUNK = 8
ROW_TILE = 256
ROW_ALIGN = 16
VMEM_LIMIT = 56 * 1024 * 1024


def _params(*sem):
    return pltpu.CompilerParams(dimension_semantics=sem, vmem_limit_bytes=VMEM_LIMIT)


def _norm_mod(h, g, shift, scale):
    y = h * lax.rsqrt(jnp.mean(h * h, axis=-1, keepdims=True) + RMS_EPS) * g
    return y * (1.0 + scale) + shift


def _mm(a, b):
    return jnp.dot(a.astype(_MXU_DTYPE), b.astype(_MXU_DTYPE), preferred_element_type=F32)


def _mod_kernel(c_ref, w_ref, b_ref, o_ref):
    c = c_ref[...]
    a = c / (1.0 + jnp.exp(-c))
    o_ref[...] = jnp.dot(a, w_ref[...], preferred_element_type=F32, precision=_HIGHEST) + b_ref[...]


def _modulation(c_all, mod_w, mod_b):
    depth, d, n = mod_w.shape
    rows = c_all.shape[0]
    tn = 1536
    return pl.pallas_call(
        _mod_kernel,
        grid=(depth, n // tn),
        in_specs=[pl.BlockSpec((rows, d), lambda l, j: (0, 0)),
                  pl.BlockSpec((None, d, tn), lambda l, j: (l, 0, j)),
                  pl.BlockSpec((None, 1, tn), lambda l, j: (l, 0, j))],
        out_specs=pl.BlockSpec((None, rows, tn), lambda l, j: (l, 0, j)),
        out_shape=jax.ShapeDtypeStruct((depth, rows, n), F32),
        compiler_params=_params("parallel", "parallel"),
        name="modulation",
    )(c_all, mod_w, mod_b.reshape(depth, 1, n))


def _mod_spec(d):
    return pl.BlockSpec((None, None, 8, d), lambda b, i: (b, jnp.minimum(i, 1), 0, 0))


def _attn_proj_kernel(h_ref, mod_ref, g_ref, w_ref, gq_ref, gk_ref, cos_ref, sin_ref,
                      qa_ref, ka_ref, va_ref, qb_ref, kb_ref, vb_ref):
    xn = _norm_mod(h_ref[...], g_ref[...], mod_ref[0:1, :], mod_ref[1:2, :])
    hp = _mm(xn, w_ref[...])
    cos = cos_ref[...]
    sin = sin_ref[...]
    lane = lax.broadcasted_iota(jnp.int32, (1, LANES), 1)
    first_half = (lane % 32) < 16
    r = lax.broadcasted_iota(jnp.int32, (LANES, LANES), 0) // HEAD_DIM
    c = lax.broadcasted_iota(jnp.int32, (LANES, LANES), 1) // HEAD_DIM
    same_head = (r == c).astype(_MXU_DTYPE)

    def rope(x):
        rot = jnp.where(first_half, -pltpu.roll(x, LANES - 16, 1), pltpu.roll(x, 16, 1))
        return x * cos + rot * sin

    def head_norm(x, g):
        ss = jnp.dot((x * x).astype(_MXU_DTYPE), same_head, preferred_element_type=F32)
        return x * lax.rsqrt(ss * (1.0 / HEAD_DIM) + RMS_EPS) * g

    scale = HEAD_DIM ** -0.5 * math.log2(math.e)
    o = 0
    for s in range(GQA_Q_W // LANES):
        x = hp[:, o + s * LANES:o + (s + 1) * LANES]
        qa_ref[:, s * LANES:(s + 1) * LANES] = (rope(head_norm(x, gq_ref[...])) * scale).astype(qa_ref.dtype)
    o += GQA_Q_W
    ka_ref[...] = rope(head_norm(hp[:, o:o + LANES], gk_ref[...])).astype(ka_ref.dtype)
    o += GQA_KV_W
    va_ref[...] = hp[:, o:o + LANES].astype(va_ref.dtype)
    o += GQA_KV_W
    for s in range(DIFF_QK_W // LANES):
        x = hp[:, o + s * LANES:o + (s + 1) * LANES]
        qb_ref[:, s * LANES:(s + 1) * LANES] = (rope(x) * scale).astype(qb_ref.dtype)
    o += DIFF_QK_W
    for s in range(DIFF_QK_W // LANES):
        x = hp[:, o + s * LANES:o + (s + 1) * LANES]
        kb_ref[:, s * LANES:(s + 1) * LANES] = rope(x).astype(kb_ref.dtype)
    o += DIFF_QK_W
    vb_ref[...] = hp[:, o:o + DIFF_V_W].astype(vb_ref.dtype)


def _attn_proj(h, mod_tab, g, w_in, gq, gk, cos, sin):
    b, nt, d = h.shape
    tm = ROW_TILE
    widths = (GQA_Q_W, GQA_KV_W, GQA_KV_W, DIFF_QK_W, DIFF_QK_W, DIFF_V_W)
    full = lambda shape: pl.BlockSpec(shape, lambda bb, i: (0,) * len(shape))
    return pl.pallas_call(
        _attn_proj_kernel,
        grid=(b, nt // tm),
        in_specs=[pl.BlockSpec((None, tm, d), lambda bb, i: (bb, i, 0)),
                  _mod_spec(d),
                  full((1, d)),
                  full(w_in.shape),
                  full((1, LANES)),
                  full((1, LANES)),
                  pl.BlockSpec((tm, LANES), lambda bb, i: (i, 0)),
                  pl.BlockSpec((tm, LANES), lambda bb, i: (i, 0))],
        out_specs=[pl.BlockSpec((None, tm, w), lambda bb, i: (bb, i, 0)) for w in widths],
        out_shape=[jax.ShapeDtypeStruct((b, nt, w), _MXU_DTYPE) for w in widths],
        compiler_params=_params("parallel", "parallel"),
        name="attn_proj",
    )(h, mod_tab, g, w_in, gq, gk, cos, sin)


def _softmax_pv(q, k, v):
    s = lax.dot_general(q, k, (((1,), (1,)), ((), ())), preferred_element_type=F32)
    e = jnp.exp2(s - jnp.max(s, axis=-1, keepdims=True))
    l = jnp.sum(e, axis=-1, keepdims=True)
    return jnp.dot(e.astype(v.dtype), v, preferred_element_type=F32) / l


def _attn_kernel(qa_ref, qb_ref, ka_ref, va_ref, kb_ref, vb_ref, lam_ref, sg_ref, wo_ref, h_ref, mod_ref,
                 o_ref, mrg_ref, *, lambda_init, ctx_len):
    lv = lam_ref[...]
    lam = (jnp.exp(jnp.sum(lv[0:1] * lv[1:2], axis=-1, keepdims=True))
           - jnp.exp(jnp.sum(lv[2:3] * lv[3:4], axis=-1, keepdims=True)) + lambda_init)

    def run(nk):
        for h in range(GQA_Q_HEADS):
            g = h // GQA_GROUP
            o = _softmax_pv(qa_ref[:, h * HEAD_DIM:(h + 1) * HEAD_DIM],
                            ka_ref[0:nk, g * HEAD_DIM:(g + 1) * HEAD_DIM],
                            va_ref[0:nk, g * HEAD_DIM:(g + 1) * HEAD_DIM])
            mrg_ref[:, h * HEAD_DIM:(h + 1) * HEAD_DIM] = o.astype(mrg_ref.dtype)
        for h in range(DIFF_HEADS):
            c0 = h * 2 * HEAD_DIM
            v = vb_ref[0:nk, c0:c0 + 2 * HEAD_DIM]
            o1 = _softmax_pv(qb_ref[:, c0:c0 + HEAD_DIM], kb_ref[0:nk, c0:c0 + HEAD_DIM], v)
            o2 = _softmax_pv(qb_ref[:, c0 + HEAD_DIM:c0 + 2 * HEAD_DIM],
                             kb_ref[0:nk, c0 + HEAD_DIM:c0 + 2 * HEAD_DIM], v)
            o = o1 - lam * o2
            o = o * lax.rsqrt(jnp.mean(o * o, axis=-1, keepdims=True) + RMS_EPS) * sg_ref[...]
            o = o * (1.0 - lambda_init)
            mrg_ref[:, GQA_Q_W + c0:GQA_Q_W + c0 + 2 * HEAD_DIM] = o.astype(mrg_ref.dtype)

    i = pl.program_id(1)

    @pl.when(i == 0)
    def _():
        run(ctx_len)

    @pl.when(i > 0)
    def _():
        run(ka_ref.shape[0])

    y = jnp.dot(mrg_ref[...], wo_ref[...], preferred_element_type=F32)
    o_ref[...] = h_ref[...] + mod_ref[2:3, :] * y


def _attention(qkv, lam_rows, subln_g, w_out, h, mod_tab, lambda_init, ctx_len):
    qa, ka, va, qb, kb, vb = qkv
    b, nt, d = h.shape
    tq = ROW_TILE
    assert ctx_len == tq
    blk = lambda w: pl.BlockSpec((None, tq, w), lambda bb, i: (bb, i, 0))
    per_batch = lambda w: pl.BlockSpec((None, nt, w), lambda bb, i: (bb, 0, 0))
    full = lambda shape: pl.BlockSpec(shape, lambda bb, i: (0,) * len(shape))
    return pl.pallas_call(
        functools.partial(_attn_kernel, lambda_init=lambda_init, ctx_len=ctx_len),
        grid=(b, nt // tq),
        in_specs=[blk(GQA_Q_W), blk(DIFF_QK_W), per_batch(GQA_KV_W), per_batch(GQA_KV_W),
                  per_batch(DIFF_QK_W), per_batch(DIFF_V_W),
                  full((8, LANES)), full((1, LANES)), full(w_out.shape), blk(d), _mod_spec(d)],
        out_specs=blk(d),
        out_shape=jax.ShapeDtypeStruct((b, nt, d), F32),
        scratch_shapes=[pltpu.VMEM((tq, GQA_Q_W + DIFF_V_W), _MXU_DTYPE)],
        compiler_params=_params("parallel", "parallel"),
        name="attention",
    )(qa, qb, ka, va, kb, vb, lam_rows, subln_g, w_out, h, mod_tab)


def _norm1_kernel(h_ref, mod_ref, g_ref, o_ref):
    o_ref[...] = _norm_mod(h_ref[...], g_ref[...], mod_ref[0:1, :], mod_ref[1:2, :])


def _norm1(h, mod_tab, g):
    b, nt, d = h.shape
    tm = ROW_TILE
    blk = pl.BlockSpec((None, tm, d), lambda bb, i: (bb, i, 0))
    return pl.pallas_call(
        _norm1_kernel,
        grid=(b, nt // tm),
        in_specs=[blk, _mod_spec(d), pl.BlockSpec((1, d), lambda bb, i: (0, 0))],
        out_specs=blk,
        out_shape=jax.ShapeDtypeStruct((b, nt, d), F32),
        compiler_params=_params("parallel", "parallel"),
        name="ssm_norm",
    )(h, mod_tab, g)


def _ssm_kernel(u_ref, win_ref, m_ref, wout_ref, lam_ref, y_ref, bd_ref, *, chunk, n_ctx_chunks):
    nb, nt, _ = u_ref.shape
    nc = nt // chunk
    n_state_slabs = bd_ref.shape[0]
    q = n_state_slabs // 4

    def chunk_rows(bi):
        parts = [u_ref[bi, pl.ds(s, nc, stride=chunk), :] for s in range(chunk)]
        return jnp.concatenate(parts, axis=1).astype(_MXU_DTYPE)

    for bi in range(nb):
        drive = jnp.dot(chunk_rows(bi), win_ref[...], preferred_element_type=F32)
        for c in range(n_state_slabs):
            bd_ref[c, pl.ds(bi, nc, stride=nb), :] = drive[:, c * LANES:(c + 1) * LANES]

    lam = lam_ref[...]

    def make_step(base):
        a_re = [lam[:, (base + c) * LANES:(base + c + 1) * LANES] for c in range(q)]
        a_im = [lam[:, (base + q + c) * LANES:(base + q + c + 1) * LANES] for c in range(q)]

        def step(k, carry):
            row = pl.multiple_of(k * nb, nb)
            out = []
            for c in range(q):
                s_re, s_im = carry[2 * c], carry[2 * c + 1]
                d_re = bd_ref[base + c, pl.ds(row, nb), :]
                d_im = bd_ref[base + q + c, pl.ds(row, nb), :]
                bd_ref[base + c, pl.ds(row, nb), :] = s_re
                bd_ref[base + q + c, pl.ds(row, nb), :] = s_im
                out.append(a_re[c] * s_re - a_im[c] * s_im + d_re)
                out.append(a_re[c] * s_im + a_im[c] * s_re + d_im)
            return tuple(out)

        return step

    zero = tuple(jnp.zeros((nb, LANES), F32) for _ in range(2 * q))
    fwd = make_step(0)
    lax.fori_loop(0, nc, fwd, zero)
    rev = make_step(2 * q)
    carry = lax.fori_loop(0, n_ctx_chunks, lambda i, cr: rev(n_ctx_chunks - 1 - i, cr), zero)
    lax.fori_loop(0, nc - n_ctx_chunks, lambda i, cr: rev(nc - 1 - i, cr), carry)

    for bi in range(nb):
        states = jnp.concatenate([bd_ref[c, pl.ds(bi, nc, stride=nb), :] for c in range(n_state_slabs)], axis=1)
        y = (jnp.dot(chunk_rows(bi), m_ref[...], preferred_element_type=F32)
             + jnp.dot(states.astype(_MXU_DTYPE), wout_ref[...], preferred_element_type=F32))
        for t in range(chunk):
            y_ref[bi, pl.ds(t, nc, stride=chunk), :] = y[:, t * LANES:(t + 1) * LANES]


def _ssm_scan(u, win, m, wout, lam_t, ctx_len, first_slab):
    b, nt, d = u.shape
    chunk = SSM_CHUNK
    nb = 4 if b % 4 == 0 else b
    n_slabs = d // LANES
    state_w = win.shape[-1]
    nc = nt // chunk
    blk = pl.BlockSpec((nb, nt, LANES), lambda j, bb: (bb, 0, j))
    table = lambda j, bb: (first_slab + j, 0, 0)
    return pl.pallas_call(
        functools.partial(_ssm_kernel, chunk=chunk, n_ctx_chunks=ctx_len // chunk),
        grid=(n_slabs, b // nb),
        in_specs=[blk,
                  pl.BlockSpec((None,) + win.shape[1:], table),
                  pl.BlockSpec((None,) + m.shape[1:], table),
                  pl.BlockSpec((None,) + wout.shape[1:], table),
                  pl.BlockSpec((None, 1, state_w), table)],
        out_specs=blk,
        out_shape=jax.ShapeDtypeStruct((b, nt, d), F32),
        scratch_shapes=[pltpu.VMEM((state_w // LANES, nc * nb, LANES), F32)],
        compiler_params=_params("parallel", "parallel"),
        name="ssm_scan",
    )(u, win, m, wout, lam_t)


def _ssm_out_kernel(y_ref, u_ref, d_ref, wa_ref, wb_ref, h_ref, mod_ref, o_ref):
    x = y_ref[...] + d_ref[...] * u_ref[...]
    z = 0.5 * x * (1.0 + jnp.tanh(math.sqrt(2.0 / math.pi) * (x + 0.044715 * (x * x * x))))
    z = z.astype(_MXU_DTYPE)
    a = jnp.dot(z, wa_ref[...], preferred_element_type=F32)
    g = jnp.dot(z, wb_ref[...], preferred_element_type=F32)
    o_ref[...] = h_ref[...] + mod_ref[2:3, :] * (a / (1.0 + jnp.exp(-g)))


def _ssm_out(y, u, d_skip, wa, wb, h, mod_tab):
    b, nt, d = h.shape
    tm = ROW_TILE
    blk = pl.BlockSpec((None, tm, d), lambda bb, i: (bb, i, 0))
    full = lambda shape: pl.BlockSpec(shape, lambda bb, i: (0,) * len(shape))
    return pl.pallas_call(
        _ssm_out_kernel,
        grid=(b, nt // tm),
        in_specs=[blk, blk, full((1, d)), full(wa.shape), full(wb.shape), blk, _mod_spec(d)],
        out_specs=blk,
        out_shape=jax.ShapeDtypeStruct((b, nt, d), F32),
        compiler_params=_params("parallel", "parallel"),
        name="ssm_out",
    )(y, u, d_skip, wa, wb, h, mod_tab)


def _split_terms(x, n):
    terms = []
    for _ in range(n - 1):
        t = x.astype(_MXU_DTYPE)
        terms.append(t)
        x = x - t.astype(F32)
    terms.append(x.astype(_MXU_DTYPE))
    return terms


def _ssm_tables_kernel(lam_ref, bt_ref, c_ref, m_ref, win_ref, wout_ref, lamt_ref, *, chunk):
    gpt = LANES // SSM_GROUP_CH
    p = SSM_STATE
    sw = gpt * p
    ci = lax.broadcasted_iota(jnp.int32, (p, sw), 0)
    oi = lax.broadcasted_iota(jnp.int32, (p, sw), 1)
    spread = (ci == oi % p).astype(_MXU_DTYPE)
    ri = lax.broadcasted_iota(jnp.int32, (LANES, sw), 0)
    oj = lax.broadcasted_iota(jnp.int32, (LANES, sw), 1)
    own_group = (ri // SSM_GROUP_CH) == (oj // p)

    def block_diag(x):
        y = jnp.zeros((LANES, sw), F32)
        for term in _split_terms(x, 3):
            y = y + jnp.dot(term, spread, preferred_element_type=F32)
        return jnp.where(own_group, y, 0.0)

    def cmul(ar, ai, br, bi):
        return ar * br - ai * bi, ar * bi + ai * br

    nt_dot = lambda a, b: lax.dot_general(a, b, (((1,), (1,)), ((), ())), precision=_HIGHEST,
                                          preferred_element_type=F32)
    taps, drive, read = [], [], []
    for x in range(2):
        lre = jnp.minimum(lam_ref[x, 0:1, :], -1e-4)
        lim = lam_ref[x, 1:2, :]
        dt = jnp.exp(lam_ref[x, 2:3, :])
        pw = []
        for j in range(chunk + 1):
            mag = jnp.exp(float(j) * (lre * dt))
            ang = float(j) * (lim * dt)
            pw.append((mag * jnp.cos(ang), mag * jnp.sin(ang)))
        nr = pw[1][0] - 1.0
        ni = pw[1][1]
        den = lre * lre + lim * lim
        coef_re = (nr * lre + ni * lim) / den
        coef_im = (ni * lre - nr * lim) / den
        bb = cmul(coef_re, coef_im, block_diag(bt_ref[x, 0]), block_diag(bt_ref[x, 1]))
        cc = (block_diag(c_ref[x, 0]), block_diag(c_ref[x, 1]))
        drive.append([cmul(pw[j][0], pw[j][1], bb[0], bb[1]) for j in range(chunk)])
        read.append([cmul(pw[j][0], pw[j][1], cc[0], cc[1]) for j in range(chunk + 1)])
        taps.append([nt_dot(u_re, cc[0]) - nt_dot(u_im, cc[1]) for u_re, u_im in drive[x]])
        lamt_ref[:, 2 * x * sw:(2 * x + 1) * sw] = pw[chunk][0]
        lamt_ref[:, (2 * x + 1) * sw:(2 * x + 2) * sw] = pw[chunk][1]

    for s in range(chunk):
        rows = slice(s * LANES, (s + 1) * LANES)
        for t in range(chunk):
            if t > s:
                blk = taps[0][t - s]
            elif t < s:
                blk = taps[1][s - t]
            else:
                blk = taps[0][0] + taps[1][0]
            m_ref[rows, t * LANES:(t + 1) * LANES] = blk.astype(m_ref.dtype)
        f_re, f_im = drive[0][chunk - 1 - s]
        r_re, r_im = drive[1][s]
        for q, part in enumerate((f_re, f_im, r_re, r_im)):
            win_ref[rows, q * sw:(q + 1) * sw] = part.astype(win_ref.dtype)
    for t in range(chunk):
        cols = slice(t * LANES, (t + 1) * LANES)
        f_re, f_im = read[0][t + 1]
        r_re, r_im = read[1][chunk - t]
        for q, part in enumerate((f_re, -f_im, r_re, -r_im)):
            wout_ref[q * sw:(q + 1) * sw, cols] = part.T.astype(wout_ref.dtype)


def _ssm_tables(a_re, a_im, log_dt, b_re, b_im, c_re, c_im, chunk):
    n_layers, _, g_total, p = a_re.shape
    gpt = LANES // SSM_GROUP_CH
    n_slabs = g_total // gpt
    n = n_layers * n_slabs
    sw = gpt * p

    def slab_lanes(x):
        return jnp.transpose(x.reshape(n_layers, 2, n_slabs, sw), (0, 2, 1, 3)).reshape(n, 2, sw)

    def slab_rows(x):
        x = x.reshape(n_layers, 2, n_slabs, LANES, p)
        return jnp.transpose(x, (0, 2, 1, 3, 4)).reshape(n, 2, LANES, p)

    lam = jnp.stack([slab_lanes(a_re), slab_lanes(a_im),
                     slab_lanes(jnp.broadcast_to(log_dt[..., None], a_re.shape))], axis=2)
    lam = jnp.pad(lam, ((0, 0), (0, 0), (0, 5), (0, 0)))
    bt = jnp.stack([slab_rows(jnp.swapaxes(b_re, -1, -2)), slab_rows(jnp.swapaxes(b_im, -1, -2))], axis=2)
    ct = jnp.stack([slab_rows(c_re), slab_rows(c_im)], axis=2)
    rows = chunk * LANES
    per_slab = lambda *shape: pl.BlockSpec((None,) + shape, lambda i: (i,) + (0,) * len(shape))
    m, win, wout, lam_t = pl.pallas_call(
        functools.partial(_ssm_tables_kernel, chunk=chunk),
        grid=(n,),
        in_specs=[per_slab(2, 8, sw), per_slab(2, 2, LANES, p), per_slab(2, 2, LANES, p)],
        out_specs=[per_slab(rows, rows), per_slab(rows, 4 * sw), per_slab(4 * sw, rows), per_slab(1, 4 * sw)],
        out_shape=[jax.ShapeDtypeStruct((n, rows, rows), _MXU_DTYPE),
                   jax.ShapeDtypeStruct((n, rows, 4 * sw), _MXU_DTYPE),
                   jax.ShapeDtypeStruct((n, 4 * sw, rows), _MXU_DTYPE),
                   jax.ShapeDtypeStruct((n, 1, 4 * sw), F32)],
        compiler_params=_params("parallel"),
        name="ssm_tables",
    )(lam, bt, ct)
    return win, m, wout, lam_t


def _router_kernel(h_ref, mod_ref, g_ref, wr_ref, br_ref, xt_ref, cmb_ref):
    xt = _norm_mod(h_ref[...], g_ref[...], mod_ref[3:4, :], mod_ref[4:5, :])
    xt_ref[...] = xt.astype(xt_ref.dtype)
    logits = jnp.dot(xt, wr_ref[...], preferred_element_type=F32, precision=_HIGHEST) + br_ref[...]
    lane = lax.broadcasted_iota(jnp.int32, (1, LANES), 1)
    lane_f = lane.astype(F32)
    neg = -jnp.inf
    big = 1e9
    gmask = (lane >= MOE_EXPERTS) & (lane < MOE_EXPERTS + MOE_GROUPS)
    gl = jnp.where(gmask, logits, neg)
    gmax = jnp.max(gl, axis=-1, keepdims=True)
    gidx = jnp.min(jnp.where(gl == gmax, lane_f, big), axis=-1, keepdims=True) - MOE_EXPERTS
    p_group = 1.0 / jnp.sum(jnp.where(gmask, jnp.exp(gl - gmax), 0.0), axis=-1, keepdims=True)
    in_group = (lane < MOE_EXPERTS) & ((lane // MOE_EPG).astype(F32) == gidx)
    el = jnp.where(in_group, logits, neg)
    v1 = jnp.max(el, axis=-1, keepdims=True)
    i1 = jnp.min(jnp.where(el == v1, lane_f, big), axis=-1, keepdims=True)
    el2 = jnp.where(lane_f == i1, neg, el)
    v2 = jnp.max(el2, axis=-1, keepdims=True)
    i2 = jnp.min(jnp.where(el2 == v2, lane_f, big), axis=-1, keepdims=True)
    t = jnp.exp(v2 - v1)
    w1 = p_group / (1.0 + t)
    w2 = p_group * t / (1.0 + t)
    cmb_ref[...] = jnp.where(lane_f == i1, w1, 0.0) + jnp.where(lane_f == i2, w2, 0.0)


def _router(h, mod_tab, g, wr, br):
    b, nt, d = h.shape
    tm = ROW_TILE
    full = lambda shape: pl.BlockSpec(shape, lambda bb, i: (0,) * len(shape))
    return pl.pallas_call(
        _router_kernel,
        grid=(b, nt // tm),
        in_specs=[pl.BlockSpec((None, tm, d), lambda bb, i: (bb, i, 0)), _mod_spec(d),
                  full((1, d)), full(wr.shape), full(br.shape)],
        out_specs=[pl.BlockSpec((None, tm, d), lambda bb, i: (bb, i, 0)),
                   pl.BlockSpec((None, tm, LANES), lambda bb, i: (bb, i, 0))],
        out_shape=[jax.ShapeDtypeStruct((b, nt, d), _MXU_DTYPE),
                   jax.ShapeDtypeStruct((b, nt, LANES), F32)],
        compiler_params=_params("parallel", "parallel"),
        name="moe_router",
    )(h, mod_tab, g, wr, br)


def _experts_kernel(xt_ref, cmb_ref, wg_ref, wu_ref, wd_ref, h_ref, mod_ref, o_ref,
                    xs_ref, cs_ref, acc_ref, pos_ref, seg_ref, *, ctx_len, nt, window):
    i = pl.program_id(0)
    e = pl.program_id(1)
    sb = xt_ref.shape[0]
    d = xt_ref.shape[1]

    @pl.when(e == 0)
    def _():
        cmb = cmb_ref[...]
        lane = lax.broadcasted_iota(jnp.int32, (1, LANES), 1)
        routed = cmb != 0.0
        goh = jnp.zeros((sb, LANES), F32)
        for g in range(MOE_GROUPS):
            in_g = routed & (lane >= g * MOE_EPG) & (lane < (g + 1) * MOE_EPG)
            hit = jnp.max(jnp.where(in_g, 1.0, 0.0), axis=-1, keepdims=True)
            goh = goh + jnp.where(lane == g, hit, 0.0)
        r_i = lax.broadcasted_iota(jnp.int32, (sb, sb), 0)
        c_i = lax.broadcasted_iota(jnp.int32, (sb, sb), 1)
        earlier = (c_i < r_i).astype(_MXU_DTYPE)
        before = jnp.dot(earlier, goh.astype(_MXU_DTYPE), preferred_element_type=F32)
        cnt = jnp.sum(goh, axis=0, keepdims=True)
        off = jnp.zeros((1, LANES), F32)
        run = jnp.zeros((1, 1), F32)
        for g in range(MOE_GROUPS):
            off = off + jnp.where(lane == g, run, 0.0)
            run = run + jnp.sum(jnp.where(lane == g, cnt, 0.0), axis=-1, keepdims=True)
        pos = jnp.sum(goh * (off + before), axis=-1, keepdims=True)
        pos_b = jnp.broadcast_to(pos, (sb, LANES))
        pos_ref[...] = pos_b
        pos_row = pos_b.T[0:1, :].astype(jnp.int32)
        perm = (r_i == pos_row).astype(_MXU_DTYPE)
        xs_ref[0:sb, :] = jnp.dot(perm, xt_ref[...], preferred_element_type=F32).astype(xs_ref.dtype)
        cs = jnp.zeros((sb, LANES), F32)
        for term in _split_terms(cmb, 3):
            cs = cs + jnp.dot(perm, term, preferred_element_type=F32)
        cs_ref[0:sb, :] = cs
        xs_ref[sb:sb + window, :] = jnp.zeros((window, d), xs_ref.dtype)
        cs_ref[sb:sb + window, :] = jnp.zeros((window, LANES), F32)
        acc_ref[...] = jnp.zeros_like(acc_ref)
        off_i = off.astype(jnp.int32)
        cnt_i = cnt.astype(jnp.int32)
        for g in range(MOE_GROUPS):
            seg_ref[g] = off_i[0, g]
            seg_ref[MOE_GROUPS + g] = cnt_i[0, g]

    g = e // MOE_EPG
    start = seg_ref[g]
    count = seg_ref[MOE_GROUPS + g]
    first = (start // ROW_ALIGN) * ROW_ALIGN
    n_win = (start - first + count + window - 1) // window
    n_win = jnp.where(count > 0, n_win, 0)
    wg = wg_ref[...]
    wu = wu_ref[...]
    wd = wd_ref[...]
    lane = lax.broadcasted_iota(jnp.int32, (1, LANES), 1)

    def window_step(k, carry):
        r0 = pl.multiple_of(first + k * window, ROW_ALIGN)
        x = xs_ref[pl.ds(r0, window), :]
        gate = jnp.dot(x, wg, preferred_element_type=F32)
        up = jnp.dot(x, wu, preferred_element_type=F32)
        w = jnp.sum(jnp.where(lane == e, cs_ref[pl.ds(r0, window), :], 0.0), axis=-1, keepdims=True)
        hid = (gate / (1.0 + jnp.exp(-gate))) * up * w
        acc_ref[pl.ds(r0, window), :] += jnp.dot(hid.astype(_MXU_DTYPE), wd, preferred_element_type=F32)
        return carry

    lax.fori_loop(0, n_win, window_step, 0)

    @pl.when(e == pl.num_programs(1) - 1)
    def _():
        c_i = lax.broadcasted_iota(jnp.int32, (sb, sb), 1)
        unperm = (c_i == pos_ref[:, 0:1].astype(jnp.int32)).astype(_MXU_DTYPE)
        y = jnp.zeros((sb, d), F32)
        for term in _split_terms(acc_ref[0:sb, :], 2):
            y = y + jnp.dot(unperm, term, preferred_element_type=F32)
        row = (i * sb) % nt + lax.broadcasted_iota(jnp.int32, (sb, 1), 0)
        gate_row = jnp.where(row < ctx_len, mod_ref[0, 5:6, :], mod_ref[1, 5:6, :])
        o_ref[...] = h_ref[...] + gate_row * y


def _experts(xt, cmb, wg, wu, wd, h, mod_tab, ctx_len):
    b, nt, d = h.shape
    sb = 1152 if nt % 1152 == 0 else ROW_TILE
    window = 320 if sb == 1152 else 96
    per_b = nt // sb
    n_exp, _, hid = wg.shape
    rows = b * nt
    blk = lambda w: pl.BlockSpec((sb, w), lambda i, e: (i, 0))
    out = pl.pallas_call(
        functools.partial(_experts_kernel, ctx_len=ctx_len, nt=nt, window=window),
        grid=(rows // sb, n_exp),
        in_specs=[blk(d), blk(LANES),
                  pl.BlockSpec((None, d, hid), lambda i, e: (e, 0, 0)),
                  pl.BlockSpec((None, d, hid), lambda i, e: (e, 0, 0)),
                  pl.BlockSpec((None, hid, d), lambda i, e: (e, 0, 0)),
                  blk(d),
                  pl.BlockSpec((None, 2, 8, d), lambda i, e: (i // per_b, 0, 0, 0))],
        out_specs=blk(d),
        out_shape=jax.ShapeDtypeStruct((rows, d), F32),
        scratch_shapes=[pltpu.VMEM((sb + window, d), _MXU_DTYPE), pltpu.VMEM((sb + window, LANES), F32),
                        pltpu.VMEM((sb + window, d), F32), pltpu.VMEM((sb, LANES), F32),
                        pltpu.SMEM((2 * MOE_GROUPS,), jnp.int32)],
        compiler_params=_params("parallel", "arbitrary"),
        name="moe_experts",
    )(xt.reshape(rows, d), cmb.reshape(rows, LANES), wg, wu, wd, h.reshape(rows, d), mod_tab)
    return out.reshape(b, nt, d)


def _final_kernel(h_ref, g_ref, o_ref):
    h = h_ref[...]
    o_ref[...] = h * lax.rsqrt(jnp.mean(h * h, axis=-1, keepdims=True) + RMS_EPS) * g_ref[...]


def _final_norm(h, g, ctx_len):
    b, nt, d = h.shape
    tm = ROW_TILE
    skip = ctx_len // tm
    return pl.pallas_call(
        _final_kernel,
        grid=(b, (nt - ctx_len) // tm),
        in_specs=[pl.BlockSpec((None, tm, d), lambda bb, i: (bb, i + skip, 0)),
                  pl.BlockSpec((1, d), lambda bb, i: (0, 0))],
        out_specs=pl.BlockSpec((None, tm, d), lambda bb, i: (bb, i, 0)),
        out_shape=jax.ShapeDtypeStruct((b, nt - ctx_len, d), F32),
        compiler_params=_params("parallel", "parallel"),
        name="final_norm",
    )(h, g)


def _rope_tables(seq_len, ctx_len):
    n_rows = seq_len // GRID_W
    rows = jnp.repeat(jnp.arange(n_rows, dtype=F32), GRID_W)
    cols = jnp.tile(jnp.arange(GRID_W, dtype=F32), n_rows)
    half = HEAD_DIM // 2
    inv = 1.0 / (ROPE_BASE ** (jnp.arange(0, half, 2, dtype=F32) / half))
    ang_r = rows[:, None] * inv
    ang_c = cols[:, None] * inv
    ang = jnp.concatenate([ang_r, ang_r, ang_c, ang_c], axis=-1)
    ang = jnp.concatenate([jnp.zeros((ctx_len, HEAD_DIM), F32), ang], axis=0)
    ang = jnp.tile(ang, (1, LANES // HEAD_DIM))
    return jnp.cos(ang), jnp.sin(ang)


def _pad_row(v, width=LANES):
    return jnp.pad(v, (0, width - v.shape[0]))[None, :]


def kernel(x, c, ctx, c_ctx, mod_w, mod_b, norm1_g, norm2_g, final_g, attn_w_in, attn_w_out, attn_q_norm_g, attn_k_norm_g, diff_lambda_q1, diff_lambda_k1, diff_lambda_q2, diff_lambda_k2, diff_subln_g, ssm_a_re, ssm_a_im, ssm_log_dt, ssm_b_re, ssm_b_im, ssm_c_re, ssm_c_im, ssm_d, ssm_glu_w_a, ssm_glu_w_b, moe_group_w, moe_group_b, moe_router_w, moe_router_b, moe_w_gate, moe_w_up, moe_w_down):
    bsz, seq, d = x.shape
    ctx_len = ctx.shape[1]
    depth = mod_w.shape[0]
    assert ctx_len == ROW_TILE and seq % ROW_TILE == 0 and seq % GRID_W == 0

    h = jnp.concatenate([ctx, x], axis=1)

    mod_rows = 16
    c_all = jnp.concatenate([c, c_ctx[None, :], jnp.zeros((mod_rows - bsz - 1, d), F32)], axis=0)
    mods = _modulation(c_all, mod_w, mod_b).reshape(depth, mod_rows, 6, d)
    mods = jnp.pad(mods, ((0, 0), (0, 0), (0, 2), (0, 0)))
    mod_tabs = jnp.stack([jnp.broadcast_to(mods[:, bsz:bsz + 1], (depth, bsz, 8, d)), mods[:, :bsz]], axis=2)

    cos, sin = _rope_tables(seq, ctx_len)
    cast = lambda w: w.astype(_MXU_DTYPE)
    ssm_tabs = _ssm_tables(ssm_a_re, ssm_a_im, ssm_log_dt, ssm_b_re, ssm_b_im, ssm_c_re, ssm_c_im, SSM_CHUNK)

    for layer in range(depth):
        mod_tab = mod_tabs[layer]
        i = layer // 2
        if layer % 2 == 0:
            lambda_init = 0.8 - 0.6 * math.exp(-0.3 * layer)
            qkv = _attn_proj(h, mod_tab, norm1_g[layer][None, :], cast(attn_w_in[i]),
                             jnp.tile(attn_q_norm_g[i], 2)[None, :], jnp.tile(attn_k_norm_g[i], 2)[None, :],
                             cos, sin)
            lam_rows = jnp.concatenate([_pad_row(diff_lambda_q1[i]), _pad_row(diff_lambda_k1[i]),
                                        _pad_row(diff_lambda_q2[i]), _pad_row(diff_lambda_k2[i]),
                                        jnp.zeros((4, LANES), F32)], axis=0)
            h = _attention(qkv, lam_rows, diff_subln_g[i][None, :], cast(attn_w_out[i]), h, mod_tab,
                           lambda_init, ctx_len)
        else:
            u = _norm1(h, mod_tab, norm1_g[layer][None, :])
            y = _ssm_scan(u, *ssm_tabs, ctx_len, i * (d // LANES))
            h = _ssm_out(y, u, ssm_d[i][None, :], cast(ssm_glu_w_a[i]), cast(ssm_glu_w_b[i]), h, mod_tab)

        wr = jnp.concatenate([jnp.transpose(moe_router_w[layer], (1, 0, 2)).reshape(d, MOE_EXPERTS),
                              moe_group_w[layer],
                              jnp.zeros((d, LANES - MOE_EXPERTS - MOE_GROUPS), F32)], axis=1)
        br = _pad_row(jnp.concatenate([moe_router_b[layer].reshape(-1), moe_group_b[layer]]))
        xt, cmb = _router(h, mod_tab, norm2_g[layer][None, :], wr, br)
        h = _experts(xt, cmb, cast(moe_w_gate[layer]), cast(moe_w_up[layer]), cast(moe_w_down[layer]),
                     h, mod_tab, ctx_len)

    return _final_norm(h, final_g[None, :], ctx_len)
```

```python
import functools
import math

import jax
import jax.numpy as jnp
from jax import lax
from jax.experimental import pallas as pl
from jax.experimental.pallas import tpu as pltpu

F32 = jnp.float32
_MXU_DTYPE = jnp.bfloat16
_HIGHEST = lax.Precision.HIGHEST

LANES = 128
HEAD_DIM = 64
GRID_W = 64
ROPE_BASE = 10000.0
GQA_Q_HEADS = 8
GQA_GROUP = 4
DIFF_HEADS = 4
GQA_Q_W = 512
GQA_KV_W = 128
DIFF_QK_W = 512
DIFF_V_W = 512
SSM_GROUP_CH = 16
SSM_STATE = 64
MOE_GROUPS = 4
MOE_EPG = 8
MOE_EXPERTS = 32
RMS_EPS = 1e-6
SSM_CHUNK = 8
ROW_TILE = 256
ROW_ALIGN = 16
EXPERTS_PER_STEP = 4
VMEM_LIMIT = 56 * 1024 * 1024


def _params(*sem):
    return pltpu.CompilerParams(dimension_semantics=sem, vmem_limit_bytes=VMEM_LIMIT)


def _norm_mod(h, g, shift, scale):
    y = h * lax.rsqrt(jnp.mean(h * h, axis=-1, keepdims=True) + RMS_EPS) * g
    return y * (1.0 + scale) + shift


def _mm(a, b):
    return jnp.dot(a.astype(_MXU_DTYPE), b.astype(_MXU_DTYPE), preferred_element_type=F32)


def _mod_kernel(c_ref, w_ref, b_ref, o_ref):
    c = c_ref[...]
    a = c / (1.0 + jnp.exp(-c))
    o_ref[...] = jnp.dot(a, w_ref[...], preferred_element_type=F32, precision=_HIGHEST) + b_ref[...]


def _modulation(c_all, mod_w, mod_b):
    depth, d, n = mod_w.shape
    rows = c_all.shape[0]
    tn = 1536
    return pl.pallas_call(
        _mod_kernel,
        grid=(depth, n // tn),
        in_specs=[pl.BlockSpec((rows, d), lambda l, j: (0, 0)),
                  pl.BlockSpec((None, d, tn), lambda l, j: (l, 0, j)),
                  pl.BlockSpec((None, 1, tn), lambda l, j: (l, 0, j))],
        out_specs=pl.BlockSpec((None, rows, tn), lambda l, j: (l, 0, j)),
        out_shape=jax.ShapeDtypeStruct((depth, rows, n), F32),
        compiler_params=_params("parallel", "parallel"),
        name="modulation",
    )(c_all, mod_w, mod_b.reshape(depth, 1, n))


def _mod_spec(d):
    return pl.BlockSpec((None, None, 8, d), lambda b, i: (b, jnp.minimum(i, 1), 0, 0))


def _attn_proj_kernel(h_ref, mod_ref, g_ref, w_ref, gq_ref, gk_ref, cos_ref, sin_ref,
                      qa_ref, ka_ref, va_ref, qb_ref, kb_ref, vb_ref):
    xn = _norm_mod(h_ref[...], g_ref[...], mod_ref[0:1, :], mod_ref[1:2, :])
    hp = _mm(xn, w_ref[...])
    cos = cos_ref[...]
    sin = sin_ref[...]
    lane = lax.broadcasted_iota(jnp.int32, (1, LANES), 1)
    first_half = (lane % 32) < 16
    r = lax.broadcasted_iota(jnp.int32, (LANES, LANES), 0) // HEAD_DIM
    c = lax.broadcasted_iota(jnp.int32, (LANES, LANES), 1) // HEAD_DIM
    same_head = (r == c).astype(_MXU_DTYPE)

    def rope(x):
        rot = jnp.where(first_half, -pltpu.roll(x, LANES - 16, 1), pltpu.roll(x, 16, 1))
        return x * cos + rot * sin

    def head_norm(x, g):
        ss = jnp.dot((x * x).astype(_MXU_DTYPE), same_head, preferred_element_type=F32)
        return x * lax.rsqrt(ss * (1.0 / HEAD_DIM) + RMS_EPS) * g

    scale = HEAD_DIM ** -0.5 * math.log2(math.e)
    o = 0
    for s in range(GQA_Q_W // LANES):
        x = hp[:, o + s * LANES:o + (s + 1) * LANES]
        qa_ref[:, s * LANES:(s + 1) * LANES] = (rope(head_norm(x, gq_ref[...])) * scale).astype(qa_ref.dtype)
    o += GQA_Q_W
    ka_ref[...] = rope(head_norm(hp[:, o:o + LANES], gk_ref[...])).astype(ka_ref.dtype)
    o += GQA_KV_W
    va_ref[...] = hp[:, o:o + LANES].astype(va_ref.dtype)
    o += GQA_KV_W
    for s in range(DIFF_QK_W // LANES):
        x = hp[:, o + s * LANES:o + (s + 1) * LANES]
        qb_ref[:, s * LANES:(s + 1) * LANES] = (rope(x) * scale).astype(qb_ref.dtype)
    o += DIFF_QK_W
    for s in range(DIFF_QK_W // LANES):
        x = hp[:, o + s * LANES:o + (s + 1) * LANES]
        kb_ref[:, s * LANES:(s + 1) * LANES] = rope(x).astype(kb_ref.dtype)
    o += DIFF_QK_W
    vb_ref[...] = hp[:, o:o + DIFF_V_W].astype(vb_ref.dtype)


def _attn_proj(h, mod_tab, g, w_in, gq, gk, cos, sin):
    b, nt, d = h.shape
    tm = ROW_TILE
    widths = (GQA_Q_W, GQA_KV_W, GQA_KV_W, DIFF_QK_W, DIFF_QK_W, DIFF_V_W)
    full = lambda shape: pl.BlockSpec(shape, lambda bb, i: (0,) * len(shape))
    return pl.pallas_call(
        _attn_proj_kernel,
        grid=(b, nt // tm),
        in_specs=[pl.BlockSpec((None, tm, d), lambda bb, i: (bb, i, 0)),
                  _mod_spec(d),
                  full((1, d)),
                  full(w_in.shape),
                  full((1, LANES)),
                  full((1, LANES)),
                  pl.BlockSpec((tm, LANES), lambda bb, i: (i, 0)),
                  pl.BlockSpec((tm, LANES), lambda bb, i: (i, 0))],
        out_specs=[pl.BlockSpec((None, tm, w), lambda bb, i: (bb, i, 0)) for w in widths],
        out_shape=[jax.ShapeDtypeStruct((b, nt, w), _MXU_DTYPE) for w in widths],
        compiler_params=_params("parallel", "parallel"),
        name="attn_proj",
    )(h, mod_tab, g, w_in, gq, gk, cos, sin)


def _softmax_pv(q, k, v):
    s = lax.dot_general(q, k, (((1,), (1,)), ((), ())), preferred_element_type=F32)
    e = jnp.exp2(s - jnp.max(s, axis=-1, keepdims=True))
    l = jnp.sum(e, axis=-1, keepdims=True)
    return jnp.dot(e.astype(v.dtype), v, preferred_element_type=F32) / l


def _attn_kernel(qa_ref, qb_ref, ka_ref, va_ref, kb_ref, vb_ref, lam_ref, sg_ref, wo_ref, h_ref, mod_ref,
                 o_ref, mrg_ref, *, lambda_init, ctx_len):
    lv = lam_ref[...]
    lam = (jnp.exp(jnp.sum(lv[0:1] * lv[1:2], axis=-1, keepdims=True))
           - jnp.exp(jnp.sum(lv[2:3] * lv[3:4], axis=-1, keepdims=True)) + lambda_init)

    def run(nk):
        for h in range(GQA_Q_HEADS):
            g = h // GQA_GROUP
            o = _softmax_pv(qa_ref[:, h * HEAD_DIM:(h + 1) * HEAD_DIM],
                            ka_ref[0:nk, g * HEAD_DIM:(g + 1) * HEAD_DIM],
                            va_ref[0:nk, g * HEAD_DIM:(g + 1) * HEAD_DIM])
            mrg_ref[:, h * HEAD_DIM:(h + 1) * HEAD_DIM] = o.astype(mrg_ref.dtype)
        for h in range(DIFF_HEADS):
            c0 = h * 2 * HEAD_DIM
            v = vb_ref[0:nk, c0:c0 + 2 * HEAD_DIM]
            o1 = _softmax_pv(qb_ref[:, c0:c0 + HEAD_DIM], kb_ref[0:nk, c0:c0 + HEAD_DIM], v)
            o2 = _softmax_pv(qb_ref[:, c0 + HEAD_DIM:c0 + 2 * HEAD_DIM],
                             kb_ref[0:nk, c0 + HEAD_DIM:c0 + 2 * HEAD_DIM], v)
            o = o1 - lam * o2
            o = o * lax.rsqrt(jnp.mean(o * o, axis=-1, keepdims=True) + RMS_EPS) * sg_ref[...]
            o = o * (1.0 - lambda_init)
            mrg_ref[:, GQA_Q_W + c0:GQA_Q_W + c0 + 2 * HEAD_DIM] = o.astype(mrg_ref.dtype)

    i = pl.program_id(1)

    @pl.when(i == 0)
    def _():
        run(ctx_len)

    @pl.when(i > 0)
    def _():
        run(ka_ref.shape[0])

    y = jnp.dot(mrg_ref[...], wo_ref[...], preferred_element_type=F32)
    o_ref[...] = h_ref[...] + mod_ref[2:3, :] * y


def _attention(qkv, lam_rows, subln_g, w_out, h, mod_tab, lambda_init, ctx_len):
    qa, ka, va, qb, kb, vb = qkv
    b, nt, d = h.shape
    tq = ROW_TILE
    assert ctx_len == tq
    blk = lambda w: pl.BlockSpec((None, tq, w), lambda bb, i: (bb, i, 0))
    per_batch = lambda w: pl.BlockSpec((None, nt, w), lambda bb, i: (bb, 0, 0))
    full = lambda shape: pl.BlockSpec(shape, lambda bb, i: (0,) * len(shape))
    return pl.pallas_call(
        functools.partial(_attn_kernel, lambda_init=lambda_init, ctx_len=ctx_len),
        grid=(b, nt // tq),
        in_specs=[blk(GQA_Q_W), blk(DIFF_QK_W), per_batch(GQA_KV_W), per_batch(GQA_KV_W),
                  per_batch(DIFF_QK_W), per_batch(DIFF_V_W),
                  full((8, LANES)), full((1, LANES)), full(w_out.shape), blk(d), _mod_spec(d)],
        out_specs=blk(d),
        out_shape=jax.ShapeDtypeStruct((b, nt, d), F32),
        scratch_shapes=[pltpu.VMEM((tq, GQA_Q_W + DIFF_V_W), _MXU_DTYPE)],
        compiler_params=_params("parallel", "parallel"),
        name="attention",
    )(qa, qb, ka, va, kb, vb, lam_rows, subln_g, w_out, h, mod_tab)


def _norm1_kernel(h_ref, mod_ref, g_ref, o_ref):
    o_ref[...] = _norm_mod(h_ref[...], g_ref[...], mod_ref[0:1, :], mod_ref[1:2, :])


def _norm1(h, mod_tab, g):
    b, nt, d = h.shape
    tm = ROW_TILE
    blk = pl.BlockSpec((None, tm, d), lambda bb, i: (bb, i, 0))
    return pl.pallas_call(
        _norm1_kernel,
        grid=(b, nt // tm),
        in_specs=[blk, _mod_spec(d), pl.BlockSpec((1, d), lambda bb, i: (0, 0))],
        out_specs=blk,
        out_shape=jax.ShapeDtypeStruct((b, nt, d), F32),
        compiler_params=_params("parallel", "parallel"),
        name="ssm_norm",
    )(h, mod_tab, g)


def _ssm_kernel(u_ref, win_ref, m_ref, wout_ref, lam_ref, y_ref, bd_ref, *, chunk, n_ctx_chunks):
    nb, nt, _ = u_ref.shape
    nc = nt // chunk
    n_state_slabs = bd_ref.shape[0]
    q = n_state_slabs // 4

    def chunk_rows(bi):
        parts = [u_ref[bi, pl.ds(s, nc, stride=chunk), :] for s in range(chunk)]
        return jnp.concatenate(parts, axis=1).astype(_MXU_DTYPE)

    for bi in range(nb):
        drive = jnp.dot(chunk_rows(bi), win_ref[...], preferred_element_type=F32)
        for c in range(n_state_slabs):
            bd_ref[c, pl.ds(bi, nc, stride=nb), :] = drive[:, c * LANES:(c + 1) * LANES]

    lam = lam_ref[...]

    def make_step(base):
        a_re = [lam[:, (base + c) * LANES:(base + c + 1) * LANES] for c in range(q)]
        a_im = [lam[:, (base + q + c) * LANES:(base + q + c + 1) * LANES] for c in range(q)]

        def step(k, carry):
            row = pl.multiple_of(k * nb, nb)
            out = []
            for c in range(q):
                s_re, s_im = carry[2 * c], carry[2 * c + 1]
                d_re = bd_ref[base + c, pl.ds(row, nb), :]
                d_im = bd_ref[base + q + c, pl.ds(row, nb), :]
                bd_ref[base + c, pl.ds(row, nb), :] = s_re
                bd_ref[base + q + c, pl.ds(row, nb), :] = s_im
                out.append(a_re[c] * s_re - a_im[c] * s_im + d_re)
                out.append(a_re[c] * s_im + a_im[c] * s_re + d_im)
            return tuple(out)

        return step

    zero = tuple(jnp.zeros((nb, LANES), F32) for _ in range(2 * q))
    fwd = make_step(0)
    lax.fori_loop(0, nc, fwd, zero)
    rev = make_step(2 * q)
    carry = lax.fori_loop(0, n_ctx_chunks, lambda i, cr: rev(n_ctx_chunks - 1 - i, cr), zero)
    lax.fori_loop(0, nc - n_ctx_chunks, lambda i, cr: rev(nc - 1 - i, cr), carry)

    for bi in range(nb):
        states = jnp.concatenate([bd_ref[c, pl.ds(bi, nc, stride=nb), :] for c in range(n_state_slabs)], axis=1)
        y = (jnp.dot(chunk_rows(bi), m_ref[...], preferred_element_type=F32)
             + jnp.dot(states.astype(_MXU_DTYPE), wout_ref[...], preferred_element_type=F32))
        for t in range(chunk):
            y_ref[bi, pl.ds(t, nc, stride=chunk), :] = y[:, t * LANES:(t + 1) * LANES]


def _ssm_scan(u, win, m, wout, lam_t, ctx_len, first_slab):
    b, nt, d = u.shape
    chunk = SSM_CHUNK
    nb = 4 if b % 4 == 0 else b
    n_slabs = d // LANES
    state_w = win.shape[-1]
    nc = nt // chunk
    blk = pl.BlockSpec((nb, nt, LANES), lambda j, bb: (bb, 0, j))
    table = lambda j, bb: (first_slab + j, 0, 0)
    return pl.pallas_call(
        functools.partial(_ssm_kernel, chunk=chunk, n_ctx_chunks=ctx_len // chunk),
        grid=(n_slabs, b // nb),
        in_specs=[blk,
                  pl.BlockSpec((None,) + win.shape[1:], table),
                  pl.BlockSpec((None,) + m.shape[1:], table),
                  pl.BlockSpec((None,) + wout.shape[1:], table),
                  pl.BlockSpec((None, 1, state_w), table)],
        out_specs=blk,
        out_shape=jax.ShapeDtypeStruct((b, nt, d), F32),
        scratch_shapes=[pltpu.VMEM((state_w // LANES, nc * nb, LANES), F32)],
        compiler_params=_params("parallel", "parallel"),
        name="ssm_scan",
    )(u, win, m, wout, lam_t)


def _ssm_out_kernel(y_ref, u_ref, d_ref, wa_ref, wb_ref, h_ref, mod_ref, o_ref):
    x = y_ref[...] + d_ref[...] * u_ref[...]
    z = 0.5 * x * (1.0 + jnp.tanh(math.sqrt(2.0 / math.pi) * (x + 0.044715 * (x * x * x))))
    z = z.astype(_MXU_DTYPE)
    a = jnp.dot(z, wa_ref[...], preferred_element_type=F32)
    g = jnp.dot(z, wb_ref[...], preferred_element_type=F32)
    o_ref[...] = h_ref[...] + mod_ref[2:3, :] * (a / (1.0 + jnp.exp(-g)))


def _ssm_out(y, u, d_skip, wa, wb, h, mod_tab):
    b, nt, d = h.shape
    tm = ROW_TILE
    blk = pl.BlockSpec((None, tm, d), lambda bb, i: (bb, i, 0))
    full = lambda shape: pl.BlockSpec(shape, lambda bb, i: (0,) * len(shape))
    return pl.pallas_call(
        _ssm_out_kernel,
        grid=(b, nt // tm),
        in_specs=[blk, blk, full((1, d)), full(wa.shape), full(wb.shape), blk, _mod_spec(d)],
        out_specs=blk,
        out_shape=jax.ShapeDtypeStruct((b, nt, d), F32),
        compiler_params=_params("parallel", "parallel"),
        name="ssm_out",
    )(y, u, d_skip, wa, wb, h, mod_tab)


def _split_terms(x, n):
    terms = []
    for _ in range(n - 1):
        t = x.astype(_MXU_DTYPE)
        terms.append(t)
        x = x - t.astype(F32)
    terms.append(x.astype(_MXU_DTYPE))
    return terms


def _ssm_tables_kernel(lam_ref, bt_ref, c_ref, m_ref, win_ref, wout_ref, lamt_ref, *, chunk):
    gpt = LANES // SSM_GROUP_CH
    p = SSM_STATE
    sw = gpt * p
    ci = lax.broadcasted_iota(jnp.int32, (p, sw), 0)
    oi = lax.broadcasted_iota(jnp.int32, (p, sw), 1)
    spread = (ci == oi % p).astype(_MXU_DTYPE)
    ri = lax.broadcasted_iota(jnp.int32, (LANES, sw), 0)
    oj = lax.broadcasted_iota(jnp.int32, (LANES, sw), 1)
    own_group = (ri // SSM_GROUP_CH) == (oj // p)

    def block_diag(x):
        y = jnp.zeros((LANES, sw), F32)
        for term in _split_terms(x, 3):
            y = y + jnp.dot(term, spread, preferred_element_type=F32)
        return jnp.where(own_group, y, 0.0)

    def cmul(ar, ai, br, bi):
        return ar * br - ai * bi, ar * bi + ai * br

    nt_dot = lambda a, b: lax.dot_general(a, b, (((1,), (1,)), ((), ())), precision=_HIGHEST,
                                          preferred_element_type=F32)
    taps, drive, read = [], [], []
    for x in range(2):
        lre = jnp.minimum(lam_ref[x, 0:1, :], -1e-4)
        lim = lam_ref[x, 1:2, :]
        dt = jnp.exp(lam_ref[x, 2:3, :])
        pw = []
        for j in range(chunk + 1):
            mag = jnp.exp(float(j) * (lre * dt))
            ang = float(j) * (lim * dt)
            pw.append((mag * jnp.cos(ang), mag * jnp.sin(ang)))
        nr = pw[1][0] - 1.0
        ni = pw[1][1]
        den = lre * lre + lim * lim
        coef_re = (nr * lre + ni * lim) / den
        coef_im = (ni * lre - nr * lim) / den
        bb = cmul(coef_re, coef_im, block_diag(bt_ref[x, 0]), block_diag(bt_ref[x, 1]))
        cc = (block_diag(c_ref[x, 0]), block_diag(c_ref[x, 1]))
        drive.append([cmul(pw[j][0], pw[j][1], bb[0], bb[1]) for j in range(chunk)])
        read.append([cmul(pw[j][0], pw[j][1], cc[0], cc[1]) for j in range(chunk + 1)])
        taps.append([nt_dot(u_re, cc[0]) - nt_dot(u_im, cc[1]) for u_re, u_im in drive[x]])
        lamt_ref[:, 2 * x * sw:(2 * x + 1) * sw] = pw[chunk][0]
        lamt_ref[:, (2 * x + 1) * sw:(2 * x + 2) * sw] = pw[chunk][1]

    for s in range(chunk):
        rows = slice(s * LANES, (s + 1) * LANES)
        for t in range(chunk):
            if t > s:
                blk = taps[0][t - s]
            elif t < s:
                blk = taps[1][s - t]
            else:
                blk = taps[0][0] + taps[1][0]
            m_ref[rows, t * LANES:(t + 1) * LANES] = blk.astype(m_ref.dtype)
        f_re, f_im = drive[0][chunk - 1 - s]
        r_re, r_im = drive[1][s]
        for q, part in enumerate((f_re, f_im, r_re, r_im)):
            win_ref[rows, q * sw:(q + 1) * sw] = part.astype(win_ref.dtype)
    for t in range(chunk):
        cols = slice(t * LANES, (t + 1) * LANES)
        f_re, f_im = read[0][t + 1]
        r_re, r_im = read[1][chunk - t]
        for q, part in enumerate((f_re, -f_im, r_re, -r_im)):
            wout_ref[q * sw:(q + 1) * sw, cols] = part.T.astype(wout_ref.dtype)


def _ssm_tables(a_re, a_im, log_dt, b_re, b_im, c_re, c_im, chunk):
    n_layers, _, g_total, p = a_re.shape
    gpt = LANES // SSM_GROUP_CH
    n_slabs = g_total // gpt
    n = n_layers * n_slabs
    sw = gpt * p

    def slab_lanes(x):
        return jnp.transpose(x.reshape(n_layers, 2, n_slabs, sw), (0, 2, 1, 3)).reshape(n, 2, sw)

    def slab_rows(x):
        x = x.reshape(n_layers, 2, n_slabs, LANES, p)
        return jnp.transpose(x, (0, 2, 1, 3, 4)).reshape(n, 2, LANES, p)

    lam = jnp.stack([slab_lanes(a_re), slab_lanes(a_im),
                     slab_lanes(jnp.broadcast_to(log_dt[..., None], a_re.shape))], axis=2)
    lam = jnp.pad(lam, ((0, 0), (0, 0), (0, 5), (0, 0)))
    bt = jnp.stack([slab_rows(jnp.swapaxes(b_re, -1, -2)), slab_rows(jnp.swapaxes(b_im, -1, -2))], axis=2)
    ct = jnp.stack([slab_rows(c_re), slab_rows(c_im)], axis=2)
    rows = chunk * LANES
    per_slab = lambda *shape: pl.BlockSpec((None,) + shape, lambda i: (i,) + (0,) * len(shape))
    m, win, wout, lam_t = pl.pallas_call(
        functools.partial(_ssm_tables_kernel, chunk=chunk),
        grid=(n,),
        in_specs=[per_slab(2, 8, sw), per_slab(2, 2, LANES, p), per_slab(2, 2, LANES, p)],
        out_specs=[per_slab(rows, rows), per_slab(rows, 4 * sw), per_slab(4 * sw, rows), per_slab(1, 4 * sw)],
        out_shape=[jax.ShapeDtypeStruct((n, rows, rows), _MXU_DTYPE),
                   jax.ShapeDtypeStruct((n, rows, 4 * sw), _MXU_DTYPE),
                   jax.ShapeDtypeStruct((n, 4 * sw, rows), _MXU_DTYPE),
                   jax.ShapeDtypeStruct((n, 1, 4 * sw), F32)],
        compiler_params=_params("parallel"),
        name="ssm_tables",
    )(lam, bt, ct)
    return win, m, wout, lam_t


def _router_kernel(h_ref, mod_ref, g_ref, wr_ref, br_ref, xt_ref, cmb_ref):
    xt = _norm_mod(h_ref[...], g_ref[...], mod_ref[3:4, :], mod_ref[4:5, :])
    xt_ref[...] = xt.astype(xt_ref.dtype)
    logits = jnp.dot(xt, wr_ref[...], preferred_element_type=F32, precision=_HIGHEST) + br_ref[...]
    lane = lax.broadcasted_iota(jnp.int32, (1, LANES), 1)
    lane_f = lane.astype(F32)
    neg = -jnp.inf
    big = 1e9
    gmask = (lane >= MOE_EXPERTS) & (lane < MOE_EXPERTS + MOE_GROUPS)
    gl = jnp.where(gmask, logits, neg)
    gmax = jnp.max(gl, axis=-1, keepdims=True)
    gidx = jnp.min(jnp.where(gl == gmax, lane_f, big), axis=-1, keepdims=True) - MOE_EXPERTS
    p_group = 1.0 / jnp.sum(jnp.where(gmask, jnp.exp(gl - gmax), 0.0), axis=-1, keepdims=True)
    in_group = (lane < MOE_EXPERTS) & ((lane // MOE_EPG).astype(F32) == gidx)
    el = jnp.where(in_group, logits, neg)
    v1 = jnp.max(el, axis=-1, keepdims=True)
    i1 = jnp.min(jnp.where(el == v1, lane_f, big), axis=-1, keepdims=True)
    el2 = jnp.where(lane_f == i1, neg, el)
    v2 = jnp.max(el2, axis=-1, keepdims=True)
    i2 = jnp.min(jnp.where(el2 == v2, lane_f, big), axis=-1, keepdims=True)
    t = jnp.exp(v2 - v1)
    w1 = p_group / (1.0 + t)
    w2 = p_group * t / (1.0 + t)
    cmb_ref[...] = jnp.where(lane_f == i1, w1, 0.0) + jnp.where(lane_f == i2, w2, 0.0)


def _router(h, mod_tab, g, wr, br):
    b, nt, d = h.shape
    tm = ROW_TILE
    full = lambda shape: pl.BlockSpec(shape, lambda bb, i: (0,) * len(shape))
    return pl.pallas_call(
        _router_kernel,
        grid=(b, nt // tm),
        in_specs=[pl.BlockSpec((None, tm, d), lambda bb, i: (bb, i, 0)), _mod_spec(d),
                  full((1, d)), full(wr.shape), full(br.shape)],
        out_specs=[pl.BlockSpec((None, tm, d), lambda bb, i: (bb, i, 0)),
                   pl.BlockSpec((None, tm, LANES), lambda bb, i: (bb, i, 0))],
        out_shape=[jax.ShapeDtypeStruct((b, nt, d), _MXU_DTYPE),
                   jax.ShapeDtypeStruct((b, nt, LANES), F32)],
        compiler_params=_params("parallel", "parallel"),
        name="moe_router",
    )(h, mod_tab, g, wr, br)


def _experts_kernel(xt_ref, cmb_ref, wg_ref, wu_ref, wd_ref, h_ref, mod_ref, o_ref,
                    xs_ref, cs_ref, acc_ref, pos_ref, seg_ref, *, ctx_len, nt, window):
    i = pl.program_id(0)
    e = pl.program_id(1)
    sb = xt_ref.shape[0]
    d = xt_ref.shape[1]

    @pl.when(e == 0)
    def _():
        cmb = cmb_ref[...]
        lane = lax.broadcasted_iota(jnp.int32, (1, LANES), 1)
        routed = cmb != 0.0
        goh = jnp.zeros((sb, LANES), F32)
        for g in range(MOE_GROUPS):
            in_g = routed & (lane >= g * MOE_EPG) & (lane < (g + 1) * MOE_EPG)
            hit = jnp.max(jnp.where(in_g, 1.0, 0.0), axis=-1, keepdims=True)
            goh = goh + jnp.where(lane == g, hit, 0.0)
        r_i = lax.broadcasted_iota(jnp.int32, (sb, sb), 0)
        c_i = lax.broadcasted_iota(jnp.int32, (sb, sb), 1)
        earlier = (c_i < r_i).astype(_MXU_DTYPE)
        before = jnp.dot(earlier, goh.astype(_MXU_DTYPE), preferred_element_type=F32)
        cnt = jnp.sum(goh, axis=0, keepdims=True)
        off = jnp.zeros((1, LANES), F32)
        run = jnp.zeros((1, 1), F32)
        for g in range(MOE_GROUPS):
            off = off + jnp.where(lane == g, run, 0.0)
            run = run + jnp.sum(jnp.where(lane == g, cnt, 0.0), axis=-1, keepdims=True)
        pos = jnp.sum(goh * (off + before), axis=-1, keepdims=True)
        pos_b = jnp.broadcast_to(pos, (sb, LANES))
        pos_ref[...] = pos_b
        pos_row = pos_b.T[0:1, :].astype(jnp.int32)
        perm = (r_i == pos_row).astype(_MXU_DTYPE)
        xs_ref[0:sb, :] = jnp.dot(perm, xt_ref[...], preferred_element_type=F32).astype(xs_ref.dtype)
        cs = jnp.zeros((sb, LANES), F32)
        for term in _split_terms(cmb, 3):
            cs = cs + jnp.dot(perm, term, preferred_element_type=F32)
        cs_ref[0:sb, :] = cs
        xs_ref[sb:sb + window, :] = jnp.zeros((window, d), xs_ref.dtype)
        cs_ref[sb:sb + window, :] = jnp.zeros((window, LANES), F32)
        acc_ref[...] = jnp.zeros_like(acc_ref)
        off_i = off.astype(jnp.int32)
        cnt_i = cnt.astype(jnp.int32)
        for g in range(MOE_GROUPS):
            seg_ref[g] = off_i[0, g]
            seg_ref[MOE_GROUPS + g] = cnt_i[0, g]

    eps = wg_ref.shape[0]
    g = (e * eps) // MOE_EPG
    start = seg_ref[g]
    count = seg_ref[MOE_GROUPS + g]
    first = (start // ROW_ALIGN) * ROW_ALIGN
    n_win = (start - first + count + window - 1) // window
    n_win = jnp.where(count > 0, n_win, 0)
    lane = lax.broadcasted_iota(jnp.int32, (1, LANES), 1)

    def window_step(k, carry):
        r0 = pl.multiple_of(first + k * window, ROW_ALIGN)
        x = xs_ref[pl.ds(r0, window), :]
        cw = cs_ref[pl.ds(r0, window), :]
        y = jnp.zeros((window, d), F32)
        for j in range(eps):
            gate = jnp.dot(x, wg_ref[j], preferred_element_type=F32)
            up = jnp.dot(x, wu_ref[j], preferred_element_type=F32)
            w = jnp.sum(jnp.where(lane == e * eps + j, cw, 0.0), axis=-1, keepdims=True)
            hid = (gate / (1.0 + jnp.exp(-gate))) * up * w
            y = y + jnp.dot(hid.astype(_MXU_DTYPE), wd_ref[j], preferred_element_type=F32)
        acc_ref[pl.ds(r0, window), :] += y
        return carry

    lax.fori_loop(0, n_win, window_step, 0)

    @pl.when(e == pl.num_programs(1) - 1)
    def _():
        c_i = lax.broadcasted_iota(jnp.int32, (sb, sb), 1)
        unperm = (c_i == pos_ref[:, 0:1].astype(jnp.int32)).astype(_MXU_DTYPE)
        y = jnp.zeros((sb, d), F32)
        for term in _split_terms(acc_ref[0:sb, :], 2):
            y = y + jnp.dot(unperm, term, preferred_element_type=F32)
        row = (i * sb) % nt + lax.broadcasted_iota(jnp.int32, (sb, 1), 0)
        gate_row = jnp.where(row < ctx_len, mod_ref[0, 5:6, :], mod_ref[1, 5:6, :])
        o_ref[...] = h_ref[...] + gate_row * y


def _experts(xt, cmb, wg, wu, wd, h, mod_tab, ctx_len):
    b, nt, d = h.shape
    sb = 1152 if nt % 1152 == 0 else ROW_TILE
    window = 320 if sb == 1152 else 96
    per_b = nt // sb
    n_exp, _, hid = wg.shape
    rows = b * nt
    blk = lambda w: pl.BlockSpec((sb, w), lambda i, e: (i, 0))
    out = pl.pallas_call(
        functools.partial(_experts_kernel, ctx_len=ctx_len, nt=nt, window=window),
        grid=(rows // sb, n_exp // EXPERTS_PER_STEP),
        in_specs=[blk(d), blk(LANES),
                  pl.BlockSpec((EXPERTS_PER_STEP, d, hid), lambda i, e: (e, 0, 0)),
                  pl.BlockSpec((EXPERTS_PER_STEP, d, hid), lambda i, e: (e, 0, 0)),
                  pl.BlockSpec((EXPERTS_PER_STEP, hid, d), lambda i, e: (e, 0, 0)),
                  blk(d),
                  pl.BlockSpec((None, 2, 8, d), lambda i, e: (i // per_b, 0, 0, 0))],
        out_specs=blk(d),
        out_shape=jax.ShapeDtypeStruct((rows, d), F32),
        scratch_shapes=[pltpu.VMEM((sb + window, d), _MXU_DTYPE), pltpu.VMEM((sb + window, LANES), F32),
                        pltpu.VMEM((sb + window, d), F32), pltpu.VMEM((sb, LANES), F32),
                        pltpu.SMEM((2 * MOE_GROUPS,), jnp.int32)],
        compiler_params=_params("parallel", "arbitrary"),
        name="moe_experts",
    )(xt.reshape(rows, d), cmb.reshape(rows, LANES), wg, wu, wd, h.reshape(rows, d), mod_tab)
    return out.reshape(b, nt, d)


def _final_kernel(h_ref, g_ref, o_ref):
    h = h_ref[...]
    o_ref[...] = h * lax.rsqrt(jnp.mean(h * h, axis=-1, keepdims=True) + RMS_EPS) * g_ref[...]


def _final_norm(h, g, ctx_len):
    b, nt, d = h.shape
    tm = ROW_TILE
    skip = ctx_len // tm
    return pl.pallas_call(
        _final_kernel,
        grid=(b, (nt - ctx_len) // tm),
        in_specs=[pl.BlockSpec((None, tm, d), lambda bb, i: (bb, i + skip, 0)),
                  pl.BlockSpec((1, d), lambda bb, i: (0, 0))],
        out_specs=pl.BlockSpec((None, tm, d), lambda bb, i: (bb, i, 0)),
        out_shape=jax.ShapeDtypeStruct((b, nt - ctx_len, d), F32),
        compiler_params=_params("parallel", "parallel"),
        name="final_norm",
    )(h, g)


def _rope_tables(seq_len, ctx_len):
    n_rows = seq_len // GRID_W
    rows = jnp.repeat(jnp.arange(n_rows, dtype=F32), GRID_W)
    cols = jnp.tile(jnp.arange(GRID_W, dtype=F32), n_rows)
    half = HEAD_DIM // 2
    inv = 1.0 / (ROPE_BASE ** (jnp.arange(0, half, 2, dtype=F32) / half))
    ang_r = rows[:, None] * inv
    ang_c = cols[:, None] * inv
    ang = jnp.concatenate([ang_r, ang_r, ang_c, ang_c], axis=-1)
    ang = jnp.concatenate([jnp.zeros((ctx_len, HEAD_DIM), F32), ang], axis=0)
    ang = jnp.tile(ang, (1, LANES // HEAD_DIM))
    return jnp.cos(ang), jnp.sin(ang)


def _pad_row(v, width=LANES):
    return jnp.pad(v, (0, width - v.shape[0]))[None, :]


def kernel(x, c, ctx, c_ctx, mod_w, mod_b, norm1_g, norm2_g, final_g, attn_w_in, attn_w_out, attn_q_norm_g, attn_k_norm_g, diff_lambda_q1, diff_lambda_k1, diff_lambda_q2, diff_lambda_k2, diff_subln_g, ssm_a_re, ssm_a_im, ssm_log_dt, ssm_b_re, ssm_b_im, ssm_c_re, ssm_c_im, ssm_d, ssm_glu_w_a, ssm_glu_w_b, moe_group_w, moe_group_b, moe_router_w, moe_router_b, moe_w_gate, moe_w_up, moe_w_down):
    bsz, seq, d = x.shape
    ctx_len = ctx.shape[1]
    depth = mod_w.shape[0]
    assert ctx_len == ROW_TILE and seq % ROW_TILE == 0 and seq % GRID_W == 0

    h = jnp.concatenate([ctx, x], axis=1)

    mod_rows = 16
    c_all = jnp.concatenate([c, c_ctx[None, :], jnp.zeros((mod_rows - bsz - 1, d), F32)], axis=0)
    mods = _modulation(c_all, mod_w, mod_b).reshape(depth, mod_rows, 6, d)
    mods = jnp.pad(mods, ((0, 0), (0, 0), (0, 2), (0, 0)))
    mod_tabs = jnp.stack([jnp.broadcast_to(mods[:, bsz:bsz + 1], (depth, bsz, 8, d)), mods[:, :bsz]], axis=2)

    cos, sin = _rope_tables(seq, ctx_len)
    cast = lambda w: w.astype(_MXU_DTYPE)
    ssm_tabs = _ssm_tables(ssm_a_re, ssm_a_im, ssm_log_dt, ssm_b_re, ssm_b_im, ssm_c_re, ssm_c_im, SSM_CHUNK)

    for layer in range(depth):
        mod_tab = mod_tabs[layer]
        i = layer // 2
        if layer % 2 == 0:
            lambda_init = 0.8 - 0.6 * math.exp(-0.3 * layer)
            qkv = _attn_proj(h, mod_tab, norm1_g[layer][None, :], cast(attn_w_in[i]),
                             jnp.tile(attn_q_norm_g[i], 2)[None, :], jnp.tile(attn_k_norm_g[i], 2)[None, :],
                             cos, sin)
            lam_rows = jnp.concatenate([_pad_row(diff_lambda_q1[i]), _pad_row(diff_lambda_k1[i]),
                                        _pad_row(diff_lambda_q2[i]), _pad_row(diff_lambda_k2[i]),
                                        jnp.zeros((4, LANES), F32)], axis=0)
            h = _attention(qkv, lam_rows, diff_subln_g[i][None, :], cast(attn_w_out[i]), h, mod_tab,
                           lambda_init, ctx_len)
        else:
            u = _norm1(h, mod_tab, norm1_g[layer][None, :])
            y = _ssm_scan(u, *ssm_tabs, ctx_len, i * (d // LANES))
            h = _ssm_out(y, u, ssm_d[i][None, :], cast(ssm_glu_w_a[i]), cast(ssm_glu_w_b[i]), h, mod_tab)

        wr = jnp.concatenate([jnp.transpose(moe_router_w[layer], (1, 0, 2)).reshape(d, MOE_EXPERTS),
                              moe_group_w[layer],
                              jnp.zeros((d, LANES - MOE_EXPERTS - MOE_GROUPS), F32)], axis=1)
        br = _pad_row(jnp.concatenate([moe_router_b[layer].reshape(-1), moe_group_b[layer]]))
        xt, cmb = _router(h, mod_tab, norm2_g[layer][None, :], wr, br)
        h = _experts(xt, cmb, cast(moe_w_gate[layer]), cast(moe_w_up[layer]), cast(moe_w_down[layer]),
                     h, mod_tab, ctx_len)

    return _final_norm(h, final_g[None, :], ctx_len)
```

```python
import functools
import math

import jax
import jax.numpy as jnp
from jax import lax
from jax.experimental import pallas as pl
from jax.experimental.pallas import tpu as pltpu

F32 = jnp.float32
_MXU_DTYPE = jnp.bfloat16
_HIGHEST = lax.Precision.HIGHEST

LANES = 128
HEAD_DIM = 64
GRID_W = 64
ROPE_BASE = 10000.0
GQA_Q_HEADS = 8
GQA_GROUP = 4
DIFF_HEADS = 4
GQA_Q_W = 512
GQA_KV_W = 128
DIFF_QK_W = 512
DIFF_V_W = 512
SSM_GROUP_CH = 16
SSM_STATE = 64
MOE_GROUPS = 4
MOE_EPG = 8
MOE_EXPERTS = 32
RMS_EPS = 1e-6
SSM_CHUNK = 8
ROW_TILE = 256
ROW_ALIGN = 16
EXPERTS_PER_STEP = 4
VMEM_LIMIT = 56 * 1024 * 1024


def _params(*sem):
    return pltpu.CompilerParams(dimension_semantics=sem, vmem_limit_bytes=VMEM_LIMIT)


def _norm_mod(h, g, shift, scale):
    y = h * lax.rsqrt(jnp.mean(h * h, axis=-1, keepdims=True) + RMS_EPS) * g
    return y * (1.0 + scale) + shift


def _mm(a, b):
    return jnp.dot(a.astype(_MXU_DTYPE), b.astype(_MXU_DTYPE), preferred_element_type=F32)


def _mod_kernel(c_ref, w_ref, b_ref, o_ref):
    c = c_ref[...]
    a = c / (1.0 + jnp.exp(-c))
    o_ref[...] = jnp.dot(a, w_ref[...], preferred_element_type=F32, precision=_HIGHEST) + b_ref[...]


def _modulation(c_all, mod_w, mod_b):
    depth, d, n = mod_w.shape
    rows = c_all.shape[0]
    tn = 1536
    return pl.pallas_call(
        _mod_kernel,
        grid=(depth, n // tn),
        in_specs=[pl.BlockSpec((rows, d), lambda l, j: (0, 0)),
                  pl.BlockSpec((None, d, tn), lambda l, j: (l, 0, j)),
                  pl.BlockSpec((None, 1, tn), lambda l, j: (l, 0, j))],
        out_specs=pl.BlockSpec((None, rows, tn), lambda l, j: (l, 0, j)),
        out_shape=jax.ShapeDtypeStruct((depth, rows, n), F32),
        compiler_params=_params("parallel", "parallel"),
        name="modulation",
    )(c_all, mod_w, mod_b.reshape(depth, 1, n))


def _mod_spec(d):
    return pl.BlockSpec((None, None, 8, d), lambda b, i: (b, jnp.minimum(i, 1), 0, 0))


def _attn_proj_kernel(h_ref, mod_ref, g_ref, w_ref, gq_ref, gk_ref, cos_ref, sin_ref,
                      qa_ref, ka_ref, va_ref, qb_ref, kb_ref, vb_ref):
    xn = _norm_mod(h_ref[...], g_ref[...], mod_ref[0:1, :], mod_ref[1:2, :])
    hp = _mm(xn, w_ref[...])
    cos = cos_ref[...]
    sin = sin_ref[...]
    lane = lax.broadcasted_iota(jnp.int32, (1, LANES), 1)
    first_half = (lane % 32) < 16
    r = lax.broadcasted_iota(jnp.int32, (LANES, LANES), 0) // HEAD_DIM
    c = lax.broadcasted_iota(jnp.int32, (LANES, LANES), 1) // HEAD_DIM
    same_head = (r == c).astype(_MXU_DTYPE)

    def rope(x):
        rot = jnp.where(first_half, -pltpu.roll(x, LANES - 16, 1), pltpu.roll(x, 16, 1))
        return x * cos + rot * sin

    def head_norm(x, g):
        ss = jnp.dot((x * x).astype(_MXU_DTYPE), same_head, preferred_element_type=F32)
        return x * lax.rsqrt(ss * (1.0 / HEAD_DIM) + RMS_EPS) * g

    scale = HEAD_DIM ** -0.5 * math.log2(math.e)
    o = 0
    for s in range(GQA_Q_W // LANES):
        x = hp[:, o + s * LANES:o + (s + 1) * LANES]
        qa_ref[:, s * LANES:(s + 1) * LANES] = (rope(head_norm(x, gq_ref[...])) * scale).astype(qa_ref.dtype)
    o += GQA_Q_W
    ka_ref[...] = rope(head_norm(hp[:, o:o + LANES], gk_ref[...])).astype(ka_ref.dtype)
    o += GQA_KV_W
    v_pair = hp[:, o:o + LANES]
    low_half = lane < HEAD_DIM
    va_ref[:, 0:LANES] = jnp.where(low_half, v_pair, 1.0).astype(va_ref.dtype)
    va_ref[:, LANES:2 * LANES] = jnp.where(low_half, pltpu.roll(v_pair, HEAD_DIM, 1), 1.0).astype(va_ref.dtype)
    o += GQA_KV_W
    for s in range(DIFF_QK_W // LANES):
        x = hp[:, o + s * LANES:o + (s + 1) * LANES]
        qb_ref[:, s * LANES:(s + 1) * LANES] = (rope(x) * scale).astype(qb_ref.dtype)
    o += DIFF_QK_W
    for s in range(DIFF_QK_W // LANES):
        x = hp[:, o + s * LANES:o + (s + 1) * LANES]
        kb_ref[:, s * LANES:(s + 1) * LANES] = rope(x).astype(kb_ref.dtype)
    o += DIFF_QK_W
    for s in range(DIFF_HEADS):
        vb_ref[:, 2 * s * LANES:(2 * s + 1) * LANES] = hp[:, o + s * LANES:o + (s + 1) * LANES].astype(vb_ref.dtype)
        vb_ref[:, (2 * s + 1) * LANES:(2 * s + 2) * LANES] = jnp.ones((hp.shape[0], LANES), vb_ref.dtype)


def _attn_proj(h, mod_tab, g, w_in, gq, gk, cos, sin):
    b, nt, d = h.shape
    tm = ROW_TILE
    widths = (GQA_Q_W, GQA_KV_W, 2 * GQA_KV_W, DIFF_QK_W, DIFF_QK_W, 2 * DIFF_V_W)
    full = lambda shape: pl.BlockSpec(shape, lambda bb, i: (0,) * len(shape))
    return pl.pallas_call(
        _attn_proj_kernel,
        grid=(b, nt // tm),
        in_specs=[pl.BlockSpec((None, tm, d), lambda bb, i: (bb, i, 0)),
                  _mod_spec(d),
                  full((1, d)),
                  full(w_in.shape),
                  full((1, LANES)),
                  full((1, LANES)),
                  pl.BlockSpec((tm, LANES), lambda bb, i: (i, 0)),
                  pl.BlockSpec((tm, LANES), lambda bb, i: (i, 0))],
        out_specs=[pl.BlockSpec((None, tm, w), lambda bb, i: (bb, i, 0)) for w in widths],
        out_shape=[jax.ShapeDtypeStruct((b, nt, w), _MXU_DTYPE) for w in widths],
        compiler_params=_params("parallel", "parallel"),
        name="attn_proj",
    )(h, mod_tab, g, w_in, gq, gk, cos, sin)


def _attn_kernel(qa_ref, qb_ref, ka_ref, va_ref, kb_ref, vb_ref, lam_ref, sg_ref, wo_ref, h_ref, mod_ref,
                 o_ref, mrg_ref, *, lambda_init, ctx_len):
    lv = lam_ref[...]
    lam = (jnp.exp(jnp.sum(lv[0:1] * lv[1:2], axis=-1, keepdims=True))
           - jnp.exp(jnp.sum(lv[2:3] * lv[3:4], axis=-1, keepdims=True)) + lambda_init)

    def run(nk):
        tq = qa_ref.shape[0]
        qk = lambda q, k: lax.dot_general(q, k, (((1,), (1,)), ((), ())), preferred_element_type=F32)
        probs = lambda s: jnp.exp2((s - jnp.max(s, axis=-1, keepdims=True)).astype(_MXU_DTYPE))

        def normalised(o, dv):
            return o[:, 0:dv] / o[:, dv:dv + 1]

        def gqa_scores(g):
            q = jnp.concatenate([qa_ref[:, h * HEAD_DIM:(h + 1) * HEAD_DIM]
                                 for h in range(g * GQA_GROUP, (g + 1) * GQA_GROUP)], axis=0)
            return [qk(q, ka_ref[0:nk, g * HEAD_DIM:(g + 1) * HEAD_DIM])]

        def gqa_attend(g, p):
            o = jnp.dot(p[0], va_ref[0:nk, g * LANES:(g + 1) * LANES], preferred_element_type=F32)
            for j in range(GQA_GROUP):
                h = g * GQA_GROUP + j
                mrg_ref[:, h * HEAD_DIM:(h + 1) * HEAD_DIM] = normalised(
                    o[j * tq:(j + 1) * tq], HEAD_DIM).astype(mrg_ref.dtype)

        def diff_scores(h):
            c0 = h * 2 * HEAD_DIM
            return [qk(qb_ref[:, c:c + HEAD_DIM], kb_ref[0:nk, c:c + HEAD_DIM]) for c in (c0, c0 + HEAD_DIM)]

        def diff_attend(h, p):
            c0 = h * 2 * HEAD_DIM
            o = jnp.dot(jnp.concatenate(p, axis=0), vb_ref[0:nk, 2 * c0:2 * c0 + 2 * LANES],
                        preferred_element_type=F32)
            o = normalised(o[0:tq], 2 * HEAD_DIM) - lam * normalised(o[tq:2 * tq], 2 * HEAD_DIM)
            o = o * lax.rsqrt(jnp.mean(o * o, axis=-1, keepdims=True) + RMS_EPS) * sg_ref[...]
            o = o * (1.0 - lambda_init)
            mrg_ref[:, GQA_Q_W + c0:GQA_Q_W + c0 + 2 * HEAD_DIM] = o.astype(mrg_ref.dtype)

        units = ([(functools.partial(gqa_scores, g), functools.partial(gqa_attend, g))
                  for g in range(GQA_Q_HEADS // GQA_GROUP)]
                 + [(functools.partial(diff_scores, h), functools.partial(diff_attend, h))
                    for h in range(DIFF_HEADS)])
        n = len(units)
        s, p = [None] * n, [None] * n
        for step in range(n + 2):
            if step < n:
                s[step] = units[step][0]()
            if 1 <= step <= n:
                p[step - 1] = [probs(x) for x in s[step - 1]]
                s[step - 1] = None
            if 2 <= step:
                units[step - 2][1](p[step - 2])
                p[step - 2] = None

    i = pl.program_id(1)

    @pl.when(i == 0)
    def _():
        run(ctx_len)

    @pl.when(i > 0)
    def _():
        run(ka_ref.shape[0])

    y = jnp.dot(mrg_ref[...], wo_ref[...], preferred_element_type=F32)
    o_ref[...] = h_ref[...] + mod_ref[2:3, :] * y


def _attention(qkv, lam_rows, subln_g, w_out, h, mod_tab, lambda_init, ctx_len):
    qa, ka, va, qb, kb, vb = qkv
    b, nt, d = h.shape
    tq = ROW_TILE
    assert ctx_len == tq
    blk = lambda w: pl.BlockSpec((None, tq, w), lambda bb, i: (bb, i, 0))
    per_batch = lambda w: pl.BlockSpec((None, nt, w), lambda bb, i: (bb, 0, 0))
    full = lambda shape: pl.BlockSpec(shape, lambda bb, i: (0,) * len(shape))
    return pl.pallas_call(
        functools.partial(_attn_kernel, lambda_init=lambda_init, ctx_len=ctx_len),
        grid=(b, nt // tq),
        in_specs=[blk(GQA_Q_W), blk(DIFF_QK_W), per_batch(GQA_KV_W), per_batch(2 * GQA_KV_W),
                  per_batch(DIFF_QK_W), per_batch(2 * DIFF_V_W),
                  full((8, LANES)), full((1, LANES)), full(w_out.shape), blk(d), _mod_spec(d)],
        out_specs=blk(d),
        out_shape=jax.ShapeDtypeStruct((b, nt, d), F32),
        scratch_shapes=[pltpu.VMEM((tq, GQA_Q_W + DIFF_V_W), _MXU_DTYPE)],
        compiler_params=_params("parallel", "parallel"),
        name="attention",
    )(qa, qb, ka, va, kb, vb, lam_rows, subln_g, w_out, h, mod_tab)


def _norm1_kernel(h_ref, mod_ref, g_ref, o_ref):
    o_ref[...] = _norm_mod(h_ref[...], g_ref[...], mod_ref[0:1, :], mod_ref[1:2, :])


def _norm1(h, mod_tab, g):
    b, nt, d = h.shape
    tm = ROW_TILE
    blk = pl.BlockSpec((None, tm, d), lambda bb, i: (bb, i, 0))
    return pl.pallas_call(
        _norm1_kernel,
        grid=(b, nt // tm),
        in_specs=[blk, _mod_spec(d), pl.BlockSpec((1, d), lambda bb, i: (0, 0))],
        out_specs=blk,
        out_shape=jax.ShapeDtypeStruct((b, nt, d), F32),
        compiler_params=_params("parallel", "parallel"),
        name="ssm_norm",
    )(h, mod_tab, g)


def _ssm_kernel(u_ref, win_ref, m_ref, wout_ref, lam_ref, y_ref, bd_ref, *, chunk, n_ctx_chunks):
    nb, nt, _ = u_ref.shape
    nc = nt // chunk
    n_state_slabs = bd_ref.shape[0]
    q = n_state_slabs // 4

    def chunk_rows(bi):
        parts = [u_ref[bi, pl.ds(s, nc, stride=chunk), :] for s in range(chunk)]
        return jnp.concatenate(parts, axis=1).astype(_MXU_DTYPE)

    for bi in range(nb):
        drive = jnp.dot(chunk_rows(bi), win_ref[...], preferred_element_type=F32)
        for c in range(n_state_slabs):
            bd_ref[c, pl.ds(bi, nc, stride=nb), :] = drive[:, c * LANES:(c + 1) * LANES]

    lam = lam_ref[...]

    def make_step(base):
        a_re = [lam[:, (base + c) * LANES:(base + c + 1) * LANES] for c in range(q)]
        a_im = [lam[:, (base + q + c) * LANES:(base + q + c + 1) * LANES] for c in range(q)]

        def step(k, carry):
            row = pl.multiple_of(k * nb, nb)
            out = []
            for c in range(q):
                s_re, s_im = carry[2 * c], carry[2 * c + 1]
                d_re = bd_ref[base + c, pl.ds(row, nb), :]
                d_im = bd_ref[base + q + c, pl.ds(row, nb), :]
                bd_ref[base + c, pl.ds(row, nb), :] = s_re
                bd_ref[base + q + c, pl.ds(row, nb), :] = s_im
                out.append(a_re[c] * s_re - a_im[c] * s_im + d_re)
                out.append(a_re[c] * s_im + a_im[c] * s_re + d_im)
            return tuple(out)

        return step

    zero = tuple(jnp.zeros((nb, LANES), F32) for _ in range(2 * q))
    fwd = make_step(0)
    lax.fori_loop(0, nc, fwd, zero)
    rev = make_step(2 * q)
    carry = lax.fori_loop(0, n_ctx_chunks, lambda i, cr: rev(n_ctx_chunks - 1 - i, cr), zero)
    lax.fori_loop(0, nc - n_ctx_chunks, lambda i, cr: rev(nc - 1 - i, cr), carry)

    for bi in range(nb):
        states = jnp.concatenate([bd_ref[c, pl.ds(bi, nc, stride=nb), :] for c in range(n_state_slabs)], axis=1)
        y = (jnp.dot(chunk_rows(bi), m_ref[...], preferred_element_type=F32)
             + jnp.dot(states.astype(_MXU_DTYPE), wout_ref[...], preferred_element_type=F32))
        for t in range(chunk):
            y_ref[bi, pl.ds(t, nc, stride=chunk), :] = y[:, t * LANES:(t + 1) * LANES]


def _ssm_scan(u, win, m, wout, lam_t, ctx_len, first_slab):
    b, nt, d = u.shape
    chunk = SSM_CHUNK
    nb = 4 if b % 4 == 0 else b
    n_slabs = d // LANES
    state_w = win.shape[-1]
    nc = nt // chunk
    blk = pl.BlockSpec((nb, nt, LANES), lambda j, bb: (bb, 0, j))
    table = lambda j, bb: (first_slab + j, 0, 0)
    return pl.pallas_call(
        functools.partial(_ssm_kernel, chunk=chunk, n_ctx_chunks=ctx_len // chunk),
        grid=(n_slabs, b // nb),
        in_specs=[blk,
                  pl.BlockSpec((None,) + win.shape[1:], table),
                  pl.BlockSpec((None,) + m.shape[1:], table),
                  pl.BlockSpec((None,) + wout.shape[1:], table),
                  pl.BlockSpec((None, 1, state_w), table)],
        out_specs=blk,
        out_shape=jax.ShapeDtypeStruct((b, nt, d), F32),
        scratch_shapes=[pltpu.VMEM((state_w // LANES, nc * nb, LANES), F32)],
        compiler_params=_params("parallel", "parallel"),
        name="ssm_scan",
    )(u, win, m, wout, lam_t)


def _ssm_out_kernel(y_ref, u_ref, d_ref, wa_ref, wb_ref, h_ref, mod_ref, o_ref):
    x = y_ref[...] + d_ref[...] * u_ref[...]
    z = 0.5 * x * (1.0 + jnp.tanh(math.sqrt(2.0 / math.pi) * (x + 0.044715 * (x * x * x))))
    z = z.astype(_MXU_DTYPE)
    a = jnp.dot(z, wa_ref[...], preferred_element_type=F32)
    g = jnp.dot(z, wb_ref[...], preferred_element_type=F32)
    o_ref[...] = h_ref[...] + mod_ref[2:3, :] * (a / (1.0 + jnp.exp(-g)))


def _ssm_out(y, u, d_skip, wa, wb, h, mod_tab):
    b, nt, d = h.shape
    tm = ROW_TILE
    blk = pl.BlockSpec((None, tm, d), lambda bb, i: (bb, i, 0))
    full = lambda shape: pl.BlockSpec(shape, lambda bb, i: (0,) * len(shape))
    return pl.pallas_call(
        _ssm_out_kernel,
        grid=(b, nt // tm),
        in_specs=[blk, blk, full((1, d)), full(wa.shape), full(wb.shape), blk, _mod_spec(d)],
        out_specs=blk,
        out_shape=jax.ShapeDtypeStruct((b, nt, d), F32),
        compiler_params=_params("parallel", "parallel"),
        name="ssm_out",
    )(y, u, d_skip, wa, wb, h, mod_tab)


def _split_terms(x, n):
    terms = []
    for _ in range(n - 1):
        t = x.astype(_MXU_DTYPE)
        terms.append(t)
        x = x - t.astype(F32)
    terms.append(x.astype(_MXU_DTYPE))
    return terms


def _ssm_tables_kernel(lam_ref, bt_ref, c_ref, m_ref, win_ref, wout_ref, lamt_ref, *, chunk):
    gpt = LANES // SSM_GROUP_CH
    p = SSM_STATE
    sw = gpt * p
    ci = lax.broadcasted_iota(jnp.int32, (p, sw), 0)
    oi = lax.broadcasted_iota(jnp.int32, (p, sw), 1)
    spread = (ci == oi % p).astype(_MXU_DTYPE)
    ri = lax.broadcasted_iota(jnp.int32, (LANES, sw), 0)
    oj = lax.broadcasted_iota(jnp.int32, (LANES, sw), 1)
    own_group = (ri // SSM_GROUP_CH) == (oj // p)

    def block_diag(x):
        y = jnp.zeros((LANES, sw), F32)
        for term in _split_terms(x, 3):
            y = y + jnp.dot(term, spread, preferred_element_type=F32)
        return jnp.where(own_group, y, 0.0)

    def cmul(ar, ai, br, bi):
        return ar * br - ai * bi, ar * bi + ai * br

    nt_dot = lambda a, b: lax.dot_general(a, b, (((1,), (1,)), ((), ())), precision=_HIGHEST,
                                          preferred_element_type=F32)
    taps, drive, read = [], [], []
    for x in range(2):
        lre = jnp.minimum(lam_ref[x, 0:1, :], -1e-4)
        lim = lam_ref[x, 1:2, :]
        dt = jnp.exp(lam_ref[x, 2:3, :])
        pw = []
        for j in range(chunk + 1):
            mag = jnp.exp(float(j) * (lre * dt))
            ang = float(j) * (lim * dt)
            pw.append((mag * jnp.cos(ang), mag * jnp.sin(ang)))
        nr = pw[1][0] - 1.0
        ni = pw[1][1]
        den = lre * lre + lim * lim
        coef_re = (nr * lre + ni * lim) / den
        coef_im = (ni * lre - nr * lim) / den
        bb = cmul(coef_re, coef_im, block_diag(bt_ref[x, 0]), block_diag(bt_ref[x, 1]))
        cc = (block_diag(c_ref[x, 0]), block_diag(c_ref[x, 1]))
        drive.append([cmul(pw[j][0], pw[j][1], bb[0], bb[1]) for j in range(chunk)])
        read.append([cmul(pw[j][0], pw[j][1], cc[0], cc[1]) for j in range(chunk + 1)])
        taps.append([nt_dot(u_re, cc[0]) - nt_dot(u_im, cc[1]) for u_re, u_im in drive[x]])
        lamt_ref[:, 2 * x * sw:(2 * x + 1) * sw] = pw[chunk][0]
        lamt_ref[:, (2 * x + 1) * sw:(2 * x + 2) * sw] = pw[chunk][1]

    for s in range(chunk):
        rows = slice(s * LANES, (s + 1) * LANES)
        for t in range(chunk):
            if t > s:
                blk = taps[0][t - s]
            elif t < s:
                blk = taps[1][s - t]
            else:
                blk = taps[0][0] + taps[1][0]
            m_ref[rows, t * LANES:(t + 1) * LANES] = blk.astype(m_ref.dtype)
        f_re, f_im = drive[0][chunk - 1 - s]
        r_re, r_im = drive[1][s]
        for q, part in enumerate((f_re, f_im, r_re, r_im)):
            win_ref[rows, q * sw:(q + 1) * sw] = part.astype(win_ref.dtype)
    for t in range(chunk):
        cols = slice(t * LANES, (t + 1) * LANES)
        f_re, f_im = read[0][t + 1]
        r_re, r_im = read[1][chunk - t]
        for q, part in enumerate((f_re, -f_im, r_re, -r_im)):
            wout_ref[q * sw:(q + 1) * sw, cols] = part.T.astype(wout_ref.dtype)


def _ssm_tables(a_re, a_im, log_dt, b_re, b_im, c_re, c_im, chunk):
    n_layers, _, g_total, p = a_re.shape
    gpt = LANES // SSM_GROUP_CH
    n_slabs = g_total // gpt
    n = n_layers * n_slabs
    sw = gpt * p

    def slab_lanes(x):
        return jnp.transpose(x.reshape(n_layers, 2, n_slabs, sw), (0, 2, 1, 3)).reshape(n, 2, sw)

    def slab_rows(x):
        x = x.reshape(n_layers, 2, n_slabs, LANES, p)
        return jnp.transpose(x, (0, 2, 1, 3, 4)).reshape(n, 2, LANES, p)

    lam = jnp.stack([slab_lanes(a_re), slab_lanes(a_im),
                     slab_lanes(jnp.broadcast_to(log_dt[..., None], a_re.shape))], axis=2)
    lam = jnp.pad(lam, ((0, 0), (0, 0), (0, 5), (0, 0)))
    bt = jnp.stack([slab_rows(jnp.swapaxes(b_re, -1, -2)), slab_rows(jnp.swapaxes(b_im, -1, -2))], axis=2)
    ct = jnp.stack([slab_rows(c_re), slab_rows(c_im)], axis=2)
    rows = chunk * LANES
    per_slab = lambda *shape: pl.BlockSpec((None,) + shape, lambda i: (i,) + (0,) * len(shape))
    m, win, wout, lam_t = pl.pallas_call(
        functools.partial(_ssm_tables_kernel, chunk=chunk),
        grid=(n,),
        in_specs=[per_slab(2, 8, sw), per_slab(2, 2, LANES, p), per_slab(2, 2, LANES, p)],
        out_specs=[per_slab(rows, rows), per_slab(rows, 4 * sw), per_slab(4 * sw, rows), per_slab(1, 4 * sw)],
        out_shape=[jax.ShapeDtypeStruct((n, rows, rows), _MXU_DTYPE),
                   jax.ShapeDtypeStruct((n, rows, 4 * sw), _MXU_DTYPE),
                   jax.ShapeDtypeStruct((n, 4 * sw, rows), _MXU_DTYPE),
                   jax.ShapeDtypeStruct((n, 1, 4 * sw), F32)],
        compiler_params=_params("parallel"),
        name="ssm_tables",
    )(lam, bt, ct)
    return win, m, wout, lam_t


def _router_kernel(h_ref, mod_ref, g_ref, wr_ref, br_ref, xt_ref, cmb_ref):
    xt = _norm_mod(h_ref[...], g_ref[...], mod_ref[3:4, :], mod_ref[4:5, :])
    xt_ref[...] = xt.astype(xt_ref.dtype)
    logits = jnp.dot(xt, wr_ref[...], preferred_element_type=F32, precision=_HIGHEST) + br_ref[...]
    lane = lax.broadcasted_iota(jnp.int32, (1, LANES), 1)
    lane_f = lane.astype(F32)
    neg = -jnp.inf
    big = 1e9
    gmask = (lane >= MOE_EXPERTS) & (lane < MOE_EXPERTS + MOE_GROUPS)
    gl = jnp.where(gmask, logits, neg)
    gmax = jnp.max(gl, axis=-1, keepdims=True)
    gidx = jnp.min(jnp.where(gl == gmax, lane_f, big), axis=-1, keepdims=True) - MOE_EXPERTS
    p_group = 1.0 / jnp.sum(jnp.where(gmask, jnp.exp(gl - gmax), 0.0), axis=-1, keepdims=True)
    in_group = (lane < MOE_EXPERTS) & ((lane // MOE_EPG).astype(F32) == gidx)
    el = jnp.where(in_group, logits, neg)
    v1 = jnp.max(el, axis=-1, keepdims=True)
    i1 = jnp.min(jnp.where(el == v1, lane_f, big), axis=-1, keepdims=True)
    el2 = jnp.where(lane_f == i1, neg, el)
    v2 = jnp.max(el2, axis=-1, keepdims=True)
    i2 = jnp.min(jnp.where(el2 == v2, lane_f, big), axis=-1, keepdims=True)
    t = jnp.exp(v2 - v1)
    w1 = p_group / (1.0 + t)
    w2 = p_group * t / (1.0 + t)
    cmb_ref[...] = jnp.where(lane_f == i1, w1, 0.0) + jnp.where(lane_f == i2, w2, 0.0)


def _router(h, mod_tab, g, wr, br):
    b, nt, d = h.shape
    tm = ROW_TILE
    full = lambda shape: pl.BlockSpec(shape, lambda bb, i: (0,) * len(shape))
    return pl.pallas_call(
        _router_kernel,
        grid=(b, nt // tm),
        in_specs=[pl.BlockSpec((None, tm, d), lambda bb, i: (bb, i, 0)), _mod_spec(d),
                  full((1, d)), full(wr.shape), full(br.shape)],
        out_specs=[pl.BlockSpec((None, tm, d), lambda bb, i: (bb, i, 0)),
                   pl.BlockSpec((None, tm, LANES), lambda bb, i: (bb, i, 0))],
        out_shape=[jax.ShapeDtypeStruct((b, nt, d), _MXU_DTYPE),
                   jax.ShapeDtypeStruct((b, nt, LANES), F32)],
        compiler_params=_params("parallel", "parallel"),
        name="moe_router",
    )(h, mod_tab, g, wr, br)


def _experts_kernel(xt_ref, cmb_ref, wg_ref, wu_ref, wd_ref, h_ref, mod_ref, o_ref,
                    xs_ref, cs_ref, acc_ref, pos_ref, seg_ref, *, ctx_len, nt, window):
    i = pl.program_id(0)
    e = pl.program_id(1)
    sb = xt_ref.shape[0]
    d = xt_ref.shape[1]

    @pl.when(e == 0)
    def _():
        cmb = cmb_ref[...]
        lane = lax.broadcasted_iota(jnp.int32, (1, LANES), 1)
        routed = cmb != 0.0
        goh = jnp.zeros((sb, LANES), F32)
        for g in range(MOE_GROUPS):
            in_g = routed & (lane >= g * MOE_EPG) & (lane < (g + 1) * MOE_EPG)
            hit = jnp.max(jnp.where(in_g, 1.0, 0.0), axis=-1, keepdims=True)
            goh = goh + jnp.where(lane == g, hit, 0.0)
        r_i = lax.broadcasted_iota(jnp.int32, (sb, sb), 0)
        c_i = lax.broadcasted_iota(jnp.int32, (sb, sb), 1)
        earlier = (c_i < r_i).astype(_MXU_DTYPE)
        before = jnp.dot(earlier, goh.astype(_MXU_DTYPE), preferred_element_type=F32)
        cnt = jnp.sum(goh, axis=0, keepdims=True)
        off = jnp.zeros((1, LANES), F32)
        run = jnp.zeros((1, 1), F32)
        for g in range(MOE_GROUPS):
            off = off + jnp.where(lane == g, run, 0.0)
            run = run + jnp.sum(jnp.where(lane == g, cnt, 0.0), axis=-1, keepdims=True)
        pos = jnp.sum(goh * (off + before), axis=-1, keepdims=True)
        pos_b = jnp.broadcast_to(pos, (sb, LANES))
        pos_ref[...] = pos_b
        pos_row = pos_b.T[0:1, :].astype(jnp.int32)
        perm = (r_i == pos_row).astype(_MXU_DTYPE)
        xs_ref[0:sb, :] = jnp.dot(perm, xt_ref[...], preferred_element_type=F32).astype(xs_ref.dtype)
        cs = jnp.zeros((sb, LANES), F32)
        for term in _split_terms(cmb, 3):
            cs = cs + jnp.dot(perm, term, preferred_element_type=F32)
        cs_ref[0:sb, :] = cs
        xs_ref[sb:sb + window, :] = jnp.zeros((window, d), xs_ref.dtype)
        cs_ref[sb:sb + window, :] = jnp.zeros((window, LANES), F32)
        acc_ref[...] = jnp.zeros_like(acc_ref)
        off_i = off.astype(jnp.int32)
        cnt_i = cnt.astype(jnp.int32)
        for g in range(MOE_GROUPS):
            seg_ref[g] = off_i[0, g]
            seg_ref[MOE_GROUPS + g] = cnt_i[0, g]

    eps = wg_ref.shape[0]
    g = (e * eps) // MOE_EPG
    start = seg_ref[g]
    count = seg_ref[MOE_GROUPS + g]
    first = (start // ROW_ALIGN) * ROW_ALIGN
    n_win = (start - first + count + window - 1) // window
    n_win = jnp.where(count > 0, n_win, 0)
    lane = lax.broadcasted_iota(jnp.int32, (1, LANES), 1)

    def window_step(k, carry):
        r0 = pl.multiple_of(first + k * window, ROW_ALIGN)
        x = xs_ref[pl.ds(r0, window), :]
        cw = cs_ref[pl.ds(r0, window), :]
        y = jnp.zeros((window, d), F32)
        for j in range(eps):
            gate = jnp.dot(x, wg_ref[j], preferred_element_type=F32)
            up = jnp.dot(x, wu_ref[j], preferred_element_type=F32)
            w = jnp.sum(jnp.where(lane == e * eps + j, cw, 0.0), axis=-1, keepdims=True)
            hid = (gate / (1.0 + jnp.exp(-gate))) * up * w
            y = y + jnp.dot(hid.astype(_MXU_DTYPE), wd_ref[j], preferred_element_type=F32)
        acc_ref[pl.ds(r0, window), :] += y
        return carry

    lax.fori_loop(0, n_win, window_step, 0)

    @pl.when(e == pl.num_programs(1) - 1)
    def _():
        c_i = lax.broadcasted_iota(jnp.int32, (sb, sb), 1)
        unperm = (c_i == pos_ref[:, 0:1].astype(jnp.int32)).astype(_MXU_DTYPE)
        y = jnp.zeros((sb, d), F32)
        for term in _split_terms(acc_ref[0:sb, :], 2):
            y = y + jnp.dot(unperm, term, preferred_element_type=F32)
        row = (i * sb) % nt + lax.broadcasted_iota(jnp.int32, (sb, 1), 0)
        gate_row = jnp.where(row < ctx_len, mod_ref[0, 5:6, :], mod_ref[1, 5:6, :])
        o_ref[...] = h_ref[...] + gate_row * y


def _experts(xt, cmb, wg, wu, wd, h, mod_tab, ctx_len):
    b, nt, d = h.shape
    sb = 1152 if nt % 1152 == 0 else ROW_TILE
    window = 320 if sb == 1152 else 96
    per_b = nt // sb
    n_exp, _, hid = wg.shape
    rows = b * nt
    blk = lambda w: pl.BlockSpec((sb, w), lambda i, e: (i, 0))
    out = pl.pallas_call(
        functools.partial(_experts_kernel, ctx_len=ctx_len, nt=nt, window=window),
        grid=(rows // sb, n_exp // EXPERTS_PER_STEP),
        in_specs=[blk(d), blk(LANES),
                  pl.BlockSpec((EXPERTS_PER_STEP, d, hid), lambda i, e: (e, 0, 0)),
                  pl.BlockSpec((EXPERTS_PER_STEP, d, hid), lambda i, e: (e, 0, 0)),
                  pl.BlockSpec((EXPERTS_PER_STEP, hid, d), lambda i, e: (e, 0, 0)),
                  blk(d),
                  pl.BlockSpec((None, 2, 8, d), lambda i, e: (i // per_b, 0, 0, 0))],
        out_specs=blk(d),
        out_shape=jax.ShapeDtypeStruct((rows, d), F32),
        scratch_shapes=[pltpu.VMEM((sb + window, d), _MXU_DTYPE), pltpu.VMEM((sb + window, LANES), F32),
                        pltpu.VMEM((sb + window, d), F32), pltpu.VMEM((sb, LANES), F32),
                        pltpu.SMEM((2 * MOE_GROUPS,), jnp.int32)],
        compiler_params=_params("parallel", "arbitrary"),
        name="moe_experts",
    )(xt.reshape(rows, d), cmb.reshape(rows, LANES), wg, wu, wd, h.reshape(rows, d), mod_tab)
    return out.reshape(b, nt, d)


def _final_kernel(h_ref, g_ref, o_ref):
    h = h_ref[...]
    o_ref[...] = h * lax.rsqrt(jnp.mean(h * h, axis=-1, keepdims=True) + RMS_EPS) * g_ref[...]


def _final_norm(h, g, ctx_len):
    b, nt, d = h.shape
    tm = ROW_TILE
    skip = ctx_len // tm
    return pl.pallas_call(
        _final_kernel,
        grid=(b, (nt - ctx_len) // tm),
        in_specs=[pl.BlockSpec((None, tm, d), lambda bb, i: (bb, i + skip, 0)),
                  pl.BlockSpec((1, d), lambda bb, i: (0, 0))],
        out_specs=pl.BlockSpec((None, tm, d), lambda bb, i: (bb, i, 0)),
        out_shape=jax.ShapeDtypeStruct((b, nt - ctx_len, d), F32),
        compiler_params=_params("parallel", "parallel"),
        name="final_norm",
    )(h, g)


def _rope_tables(seq_len, ctx_len):
    n_rows = seq_len // GRID_W
    rows = jnp.repeat(jnp.arange(n_rows, dtype=F32), GRID_W)
    cols = jnp.tile(jnp.arange(GRID_W, dtype=F32), n_rows)
    half = HEAD_DIM // 2
    inv = 1.0 / (ROPE_BASE ** (jnp.arange(0, half, 2, dtype=F32) / half))
    ang_r = rows[:, None] * inv
    ang_c = cols[:, None] * inv
    ang = jnp.concatenate([ang_r, ang_r, ang_c, ang_c], axis=-1)
    ang = jnp.concatenate([jnp.zeros((ctx_len, HEAD_DIM), F32), ang], axis=0)
    ang = jnp.tile(ang, (1, LANES // HEAD_DIM))
    return jnp.cos(ang), jnp.sin(ang)


def _pad_row(v, width=LANES):
    return jnp.pad(v, (0, width - v.shape[0]))[None, :]


def kernel(x, c, ctx, c_ctx, mod_w, mod_b, norm1_g, norm2_g, final_g, attn_w_in, attn_w_out, attn_q_norm_g, attn_k_norm_g, diff_lambda_q1, diff_lambda_k1, diff_lambda_q2, diff_lambda_k2, diff_subln_g, ssm_a_re, ssm_a_im, ssm_log_dt, ssm_b_re, ssm_b_im, ssm_c_re, ssm_c_im, ssm_d, ssm_glu_w_a, ssm_glu_w_b, moe_group_w, moe_group_b, moe_router_w, moe_router_b, moe_w_gate, moe_w_up, moe_w_down):
    bsz, seq, d = x.shape
    ctx_len = ctx.shape[1]
    depth = mod_w.shape[0]
    assert ctx_len == ROW_TILE and seq % ROW_TILE == 0 and seq % GRID_W == 0

    h = jnp.concatenate([ctx, x], axis=1)

    mod_rows = 16
    c_all = jnp.concatenate([c, c_ctx[None, :], jnp.zeros((mod_rows - bsz - 1, d), F32)], axis=0)
    mods = _modulation(c_all, mod_w, mod_b).reshape(depth, mod_rows, 6, d)
    mods = jnp.pad(mods, ((0, 0), (0, 0), (0, 2), (0, 0)))
    mod_tabs = jnp.stack([jnp.broadcast_to(mods[:, bsz:bsz + 1], (depth, bsz, 8, d)), mods[:, :bsz]], axis=2)

    cos, sin = _rope_tables(seq, ctx_len)
    cast = lambda w: w.astype(_MXU_DTYPE)
    ssm_tabs = _ssm_tables(ssm_a_re, ssm_a_im, ssm_log_dt, ssm_b_re, ssm_b_im, ssm_c_re, ssm_c_im, SSM_CHUNK)

    for layer in range(depth):
        mod_tab = mod_tabs[layer]
        i = layer // 2
        if layer % 2 == 0:
            lambda_init = 0.8 - 0.6 * math.exp(-0.3 * layer)
            qkv = _attn_proj(h, mod_tab, norm1_g[layer][None, :], cast(attn_w_in[i]),
                             jnp.tile(attn_q_norm_g[i], 2)[None, :], jnp.tile(attn_k_norm_g[i], 2)[None, :],
                             cos, sin)
            lam_rows = jnp.concatenate([_pad_row(diff_lambda_q1[i]), _pad_row(diff_lambda_k1[i]),
                                        _pad_row(diff_lambda_q2[i]), _pad_row(diff_lambda_k2[i]),
                                        jnp.zeros((4, LANES), F32)], axis=0)
            h = _attention(qkv, lam_rows, diff_subln_g[i][None, :], cast(attn_w_out[i]), h, mod_tab,
                           lambda_init, ctx_len)
        else:
            u = _norm1(h, mod_tab, norm1_g[layer][None, :])
            y = _ssm_scan(u, *ssm_tabs, ctx_len, i * (d // LANES))
            h = _ssm_out(y, u, ssm_d[i][None, :], cast(ssm_glu_w_a[i]), cast(ssm_glu_w_b[i]), h, mod_tab)

        wr = jnp.concatenate([jnp.transpose(moe_router_w[layer], (1, 0, 2)).reshape(d, MOE_EXPERTS),
                              moe_group_w[layer],
                              jnp.zeros((d, LANES - MOE_EXPERTS - MOE_GROUPS), F32)], axis=1)
        br = _pad_row(jnp.concatenate([moe_router_b[layer].reshape(-1), moe_group_b[layer]]))
        xt, cmb = _router(h, mod_tab, norm2_g[layer][None, :], wr, br)
        h = _experts(xt, cmb, cast(moe_w_gate[layer]), cast(moe_w_up[layer]), cast(moe_w_down[layer]),
                     h, mod_tab, ctx_len)

    return _final_norm(h, final_g[None, :], ctx_len)
```

```python
import functools
import math

import jax
import jax.numpy as jnp
from jax import lax
from jax.experimental import pallas as pl
from jax.experimental.pallas import tpu as pltpu

F32 = jnp.float32
_MXU_DTYPE = jnp.bfloat16
_HIGHEST = lax.Precision.HIGHEST

LANES = 128
HEAD_DIM = 64
GRID_W = 64
ROPE_BASE = 10000.0
GQA_Q_HEADS = 8
GQA_GROUP = 4
DIFF_HEADS = 4
GQA_Q_W = 512
GQA_KV_W = 128
DIFF_QK_W = 512
DIFF_V_W = 512
SSM_GROUP_CH = 16
SSM_STATE = 64
MOE_GROUPS = 4
MOE_EPG = 8
MOE_EXPERTS = 32
RMS_EPS = 1e-6
SSM_CHUNK = 8
ROW_TILE = 256
ROW_ALIGN = 16
EXPERTS_PER_STEP = 4
VMEM_LIMIT = 56 * 1024 * 1024


def _params(*sem):
    return pltpu.CompilerParams(dimension_semantics=sem, vmem_limit_bytes=VMEM_LIMIT)


def _norm_mod(h, g, shift, scale):
    y = h * lax.rsqrt(jnp.mean(h * h, axis=-1, keepdims=True) + RMS_EPS) * g
    return y * (1.0 + scale) + shift


def _mm(a, b):
    return jnp.dot(a.astype(_MXU_DTYPE), b.astype(_MXU_DTYPE), preferred_element_type=F32)


def _mod_kernel(c_ref, w_ref, b_ref, o_ref):
    c = c_ref[...]
    a = c / (1.0 + jnp.exp(-c))
    o_ref[...] = jnp.dot(a, w_ref[...], preferred_element_type=F32, precision=_HIGHEST) + b_ref[...]


def _modulation(c_all, mod_w, mod_b):
    depth, d, n = mod_w.shape
    rows = c_all.shape[0]
    tn = 1536
    return pl.pallas_call(
        _mod_kernel,
        grid=(depth, n // tn),
        in_specs=[pl.BlockSpec((rows, d), lambda l, j: (0, 0)),
                  pl.BlockSpec((None, d, tn), lambda l, j: (l, 0, j)),
                  pl.BlockSpec((None, 1, tn), lambda l, j: (l, 0, j))],
        out_specs=pl.BlockSpec((None, rows, tn), lambda l, j: (l, 0, j)),
        out_shape=jax.ShapeDtypeStruct((depth, rows, n), F32),
        compiler_params=_params("parallel", "parallel"),
        name="modulation",
    )(c_all, mod_w, mod_b.reshape(depth, 1, n))


def _mod_spec(d):
    return pl.BlockSpec((None, None, 8, d), lambda b, i: (b, jnp.minimum(i, 1), 0, 0))


def _attn_proj_kernel(h_ref, mod_ref, g_ref, w_ref, gq_ref, gk_ref, cos_ref, sin_ref,
                      qa_ref, ka_ref, va_ref, qb_ref, kb_ref, vb_ref):
    xn = _norm_mod(h_ref[...], g_ref[...], mod_ref[0:1, :], mod_ref[1:2, :])
    hp = _mm(xn, w_ref[...])
    cos = cos_ref[...]
    sin = sin_ref[...]
    lane = lax.broadcasted_iota(jnp.int32, (1, LANES), 1)
    first_half = (lane % 32) < 16
    r = lax.broadcasted_iota(jnp.int32, (LANES, LANES), 0) // HEAD_DIM
    c = lax.broadcasted_iota(jnp.int32, (LANES, LANES), 1) // HEAD_DIM
    same_head = (r == c).astype(_MXU_DTYPE)

    def rope(x):
        rot = jnp.where(first_half, -pltpu.roll(x, LANES - 16, 1), pltpu.roll(x, 16, 1))
        return x * cos + rot * sin

    def head_norm(x, g):
        ss = jnp.dot((x * x).astype(_MXU_DTYPE), same_head, preferred_element_type=F32)
        return x * lax.rsqrt(ss * (1.0 / HEAD_DIM) + RMS_EPS) * g

    scale = HEAD_DIM ** -0.5 * math.log2(math.e)
    o = 0
    for s in range(GQA_Q_W // LANES):
        x = hp[:, o + s * LANES:o + (s + 1) * LANES]
        qa_ref[:, s * LANES:(s + 1) * LANES] = (rope(head_norm(x, gq_ref[...])) * scale).astype(qa_ref.dtype)
    o += GQA_Q_W
    ka_ref[...] = rope(head_norm(hp[:, o:o + LANES], gk_ref[...])).astype(ka_ref.dtype)
    o += GQA_KV_W
    v_pair = hp[:, o:o + LANES]
    low_half = lane < HEAD_DIM
    va_ref[:, 0:LANES] = jnp.where(low_half, v_pair, 1.0).astype(va_ref.dtype)
    va_ref[:, LANES:2 * LANES] = jnp.where(low_half, pltpu.roll(v_pair, HEAD_DIM, 1), 1.0).astype(va_ref.dtype)
    o += GQA_KV_W
    for s in range(DIFF_QK_W // LANES):
        x = hp[:, o + s * LANES:o + (s + 1) * LANES]
        qb_ref[:, s * LANES:(s + 1) * LANES] = (rope(x) * scale).astype(qb_ref.dtype)
    o += DIFF_QK_W
    for s in range(DIFF_QK_W // LANES):
        x = hp[:, o + s * LANES:o + (s + 1) * LANES]
        kb_ref[:, s * LANES:(s + 1) * LANES] = rope(x).astype(kb_ref.dtype)
    o += DIFF_QK_W
    for s in range(DIFF_HEADS):
        vb_ref[:, 2 * s * LANES:(2 * s + 1) * LANES] = hp[:, o + s * LANES:o + (s + 1) * LANES].astype(vb_ref.dtype)
        vb_ref[:, (2 * s + 1) * LANES:(2 * s + 2) * LANES] = jnp.ones((hp.shape[0], LANES), vb_ref.dtype)


def _attn_proj(h, mod_tab, g, w_in, gq, gk, cos, sin):
    b, nt, d = h.shape
    tm = ROW_TILE
    widths = (GQA_Q_W, GQA_KV_W, 2 * GQA_KV_W, DIFF_QK_W, DIFF_QK_W, 2 * DIFF_V_W)
    full = lambda shape: pl.BlockSpec(shape, lambda bb, i: (0,) * len(shape))
    return pl.pallas_call(
        _attn_proj_kernel,
        grid=(b, nt // tm),
        in_specs=[pl.BlockSpec((None, tm, d), lambda bb, i: (bb, i, 0)),
                  _mod_spec(d),
                  full((1, d)),
                  full(w_in.shape),
                  full((1, LANES)),
                  full((1, LANES)),
                  pl.BlockSpec((tm, LANES), lambda bb, i: (i, 0)),
                  pl.BlockSpec((tm, LANES), lambda bb, i: (i, 0))],
        out_specs=[pl.BlockSpec((None, tm, w), lambda bb, i: (bb, i, 0)) for w in widths],
        out_shape=[jax.ShapeDtypeStruct((b, nt, w), _MXU_DTYPE) for w in widths],
        compiler_params=_params("parallel", "parallel"),
        name="attn_proj",
    )(h, mod_tab, g, w_in, gq, gk, cos, sin)


def _attn_kernel(qa_ref, qb_ref, ka_ref, va_ref, kb_ref, vb_ref, lam_ref, sg_ref, wo_ref, h_ref, mod_ref,
                 o_ref, mrg_ref, *, lambda_init, ctx_len):
    lv = lam_ref[...]
    lam = (jnp.exp(jnp.sum(lv[0:1] * lv[1:2], axis=-1, keepdims=True))
           - jnp.exp(jnp.sum(lv[2:3] * lv[3:4], axis=-1, keepdims=True)) + lambda_init)

    def run(nk):
        tq = qa_ref.shape[0]
        qk = lambda q, k: lax.dot_general(q, k, (((1,), (1,)), ((), ())), preferred_element_type=F32)
        probs = lambda s: jnp.exp2((s - jnp.max(s, axis=-1, keepdims=True)).astype(_MXU_DTYPE))

        def normalised(o, dv):
            return o[:, 0:dv] / o[:, dv:dv + 1]

        def gqa_scores(g):
            q = jnp.concatenate([qa_ref[:, h * HEAD_DIM:(h + 1) * HEAD_DIM]
                                 for h in range(g * GQA_GROUP, (g + 1) * GQA_GROUP)], axis=0)
            return [qk(q, ka_ref[0:nk, g * HEAD_DIM:(g + 1) * HEAD_DIM])]

        def gqa_attend(g, p):
            o = jnp.dot(p[0], va_ref[0:nk, g * LANES:(g + 1) * LANES], preferred_element_type=F32)
            for j in range(GQA_GROUP):
                h = g * GQA_GROUP + j
                mrg_ref[:, h * HEAD_DIM:(h + 1) * HEAD_DIM] = normalised(
                    o[j * tq:(j + 1) * tq], HEAD_DIM).astype(mrg_ref.dtype)

        def diff_scores(h):
            c0 = h * 2 * HEAD_DIM
            return [qk(qb_ref[:, c:c + HEAD_DIM], kb_ref[0:nk, c:c + HEAD_DIM]) for c in (c0, c0 + HEAD_DIM)]

        def diff_attend(h, p):
            c0 = h * 2 * HEAD_DIM
            o = jnp.dot(jnp.concatenate(p, axis=0), vb_ref[0:nk, 2 * c0:2 * c0 + 2 * LANES],
                        preferred_element_type=F32)
            o = normalised(o[0:tq], 2 * HEAD_DIM) - lam * normalised(o[tq:2 * tq], 2 * HEAD_DIM)
            o = o * lax.rsqrt(jnp.mean(o * o, axis=-1, keepdims=True) + RMS_EPS) * sg_ref[...]
            o = o * (1.0 - lambda_init)
            mrg_ref[:, GQA_Q_W + c0:GQA_Q_W + c0 + 2 * HEAD_DIM] = o.astype(mrg_ref.dtype)

        units = ([(functools.partial(gqa_scores, g), functools.partial(gqa_attend, g))
                  for g in range(GQA_Q_HEADS // GQA_GROUP)]
                 + [(functools.partial(diff_scores, h), functools.partial(diff_attend, h))
                    for h in range(DIFF_HEADS)])
        n = len(units)
        s, p = [None] * n, [None] * n
        for step in range(n + 2):
            if step < n:
                s[step] = units[step][0]()
            if 1 <= step <= n:
                p[step - 1] = [probs(x) for x in s[step - 1]]
                s[step - 1] = None
            if 2 <= step:
                units[step - 2][1](p[step - 2])
                p[step - 2] = None

    i = pl.program_id(1)

    @pl.when(i == 0)
    def _():
        run(ctx_len)

    @pl.when(i > 0)
    def _():
        run(ka_ref.shape[0])

    y = jnp.dot(mrg_ref[...], wo_ref[...], preferred_element_type=F32)
    o_ref[...] = h_ref[...] + mod_ref[2:3, :] * y


def _attention(qkv, lam_rows, subln_g, w_out, h, mod_tab, lambda_init, ctx_len):
    qa, ka, va, qb, kb, vb = qkv
    b, nt, d = h.shape
    tq = ROW_TILE
    assert ctx_len == tq
    blk = lambda w: pl.BlockSpec((None, tq, w), lambda bb, i: (bb, i, 0))
    per_batch = lambda w: pl.BlockSpec((None, nt, w), lambda bb, i: (bb, 0, 0))
    full = lambda shape: pl.BlockSpec(shape, lambda bb, i: (0,) * len(shape))
    return pl.pallas_call(
        functools.partial(_attn_kernel, lambda_init=lambda_init, ctx_len=ctx_len),
        grid=(b, nt // tq),
        in_specs=[blk(GQA_Q_W), blk(DIFF_QK_W), per_batch(GQA_KV_W), per_batch(2 * GQA_KV_W),
                  per_batch(DIFF_QK_W), per_batch(2 * DIFF_V_W),
                  full((8, LANES)), full((1, LANES)), full(w_out.shape), blk(d), _mod_spec(d)],
        out_specs=blk(d),
        out_shape=jax.ShapeDtypeStruct((b, nt, d), F32),
        scratch_shapes=[pltpu.VMEM((tq, GQA_Q_W + DIFF_V_W), _MXU_DTYPE)],
        compiler_params=_params("parallel", "parallel"),
        name="attention",
    )(qa, qb, ka, va, kb, vb, lam_rows, subln_g, w_out, h, mod_tab)


def _norm1_kernel(h_ref, mod_ref, g_ref, o_ref):
    o_ref[...] = _norm_mod(h_ref[...], g_ref[...], mod_ref[0:1, :], mod_ref[1:2, :])


def _norm1(h, mod_tab, g):
    b, nt, d = h.shape
    tm = ROW_TILE
    blk = pl.BlockSpec((None, tm, d), lambda bb, i: (bb, i, 0))
    return pl.pallas_call(
        _norm1_kernel,
        grid=(b, nt // tm),
        in_specs=[blk, _mod_spec(d), pl.BlockSpec((1, d), lambda bb, i: (0, 0))],
        out_specs=blk,
        out_shape=jax.ShapeDtypeStruct((b, nt, d), F32),
        compiler_params=_params("parallel", "parallel"),
        name="ssm_norm",
    )(h, mod_tab, g)


def _ssm_kernel(u_ref, win_ref, m_ref, wout_ref, lam_ref, y_ref, bd_ref, *, chunk, n_ctx_chunks):
    nb, nt, _ = u_ref.shape
    nc = nt // chunk
    n_state_slabs = bd_ref.shape[0]
    q = n_state_slabs // 4

    def chunk_rows(bi):
        parts = [u_ref[bi, pl.ds(s, nc, stride=chunk), :] for s in range(chunk)]
        return jnp.concatenate(parts, axis=1).astype(_MXU_DTYPE)

    for bi in range(nb):
        drive = jnp.dot(chunk_rows(bi), win_ref[...], preferred_element_type=F32)
        for c in range(n_state_slabs):
            bd_ref[c, pl.ds(bi, nc, stride=nb), :] = drive[:, c * LANES:(c + 1) * LANES]

    lam = lam_ref[...]

    def make_step(base):
        a_re = [lam[:, (base + c) * LANES:(base + c + 1) * LANES] for c in range(q)]
        a_im = [lam[:, (base + q + c) * LANES:(base + q + c + 1) * LANES] for c in range(q)]

        def step(k, carry):
            row = pl.multiple_of(k * nb, nb)
            out = []
            for c in range(q):
                s_re, s_im = carry[2 * c], carry[2 * c + 1]
                d_re = bd_ref[base + c, pl.ds(row, nb), :]
                d_im = bd_ref[base + q + c, pl.ds(row, nb), :]
                bd_ref[base + c, pl.ds(row, nb), :] = s_re
                bd_ref[base + q + c, pl.ds(row, nb), :] = s_im
                out.append(a_re[c] * s_re - a_im[c] * s_im + d_re)
                out.append(a_re[c] * s_im + a_im[c] * s_re + d_im)
            return tuple(out)

        return step

    zero = tuple(jnp.zeros((nb, LANES), F32) for _ in range(2 * q))
    fwd = make_step(0)
    lax.fori_loop(0, nc, fwd, zero)
    rev = make_step(2 * q)
    carry = lax.fori_loop(0, n_ctx_chunks, lambda i, cr: rev(n_ctx_chunks - 1 - i, cr), zero)
    lax.fori_loop(0, nc - n_ctx_chunks, lambda i, cr: rev(nc - 1 - i, cr), carry)

    for bi in range(nb):
        states = jnp.concatenate([bd_ref[c, pl.ds(bi, nc, stride=nb), :] for c in range(n_state_slabs)], axis=1)
        y = (jnp.dot(chunk_rows(bi), m_ref[...], preferred_element_type=F32)
             + jnp.dot(states.astype(_MXU_DTYPE), wout_ref[...], preferred_element_type=F32))
        for t in range(chunk):
            y_ref[bi, pl.ds(t, nc, stride=chunk), :] = y[:, t * LANES:(t + 1) * LANES]


def _ssm_scan(u, win, m, wout, lam_t, ctx_len, first_slab):
    b, nt, d = u.shape
    chunk = SSM_CHUNK
    nb = 4 if b % 4 == 0 else b
    n_slabs = d // LANES
    state_w = win.shape[-1]
    nc = nt // chunk
    blk = pl.BlockSpec((nb, nt, LANES), lambda j, bb: (bb, 0, j))
    table = lambda j, bb: (first_slab + j, 0, 0)
    return pl.pallas_call(
        functools.partial(_ssm_kernel, chunk=chunk, n_ctx_chunks=ctx_len // chunk),
        grid=(n_slabs, b // nb),
        in_specs=[blk,
                  pl.BlockSpec((None,) + win.shape[1:], table),
                  pl.BlockSpec((None,) + m.shape[1:], table),
                  pl.BlockSpec((None,) + wout.shape[1:], table),
                  pl.BlockSpec((None, 1, state_w), table)],
        out_specs=blk,
        out_shape=jax.ShapeDtypeStruct((b, nt, d), F32),
        scratch_shapes=[pltpu.VMEM((state_w // LANES, nc * nb, LANES), F32)],
        compiler_params=_params("parallel", "parallel"),
        name="ssm_scan",
    )(u, win, m, wout, lam_t)


def _ssm_out_kernel(y_ref, u_ref, d_ref, wa_ref, wb_ref, h_ref, mod_ref, o_ref):
    x = y_ref[...] + d_ref[...] * u_ref[...]
    z = 0.5 * x * (1.0 + jnp.tanh(math.sqrt(2.0 / math.pi) * (x + 0.044715 * (x * x * x))))
    z = z.astype(_MXU_DTYPE)
    a = jnp.dot(z, wa_ref[...], preferred_element_type=F32)
    g = jnp.dot(z, wb_ref[...], preferred_element_type=F32)
    o_ref[...] = h_ref[...] + mod_ref[2:3, :] * (a / (1.0 + jnp.exp(-g)))


def _ssm_out(y, u, d_skip, wa, wb, h, mod_tab):
    b, nt, d = h.shape
    tm = ROW_TILE
    blk = pl.BlockSpec((None, tm, d), lambda bb, i: (bb, i, 0))
    full = lambda shape: pl.BlockSpec(shape, lambda bb, i: (0,) * len(shape))
    return pl.pallas_call(
        _ssm_out_kernel,
        grid=(b, nt // tm),
        in_specs=[blk, blk, full((1, d)), full(wa.shape), full(wb.shape), blk, _mod_spec(d)],
        out_specs=blk,
        out_shape=jax.ShapeDtypeStruct((b, nt, d), F32),
        compiler_params=_params("parallel", "parallel"),
        name="ssm_out",
    )(y, u, d_skip, wa, wb, h, mod_tab)


def _split_terms(x, n):
    terms = []
    for _ in range(n - 1):
        t = x.astype(_MXU_DTYPE)
        terms.append(t)
        x = x - t.astype(F32)
    terms.append(x.astype(_MXU_DTYPE))
    return terms


def _ssm_tables_kernel(lam_ref, bt_ref, c_ref, m_ref, win_ref, wout_ref, lamt_ref, *, chunk):
    gpt = LANES // SSM_GROUP_CH
    p = SSM_STATE
    sw = gpt * p
    ci = lax.broadcasted_iota(jnp.int32, (p, sw), 0)
    oi = lax.broadcasted_iota(jnp.int32, (p, sw), 1)
    spread = (ci == oi % p).astype(_MXU_DTYPE)
    ri = lax.broadcasted_iota(jnp.int32, (LANES, sw), 0)
    oj = lax.broadcasted_iota(jnp.int32, (LANES, sw), 1)
    own_group = (ri // SSM_GROUP_CH) == (oj // p)

    def block_diag(x):
        y = jnp.zeros((LANES, sw), F32)
        for term in _split_terms(x, 3):
            y = y + jnp.dot(term, spread, preferred_element_type=F32)
        return jnp.where(own_group, y, 0.0)

    def cmul(ar, ai, br, bi):
        return ar * br - ai * bi, ar * bi + ai * br

    nt_dot = lambda a, b: lax.dot_general(a, b, (((1,), (1,)), ((), ())), precision=_HIGHEST,
                                          preferred_element_type=F32)
    taps, drive, read = [], [], []
    for x in range(2):
        lre = jnp.minimum(lam_ref[x, 0:1, :], -1e-4)
        lim = lam_ref[x, 1:2, :]
        dt = jnp.exp(lam_ref[x, 2:3, :])
        pw = []
        for j in range(chunk + 1):
            mag = jnp.exp(float(j) * (lre * dt))
            ang = float(j) * (lim * dt)
            pw.append((mag * jnp.cos(ang), mag * jnp.sin(ang)))
        nr = pw[1][0] - 1.0
        ni = pw[1][1]
        den = lre * lre + lim * lim
        coef_re = (nr * lre + ni * lim) / den
        coef_im = (ni * lre - nr * lim) / den
        bb = cmul(coef_re, coef_im, block_diag(bt_ref[x, 0]), block_diag(bt_ref[x, 1]))
        cc = (block_diag(c_ref[x, 0]), block_diag(c_ref[x, 1]))
        drive.append([cmul(pw[j][0], pw[j][1], bb[0], bb[1]) for j in range(chunk)])
        read.append([cmul(pw[j][0], pw[j][1], cc[0], cc[1]) for j in range(chunk + 1)])
        taps.append([nt_dot(u_re, cc[0]) - nt_dot(u_im, cc[1]) for u_re, u_im in drive[x]])
        lamt_ref[:, 2 * x * sw:(2 * x + 1) * sw] = pw[chunk][0]
        lamt_ref[:, (2 * x + 1) * sw:(2 * x + 2) * sw] = pw[chunk][1]

    for s in range(chunk):
        rows = slice(s * LANES, (s + 1) * LANES)
        for t in range(chunk):
            if t > s:
                blk = taps[0][t - s]
            elif t < s:
                blk = taps[1][s - t]
            else:
                blk = taps[0][0] + taps[1][0]
            m_ref[rows, t * LANES:(t + 1) * LANES] = blk.astype(m_ref.dtype)
        f_re, f_im = drive[0][chunk - 1 - s]
        r_re, r_im = drive[1][s]
        for q, part in enumerate((f_re, f_im, r_re, r_im)):
            win_ref[rows, q * sw:(q + 1) * sw] = part.astype(win_ref.dtype)
    for t in range(chunk):
        cols = slice(t * LANES, (t + 1) * LANES)
        f_re, f_im = read[0][t + 1]
        r_re, r_im = read[1][chunk - t]
        for q, part in enumerate((f_re, -f_im, r_re, -r_im)):
            wout_ref[q * sw:(q + 1) * sw, cols] = part.T.astype(wout_ref.dtype)


def _ssm_tables(a_re, a_im, log_dt, b_re, b_im, c_re, c_im, chunk):
    n_layers, _, g_total, p = a_re.shape
    gpt = LANES // SSM_GROUP_CH
    n_slabs = g_total // gpt
    n = n_layers * n_slabs
    sw = gpt * p

    def slab_lanes(x):
        return jnp.transpose(x.reshape(n_layers, 2, n_slabs, sw), (0, 2, 1, 3)).reshape(n, 2, sw)

    def slab_rows(x):
        x = x.reshape(n_layers, 2, n_slabs, LANES, p)
        return jnp.transpose(x, (0, 2, 1, 3, 4)).reshape(n, 2, LANES, p)

    lam = jnp.stack([slab_lanes(a_re), slab_lanes(a_im),
                     slab_lanes(jnp.broadcast_to(log_dt[..., None], a_re.shape))], axis=2)
    lam = jnp.pad(lam, ((0, 0), (0, 0), (0, 5), (0, 0)))
    bt = jnp.stack([slab_rows(jnp.swapaxes(b_re, -1, -2)), slab_rows(jnp.swapaxes(b_im, -1, -2))], axis=2)
    ct = jnp.stack([slab_rows(c_re), slab_rows(c_im)], axis=2)
    rows = chunk * LANES
    per_slab = lambda *shape: pl.BlockSpec((None,) + shape, lambda i: (i,) + (0,) * len(shape))
    m, win, wout, lam_t = pl.pallas_call(
        functools.partial(_ssm_tables_kernel, chunk=chunk),
        grid=(n,),
        in_specs=[per_slab(2, 8, sw), per_slab(2, 2, LANES, p), per_slab(2, 2, LANES, p)],
        out_specs=[per_slab(rows, rows), per_slab(rows, 4 * sw), per_slab(4 * sw, rows), per_slab(1, 4 * sw)],
        out_shape=[jax.ShapeDtypeStruct((n, rows, rows), _MXU_DTYPE),
                   jax.ShapeDtypeStruct((n, rows, 4 * sw), _MXU_DTYPE),
                   jax.ShapeDtypeStruct((n, 4 * sw, rows), _MXU_DTYPE),
                   jax.ShapeDtypeStruct((n, 1, 4 * sw), F32)],
        compiler_params=_params("parallel"),
        name="ssm_tables",
    )(lam, bt, ct)
    return win, m, wout, lam_t


def _router_kernel(h_ref, mod_ref, g_ref, wr_ref, br_ref, xt_ref, cmb_ref):
    xt = _norm_mod(h_ref[...], g_ref[...], mod_ref[3:4, :], mod_ref[4:5, :])
    xt_ref[...] = xt.astype(xt_ref.dtype)
    x_hi, x_lo = _split_terms(xt, 2)
    w_hi, w_lo = _split_terms(wr_ref[...], 2)
    dot = lambda a, b: jnp.dot(a, b, preferred_element_type=F32)
    logits = dot(x_hi, w_hi) + (dot(x_hi, w_lo) + dot(x_lo, w_hi)) + br_ref[...]
    lane = lax.broadcasted_iota(jnp.int32, (1, LANES), 1)
    lane_f = lane.astype(F32)
    neg = -jnp.inf
    big = 1e9
    gmask = (lane >= MOE_EXPERTS) & (lane < MOE_EXPERTS + MOE_GROUPS)
    gl = jnp.where(gmask, logits, neg)
    gmax = jnp.max(gl, axis=-1, keepdims=True)
    gidx = jnp.min(jnp.where(gl == gmax, lane_f, big), axis=-1, keepdims=True) - MOE_EXPERTS
    p_group = 1.0 / jnp.sum(jnp.where(gmask, jnp.exp(gl - gmax), 0.0), axis=-1, keepdims=True)
    in_group = (lane < MOE_EXPERTS) & ((lane // MOE_EPG).astype(F32) == gidx)
    el = jnp.where(in_group, logits, neg)
    v1 = jnp.max(el, axis=-1, keepdims=True)
    i1 = jnp.min(jnp.where(el == v1, lane_f, big), axis=-1, keepdims=True)
    el2 = jnp.where(lane_f == i1, neg, el)
    v2 = jnp.max(el2, axis=-1, keepdims=True)
    i2 = jnp.min(jnp.where(el2 == v2, lane_f, big), axis=-1, keepdims=True)
    t = jnp.exp(v2 - v1)
    w1 = p_group / (1.0 + t)
    w2 = p_group * t / (1.0 + t)
    cmb_ref[...] = jnp.where(lane_f == i1, w1, 0.0) + jnp.where(lane_f == i2, w2, 0.0)


def _router(h, mod_tab, g, wr, br):
    b, nt, d = h.shape
    tm = ROW_TILE
    full = lambda shape: pl.BlockSpec(shape, lambda bb, i: (0,) * len(shape))
    return pl.pallas_call(
        _router_kernel,
        grid=(b, nt // tm),
        in_specs=[pl.BlockSpec((None, tm, d), lambda bb, i: (bb, i, 0)), _mod_spec(d),
                  full((1, d)), full(wr.shape), full(br.shape)],
        out_specs=[pl.BlockSpec((None, tm, d), lambda bb, i: (bb, i, 0)),
                   pl.BlockSpec((None, tm, LANES), lambda bb, i: (bb, i, 0))],
        out_shape=[jax.ShapeDtypeStruct((b, nt, d), _MXU_DTYPE),
                   jax.ShapeDtypeStruct((b, nt, LANES), F32)],
        compiler_params=_params("parallel", "parallel"),
        name="moe_router",
    )(h, mod_tab, g, wr, br)


def _experts_kernel(xt_ref, cmb_ref, wg_ref, wu_ref, wd_ref, h_ref, mod_ref, o_ref,
                    xs_ref, cs_ref, acc_ref, pos_ref, seg_ref, *, ctx_len, nt, window):
    i = pl.program_id(0)
    e = pl.program_id(1)
    sb = xt_ref.shape[0]
    d = xt_ref.shape[1]

    @pl.when(e == 0)
    def _():
        cmb = cmb_ref[...]
        lane = lax.broadcasted_iota(jnp.int32, (1, LANES), 1)
        routed = cmb != 0.0
        goh = jnp.zeros((sb, LANES), F32)
        for g in range(MOE_GROUPS):
            in_g = routed & (lane >= g * MOE_EPG) & (lane < (g + 1) * MOE_EPG)
            hit = jnp.max(jnp.where(in_g, 1.0, 0.0), axis=-1, keepdims=True)
            goh = goh + jnp.where(lane == g, hit, 0.0)
        tri = (lax.broadcasted_iota(jnp.int32, (LANES, LANES), 1)
               < lax.broadcasted_iota(jnp.int32, (LANES, LANES), 0)).astype(_MXU_DTYPE)
        cnt = jnp.zeros((1, LANES), F32)
        parts = []
        for k in range(sb // LANES):
            tile = goh[k * LANES:(k + 1) * LANES]
            parts.append(jnp.dot(tri, tile.astype(_MXU_DTYPE), preferred_element_type=F32) + cnt)
            cnt = cnt + jnp.sum(tile, axis=0, keepdims=True)
        before = jnp.concatenate(parts, axis=0)
        r_i = lax.broadcasted_iota(jnp.int32, (sb, sb), 0)
        off = jnp.zeros((1, LANES), F32)
        run = jnp.zeros((1, 1), F32)
        for g in range(MOE_GROUPS):
            off = off + jnp.where(lane == g, run, 0.0)
            run = run + jnp.sum(jnp.where(lane == g, cnt, 0.0), axis=-1, keepdims=True)
        pos = jnp.sum(goh * (off + before), axis=-1, keepdims=True)
        pos_b = jnp.broadcast_to(pos, (sb, LANES))
        pos_ref[...] = pos_b
        pos_row = pos_b.T[0:1, :].astype(jnp.int32)
        perm = (r_i == pos_row).astype(_MXU_DTYPE)
        xs_ref[0:sb, :] = jnp.dot(perm, xt_ref[...], preferred_element_type=F32).astype(xs_ref.dtype)
        cs = jnp.zeros((sb, LANES), F32)
        for term in _split_terms(cmb, 3):
            cs = cs + jnp.dot(perm, term, preferred_element_type=F32)
        cs_ref[0:sb, :] = cs
        xs_ref[sb:sb + window, :] = jnp.zeros((window, d), xs_ref.dtype)
        cs_ref[sb:sb + window, :] = jnp.zeros((window, LANES), F32)
        acc_ref[...] = jnp.zeros_like(acc_ref)
        off_i = off.astype(jnp.int32)
        cnt_i = cnt.astype(jnp.int32)
        for g in range(MOE_GROUPS):
            seg_ref[g] = off_i[0, g]
            seg_ref[MOE_GROUPS + g] = cnt_i[0, g]

    eps = wg_ref.shape[0]
    g = (e * eps) // MOE_EPG
    start = seg_ref[g]
    count = seg_ref[MOE_GROUPS + g]
    first = (start // ROW_ALIGN) * ROW_ALIGN
    n_win = (start - first + count + window - 1) // window
    n_win = jnp.where(count > 0, n_win, 0)
    lane = lax.broadcasted_iota(jnp.int32, (1, LANES), 1)

    def window_step(k, carry):
        r0 = pl.multiple_of(first + k * window, ROW_ALIGN)
        x = xs_ref[pl.ds(r0, window), :]
        cw = cs_ref[pl.ds(r0, window), :]
        y = jnp.zeros((window, d), F32)
        for j in range(eps):
            gate = jnp.dot(x, wg_ref[j], preferred_element_type=F32)
            up = jnp.dot(x, wu_ref[j], preferred_element_type=F32)
            w = jnp.sum(jnp.where(lane == e * eps + j, cw, 0.0), axis=-1, keepdims=True)
            hid = (gate / (1.0 + jnp.exp(-gate))) * up * w
            y = y + jnp.dot(hid.astype(_MXU_DTYPE), wd_ref[j], preferred_element_type=F32)
        acc_ref[pl.ds(r0, window), :] += y
        return carry

    lax.fori_loop(0, n_win, window_step, 0)

    @pl.when(e == pl.num_programs(1) - 1)
    def _():
        c_i = lax.broadcasted_iota(jnp.int32, (sb, sb), 1)
        unperm = (c_i == pos_ref[:, 0:1].astype(jnp.int32)).astype(_MXU_DTYPE)
        y = jnp.zeros((sb, d), F32)
        for term in _split_terms(acc_ref[0:sb, :], 2):
            y = y + jnp.dot(unperm, term, preferred_element_type=F32)
        row = (i * sb) % nt + lax.broadcasted_iota(jnp.int32, (sb, 1), 0)
        gate_row = jnp.where(row < ctx_len, mod_ref[0, 5:6, :], mod_ref[1, 5:6, :])
        o_ref[...] = h_ref[...] + gate_row * y


def _experts(xt, cmb, wg, wu, wd, h, mod_tab, ctx_len):
    b, nt, d = h.shape
    sb = 1152 if nt % 1152 == 0 else ROW_TILE
    window = 336 if sb == 1152 else 96
    per_b = nt // sb
    n_exp, _, hid = wg.shape
    rows = b * nt
    blk = lambda w: pl.BlockSpec((sb, w), lambda i, e: (i, 0))
    out = pl.pallas_call(
        functools.partial(_experts_kernel, ctx_len=ctx_len, nt=nt, window=window),
        grid=(rows // sb, n_exp // EXPERTS_PER_STEP),
        in_specs=[blk(d), blk(LANES),
                  pl.BlockSpec((EXPERTS_PER_STEP, d, hid), lambda i, e: (e, 0, 0)),
                  pl.BlockSpec((EXPERTS_PER_STEP, d, hid), lambda i, e: (e, 0, 0)),
                  pl.BlockSpec((EXPERTS_PER_STEP, hid, d), lambda i, e: (e, 0, 0)),
                  blk(d),
                  pl.BlockSpec((None, 2, 8, d), lambda i, e: (i // per_b, 0, 0, 0))],
        out_specs=blk(d),
        out_shape=jax.ShapeDtypeStruct((rows, d), F32),
        scratch_shapes=[pltpu.VMEM((sb + window, d), _MXU_DTYPE), pltpu.VMEM((sb + window, LANES), F32),
                        pltpu.VMEM((sb + window, d), F32), pltpu.VMEM((sb, LANES), F32),
                        pltpu.SMEM((2 * MOE_GROUPS,), jnp.int32)],
        compiler_params=_params("parallel", "arbitrary"),
        name="moe_experts",
    )(xt.reshape(rows, d), cmb.reshape(rows, LANES), wg, wu, wd, h.reshape(rows, d), mod_tab)
    return out.reshape(b, nt, d)


def _final_kernel(h_ref, g_ref, o_ref):
    h = h_ref[...]
    o_ref[...] = h * lax.rsqrt(jnp.mean(h * h, axis=-1, keepdims=True) + RMS_EPS) * g_ref[...]


def _final_norm(h, g, ctx_len):
    b, nt, d = h.shape
    tm = ROW_TILE
    skip = ctx_len // tm
    return pl.pallas_call(
        _final_kernel,
        grid=(b, (nt - ctx_len) // tm),
        in_specs=[pl.BlockSpec((None, tm, d), lambda bb, i: (bb, i + skip, 0)),
                  pl.BlockSpec((1, d), lambda bb, i: (0, 0))],
        out_specs=pl.BlockSpec((None, tm, d), lambda bb, i: (bb, i, 0)),
        out_shape=jax.ShapeDtypeStruct((b, nt - ctx_len, d), F32),
        compiler_params=_params("parallel", "parallel"),
        name="final_norm",
    )(h, g)


def _rope_tables(seq_len, ctx_len):
    n_rows = seq_len // GRID_W
    rows = jnp.repeat(jnp.arange(n_rows, dtype=F32), GRID_W)
    cols = jnp.tile(jnp.arange(GRID_W, dtype=F32), n_rows)
    half = HEAD_DIM // 2
    inv = 1.0 / (ROPE_BASE ** (jnp.arange(0, half, 2, dtype=F32) / half))
    ang_r = rows[:, None] * inv
    ang_c = cols[:, None] * inv
    ang = jnp.concatenate([ang_r, ang_r, ang_c, ang_c], axis=-1)
    ang = jnp.concatenate([jnp.zeros((ctx_len, HEAD_DIM), F32), ang], axis=0)
    ang = jnp.tile(ang, (1, LANES // HEAD_DIM))
    return jnp.cos(ang), jnp.sin(ang)


def _pad_row(v, width=LANES):
    return jnp.pad(v, (0, width - v.shape[0]))[None, :]


def kernel(x, c, ctx, c_ctx, mod_w, mod_b, norm1_g, norm2_g, final_g, attn_w_in, attn_w_out, attn_q_norm_g, attn_k_norm_g, diff_lambda_q1, diff_lambda_k1, diff_lambda_q2, diff_lambda_k2, diff_subln_g, ssm_a_re, ssm_a_im, ssm_log_dt, ssm_b_re, ssm_b_im, ssm_c_re, ssm_c_im, ssm_d, ssm_glu_w_a, ssm_glu_w_b, moe_group_w, moe_group_b, moe_router_w, moe_router_b, moe_w_gate, moe_w_up, moe_w_down):
    bsz, seq, d = x.shape
    ctx_len = ctx.shape[1]
    depth = mod_w.shape[0]
    assert ctx_len == ROW_TILE and seq % ROW_TILE == 0 and seq % GRID_W == 0

    h = jnp.concatenate([ctx, x], axis=1)

    mod_rows = 16
    c_all = jnp.concatenate([c, c_ctx[None, :], jnp.zeros((mod_rows - bsz - 1, d), F32)], axis=0)
    mods = _modulation(c_all, mod_w, mod_b).reshape(depth, mod_rows, 6, d)
    mods = jnp.pad(mods, ((0, 0), (0, 0), (0, 2), (0, 0)))
    mod_tabs = jnp.stack([jnp.broadcast_to(mods[:, bsz:bsz + 1], (depth, bsz, 8, d)), mods[:, :bsz]], axis=2)

    cos, sin = _rope_tables(seq, ctx_len)
    cast = lambda w: w.astype(_MXU_DTYPE)
    ssm_tabs = _ssm_tables(ssm_a_re, ssm_a_im, ssm_log_dt, ssm_b_re, ssm_b_im, ssm_c_re, ssm_c_im, SSM_CHUNK)

    for layer in range(depth):
        mod_tab = mod_tabs[layer]
        i = layer // 2
        if layer % 2 == 0:
            lambda_init = 0.8 - 0.6 * math.exp(-0.3 * layer)
            qkv = _attn_proj(h, mod_tab, norm1_g[layer][None, :], cast(attn_w_in[i]),
                             jnp.tile(attn_q_norm_g[i], 2)[None, :], jnp.tile(attn_k_norm_g[i], 2)[None, :],
                             cos, sin)
            lam_rows = jnp.concatenate([_pad_row(diff_lambda_q1[i]), _pad_row(diff_lambda_k1[i]),
                                        _pad_row(diff_lambda_q2[i]), _pad_row(diff_lambda_k2[i]),
                                        jnp.zeros((4, LANES), F32)], axis=0)
            h = _attention(qkv, lam_rows, diff_subln_g[i][None, :], cast(attn_w_out[i]), h, mod_tab,
                           lambda_init, ctx_len)
        else:
            u = _norm1(h, mod_tab, norm1_g[layer][None, :])
            y = _ssm_scan(u, *ssm_tabs, ctx_len, i * (d // LANES))
            h = _ssm_out(y, u, ssm_d[i][None, :], cast(ssm_glu_w_a[i]), cast(ssm_glu_w_b[i]), h, mod_tab)

        wr = jnp.concatenate([jnp.transpose(moe_router_w[layer], (1, 0, 2)).reshape(d, MOE_EXPERTS),
                              moe_group_w[layer],
                              jnp.zeros((d, LANES - MOE_EXPERTS - MOE_GROUPS), F32)], axis=1)
        br = _pad_row(jnp.concatenate([moe_router_b[layer].reshape(-1), moe_group_b[layer]]))
        xt, cmb = _router(h, mod_tab, norm2_g[layer][None, :], wr, br)
        h = _experts(xt, cmb, cast(moe_w_gate[layer]), cast(moe_w_up[layer]), cast(moe_w_down[layer]),
                     h, mod_tab, ctx_len)

    return _final_norm(h, final_g[None, :], ctx_len)
```

```python
import functools
import math

import jax
import jax.numpy as jnp
from jax import lax
from jax.experimental import pallas as pl
from jax.experimental.pallas import tpu as pltpu

F32 = jnp.float32
_MXU_DTYPE = jnp.bfloat16
_HIGHEST = lax.Precision.HIGHEST

LANES = 128
HEAD_DIM = 64
GRID_W = 64
ROPE_BASE = 10000.0
GQA_Q_HEADS = 8
GQA_GROUP = 4
DIFF_HEADS = 4
GQA_Q_W = 512
GQA_KV_W = 128
DIFF_QK_W = 512
DIFF_V_W = 512
SSM_GROUP_CH = 16
SSM_STATE = 64
MOE_GROUPS = 4
MOE_EPG = 8
MOE_EXPERTS = 32
RMS_EPS = 1e-6
SSM_CHUNK = 8
ROW_TILE = 256
ROW_ALIGN = 16
EXPERTS_PER_STEP = 4
WEIGHT_SLOTS = 3
VMEM_LIMIT = 62 * 1024 * 1024


def _params(*sem):
    return pltpu.CompilerParams(dimension_semantics=sem, vmem_limit_bytes=VMEM_LIMIT)


def _norm_mod(h, g, shift, scale):
    y = h * lax.rsqrt(jnp.mean(h * h, axis=-1, keepdims=True) + RMS_EPS) * g
    return y * (1.0 + scale) + shift


def _mm(a, b):
    return jnp.dot(a.astype(_MXU_DTYPE), b.astype(_MXU_DTYPE), preferred_element_type=F32)


def _mod_kernel(c_ref, w_ref, b_ref, o_ref):
    c = c_ref[...]
    a = c / (1.0 + jnp.exp(-c))
    o_ref[...] = jnp.dot(a, w_ref[...], preferred_element_type=F32, precision=_HIGHEST) + b_ref[...]


def _modulation(c_all, mod_w, mod_b):
    depth, d, n = mod_w.shape
    rows = c_all.shape[0]
    tn = 1536
    return pl.pallas_call(
        _mod_kernel,
        grid=(depth, n // tn),
        in_specs=[pl.BlockSpec((rows, d), lambda l, j: (0, 0)),
                  pl.BlockSpec((None, d, tn), lambda l, j: (l, 0, j)),
                  pl.BlockSpec((None, 1, tn), lambda l, j: (l, 0, j))],
        out_specs=pl.BlockSpec((None, rows, tn), lambda l, j: (l, 0, j)),
        out_shape=jax.ShapeDtypeStruct((depth, rows, n), F32),
        compiler_params=_params("parallel", "parallel"),
        name="modulation",
    )(c_all, mod_w, mod_b.reshape(depth, 1, n))


def _mod_spec(d):
    return pl.BlockSpec((None, None, 8, d), lambda b, i: (b, jnp.minimum(i, 1), 0, 0))


def _attn_proj_kernel(h_ref, mod_ref, g_ref, w_ref, gq_ref, gk_ref, cos_ref, sin_ref,
                      qa_ref, ka_ref, va_ref, qb_ref, kb_ref, vb_ref):
    xn = _norm_mod(h_ref[...], g_ref[...], mod_ref[0:1, :], mod_ref[1:2, :])
    hp = _mm(xn, w_ref[...])
    cos = cos_ref[...]
    sin = sin_ref[...]
    lane = lax.broadcasted_iota(jnp.int32, (1, LANES), 1)
    first_half = (lane % 32) < 16
    r = lax.broadcasted_iota(jnp.int32, (LANES, LANES), 0) // HEAD_DIM
    c = lax.broadcasted_iota(jnp.int32, (LANES, LANES), 1) // HEAD_DIM
    same_head = (r == c).astype(_MXU_DTYPE)

    def rope(x):
        rot = jnp.where(first_half, -pltpu.roll(x, LANES - 16, 1), pltpu.roll(x, 16, 1))
        return x * cos + rot * sin

    def head_norm(x, g):
        ss = jnp.dot((x * x).astype(_MXU_DTYPE), same_head, preferred_element_type=F32)
        return x * lax.rsqrt(ss * (1.0 / HEAD_DIM) + RMS_EPS) * g

    scale = HEAD_DIM ** -0.5 * math.log2(math.e)
    o = 0
    for s in range(GQA_Q_W // LANES):
        x = hp[:, o + s * LANES:o + (s + 1) * LANES]
        qa_ref[:, s * LANES:(s + 1) * LANES] = (rope(head_norm(x, gq_ref[...])) * scale).astype(qa_ref.dtype)
    o += GQA_Q_W
    ka_ref[...] = rope(head_norm(hp[:, o:o + LANES], gk_ref[...])).astype(ka_ref.dtype)
    o += GQA_KV_W
    v_pair = hp[:, o:o + LANES]
    low_half = lane < HEAD_DIM
    va_ref[:, 0:LANES] = jnp.where(low_half, v_pair, 1.0).astype(va_ref.dtype)
    va_ref[:, LANES:2 * LANES] = jnp.where(low_half, pltpu.roll(v_pair, HEAD_DIM, 1), 1.0).astype(va_ref.dtype)
    o += GQA_KV_W
    for s in range(DIFF_QK_W // LANES):
        x = hp[:, o + s * LANES:o + (s + 1) * LANES]
        qb_ref[:, s * LANES:(s + 1) * LANES] = (rope(x) * scale).astype(qb_ref.dtype)
    o += DIFF_QK_W
    for s in range(DIFF_QK_W // LANES):
        x = hp[:, o + s * LANES:o + (s + 1) * LANES]
        kb_ref[:, s * LANES:(s + 1) * LANES] = rope(x).astype(kb_ref.dtype)
    o += DIFF_QK_W
    for s in range(DIFF_HEADS):
        vb_ref[:, 2 * s * LANES:(2 * s + 1) * LANES] = hp[:, o + s * LANES:o + (s + 1) * LANES].astype(vb_ref.dtype)
        vb_ref[:, (2 * s + 1) * LANES:(2 * s + 2) * LANES] = jnp.ones((hp.shape[0], LANES), vb_ref.dtype)


def _attn_proj(h, mod_tab, g, w_in, gq, gk, cos, sin):
    b, nt, d = h.shape
    tm = ROW_TILE
    widths = (GQA_Q_W, GQA_KV_W, 2 * GQA_KV_W, DIFF_QK_W, DIFF_QK_W, 2 * DIFF_V_W)
    full = lambda shape: pl.BlockSpec(shape, lambda bb, i: (0,) * len(shape))
    return pl.pallas_call(
        _attn_proj_kernel,
        grid=(b, nt // tm),
        in_specs=[pl.BlockSpec((None, tm, d), lambda bb, i: (bb, i, 0)),
                  _mod_spec(d),
                  full((1, d)),
                  full(w_in.shape),
                  full((1, LANES)),
                  full((1, LANES)),
                  pl.BlockSpec((tm, LANES), lambda bb, i: (i, 0)),
                  pl.BlockSpec((tm, LANES), lambda bb, i: (i, 0))],
        out_specs=[pl.BlockSpec((None, tm, w), lambda bb, i: (bb, i, 0)) for w in widths],
        out_shape=[jax.ShapeDtypeStruct((b, nt, w), _MXU_DTYPE) for w in widths],
        compiler_params=_params("parallel", "parallel"),
        name="attn_proj",
    )(h, mod_tab, g, w_in, gq, gk, cos, sin)


def _attn_kernel(qa_ref, qb_ref, ka_ref, va_ref, kb_ref, vb_ref, lam_ref, sg_ref, wo_ref, h_ref, mod_ref,
                 o_ref, mrg_ref, *, lambda_init, ctx_len):
    lv = lam_ref[...]
    lam = (jnp.exp(jnp.sum(lv[0:1] * lv[1:2], axis=-1, keepdims=True))
           - jnp.exp(jnp.sum(lv[2:3] * lv[3:4], axis=-1, keepdims=True)) + lambda_init)

    def run(nk):
        tq = qa_ref.shape[0]
        qk = lambda q, k: lax.dot_general(q, k, (((1,), (1,)), ((), ())), preferred_element_type=F32)
        probs = lambda s: jnp.exp2((s - jnp.max(s, axis=-1, keepdims=True)).astype(_MXU_DTYPE))

        def normalised(o, dv):
            return o[:, 0:dv] / o[:, dv:dv + 1]

        def gqa_scores(g):
            q = jnp.concatenate([qa_ref[:, h * HEAD_DIM:(h + 1) * HEAD_DIM]
                                 for h in range(g * GQA_GROUP, (g + 1) * GQA_GROUP)], axis=0)
            return [qk(q, ka_ref[0:nk, g * HEAD_DIM:(g + 1) * HEAD_DIM])]

        def gqa_attend(g, p):
            o = jnp.dot(p[0], va_ref[0:nk, g * LANES:(g + 1) * LANES], preferred_element_type=F32)
            for j in range(GQA_GROUP):
                h = g * GQA_GROUP + j
                mrg_ref[:, h * HEAD_DIM:(h + 1) * HEAD_DIM] = normalised(
                    o[j * tq:(j + 1) * tq], HEAD_DIM).astype(mrg_ref.dtype)

        def diff_scores(h):
            c0 = h * 2 * HEAD_DIM
            return [qk(qb_ref[:, c:c + HEAD_DIM], kb_ref[0:nk, c:c + HEAD_DIM]) for c in (c0, c0 + HEAD_DIM)]

        def diff_attend(h, p):
            c0 = h * 2 * HEAD_DIM
            o = jnp.dot(jnp.concatenate(p, axis=0), vb_ref[0:nk, 2 * c0:2 * c0 + 2 * LANES],
                        preferred_element_type=F32)
            o = normalised(o[0:tq], 2 * HEAD_DIM) - lam * normalised(o[tq:2 * tq], 2 * HEAD_DIM)
            o = o * lax.rsqrt(jnp.mean(o * o, axis=-1, keepdims=True) + RMS_EPS) * sg_ref[...]
            o = o * (1.0 - lambda_init)
            mrg_ref[:, GQA_Q_W + c0:GQA_Q_W + c0 + 2 * HEAD_DIM] = o.astype(mrg_ref.dtype)

        units = ([(functools.partial(gqa_scores, g), functools.partial(gqa_attend, g))
                  for g in range(GQA_Q_HEADS // GQA_GROUP)]
                 + [(functools.partial(diff_scores, h), functools.partial(diff_attend, h))
                    for h in range(DIFF_HEADS)])
        n = len(units)
        s, p = [None] * n, [None] * n
        for step in range(n + 2):
            if step < n:
                s[step] = units[step][0]()
            if 1 <= step <= n:
                p[step - 1] = [probs(x) for x in s[step - 1]]
                s[step - 1] = None
            if 2 <= step:
                units[step - 2][1](p[step - 2])
                p[step - 2] = None

    i = pl.program_id(1)

    @pl.when(i == 0)
    def _():
        run(ctx_len)

    @pl.when(i > 0)
    def _():
        run(ka_ref.shape[0])

    y = jnp.dot(mrg_ref[...], wo_ref[...], preferred_element_type=F32)
    o_ref[...] = h_ref[...] + mod_ref[2:3, :] * y


def _attention(qkv, lam_rows, subln_g, w_out, h, mod_tab, lambda_init, ctx_len):
    qa, ka, va, qb, kb, vb = qkv
    b, nt, d = h.shape
    tq = ROW_TILE
    assert ctx_len == tq
    blk = lambda w: pl.BlockSpec((None, tq, w), lambda bb, i: (bb, i, 0))
    per_batch = lambda w: pl.BlockSpec((None, nt, w), lambda bb, i: (bb, 0, 0))
    full = lambda shape: pl.BlockSpec(shape, lambda bb, i: (0,) * len(shape))
    return pl.pallas_call(
        functools.partial(_attn_kernel, lambda_init=lambda_init, ctx_len=ctx_len),
        grid=(b, nt // tq),
        in_specs=[blk(GQA_Q_W), blk(DIFF_QK_W), per_batch(GQA_KV_W), per_batch(2 * GQA_KV_W),
                  per_batch(DIFF_QK_W), per_batch(2 * DIFF_V_W),
                  full((8, LANES)), full((1, LANES)), full(w_out.shape), blk(d), _mod_spec(d)],
        out_specs=blk(d),
        out_shape=jax.ShapeDtypeStruct((b, nt, d), F32),
        scratch_shapes=[pltpu.VMEM((tq, GQA_Q_W + DIFF_V_W), _MXU_DTYPE)],
        compiler_params=_params("parallel", "parallel"),
        name="attention",
    )(qa, qb, ka, va, kb, vb, lam_rows, subln_g, w_out, h, mod_tab)


def _norm1_kernel(h_ref, mod_ref, g_ref, o_ref):
    o_ref[...] = _norm_mod(h_ref[...], g_ref[...], mod_ref[0:1, :], mod_ref[1:2, :])


def _norm1(h, mod_tab, g):
    b, nt, d = h.shape
    tm = ROW_TILE
    blk = pl.BlockSpec((None, tm, d), lambda bb, i: (bb, i, 0))
    return pl.pallas_call(
        _norm1_kernel,
        grid=(b, nt // tm),
        in_specs=[blk, _mod_spec(d), pl.BlockSpec((1, d), lambda bb, i: (0, 0))],
        out_specs=blk,
        out_shape=jax.ShapeDtypeStruct((b, nt, d), F32),
        compiler_params=_params("parallel", "parallel"),
        name="ssm_norm",
    )(h, mod_tab, g)


def _ssm_kernel(u_ref, win_ref, m_ref, wout_ref, lam_ref, y_ref, bd_ref, *, chunk, n_ctx_chunks):
    nb, nt, _ = u_ref.shape
    nc = nt // chunk
    n_state_slabs = bd_ref.shape[0]
    q = n_state_slabs // 4

    def chunk_rows(bi):
        parts = [u_ref[bi, pl.ds(s, nc, stride=chunk), :] for s in range(chunk)]
        return jnp.concatenate(parts, axis=1).astype(_MXU_DTYPE)

    for bi in range(nb):
        drive = jnp.dot(chunk_rows(bi), win_ref[...], preferred_element_type=F32)
        for c in range(n_state_slabs):
            bd_ref[c, pl.ds(bi, nc, stride=nb), :] = drive[:, c * LANES:(c + 1) * LANES]

    lam = lam_ref[...]

    def make_step(base):
        a_re = [lam[:, (base + c) * LANES:(base + c + 1) * LANES] for c in range(q)]
        a_im = [lam[:, (base + q + c) * LANES:(base + q + c + 1) * LANES] for c in range(q)]

        def step(k, carry):
            row = pl.multiple_of(k * nb, nb)
            out = []
            for c in range(q):
                s_re, s_im = carry[2 * c], carry[2 * c + 1]
                d_re = bd_ref[base + c, pl.ds(row, nb), :]
                d_im = bd_ref[base + q + c, pl.ds(row, nb), :]
                bd_ref[base + c, pl.ds(row, nb), :] = s_re
                bd_ref[base + q + c, pl.ds(row, nb), :] = s_im
                out.append(a_re[c] * s_re - a_im[c] * s_im + d_re)
                out.append(a_re[c] * s_im + a_im[c] * s_re + d_im)
            return tuple(out)

        return step

    zero = tuple(jnp.zeros((nb, LANES), F32) for _ in range(2 * q))
    fwd = make_step(0)
    lax.fori_loop(0, nc, fwd, zero)
    rev = make_step(2 * q)
    carry = lax.fori_loop(0, n_ctx_chunks, lambda i, cr: rev(n_ctx_chunks - 1 - i, cr), zero)
    lax.fori_loop(0, nc - n_ctx_chunks, lambda i, cr: rev(nc - 1 - i, cr), carry)

    for bi in range(nb):
        states = jnp.concatenate([bd_ref[c, pl.ds(bi, nc, stride=nb), :] for c in range(n_state_slabs)], axis=1)
        y = (jnp.dot(chunk_rows(bi), m_ref[...], preferred_element_type=F32)
             + jnp.dot(states.astype(_MXU_DTYPE), wout_ref[...], preferred_element_type=F32))
        for t in range(chunk):
            y_ref[bi, pl.ds(t, nc, stride=chunk), :] = y[:, t * LANES:(t + 1) * LANES]


def _ssm_scan(u, win, m, wout, lam_t, ctx_len, first_slab):
    b, nt, d = u.shape
    chunk = SSM_CHUNK
    nb = 4 if b % 4 == 0 else b
    n_slabs = d // LANES
    state_w = win.shape[-1]
    nc = nt // chunk
    blk = pl.BlockSpec((nb, nt, LANES), lambda j, bb: (bb, 0, j))
    table = lambda j, bb: (first_slab + j, 0, 0)
    return pl.pallas_call(
        functools.partial(_ssm_kernel, chunk=chunk, n_ctx_chunks=ctx_len // chunk),
        grid=(n_slabs, b // nb),
        in_specs=[blk,
                  pl.BlockSpec((None,) + win.shape[1:], table),
                  pl.BlockSpec((None,) + m.shape[1:], table),
                  pl.BlockSpec((None,) + wout.shape[1:], table),
                  pl.BlockSpec((None, 1, state_w), table)],
        out_specs=blk,
        out_shape=jax.ShapeDtypeStruct((b, nt, d), F32),
        scratch_shapes=[pltpu.VMEM((state_w // LANES, nc * nb, LANES), F32)],
        compiler_params=_params("parallel", "parallel"),
        name="ssm_scan",
    )(u, win, m, wout, lam_t)


def _ssm_out_kernel(y_ref, u_ref, d_ref, wa_ref, wb_ref, h_ref, mod_ref, o_ref):
    x = y_ref[...] + d_ref[...] * u_ref[...]
    z = 0.5 * x * (1.0 + jnp.tanh(math.sqrt(2.0 / math.pi) * (x + 0.044715 * (x * x * x))))
    z = z.astype(_MXU_DTYPE)
    a = jnp.dot(z, wa_ref[...], preferred_element_type=F32)
    g = jnp.dot(z, wb_ref[...], preferred_element_type=F32)
    o_ref[...] = h_ref[...] + mod_ref[2:3, :] * (a / (1.0 + jnp.exp(-g)))


def _ssm_out(y, u, d_skip, wa, wb, h, mod_tab):
    b, nt, d = h.shape
    tm = ROW_TILE
    blk = pl.BlockSpec((None, tm, d), lambda bb, i: (bb, i, 0))
    full = lambda shape: pl.BlockSpec(shape, lambda bb, i: (0,) * len(shape))
    return pl.pallas_call(
        _ssm_out_kernel,
        grid=(b, nt // tm),
        in_specs=[blk, blk, full((1, d)), full(wa.shape), full(wb.shape), blk, _mod_spec(d)],
        out_specs=blk,
        out_shape=jax.ShapeDtypeStruct((b, nt, d), F32),
        compiler_params=_params("parallel", "parallel"),
        name="ssm_out",
    )(y, u, d_skip, wa, wb, h, mod_tab)


def _split_terms(x, n):
    terms = []
    for _ in range(n - 1):
        t = x.astype(_MXU_DTYPE)
        terms.append(t)
        x = x - t.astype(F32)
    terms.append(x.astype(_MXU_DTYPE))
    return terms


def _ssm_tables_kernel(lam_ref, bt_ref, c_ref, m_ref, win_ref, wout_ref, lamt_ref, *, chunk):
    gpt = LANES // SSM_GROUP_CH
    p = SSM_STATE
    sw = gpt * p
    ci = lax.broadcasted_iota(jnp.int32, (p, sw), 0)
    oi = lax.broadcasted_iota(jnp.int32, (p, sw), 1)
    spread = (ci == oi % p).astype(_MXU_DTYPE)
    ri = lax.broadcasted_iota(jnp.int32, (LANES, sw), 0)
    oj = lax.broadcasted_iota(jnp.int32, (LANES, sw), 1)
    own_group = (ri // SSM_GROUP_CH) == (oj // p)

    def block_diag(x):
        y = jnp.zeros((LANES, sw), F32)
        for term in _split_terms(x, 3):
            y = y + jnp.dot(term, spread, preferred_element_type=F32)
        return jnp.where(own_group, y, 0.0)

    def cmul(ar, ai, br, bi):
        return ar * br - ai * bi, ar * bi + ai * br

    nt_dot = lambda a, b: lax.dot_general(a, b, (((1,), (1,)), ((), ())), precision=_HIGHEST,
                                          preferred_element_type=F32)
    taps, drive, read = [], [], []
    for x in range(2):
        lre = jnp.minimum(lam_ref[x, 0:1, :], -1e-4)
        lim = lam_ref[x, 1:2, :]
        dt = jnp.exp(lam_ref[x, 2:3, :])
        pw = []
        for j in range(chunk + 1):
            mag = jnp.exp(float(j) * (lre * dt))
            ang = float(j) * (lim * dt)
            pw.append((mag * jnp.cos(ang), mag * jnp.sin(ang)))
        nr = pw[1][0] - 1.0
        ni = pw[1][1]
        den = lre * lre + lim * lim
        coef_re = (nr * lre + ni * lim) / den
        coef_im = (ni * lre - nr * lim) / den
        bb = cmul(coef_re, coef_im, block_diag(bt_ref[x, 0]), block_diag(bt_ref[x, 1]))
        cc = (block_diag(c_ref[x, 0]), block_diag(c_ref[x, 1]))
        drive.append([cmul(pw[j][0], pw[j][1], bb[0], bb[1]) for j in range(chunk)])
        read.append([cmul(pw[j][0], pw[j][1], cc[0], cc[1]) for j in range(chunk + 1)])
        taps.append([nt_dot(u_re, cc[0]) - nt_dot(u_im, cc[1]) for u_re, u_im in drive[x]])
        lamt_ref[:, 2 * x * sw:(2 * x + 1) * sw] = pw[chunk][0]
        lamt_ref[:, (2 * x + 1) * sw:(2 * x + 2) * sw] = pw[chunk][1]

    for s in range(chunk):
        rows = slice(s * LANES, (s + 1) * LANES)
        for t in range(chunk):
            if t > s:
                blk = taps[0][t - s]
            elif t < s:
                blk = taps[1][s - t]
            else:
                blk = taps[0][0] + taps[1][0]
            m_ref[rows, t * LANES:(t + 1) * LANES] = blk.astype(m_ref.dtype)
        f_re, f_im = drive[0][chunk - 1 - s]
        r_re, r_im = drive[1][s]
        for q, part in enumerate((f_re, f_im, r_re, r_im)):
            win_ref[rows, q * sw:(q + 1) * sw] = part.astype(win_ref.dtype)
    for t in range(chunk):
        cols = slice(t * LANES, (t + 1) * LANES)
        f_re, f_im = read[0][t + 1]
        r_re, r_im = read[1][chunk - t]
        for q, part in enumerate((f_re, -f_im, r_re, -r_im)):
            wout_ref[q * sw:(q + 1) * sw, cols] = part.T.astype(wout_ref.dtype)


def _ssm_tables(a_re, a_im, log_dt, b_re, b_im, c_re, c_im, chunk):
    n_layers, _, g_total, p = a_re.shape
    gpt = LANES // SSM_GROUP_CH
    n_slabs = g_total // gpt
    n = n_layers * n_slabs
    sw = gpt * p

    def slab_lanes(x):
        return jnp.transpose(x.reshape(n_layers, 2, n_slabs, sw), (0, 2, 1, 3)).reshape(n, 2, sw)

    def slab_rows(x):
        x = x.reshape(n_layers, 2, n_slabs, LANES, p)
        return jnp.transpose(x, (0, 2, 1, 3, 4)).reshape(n, 2, LANES, p)

    lam = jnp.stack([slab_lanes(a_re), slab_lanes(a_im),
                     slab_lanes(jnp.broadcast_to(log_dt[..., None], a_re.shape))], axis=2)
    lam = jnp.pad(lam, ((0, 0), (0, 0), (0, 5), (0, 0)))
    bt = jnp.stack([slab_rows(jnp.swapaxes(b_re, -1, -2)), slab_rows(jnp.swapaxes(b_im, -1, -2))], axis=2)
    ct = jnp.stack([slab_rows(c_re), slab_rows(c_im)], axis=2)
    rows = chunk * LANES
    per_slab = lambda *shape: pl.BlockSpec((None,) + shape, lambda i: (i,) + (0,) * len(shape))
    m, win, wout, lam_t = pl.pallas_call(
        functools.partial(_ssm_tables_kernel, chunk=chunk),
        grid=(n,),
        in_specs=[per_slab(2, 8, sw), per_slab(2, 2, LANES, p), per_slab(2, 2, LANES, p)],
        out_specs=[per_slab(rows, rows), per_slab(rows, 4 * sw), per_slab(4 * sw, rows), per_slab(1, 4 * sw)],
        out_shape=[jax.ShapeDtypeStruct((n, rows, rows), _MXU_DTYPE),
                   jax.ShapeDtypeStruct((n, rows, 4 * sw), _MXU_DTYPE),
                   jax.ShapeDtypeStruct((n, 4 * sw, rows), _MXU_DTYPE),
                   jax.ShapeDtypeStruct((n, 1, 4 * sw), F32)],
        compiler_params=_params("parallel"),
        name="ssm_tables",
    )(lam, bt, ct)
    return win, m, wout, lam_t


def _router_kernel(h_ref, mod_ref, g_ref, wr_ref, br_ref, xt_ref, cmb_ref):
    xt = _norm_mod(h_ref[...], g_ref[...], mod_ref[3:4, :], mod_ref[4:5, :])
    xt_ref[...] = xt.astype(xt_ref.dtype)
    x_hi, x_lo = _split_terms(xt, 2)
    w_hi, w_lo = _split_terms(wr_ref[...], 2)
    dot = lambda a, b: jnp.dot(a, b, preferred_element_type=F32)
    logits = dot(x_hi, w_hi) + (dot(x_hi, w_lo) + dot(x_lo, w_hi)) + br_ref[...]
    lane = lax.broadcasted_iota(jnp.int32, (1, LANES), 1)
    lane_f = lane.astype(F32)
    neg = -jnp.inf
    big = 1e9
    gmask = (lane >= MOE_EXPERTS) & (lane < MOE_EXPERTS + MOE_GROUPS)
    gl = jnp.where(gmask, logits, neg)
    gmax = jnp.max(gl, axis=-1, keepdims=True)
    gidx = jnp.min(jnp.where(gl == gmax, lane_f, big), axis=-1, keepdims=True) - MOE_EXPERTS
    p_group = 1.0 / jnp.sum(jnp.where(gmask, jnp.exp(gl - gmax), 0.0), axis=-1, keepdims=True)
    in_group = (lane < MOE_EXPERTS) & ((lane // MOE_EPG).astype(F32) == gidx)
    el = jnp.where(in_group, logits, neg)
    v1 = jnp.max(el, axis=-1, keepdims=True)
    i1 = jnp.min(jnp.where(el == v1, lane_f, big), axis=-1, keepdims=True)
    el2 = jnp.where(lane_f == i1, neg, el)
    v2 = jnp.max(el2, axis=-1, keepdims=True)
    i2 = jnp.min(jnp.where(el2 == v2, lane_f, big), axis=-1, keepdims=True)
    t = jnp.exp(v2 - v1)
    w1 = p_group / (1.0 + t)
    w2 = p_group * t / (1.0 + t)
    cmb_ref[...] = jnp.where(lane_f == i1, w1, 0.0) + jnp.where(lane_f == i2, w2, 0.0)


def _router(h, mod_tab, g, wr, br):
    b, nt, d = h.shape
    tm = ROW_TILE
    full = lambda shape: pl.BlockSpec(shape, lambda bb, i: (0,) * len(shape))
    return pl.pallas_call(
        _router_kernel,
        grid=(b, nt // tm),
        in_specs=[pl.BlockSpec((None, tm, d), lambda bb, i: (bb, i, 0)), _mod_spec(d),
                  full((1, d)), full(wr.shape), full(br.shape)],
        out_specs=[pl.BlockSpec((None, tm, d), lambda bb, i: (bb, i, 0)),
                   pl.BlockSpec((None, tm, LANES), lambda bb, i: (bb, i, 0))],
        out_shape=[jax.ShapeDtypeStruct((b, nt, d), _MXU_DTYPE),
                   jax.ShapeDtypeStruct((b, nt, LANES), F32)],
        compiler_params=_params("parallel", "parallel"),
        name="moe_router",
    )(h, mod_tab, g, wr, br)


def _experts_kernel(xt_ref, cmb_ref, wg_hbm, wu_hbm, wd_hbm, h_ref, mod_ref, o_ref,
                    xs_ref, cs_ref, acc_ref, pos_ref, seg_ref, wg_buf, wu_buf, wd_buf, w_sem,
                    *, ctx_len, nt, window):
    i = pl.program_id(0)
    e = pl.program_id(1)
    sb = xt_ref.shape[0]
    d = xt_ref.shape[1]

    n_e = pl.num_programs(1)
    step = i * n_e + e
    n_steps = pl.num_programs(0) * n_e
    eps = wg_buf.shape[1]

    def weight_copies(src_step, slot):
        first_expert = lax.rem(src_step, n_e) * eps
        pairs = ((wg_hbm, wg_buf), (wu_hbm, wu_buf), (wd_hbm, wd_buf))
        return [pltpu.make_async_copy(w.at[pl.ds(first_expert, eps)], buf.at[slot], w_sem.at[k, slot])
                for k, (w, buf) in enumerate(pairs)]

    @pl.when(step == 0)
    def _():
        for ahead in range(WEIGHT_SLOTS - 1):
            for c in weight_copies(ahead, ahead):
                c.start()

    @pl.when(step + (WEIGHT_SLOTS - 1) < n_steps)
    def _():
        nxt = step + (WEIGHT_SLOTS - 1)
        for c in weight_copies(nxt, lax.rem(nxt, WEIGHT_SLOTS)):
            c.start()

    @pl.when(e == 0)
    def _():
        cmb = cmb_ref[...]
        lane = lax.broadcasted_iota(jnp.int32, (1, LANES), 1)
        routed = cmb != 0.0
        goh = jnp.zeros((sb, LANES), F32)
        for g in range(MOE_GROUPS):
            in_g = routed & (lane >= g * MOE_EPG) & (lane < (g + 1) * MOE_EPG)
            hit = jnp.max(jnp.where(in_g, 1.0, 0.0), axis=-1, keepdims=True)
            goh = goh + jnp.where(lane == g, hit, 0.0)
        tri = (lax.broadcasted_iota(jnp.int32, (LANES, LANES), 1)
               < lax.broadcasted_iota(jnp.int32, (LANES, LANES), 0)).astype(_MXU_DTYPE)
        cnt = jnp.zeros((1, LANES), F32)
        parts = []
        for k in range(sb // LANES):
            tile = goh[k * LANES:(k + 1) * LANES]
            parts.append(jnp.dot(tri, tile.astype(_MXU_DTYPE), preferred_element_type=F32) + cnt)
            cnt = cnt + jnp.sum(tile, axis=0, keepdims=True)
        before = jnp.concatenate(parts, axis=0)
        r_i = lax.broadcasted_iota(jnp.int32, (sb, sb), 0)
        off = jnp.zeros((1, LANES), F32)
        run = jnp.zeros((1, 1), F32)
        for g in range(MOE_GROUPS):
            off = off + jnp.where(lane == g, run, 0.0)
            run = run + jnp.sum(jnp.where(lane == g, cnt, 0.0), axis=-1, keepdims=True)
        pos = jnp.sum(goh * (off + before), axis=-1, keepdims=True)
        pos_b = jnp.broadcast_to(pos, (sb, LANES))
        pos_ref[...] = pos_b
        pos_row = pos_b.T[0:1, :].astype(jnp.int32)
        perm = (r_i == pos_row).astype(_MXU_DTYPE)
        xs_ref[0:sb, :] = jnp.dot(perm, xt_ref[...], preferred_element_type=F32).astype(xs_ref.dtype)
        cs = jnp.zeros((sb, LANES), F32)
        for term in _split_terms(cmb, 3):
            cs = cs + jnp.dot(perm, term, preferred_element_type=F32)
        cs_ref[0:sb, :] = cs
        xs_ref[sb:sb + window, :] = jnp.zeros((window, d), xs_ref.dtype)
        cs_ref[sb:sb + window, :] = jnp.zeros((window, LANES), F32)
        acc_ref[...] = jnp.zeros_like(acc_ref)
        off_i = off.astype(jnp.int32)
        cnt_i = cnt.astype(jnp.int32)
        for g in range(MOE_GROUPS):
            seg_ref[g] = off_i[0, g]
            seg_ref[MOE_GROUPS + g] = cnt_i[0, g]

    slot = lax.rem(step, WEIGHT_SLOTS)
    for c in weight_copies(step, slot):
        c.wait()
    g = (e * eps) // MOE_EPG
    start = seg_ref[g]
    count = seg_ref[MOE_GROUPS + g]
    first = (start // ROW_ALIGN) * ROW_ALIGN
    n_win = (start - first + count + window - 1) // window
    n_win = jnp.where(count > 0, n_win, 0)
    lane = lax.broadcasted_iota(jnp.int32, (1, LANES), 1)

    def window_step(k, carry):
        r0 = pl.multiple_of(first + k * window, ROW_ALIGN)
        x = xs_ref[pl.ds(r0, window), :]
        cw = cs_ref[pl.ds(r0, window), :]
        y = jnp.zeros((window, d), F32)
        for j in range(eps):
            gate = jnp.dot(x, wg_buf[slot, j], preferred_element_type=F32)
            up = jnp.dot(x, wu_buf[slot, j], preferred_element_type=F32)
            w = jnp.sum(jnp.where(lane == e * eps + j, cw, 0.0), axis=-1, keepdims=True)
            hid = (gate / (1.0 + jnp.exp(-gate))) * up * w
            y = y + jnp.dot(hid.astype(_MXU_DTYPE), wd_buf[slot, j], preferred_element_type=F32)
        acc_ref[pl.ds(r0, window), :] += y
        return carry

    lax.fori_loop(0, n_win, window_step, 0)

    @pl.when(e == pl.num_programs(1) - 1)
    def _():
        c_i = lax.broadcasted_iota(jnp.int32, (sb, sb), 1)
        unperm = (c_i == pos_ref[:, 0:1].astype(jnp.int32)).astype(_MXU_DTYPE)
        y = jnp.zeros((sb, d), F32)
        for term in _split_terms(acc_ref[0:sb, :], 2):
            y = y + jnp.dot(unperm, term, preferred_element_type=F32)
        row = (i * sb) % nt + lax.broadcasted_iota(jnp.int32, (sb, 1), 0)
        gate_row = jnp.where(row < ctx_len, mod_ref[0, 5:6, :], mod_ref[1, 5:6, :])
        o_ref[...] = h_ref[...] + gate_row * y


def _experts(xt, cmb, wg, wu, wd, h, mod_tab, ctx_len):
    b, nt, d = h.shape
    sb = 1152 if nt % 1152 == 0 else ROW_TILE
    window = 336 if sb == 1152 else 96
    per_b = nt // sb
    n_exp, _, hid = wg.shape
    rows = b * nt
    blk = lambda w: pl.BlockSpec((sb, w), lambda i, e: (i, 0))
    out = pl.pallas_call(
        functools.partial(_experts_kernel, ctx_len=ctx_len, nt=nt, window=window),
        grid=(rows // sb, n_exp // EXPERTS_PER_STEP),
        in_specs=[blk(d), blk(LANES),
                  pl.BlockSpec(memory_space=pl.ANY),
                  pl.BlockSpec(memory_space=pl.ANY),
                  pl.BlockSpec(memory_space=pl.ANY),
                  blk(d),
                  pl.BlockSpec((None, 2, 8, d), lambda i, e: (i // per_b, 0, 0, 0))],
        out_specs=blk(d),
        out_shape=jax.ShapeDtypeStruct((rows, d), F32),
        scratch_shapes=[pltpu.VMEM((sb + window, d), _MXU_DTYPE), pltpu.VMEM((sb + window, LANES), F32),
                        pltpu.VMEM((sb + window, d), F32), pltpu.VMEM((sb, LANES), F32),
                        pltpu.SMEM((2 * MOE_GROUPS,), jnp.int32),
                        pltpu.VMEM((WEIGHT_SLOTS, EXPERTS_PER_STEP, d, hid), wg.dtype),
                        pltpu.VMEM((WEIGHT_SLOTS, EXPERTS_PER_STEP, d, hid), wu.dtype),
                        pltpu.VMEM((WEIGHT_SLOTS, EXPERTS_PER_STEP, hid, d), wd.dtype),
                        pltpu.SemaphoreType.DMA((3, WEIGHT_SLOTS))],
        compiler_params=_params("arbitrary", "arbitrary"),
        name="moe_experts",
    )(xt.reshape(rows, d), cmb.reshape(rows, LANES), wg, wu, wd, h.reshape(rows, d), mod_tab)
    return out.reshape(b, nt, d)


def _final_kernel(h_ref, g_ref, o_ref):
    h = h_ref[...]
    o_ref[...] = h * lax.rsqrt(jnp.mean(h * h, axis=-1, keepdims=True) + RMS_EPS) * g_ref[...]


def _final_norm(h, g, ctx_len):
    b, nt, d = h.shape
    tm = ROW_TILE
    skip = ctx_len // tm
    return pl.pallas_call(
        _final_kernel,
        grid=(b, (nt - ctx_len) // tm),
        in_specs=[pl.BlockSpec((None, tm, d), lambda bb, i: (bb, i + skip, 0)),
                  pl.BlockSpec((1, d), lambda bb, i: (0, 0))],
        out_specs=pl.BlockSpec((None, tm, d), lambda bb, i: (bb, i, 0)),
        out_shape=jax.ShapeDtypeStruct((b, nt - ctx_len, d), F32),
        compiler_params=_params("parallel", "parallel"),
        name="final_norm",
    )(h, g)


def _rope_tables(seq_len, ctx_len):
    n_rows = seq_len // GRID_W
    rows = jnp.repeat(jnp.arange(n_rows, dtype=F32), GRID_W)
    cols = jnp.tile(jnp.arange(GRID_W, dtype=F32), n_rows)
    half = HEAD_DIM // 2
    inv = 1.0 / (ROPE_BASE ** (jnp.arange(0, half, 2, dtype=F32) / half))
    ang_r = rows[:, None] * inv
    ang_c = cols[:, None] * inv
    ang = jnp.concatenate([ang_r, ang_r, ang_c, ang_c], axis=-1)
    ang = jnp.concatenate([jnp.zeros((ctx_len, HEAD_DIM), F32), ang], axis=0)
    ang = jnp.tile(ang, (1, LANES // HEAD_DIM))
    return jnp.cos(ang), jnp.sin(ang)


def _pad_row(v, width=LANES):
    return jnp.pad(v, (0, width - v.shape[0]))[None, :]


def kernel(x, c, ctx, c_ctx, mod_w, mod_b, norm1_g, norm2_g, final_g, attn_w_in, attn_w_out, attn_q_norm_g, attn_k_norm_g, diff_lambda_q1, diff_lambda_k1, diff_lambda_q2, diff_lambda_k2, diff_subln_g, ssm_a_re, ssm_a_im, ssm_log_dt, ssm_b_re, ssm_b_im, ssm_c_re, ssm_c_im, ssm_d, ssm_glu_w_a, ssm_glu_w_b, moe_group_w, moe_group_b, moe_router_w, moe_router_b, moe_w_gate, moe_w_up, moe_w_down):
    bsz, seq, d = x.shape
    ctx_len = ctx.shape[1]
    depth = mod_w.shape[0]
    assert ctx_len == ROW_TILE and seq % ROW_TILE == 0 and seq % GRID_W == 0

    h = jnp.concatenate([ctx, x], axis=1)

    mod_rows = 16
    c_all = jnp.concatenate([c, c_ctx[None, :], jnp.zeros((mod_rows - bsz - 1, d), F32)], axis=0)
    mods = _modulation(c_all, mod_w, mod_b).reshape(depth, mod_rows, 6, d)
    mods = jnp.pad(mods, ((0, 0), (0, 0), (0, 2), (0, 0)))
    mod_tabs = jnp.stack([jnp.broadcast_to(mods[:, bsz:bsz + 1], (depth, bsz, 8, d)), mods[:, :bsz]], axis=2)

    cos, sin = _rope_tables(seq, ctx_len)
    cast = lambda w: w.astype(_MXU_DTYPE)
    ssm_tabs = _ssm_tables(ssm_a_re, ssm_a_im, ssm_log_dt, ssm_b_re, ssm_b_im, ssm_c_re, ssm_c_im, SSM_CHUNK)

    for layer in range(depth):
        mod_tab = mod_tabs[layer]
        i = layer // 2
        if layer % 2 == 0:
            lambda_init = 0.8 - 0.6 * math.exp(-0.3 * layer)
            qkv = _attn_proj(h, mod_tab, norm1_g[layer][None, :], cast(attn_w_in[i]),
                             jnp.tile(attn_q_norm_g[i], 2)[None, :], jnp.tile(attn_k_norm_g[i], 2)[None, :],
                             cos, sin)
            lam_rows = jnp.concatenate([_pad_row(diff_lambda_q1[i]), _pad_row(diff_lambda_k1[i]),
                                        _pad_row(diff_lambda_q2[i]), _pad_row(diff_lambda_k2[i]),
                                        jnp.zeros((4, LANES), F32)], axis=0)
            h = _attention(qkv, lam_rows, diff_subln_g[i][None, :], cast(attn_w_out[i]), h, mod_tab,
                           lambda_init, ctx_len)
        else:
            u = _norm1(h, mod_tab, norm1_g[layer][None, :])
            y = _ssm_scan(u, *ssm_tabs, ctx_len, i * (d // LANES))
            h = _ssm_out(y, u, ssm_d[i][None, :], cast(ssm_glu_w_a[i]), cast(ssm_glu_w_b[i]), h, mod_tab)

        wr = jnp.concatenate([jnp.transpose(moe_router_w[layer], (1, 0, 2)).reshape(d, MOE_EXPERTS),
                              moe_group_w[layer],
                              jnp.zeros((d, LANES - MOE_EXPERTS - MOE_GROUPS), F32)], axis=1)
        br = _pad_row(jnp.concatenate([moe_router_b[layer].reshape(-1), moe_group_b[layer]]))
        xt, cmb = _router(h, mod_tab, norm2_g[layer][None, :], wr, br)
        h = _experts(xt, cmb, cast(moe_w_gate[layer]), cast(moe_w_up[layer]), cast(moe_w_down[layer]),
                     h, mod_tab, ctx_len)

    return _final_norm(h, final_g[None, :], ctx_len)
```

```python
import functools
import math

import jax
import jax.numpy as jnp
from jax import lax
from jax.experimental import pallas as pl
from jax.experimental.pallas import tpu as pltpu

F32 = jnp.float32
_MXU_DTYPE = jnp.bfloat16
_HIGHEST = lax.Precision.HIGHEST

LANES = 128
HEAD_DIM = 64
GRID_W = 64
ROPE_BASE = 10000.0
GQA_Q_HEADS = 8
GQA_GROUP = 4
DIFF_HEADS = 4
GQA_Q_W = 512
GQA_KV_W = 128
DIFF_QK_W = 512
DIFF_V_W = 512
SSM_GROUP_CH = 16
SSM_STATE = 64
MOE_GROUPS = 4
MOE_EPG = 8
MOE_EXPERTS = 32
RMS_EPS = 1e-6
SSM_CHUNK = 8
ROW_TILE = 256
ROW_ALIGN = 16
EXPERTS_PER_STEP = 4
VMEM_LIMIT = 60 * 1024 * 1024


def _params(*sem):
    return pltpu.CompilerParams(dimension_semantics=sem, vmem_limit_bytes=VMEM_LIMIT)


def _norm_mod(h, g, shift, scale):
    y = h * lax.rsqrt(jnp.mean(h * h, axis=-1, keepdims=True) + RMS_EPS) * g
    return y * (1.0 + scale) + shift


def _mm(a, b):
    return jnp.dot(a.astype(_MXU_DTYPE), b.astype(_MXU_DTYPE), preferred_element_type=F32)


def _mod_kernel(c_ref, w_ref, b_ref, o_ref):
    c = c_ref[...]
    a = c / (1.0 + jnp.exp(-c))
    o_ref[...] = jnp.dot(a, w_ref[...], preferred_element_type=F32, precision=_HIGHEST) + b_ref[...]


def _modulation(c_all, mod_w, mod_b):
    depth, d, n = mod_w.shape
    rows = c_all.shape[0]
    tn = 1536
    return pl.pallas_call(
        _mod_kernel,
        grid=(depth, n // tn),
        in_specs=[pl.BlockSpec((rows, d), lambda l, j: (0, 0)),
                  pl.BlockSpec((None, d, tn), lambda l, j: (l, 0, j)),
                  pl.BlockSpec((None, 1, tn), lambda l, j: (l, 0, j))],
        out_specs=pl.BlockSpec((None, rows, tn), lambda l, j: (l, 0, j)),
        out_shape=jax.ShapeDtypeStruct((depth, rows, n), F32),
        compiler_params=_params("parallel", "parallel"),
        name="modulation",
    )(c_all, mod_w, mod_b.reshape(depth, 1, n))


def _mod_spec(d):
    return pl.BlockSpec((None, None, 8, d), lambda b, i: (b, jnp.minimum(i, 1), 0, 0))


def _attn_proj_kernel(h_ref, mod_ref, g_ref, w_ref, gq_ref, gk_ref, cos_ref, sin_ref,
                      qa_ref, ka_ref, va_ref, qb_ref, kb_ref, vb_ref):
    xn = _norm_mod(h_ref[...], g_ref[...], mod_ref[0:1, :], mod_ref[1:2, :])
    hp = _mm(xn, w_ref[...])
    cos = cos_ref[...]
    sin = sin_ref[...]
    lane = lax.broadcasted_iota(jnp.int32, (1, LANES), 1)
    first_half = (lane % 32) < 16
    r = lax.broadcasted_iota(jnp.int32, (LANES, LANES), 0) // HEAD_DIM
    c = lax.broadcasted_iota(jnp.int32, (LANES, LANES), 1) // HEAD_DIM
    same_head = (r == c).astype(_MXU_DTYPE)

    def rope(x):
        rot = jnp.where(first_half, -pltpu.roll(x, LANES - 16, 1), pltpu.roll(x, 16, 1))
        return x * cos + rot * sin

    def head_norm(x, g):
        ss = jnp.dot((x * x).astype(_MXU_DTYPE), same_head, preferred_element_type=F32)
        return x * lax.rsqrt(ss * (1.0 / HEAD_DIM) + RMS_EPS) * g

    scale = HEAD_DIM ** -0.5 * math.log2(math.e)
    o = 0
    for s in range(GQA_Q_W // LANES):
        x = hp[:, o + s * LANES:o + (s + 1) * LANES]
        qa_ref[:, s * LANES:(s + 1) * LANES] = (rope(head_norm(x, gq_ref[...])) * scale).astype(qa_ref.dtype)
    o += GQA_Q_W
    ka_ref[...] = rope(head_norm(hp[:, o:o + LANES], gk_ref[...])).astype(ka_ref.dtype)
    o += GQA_KV_W
    v_pair = hp[:, o:o + LANES]
    low_half = lane < HEAD_DIM
    va_ref[:, 0:LANES] = jnp.where(low_half, v_pair, 1.0).astype(va_ref.dtype)
    va_ref[:, LANES:2 * LANES] = jnp.where(low_half, pltpu.roll(v_pair, HEAD_DIM, 1), 1.0).astype(va_ref.dtype)
    o += GQA_KV_W
    for s in range(DIFF_QK_W // LANES):
        x = hp[:, o + s * LANES:o + (s + 1) * LANES]
        qb_ref[:, s * LANES:(s + 1) * LANES] = (rope(x) * scale).astype(qb_ref.dtype)
    o += DIFF_QK_W
    for s in range(DIFF_QK_W // LANES):
        x = hp[:, o + s * LANES:o + (s + 1) * LANES]
        kb_ref[:, s * LANES:(s + 1) * LANES] = rope(x).astype(kb_ref.dtype)
    o += DIFF_QK_W
    for s in range(DIFF_HEADS):
        vb_ref[:, 2 * s * LANES:(2 * s + 1) * LANES] = hp[:, o + s * LANES:o + (s + 1) * LANES].astype(vb_ref.dtype)
        vb_ref[:, (2 * s + 1) * LANES:(2 * s + 2) * LANES] = jnp.ones((hp.shape[0], LANES), vb_ref.dtype)


def _attn_proj(h, mod_tab, g, w_in, gq, gk, cos, sin):
    b, nt, d = h.shape
    tm = ROW_TILE
    widths = (GQA_Q_W, GQA_KV_W, 2 * GQA_KV_W, DIFF_QK_W, DIFF_QK_W, 2 * DIFF_V_W)
    full = lambda shape: pl.BlockSpec(shape, lambda bb, i: (0,) * len(shape))
    return pl.pallas_call(
        _attn_proj_kernel,
        grid=(b, nt // tm),
        in_specs=[pl.BlockSpec((None, tm, d), lambda bb, i: (bb, i, 0)),
                  _mod_spec(d),
                  full((1, d)),
                  full(w_in.shape),
                  full((1, LANES)),
                  full((1, LANES)),
                  pl.BlockSpec((tm, LANES), lambda bb, i: (i, 0)),
                  pl.BlockSpec((tm, LANES), lambda bb, i: (i, 0))],
        out_specs=[pl.BlockSpec((None, tm, w), lambda bb, i: (bb, i, 0)) for w in widths],
        out_shape=[jax.ShapeDtypeStruct((b, nt, w), _MXU_DTYPE) for w in widths],
        compiler_params=_params("parallel", "parallel"),
        name="attn_proj",
    )(h, mod_tab, g, w_in, gq, gk, cos, sin)


def _attn_kernel(qa_ref, qb_ref, ka_ref, va_ref, kb_ref, vb_ref, lam_ref, sg_ref, wo_ref, h_ref, mod_ref,
                 o_ref, mrg_ref, *, lambda_init, ctx_len):
    lv = lam_ref[...]
    lam = (jnp.exp(jnp.sum(lv[0:1] * lv[1:2], axis=-1, keepdims=True))
           - jnp.exp(jnp.sum(lv[2:3] * lv[3:4], axis=-1, keepdims=True)) + lambda_init)

    def run(nk):
        tq = qa_ref.shape[0]
        qk = lambda q, k: lax.dot_general(q, k, (((1,), (1,)), ((), ())), preferred_element_type=F32)
        probs = lambda s: jnp.exp2((s - jnp.max(s, axis=-1, keepdims=True)).astype(_MXU_DTYPE))

        def normalised(o, dv):
            return o[:, 0:dv] / o[:, dv:dv + 1]

        def gqa_scores(g):
            q = jnp.concatenate([qa_ref[:, h * HEAD_DIM:(h + 1) * HEAD_DIM]
                                 for h in range(g * GQA_GROUP, (g + 1) * GQA_GROUP)], axis=0)
            return [qk(q, ka_ref[0:nk, g * HEAD_DIM:(g + 1) * HEAD_DIM])]

        def gqa_attend(g, p):
            o = jnp.dot(p[0], va_ref[0:nk, g * LANES:(g + 1) * LANES], preferred_element_type=F32)
            for j in range(GQA_GROUP):
                h = g * GQA_GROUP + j
                mrg_ref[:, h * HEAD_DIM:(h + 1) * HEAD_DIM] = normalised(
                    o[j * tq:(j + 1) * tq], HEAD_DIM).astype(mrg_ref.dtype)

        def diff_scores(h):
            c0 = h * 2 * HEAD_DIM
            return [qk(qb_ref[:, c:c + HEAD_DIM], kb_ref[0:nk, c:c + HEAD_DIM]) for c in (c0, c0 + HEAD_DIM)]

        def diff_attend(h, p):
            c0 = h * 2 * HEAD_DIM
            o = jnp.dot(jnp.concatenate(p, axis=0), vb_ref[0:nk, 2 * c0:2 * c0 + 2 * LANES],
                        preferred_element_type=F32)
            o = normalised(o[0:tq], 2 * HEAD_DIM) - lam * normalised(o[tq:2 * tq], 2 * HEAD_DIM)
            o = o * lax.rsqrt(jnp.mean(o * o, axis=-1, keepdims=True) + RMS_EPS) * sg_ref[...]
            o = o * (1.0 - lambda_init)
            mrg_ref[:, GQA_Q_W + c0:GQA_Q_W + c0 + 2 * HEAD_DIM] = o.astype(mrg_ref.dtype)

        units = ([(functools.partial(gqa_scores, g), functools.partial(gqa_attend, g))
                  for g in range(GQA_Q_HEADS // GQA_GROUP)]
                 + [(functools.partial(diff_scores, h), functools.partial(diff_attend, h))
                    for h in range(DIFF_HEADS)])
        n = len(units)
        s, p = [None] * n, [None] * n
        for step in range(n + 2):
            if step < n:
                s[step] = units[step][0]()
            if 1 <= step <= n:
                p[step - 1] = [probs(x) for x in s[step - 1]]
                s[step - 1] = None
            if 2 <= step:
                units[step - 2][1](p[step - 2])
                p[step - 2] = None

    i = pl.program_id(1)

    @pl.when(i == 0)
    def _():
        run(ctx_len)

    @pl.when(i > 0)
    def _():
        run(ka_ref.shape[0])

    y = jnp.dot(mrg_ref[...], wo_ref[...], preferred_element_type=F32)
    o_ref[...] = h_ref[...] + mod_ref[2:3, :] * y


def _attention(qkv, lam_rows, subln_g, w_out, h, mod_tab, lambda_init, ctx_len):
    qa, ka, va, qb, kb, vb = qkv
    b, nt, d = h.shape
    tq = ROW_TILE
    assert ctx_len == tq
    blk = lambda w: pl.BlockSpec((None, tq, w), lambda bb, i: (bb, i, 0))
    per_batch = lambda w: pl.BlockSpec((None, nt, w), lambda bb, i: (bb, 0, 0))
    full = lambda shape: pl.BlockSpec(shape, lambda bb, i: (0,) * len(shape))
    return pl.pallas_call(
        functools.partial(_attn_kernel, lambda_init=lambda_init, ctx_len=ctx_len),
        grid=(b, nt // tq),
        in_specs=[blk(GQA_Q_W), blk(DIFF_QK_W), per_batch(GQA_KV_W), per_batch(2 * GQA_KV_W),
                  per_batch(DIFF_QK_W), per_batch(2 * DIFF_V_W),
                  full((8, LANES)), full((1, LANES)), full(w_out.shape), blk(d), _mod_spec(d)],
        out_specs=blk(d),
        out_shape=jax.ShapeDtypeStruct((b, nt, d), F32),
        scratch_shapes=[pltpu.VMEM((tq, GQA_Q_W + DIFF_V_W), _MXU_DTYPE)],
        compiler_params=_params("parallel", "parallel"),
        name="attention",
    )(qa, qb, ka, va, kb, vb, lam_rows, subln_g, w_out, h, mod_tab)


def _norm1_kernel(h_ref, mod_ref, g_ref, o_ref):
    o_ref[...] = _norm_mod(h_ref[...], g_ref[...], mod_ref[0:1, :], mod_ref[1:2, :])


def _norm1(h, mod_tab, g):
    b, nt, d = h.shape
    tm = ROW_TILE
    blk = pl.BlockSpec((None, tm, d), lambda bb, i: (bb, i, 0))
    return pl.pallas_call(
        _norm1_kernel,
        grid=(b, nt // tm),
        in_specs=[blk, _mod_spec(d), pl.BlockSpec((1, d), lambda bb, i: (0, 0))],
        out_specs=blk,
        out_shape=jax.ShapeDtypeStruct((b, nt, d), F32),
        compiler_params=_params("parallel", "parallel"),
        name="ssm_norm",
    )(h, mod_tab, g)


def _ssm_kernel(u_ref, win_ref, m_ref, wout_ref, lam_ref, y_ref, bd_ref, *, chunk, n_ctx_chunks):
    nb, nt, _ = u_ref.shape
    nc = nt // chunk
    n_state_slabs = bd_ref.shape[0]
    q = n_state_slabs // 4

    def chunk_rows(bi):
        parts = [u_ref[bi, pl.ds(s, nc, stride=chunk), :] for s in range(chunk)]
        return jnp.concatenate(parts, axis=1).astype(_MXU_DTYPE)

    for bi in range(nb):
        drive = jnp.dot(chunk_rows(bi), win_ref[...], preferred_element_type=F32)
        for c in range(n_state_slabs):
            bd_ref[c, pl.ds(bi, nc, stride=nb), :] = drive[:, c * LANES:(c + 1) * LANES]

    lam = lam_ref[...]

    def make_step(base):
        a_re = [lam[:, (base + c) * LANES:(base + c + 1) * LANES] for c in range(q)]
        a_im = [lam[:, (base + q + c) * LANES:(base + q + c + 1) * LANES] for c in range(q)]

        def step(k, carry):
            row = pl.multiple_of(k * nb, nb)
            out = []
            for c in range(q):
                s_re, s_im = carry[2 * c], carry[2 * c + 1]
                d_re = bd_ref[base + c, pl.ds(row, nb), :]
                d_im = bd_ref[base + q + c, pl.ds(row, nb), :]
                bd_ref[base + c, pl.ds(row, nb), :] = s_re
                bd_ref[base + q + c, pl.ds(row, nb), :] = s_im
                out.append(a_re[c] * s_re - a_im[c] * s_im + d_re)
                out.append(a_re[c] * s_im + a_im[c] * s_re + d_im)
            return tuple(out)

        return step

    zero = tuple(jnp.zeros((nb, LANES), F32) for _ in range(2 * q))
    fwd = make_step(0)
    lax.fori_loop(0, nc, fwd, zero)
    rev = make_step(2 * q)
    carry = lax.fori_loop(0, n_ctx_chunks, lambda i, cr: rev(n_ctx_chunks - 1 - i, cr), zero)
    lax.fori_loop(0, nc - n_ctx_chunks, lambda i, cr: rev(nc - 1 - i, cr), carry)

    for bi in range(nb):
        states = jnp.concatenate([bd_ref[c, pl.ds(bi, nc, stride=nb), :] for c in range(n_state_slabs)], axis=1)
        y = (jnp.dot(chunk_rows(bi), m_ref[...], preferred_element_type=F32)
             + jnp.dot(states.astype(_MXU_DTYPE), wout_ref[...], preferred_element_type=F32))
        for t in range(chunk):
            y_ref[bi, pl.ds(t, nc, stride=chunk), :] = y[:, t * LANES:(t + 1) * LANES]


def _ssm_scan(u, win, m, wout, lam_t, ctx_len, first_slab):
    b, nt, d = u.shape
    chunk = SSM_CHUNK
    nb = 4 if b % 4 == 0 else b
    n_slabs = d // LANES
    state_w = win.shape[-1]
    nc = nt // chunk
    blk = pl.BlockSpec((nb, nt, LANES), lambda j, bb: (bb, 0, j))
    table = lambda j, bb: (first_slab + j, 0, 0)
    return pl.pallas_call(
        functools.partial(_ssm_kernel, chunk=chunk, n_ctx_chunks=ctx_len // chunk),
        grid=(n_slabs, b // nb),
        in_specs=[blk,
                  pl.BlockSpec((None,) + win.shape[1:], table),
                  pl.BlockSpec((None,) + m.shape[1:], table),
                  pl.BlockSpec((None,) + wout.shape[1:], table),
                  pl.BlockSpec((None, 1, state_w), table)],
        out_specs=blk,
        out_shape=jax.ShapeDtypeStruct((b, nt, d), F32),
        scratch_shapes=[pltpu.VMEM((state_w // LANES, nc * nb, LANES), F32)],
        compiler_params=_params("parallel", "parallel"),
        name="ssm_scan",
    )(u, win, m, wout, lam_t)


def _ssm_out_kernel(y_ref, u_ref, d_ref, wa_ref, wb_ref, h_ref, mod_ref, o_ref):
    x = y_ref[...] + d_ref[...] * u_ref[...]
    z = 0.5 * x * (1.0 + jnp.tanh(math.sqrt(2.0 / math.pi) * (x + 0.044715 * (x * x * x))))
    z = z.astype(_MXU_DTYPE)
    a = jnp.dot(z, wa_ref[...], preferred_element_type=F32)
    g = jnp.dot(z, wb_ref[...], preferred_element_type=F32)
    o_ref[...] = h_ref[...] + mod_ref[2:3, :] * (a / (1.0 + jnp.exp(-g)))


def _ssm_out(y, u, d_skip, wa, wb, h, mod_tab):
    b, nt, d = h.shape
    tm = ROW_TILE
    blk = pl.BlockSpec((None, tm, d), lambda bb, i: (bb, i, 0))
    full = lambda shape: pl.BlockSpec(shape, lambda bb, i: (0,) * len(shape))
    return pl.pallas_call(
        _ssm_out_kernel,
        grid=(b, nt // tm),
        in_specs=[blk, blk, full((1, d)), full(wa.shape), full(wb.shape), blk, _mod_spec(d)],
        out_specs=blk,
        out_shape=jax.ShapeDtypeStruct((b, nt, d), F32),
        compiler_params=_params("parallel", "parallel"),
        name="ssm_out",
    )(y, u, d_skip, wa, wb, h, mod_tab)


def _split_terms(x, n):
    terms = []
    for _ in range(n - 1):
        t = x.astype(_MXU_DTYPE)
        terms.append(t)
        x = x - t.astype(F32)
    terms.append(x.astype(_MXU_DTYPE))
    return terms


def _ssm_tables_kernel(lam_ref, bt_ref, c_ref, m_ref, win_ref, wout_ref, lamt_ref, *, chunk):
    gpt = LANES // SSM_GROUP_CH
    p = SSM_STATE
    sw = gpt * p
    ci = lax.broadcasted_iota(jnp.int32, (p, sw), 0)
    oi = lax.broadcasted_iota(jnp.int32, (p, sw), 1)
    spread = (ci == oi % p).astype(_MXU_DTYPE)
    ri = lax.broadcasted_iota(jnp.int32, (LANES, sw), 0)
    oj = lax.broadcasted_iota(jnp.int32, (LANES, sw), 1)
    own_group = (ri // SSM_GROUP_CH) == (oj // p)

    def block_diag(x):
        y = jnp.zeros((LANES, sw), F32)
        for term in _split_terms(x, 3):
            y = y + jnp.dot(term, spread, preferred_element_type=F32)
        return jnp.where(own_group, y, 0.0)

    def cmul(ar, ai, br, bi):
        return ar * br - ai * bi, ar * bi + ai * br

    nt_dot = lambda a, b: lax.dot_general(a, b, (((1,), (1,)), ((), ())), precision=_HIGHEST,
                                          preferred_element_type=F32)
    taps, drive, read = [], [], []
    for x in range(2):
        lre = jnp.minimum(lam_ref[x, 0:1, :], -1e-4)
        lim = lam_ref[x, 1:2, :]
        dt = jnp.exp(lam_ref[x, 2:3, :])
        pw = []
        for j in range(chunk + 1):
            mag = jnp.exp(float(j) * (lre * dt))
            ang = float(j) * (lim * dt)
            pw.append((mag * jnp.cos(ang), mag * jnp.sin(ang)))
        nr = pw[1][0] - 1.0
        ni = pw[1][1]
        den = lre * lre + lim * lim
        coef_re = (nr * lre + ni * lim) / den
        coef_im = (ni * lre - nr * lim) / den
        bb = cmul(coef_re, coef_im, block_diag(bt_ref[x, 0]), block_diag(bt_ref[x, 1]))
        cc = (block_diag(c_ref[x, 0]), block_diag(c_ref[x, 1]))
        drive.append([cmul(pw[j][0], pw[j][1], bb[0], bb[1]) for j in range(chunk)])
        read.append([cmul(pw[j][0], pw[j][1], cc[0], cc[1]) for j in range(chunk + 1)])
        taps.append([nt_dot(u_re, cc[0]) - nt_dot(u_im, cc[1]) for u_re, u_im in drive[x]])
        lamt_ref[:, 2 * x * sw:(2 * x + 1) * sw] = pw[chunk][0]
        lamt_ref[:, (2 * x + 1) * sw:(2 * x + 2) * sw] = pw[chunk][1]

    for s in range(chunk):
        rows = slice(s * LANES, (s + 1) * LANES)
        for t in range(chunk):
            if t > s:
                blk = taps[0][t - s]
            elif t < s:
                blk = taps[1][s - t]
            else:
                blk = taps[0][0] + taps[1][0]
            m_ref[rows, t * LANES:(t + 1) * LANES] = blk.astype(m_ref.dtype)
        f_re, f_im = drive[0][chunk - 1 - s]
        r_re, r_im = drive[1][s]
        for q, part in enumerate((f_re, f_im, r_re, r_im)):
            win_ref[rows, q * sw:(q + 1) * sw] = part.astype(win_ref.dtype)
    for t in range(chunk):
        cols = slice(t * LANES, (t + 1) * LANES)
        f_re, f_im = read[0][t + 1]
        r_re, r_im = read[1][chunk - t]
        for q, part in enumerate((f_re, -f_im, r_re, -r_im)):
            wout_ref[q * sw:(q + 1) * sw, cols] = part.T.astype(wout_ref.dtype)


def _ssm_tables(a_re, a_im, log_dt, b_re, b_im, c_re, c_im, chunk):
    n_layers, _, g_total, p = a_re.shape
    gpt = LANES // SSM_GROUP_CH
    n_slabs = g_total // gpt
    n = n_layers * n_slabs
    sw = gpt * p

    def slab_lanes(x):
        return jnp.transpose(x.reshape(n_layers, 2, n_slabs, sw), (0, 2, 1, 3)).reshape(n, 2, sw)

    def slab_rows(x):
        x = x.reshape(n_layers, 2, n_slabs, LANES, p)
        return jnp.transpose(x, (0, 2, 1, 3, 4)).reshape(n, 2, LANES, p)

    lam = jnp.stack([slab_lanes(a_re), slab_lanes(a_im),
                     slab_lanes(jnp.broadcast_to(log_dt[..., None], a_re.shape))], axis=2)
    lam = jnp.pad(lam, ((0, 0), (0, 0), (0, 5), (0, 0)))
    bt = jnp.stack([slab_rows(jnp.swapaxes(b_re, -1, -2)), slab_rows(jnp.swapaxes(b_im, -1, -2))], axis=2)
    ct = jnp.stack([slab_rows(c_re), slab_rows(c_im)], axis=2)
    rows = chunk * LANES
    per_slab = lambda *shape: pl.BlockSpec((None,) + shape, lambda i: (i,) + (0,) * len(shape))
    m, win, wout, lam_t = pl.pallas_call(
        functools.partial(_ssm_tables_kernel, chunk=chunk),
        grid=(n,),
        in_specs=[per_slab(2, 8, sw), per_slab(2, 2, LANES, p), per_slab(2, 2, LANES, p)],
        out_specs=[per_slab(rows, rows), per_slab(rows, 4 * sw), per_slab(4 * sw, rows), per_slab(1, 4 * sw)],
        out_shape=[jax.ShapeDtypeStruct((n, rows, rows), _MXU_DTYPE),
                   jax.ShapeDtypeStruct((n, rows, 4 * sw), _MXU_DTYPE),
                   jax.ShapeDtypeStruct((n, 4 * sw, rows), _MXU_DTYPE),
                   jax.ShapeDtypeStruct((n, 1, 4 * sw), F32)],
        compiler_params=_params("parallel"),
        name="ssm_tables",
    )(lam, bt, ct)
    return win, m, wout, lam_t


def _router_kernel(h_ref, mod_ref, g_ref, wr_ref, br_ref, xt_ref, cmb_ref):
    xt = _norm_mod(h_ref[...], g_ref[...], mod_ref[3:4, :], mod_ref[4:5, :])
    xt_ref[...] = xt.astype(xt_ref.dtype)
    x_hi, x_lo = _split_terms(xt, 2)
    w_hi, w_lo = _split_terms(wr_ref[...], 2)
    dot = lambda a, b: jnp.dot(a, b, preferred_element_type=F32)
    logits = dot(x_hi, w_hi) + (dot(x_hi, w_lo) + dot(x_lo, w_hi)) + br_ref[...]
    lane = lax.broadcasted_iota(jnp.int32, (1, LANES), 1)
    lane_f = lane.astype(F32)
    neg = -jnp.inf
    big = 1e9
    gmask = (lane >= MOE_EXPERTS) & (lane < MOE_EXPERTS + MOE_GROUPS)
    gl = jnp.where(gmask, logits, neg)
    gmax = jnp.max(gl, axis=-1, keepdims=True)
    gidx = jnp.min(jnp.where(gl == gmax, lane_f, big), axis=-1, keepdims=True) - MOE_EXPERTS
    p_group = 1.0 / jnp.sum(jnp.where(gmask, jnp.exp(gl - gmax), 0.0), axis=-1, keepdims=True)
    in_group = (lane < MOE_EXPERTS) & ((lane // MOE_EPG).astype(F32) == gidx)
    el = jnp.where(in_group, logits, neg)
    v1 = jnp.max(el, axis=-1, keepdims=True)
    i1 = jnp.min(jnp.where(el == v1, lane_f, big), axis=-1, keepdims=True)
    el2 = jnp.where(lane_f == i1, neg, el)
    v2 = jnp.max(el2, axis=-1, keepdims=True)
    i2 = jnp.min(jnp.where(el2 == v2, lane_f, big), axis=-1, keepdims=True)
    t = jnp.exp(v2 - v1)
    w1 = p_group / (1.0 + t)
    w2 = p_group * t / (1.0 + t)
    cmb_ref[...] = jnp.where(lane_f == i1, w1, 0.0) + jnp.where(lane_f == i2, w2, 0.0)


def _router(h, mod_tab, g, wr, br):
    b, nt, d = h.shape
    tm = ROW_TILE
    full = lambda shape: pl.BlockSpec(shape, lambda bb, i: (0,) * len(shape))
    return pl.pallas_call(
        _router_kernel,
        grid=(b, nt // tm),
        in_specs=[pl.BlockSpec((None, tm, d), lambda bb, i: (bb, i, 0)), _mod_spec(d),
                  full((1, d)), full(wr.shape), full(br.shape)],
        out_specs=[pl.BlockSpec((None, tm, d), lambda bb, i: (bb, i, 0)),
                   pl.BlockSpec((None, tm, LANES), lambda bb, i: (bb, i, 0))],
        out_shape=[jax.ShapeDtypeStruct((b, nt, d), _MXU_DTYPE),
                   jax.ShapeDtypeStruct((b, nt, LANES), F32)],
        compiler_params=_params("parallel", "parallel"),
        name="moe_router",
    )(h, mod_tab, g, wr, br)


def _experts_kernel(xt_ref, cmb_ref, wg_ref, wu_ref, wd_ref, h_ref, mod_ref, o_ref,
                    xs_ref, cs_ref, acc_ref, pos_ref, seg_ref, *, ctx_len, nt, windows):
    i = pl.program_id(0)
    e = pl.program_id(1)
    sb = xt_ref.shape[0]
    d = xt_ref.shape[1]

    window = windows[-1]

    @pl.when(e == 0)
    def _():
        cmb = cmb_ref[...]
        lane = lax.broadcasted_iota(jnp.int32, (1, LANES), 1)
        routed = cmb != 0.0
        goh = jnp.zeros((sb, LANES), F32)
        for g in range(MOE_GROUPS):
            in_g = routed & (lane >= g * MOE_EPG) & (lane < (g + 1) * MOE_EPG)
            hit = jnp.max(jnp.where(in_g, 1.0, 0.0), axis=-1, keepdims=True)
            goh = goh + jnp.where(lane == g, hit, 0.0)
        tri = (lax.broadcasted_iota(jnp.int32, (LANES, LANES), 1)
               < lax.broadcasted_iota(jnp.int32, (LANES, LANES), 0)).astype(_MXU_DTYPE)
        cnt = jnp.zeros((1, LANES), F32)
        parts = []
        for k in range(sb // LANES):
            tile = goh[k * LANES:(k + 1) * LANES]
            parts.append(jnp.dot(tri, tile.astype(_MXU_DTYPE), preferred_element_type=F32) + cnt)
            cnt = cnt + jnp.sum(tile, axis=0, keepdims=True)
        before = jnp.concatenate(parts, axis=0)
        r_i = lax.broadcasted_iota(jnp.int32, (sb, sb), 0)
        off = jnp.zeros((1, LANES), F32)
        run = jnp.zeros((1, 1), F32)
        for g in range(MOE_GROUPS):
            off = off + jnp.where(lane == g, run, 0.0)
            run = run + jnp.sum(jnp.where(lane == g, cnt, 0.0), axis=-1, keepdims=True)
        pos = jnp.sum(goh * (off + before), axis=-1, keepdims=True)
        pos_b = jnp.broadcast_to(pos, (sb, LANES))
        pos_ref[...] = pos_b
        pos_row = pos_b.T[0:1, :].astype(jnp.int32)
        perm = (r_i == pos_row).astype(_MXU_DTYPE)
        xs_ref[0:sb, :] = jnp.dot(perm, xt_ref[...], preferred_element_type=F32).astype(xs_ref.dtype)
        cs = jnp.zeros((sb, LANES), F32)
        for term in _split_terms(cmb, 3):
            cs = cs + jnp.dot(perm, term, preferred_element_type=F32)
        cs_ref[0:sb, :] = cs
        xs_ref[sb:sb + window, :] = jnp.zeros((window, d), xs_ref.dtype)
        cs_ref[sb:sb + window, :] = jnp.zeros((window, LANES), F32)
        acc_ref[...] = jnp.zeros_like(acc_ref)
        off_i = off.astype(jnp.int32)
        cnt_i = cnt.astype(jnp.int32)
        for g in range(MOE_GROUPS):
            seg_ref[g] = off_i[0, g]
            seg_ref[MOE_GROUPS + g] = cnt_i[0, g]

    eps = wg_ref.shape[0]
    g = (e * eps) // MOE_EPG
    start = seg_ref[g]
    count = seg_ref[MOE_GROUPS + g]
    first = (start // ROW_ALIGN) * ROW_ALIGN
    span = start - first + count
    lane = lax.broadcasted_iota(jnp.int32, (1, LANES), 1)

    def apply_experts(r0, rows):
        r0 = pl.multiple_of(r0, ROW_ALIGN)
        x = xs_ref[pl.ds(r0, rows), :]
        cw = cs_ref[pl.ds(r0, rows), :]
        y = jnp.zeros((rows, d), F32)
        for j in range(eps):
            gate = jnp.dot(x, wg_ref[j], preferred_element_type=F32)
            up = jnp.dot(x, wu_ref[j], preferred_element_type=F32)
            w = jnp.sum(jnp.where(lane == e * eps + j, cw, 0.0), axis=-1, keepdims=True)
            hid = (gate / (1.0 + jnp.exp(-gate))) * up * w
            y = y + jnp.dot(hid.astype(_MXU_DTYPE), wd_ref[j], preferred_element_type=F32)
        acc_ref[pl.ds(r0, rows), :] += y

    smaller = 0
    for rows in windows:
        @pl.when((count > 0) & (span > smaller) & (span <= rows))
        def _(rows=rows):
            apply_experts(first, rows)
        smaller = rows

    @pl.when(span > window)
    def _():
        def window_step(k, carry):
            apply_experts(first + k * window, window)
            return carry
        lax.fori_loop(0, (span + window - 1) // window, window_step, 0)

    @pl.when(e == pl.num_programs(1) - 1)
    def _():
        c_i = lax.broadcasted_iota(jnp.int32, (sb, sb), 1)
        unperm = (c_i == pos_ref[:, 0:1].astype(jnp.int32)).astype(_MXU_DTYPE)
        y = jnp.zeros((sb, d), F32)
        for term in _split_terms(acc_ref[0:sb, :], 2):
            y = y + jnp.dot(unperm, term, preferred_element_type=F32)
        row = (i * sb) % nt + lax.broadcasted_iota(jnp.int32, (sb, 1), 0)
        gate_row = jnp.where(row < ctx_len, mod_ref[0, 5:6, :], mod_ref[1, 5:6, :])
        o_ref[...] = h_ref[...] + gate_row * y


def _experts(xt, cmb, wg, wu, wd, h, mod_tab, ctx_len):
    b, nt, d = h.shape
    sb = 1152 if nt % 1152 == 0 else ROW_TILE
    windows = (128, 192, 256, 336, 448, 576) if sb == 1152 else (48, 96)
    window = windows[-1]
    per_b = nt // sb
    n_exp, _, hid = wg.shape
    rows = b * nt
    blk = lambda w: pl.BlockSpec((sb, w), lambda i, e: (i, 0))
    out = pl.pallas_call(
        functools.partial(_experts_kernel, ctx_len=ctx_len, nt=nt, windows=windows),
        grid=(rows // sb, n_exp // EXPERTS_PER_STEP),
        in_specs=[blk(d), blk(LANES),
                  pl.BlockSpec((EXPERTS_PER_STEP, d, hid), lambda i, e: (e, 0, 0)),
                  pl.BlockSpec((EXPERTS_PER_STEP, d, hid), lambda i, e: (e, 0, 0)),
                  pl.BlockSpec((EXPERTS_PER_STEP, hid, d), lambda i, e: (e, 0, 0)),
                  blk(d),
                  pl.BlockSpec((None, 2, 8, d), lambda i, e: (i // per_b, 0, 0, 0))],
        out_specs=blk(d),
        out_shape=jax.ShapeDtypeStruct((rows, d), F32),
        scratch_shapes=[pltpu.VMEM((sb + window, d), _MXU_DTYPE), pltpu.VMEM((sb + window, LANES), F32),
                        pltpu.VMEM((sb + window, d), F32), pltpu.VMEM((sb, LANES), F32),
                        pltpu.SMEM((2 * MOE_GROUPS,), jnp.int32)],
        compiler_params=_params("parallel", "arbitrary"),
        name="moe_experts",
    )(xt.reshape(rows, d), cmb.reshape(rows, LANES), wg, wu, wd, h.reshape(rows, d), mod_tab)
    return out.reshape(b, nt, d)


def _final_kernel(h_ref, g_ref, o_ref):
    h = h_ref[...]
    o_ref[...] = h * lax.rsqrt(jnp.mean(h * h, axis=-1, keepdims=True) + RMS_EPS) * g_ref[...]


def _final_norm(h, g, ctx_len):
    b, nt, d = h.shape
    tm = ROW_TILE
    skip = ctx_len // tm
    return pl.pallas_call(
        _final_kernel,
        grid=(b, (nt - ctx_len) // tm),
        in_specs=[pl.BlockSpec((None, tm, d), lambda bb, i: (bb, i + skip, 0)),
                  pl.BlockSpec((1, d), lambda bb, i: (0, 0))],
        out_specs=pl.BlockSpec((None, tm, d), lambda bb, i: (bb, i, 0)),
        out_shape=jax.ShapeDtypeStruct((b, nt - ctx_len, d), F32),
        compiler_params=_params("parallel", "parallel"),
        name="final_norm",
    )(h, g)


def _rope_tables(seq_len, ctx_len):
    n_rows = seq_len // GRID_W
    rows = jnp.repeat(jnp.arange(n_rows, dtype=F32), GRID_W)
    cols = jnp.tile(jnp.arange(GRID_W, dtype=F32), n_rows)
    half = HEAD_DIM // 2
    inv = 1.0 / (ROPE_BASE ** (jnp.arange(0, half, 2, dtype=F32) / half))
    ang_r = rows[:, None] * inv
    ang_c = cols[:, None] * inv
    ang = jnp.concatenate([ang_r, ang_r, ang_c, ang_c], axis=-1)
    ang = jnp.concatenate([jnp.zeros((ctx_len, HEAD_DIM), F32), ang], axis=0)
    ang = jnp.tile(ang, (1, LANES // HEAD_DIM))
    return jnp.cos(ang), jnp.sin(ang)


def _pad_row(v, width=LANES):
    return jnp.pad(v, (0, width - v.shape[0]))[None, :]


def kernel(x, c, ctx, c_ctx, mod_w, mod_b, norm1_g, norm2_g, final_g, attn_w_in, attn_w_out, attn_q_norm_g, attn_k_norm_g, diff_lambda_q1, diff_lambda_k1, diff_lambda_q2, diff_lambda_k2, diff_subln_g, ssm_a_re, ssm_a_im, ssm_log_dt, ssm_b_re, ssm_b_im, ssm_c_re, ssm_c_im, ssm_d, ssm_glu_w_a, ssm_glu_w_b, moe_group_w, moe_group_b, moe_router_w, moe_router_b, moe_w_gate, moe_w_up, moe_w_down):
    bsz, seq, d = x.shape
    ctx_len = ctx.shape[1]
    depth = mod_w.shape[0]
    assert ctx_len == ROW_TILE and seq % ROW_TILE == 0 and seq % GRID_W == 0

    h = jnp.concatenate([ctx, x], axis=1)

    mod_rows = 16
    c_all = jnp.concatenate([c, c_ctx[None, :], jnp.zeros((mod_rows - bsz - 1, d), F32)], axis=0)
    mods = _modulation(c_all, mod_w, mod_b).reshape(depth, mod_rows, 6, d)
    mods = jnp.pad(mods, ((0, 0), (0, 0), (0, 2), (0, 0)))
    mod_tabs = jnp.stack([jnp.broadcast_to(mods[:, bsz:bsz + 1], (depth, bsz, 8, d)), mods[:, :bsz]], axis=2)

    cos, sin = _rope_tables(seq, ctx_len)
    cast = lambda w: w.astype(_MXU_DTYPE)
    ssm_tabs = _ssm_tables(ssm_a_re, ssm_a_im, ssm_log_dt, ssm_b_re, ssm_b_im, ssm_c_re, ssm_c_im, SSM_CHUNK)

    for layer in range(depth):
        mod_tab = mod_tabs[layer]
        i = layer // 2
        if layer % 2 == 0:
            lambda_init = 0.8 - 0.6 * math.exp(-0.3 * layer)
            qkv = _attn_proj(h, mod_tab, norm1_g[layer][None, :], cast(attn_w_in[i]),
                             jnp.tile(attn_q_norm_g[i], 2)[None, :], jnp.tile(attn_k_norm_g[i], 2)[None, :],
                             cos, sin)
            lam_rows = jnp.concatenate([_pad_row(diff_lambda_q1[i]), _pad_row(diff_lambda_k1[i]),
                                        _pad_row(diff_lambda_q2[i]), _pad_row(diff_lambda_k2[i]),
                                        jnp.zeros((4, LANES), F32)], axis=0)
            h = _attention(qkv, lam_rows, diff_subln_g[i][None, :], cast(attn_w_out[i]), h, mod_tab,
                           lambda_init, ctx_len)
        else:
            u = _norm1(h, mod_tab, norm1_g[layer][None, :])
            y = _ssm_scan(u, *ssm_tabs, ctx_len, i * (d // LANES))
            h = _ssm_out(y, u, ssm_d[i][None, :], cast(ssm_glu_w_a[i]), cast(ssm_glu_w_b[i]), h, mod_tab)

        wr = jnp.concatenate([jnp.transpose(moe_router_w[layer], (1, 0, 2)).reshape(d, MOE_EXPERTS),
                              moe_group_w[layer],
                              jnp.zeros((d, LANES - MOE_EXPERTS - MOE_GROUPS), F32)], axis=1)
        br = _pad_row(jnp.concatenate([moe_router_b[layer].reshape(-1), moe_group_b[layer]]))
        xt, cmb = _router(h, mod_tab, norm2_g[layer][None, :], wr, br)
        h = _experts(xt, cmb, cast(moe_w_gate[layer]), cast(moe_w_up[layer]), cast(moe_w_down[layer]),
                     h, mod_tab, ctx_len)

    return _final_norm(h, final_g[None, :], ctx_len)
```

```python
import functools
import math

import jax
import jax.numpy as jnp
from jax import lax
from jax.experimental import pallas as pl
from jax.experimental.pallas import tpu as pltpu

F32 = jnp.float32
_MXU_DTYPE = jnp.bfloat16
_HIGHEST = lax.Precision.HIGHEST

LANES = 128
HEAD_DIM = 64
GRID_W = 64
ROPE_BASE = 10000.0
GQA_Q_HEADS = 8
GQA_GROUP = 4
DIFF_HEADS = 4
GQA_Q_W = 512
GQA_KV_W = 128
DIFF_QK_W = 512
DIFF_V_W = 512
SSM_GROUP_CH = 16
SSM_STATE = 64
MOE_GROUPS = 4
MOE_EPG = 8
MOE_EXPERTS = 32
RMS_EPS = 1e-6
SSM_CHUNK = 8
ROW_TILE = 256
ROW_ALIGN = 16
EXPERTS_PER_STEP = 4
VMEM_LIMIT = 60 * 1024 * 1024


def _params(*sem):
    return pltpu.CompilerParams(dimension_semantics=sem, vmem_limit_bytes=VMEM_LIMIT)


def _norm_mod(h, g, shift, scale):
    y = h * lax.rsqrt(jnp.mean(h * h, axis=-1, keepdims=True) + RMS_EPS) * g
    return y * (1.0 + scale) + shift


def _mm(a, b):
    return jnp.dot(a.astype(_MXU_DTYPE), b.astype(_MXU_DTYPE), preferred_element_type=F32)


def _mod_kernel(c_ref, w_ref, b_ref, o_ref):
    c = c_ref[...]
    a = c / (1.0 + jnp.exp(-c))
    o_ref[...] = jnp.dot(a, w_ref[...], preferred_element_type=F32, precision=_HIGHEST) + b_ref[...]


def _modulation(c_all, mod_w, mod_b):
    depth, d, n = mod_w.shape
    rows = c_all.shape[0]
    tn = 1536
    return pl.pallas_call(
        _mod_kernel,
        grid=(depth, n // tn),
        in_specs=[pl.BlockSpec((rows, d), lambda l, j: (0, 0)),
                  pl.BlockSpec((None, d, tn), lambda l, j: (l, 0, j)),
                  pl.BlockSpec((None, 1, tn), lambda l, j: (l, 0, j))],
        out_specs=pl.BlockSpec((None, rows, tn), lambda l, j: (l, 0, j)),
        out_shape=jax.ShapeDtypeStruct((depth, rows, n), F32),
        compiler_params=_params("parallel", "parallel"),
        name="modulation",
    )(c_all, mod_w, mod_b.reshape(depth, 1, n))


def _mod_spec(d):
    return pl.BlockSpec((None, None, 8, d), lambda b, i: (b, jnp.minimum(i, 1), 0, 0))


def _attn_proj_kernel(h_ref, mod_ref, g_ref, w_ref, gq_ref, gk_ref, cos_ref, sin_ref,
                      qa_ref, ka_ref, va_ref, qb_ref, kb_ref, vb_ref):
    xn = _norm_mod(h_ref[...], g_ref[...], mod_ref[0:1, :], mod_ref[1:2, :])
    hp = _mm(xn, w_ref[...])
    cos = cos_ref[...]
    sin = sin_ref[...]
    lane = lax.broadcasted_iota(jnp.int32, (1, LANES), 1)
    first_half = (lane % 32) < 16
    r = lax.broadcasted_iota(jnp.int32, (LANES, LANES), 0) // HEAD_DIM
    c = lax.broadcasted_iota(jnp.int32, (LANES, LANES), 1) // HEAD_DIM
    same_head = (r == c).astype(_MXU_DTYPE)

    def rope(x):
        rot = jnp.where(first_half, -pltpu.roll(x, LANES - 16, 1), pltpu.roll(x, 16, 1))
        return x * cos + rot * sin

    def head_norm(x, g):
        ss = jnp.dot((x * x).astype(_MXU_DTYPE), same_head, preferred_element_type=F32)
        return x * lax.rsqrt(ss * (1.0 / HEAD_DIM) + RMS_EPS) * g

    scale = HEAD_DIM ** -0.5 * math.log2(math.e)
    o = 0
    for s in range(GQA_Q_W // LANES):
        x = hp[:, o + s * LANES:o + (s + 1) * LANES]
        qa_ref[:, s * LANES:(s + 1) * LANES] = (rope(head_norm(x, gq_ref[...])) * scale).astype(qa_ref.dtype)
    o += GQA_Q_W
    ka_ref[...] = rope(head_norm(hp[:, o:o + LANES], gk_ref[...])).astype(ka_ref.dtype)
    o += GQA_KV_W
    v_pair = hp[:, o:o + LANES]
    low_half = lane < HEAD_DIM
    va_ref[:, 0:LANES] = jnp.where(low_half, v_pair, 1.0).astype(va_ref.dtype)
    va_ref[:, LANES:2 * LANES] = jnp.where(low_half, pltpu.roll(v_pair, HEAD_DIM, 1), 1.0).astype(va_ref.dtype)
    o += GQA_KV_W
    for s in range(DIFF_QK_W // LANES):
        x = hp[:, o + s * LANES:o + (s + 1) * LANES]
        qb_ref[:, s * LANES:(s + 1) * LANES] = (rope(x) * scale).astype(qb_ref.dtype)
    o += DIFF_QK_W
    for s in range(DIFF_QK_W // LANES):
        x = hp[:, o + s * LANES:o + (s + 1) * LANES]
        kb_ref[:, s * LANES:(s + 1) * LANES] = rope(x).astype(kb_ref.dtype)
    o += DIFF_QK_W
    for s in range(DIFF_HEADS):
        vb_ref[:, 2 * s * LANES:(2 * s + 1) * LANES] = hp[:, o + s * LANES:o + (s + 1) * LANES].astype(vb_ref.dtype)
        vb_ref[:, (2 * s + 1) * LANES:(2 * s + 2) * LANES] = jnp.ones((hp.shape[0], LANES), vb_ref.dtype)


def _attn_proj(h, mod_tab, g, w_in, gq, gk, cos, sin):
    b, nt, d = h.shape
    tm = ROW_TILE
    widths = (GQA_Q_W, GQA_KV_W, 2 * GQA_KV_W, DIFF_QK_W, DIFF_QK_W, 2 * DIFF_V_W)
    full = lambda shape: pl.BlockSpec(shape, lambda bb, i: (0,) * len(shape))
    return pl.pallas_call(
        _attn_proj_kernel,
        grid=(b, nt // tm),
        in_specs=[pl.BlockSpec((None, tm, d), lambda bb, i: (bb, i, 0)),
                  _mod_spec(d),
                  full((1, d)),
                  full(w_in.shape),
                  full((1, LANES)),
                  full((1, LANES)),
                  pl.BlockSpec((tm, LANES), lambda bb, i: (i, 0)),
                  pl.BlockSpec((tm, LANES), lambda bb, i: (i, 0))],
        out_specs=[pl.BlockSpec((None, tm, w), lambda bb, i: (bb, i, 0)) for w in widths],
        out_shape=[jax.ShapeDtypeStruct((b, nt, w), _MXU_DTYPE) for w in widths],
        compiler_params=_params("parallel", "parallel"),
        name="attn_proj",
    )(h, mod_tab, g, w_in, gq, gk, cos, sin)


def _attn_kernel(qa_ref, qb_ref, ka_ref, va_ref, kb_ref, vb_ref, lam_ref, sg_ref, wo_ref, h_ref, mod_ref,
                 o_ref, mrg_ref, *, lambda_init, ctx_len):
    lv = lam_ref[...]
    lam = (jnp.exp(jnp.sum(lv[0:1] * lv[1:2], axis=-1, keepdims=True))
           - jnp.exp(jnp.sum(lv[2:3] * lv[3:4], axis=-1, keepdims=True)) + lambda_init)

    def run(nk):
        tq = qa_ref.shape[0]
        qk = lambda q, k: lax.dot_general(q, k, (((1,), (1,)), ((), ())), preferred_element_type=F32)
        probs = lambda s: jnp.exp2((s - jnp.max(s, axis=-1, keepdims=True)).astype(_MXU_DTYPE))

        def normalised(o, dv):
            return o[:, 0:dv] / o[:, dv:dv + 1]

        def gqa_scores(g):
            q = jnp.concatenate([qa_ref[:, h * HEAD_DIM:(h + 1) * HEAD_DIM]
                                 for h in range(g * GQA_GROUP, (g + 1) * GQA_GROUP)], axis=0)
            return [qk(q, ka_ref[0:nk, g * HEAD_DIM:(g + 1) * HEAD_DIM])]

        def gqa_attend(g, p):
            o = jnp.dot(p[0], va_ref[0:nk, g * LANES:(g + 1) * LANES], preferred_element_type=F32)
            for j in range(GQA_GROUP):
                h = g * GQA_GROUP + j
                mrg_ref[:, h * HEAD_DIM:(h + 1) * HEAD_DIM] = normalised(
                    o[j * tq:(j + 1) * tq], HEAD_DIM).astype(mrg_ref.dtype)

        def diff_scores(h):
            c0 = h * 2 * HEAD_DIM
            return [qk(qb_ref[:, c:c + HEAD_DIM], kb_ref[0:nk, c:c + HEAD_DIM]) for c in (c0, c0 + HEAD_DIM)]

        def diff_attend(h, p):
            c0 = h * 2 * HEAD_DIM
            o = jnp.dot(jnp.concatenate(p, axis=0), vb_ref[0:nk, 2 * c0:2 * c0 + 2 * LANES],
                        preferred_element_type=F32)
            o = normalised(o[0:tq], 2 * HEAD_DIM) - lam * normalised(o[tq:2 * tq], 2 * HEAD_DIM)
            o = o * lax.rsqrt(jnp.mean(o * o, axis=-1, keepdims=True) + RMS_EPS) * sg_ref[...]
            o = o * (1.0 - lambda_init)
            mrg_ref[:, GQA_Q_W + c0:GQA_Q_W + c0 + 2 * HEAD_DIM] = o.astype(mrg_ref.dtype)

        units = ([(functools.partial(gqa_scores, g), functools.partial(gqa_attend, g))
                  for g in range(GQA_Q_HEADS // GQA_GROUP)]
                 + [(functools.partial(diff_scores, h), functools.partial(diff_attend, h))
                    for h in range(DIFF_HEADS)])
        n = len(units)
        s, p = [None] * n, [None] * n
        for step in range(n + 2):
            if step < n:
                s[step] = units[step][0]()
            if 1 <= step <= n:
                p[step - 1] = [probs(x) for x in s[step - 1]]
                s[step - 1] = None
            if 2 <= step:
                units[step - 2][1](p[step - 2])
                p[step - 2] = None

    i = pl.program_id(1)

    @pl.when(i == 0)
    def _():
        run(ctx_len)

    @pl.when(i > 0)
    def _():
        run(ka_ref.shape[0])

    y = jnp.dot(mrg_ref[...], wo_ref[...], preferred_element_type=F32)
    o_ref[...] = h_ref[...] + mod_ref[2:3, :] * y


def _attention(qkv, lam_rows, subln_g, w_out, h, mod_tab, lambda_init, ctx_len):
    qa, ka, va, qb, kb, vb = qkv
    b, nt, d = h.shape
    tq = ROW_TILE
    assert ctx_len == tq
    blk = lambda w: pl.BlockSpec((None, tq, w), lambda bb, i: (bb, i, 0))
    per_batch = lambda w: pl.BlockSpec((None, nt, w), lambda bb, i: (bb, 0, 0))
    full = lambda shape: pl.BlockSpec(shape, lambda bb, i: (0,) * len(shape))
    return pl.pallas_call(
        functools.partial(_attn_kernel, lambda_init=lambda_init, ctx_len=ctx_len),
        grid=(b, nt // tq),
        in_specs=[blk(GQA_Q_W), blk(DIFF_QK_W), per_batch(GQA_KV_W), per_batch(2 * GQA_KV_W),
                  per_batch(DIFF_QK_W), per_batch(2 * DIFF_V_W),
                  full((8, LANES)), full((1, LANES)), full(w_out.shape), blk(d), _mod_spec(d)],
        out_specs=blk(d),
        out_shape=jax.ShapeDtypeStruct((b, nt, d), F32),
        scratch_shapes=[pltpu.VMEM((tq, GQA_Q_W + DIFF_V_W), _MXU_DTYPE)],
        compiler_params=_params("parallel", "parallel"),
        name="attention",
    )(qa, qb, ka, va, kb, vb, lam_rows, subln_g, w_out, h, mod_tab)


def _norm1_kernel(h_ref, mod_ref, g_ref, o_ref):
    o_ref[...] = _norm_mod(h_ref[...], g_ref[...], mod_ref[0:1, :], mod_ref[1:2, :])


def _norm1(h, mod_tab, g):
    b, nt, d = h.shape
    tm = ROW_TILE
    blk = pl.BlockSpec((None, tm, d), lambda bb, i: (bb, i, 0))
    return pl.pallas_call(
        _norm1_kernel,
        grid=(b, nt // tm),
        in_specs=[blk, _mod_spec(d), pl.BlockSpec((1, d), lambda bb, i: (0, 0))],
        out_specs=blk,
        out_shape=jax.ShapeDtypeStruct((b, nt, d), F32),
        compiler_params=_params("parallel", "parallel"),
        name="ssm_norm",
    )(h, mod_tab, g)


def _ssm_kernel(u_ref, win_ref, m_ref, wout_ref, lam_ref, y_ref, bd_ref, *, chunk, n_ctx_chunks):
    nb, nt, _ = u_ref.shape
    nc = nt // chunk
    n_state_slabs = bd_ref.shape[0]
    q = n_state_slabs // 4

    def chunk_rows(bi):
        parts = [u_ref[bi, pl.ds(s, nc, stride=chunk), :] for s in range(chunk)]
        return jnp.concatenate(parts, axis=1).astype(_MXU_DTYPE)

    for bi in range(nb):
        drive = jnp.dot(chunk_rows(bi), win_ref[...], preferred_element_type=F32)
        for c in range(n_state_slabs):
            bd_ref[c, pl.ds(bi, nc, stride=nb), :] = drive[:, c * LANES:(c + 1) * LANES]

    lam = lam_ref[...]

    def make_step(base):
        a_re = [lam[:, (base + c) * LANES:(base + c + 1) * LANES] for c in range(q)]
        a_im = [lam[:, (base + q + c) * LANES:(base + q + c + 1) * LANES] for c in range(q)]

        def step(k, carry):
            row = pl.multiple_of(k * nb, nb)
            out = []
            for c in range(q):
                s_re, s_im = carry[2 * c], carry[2 * c + 1]
                d_re = bd_ref[base + c, pl.ds(row, nb), :]
                d_im = bd_ref[base + q + c, pl.ds(row, nb), :]
                bd_ref[base + c, pl.ds(row, nb), :] = s_re
                bd_ref[base + q + c, pl.ds(row, nb), :] = s_im
                out.append(a_re[c] * s_re - a_im[c] * s_im + d_re)
                out.append(a_re[c] * s_im + a_im[c] * s_re + d_im)
            return tuple(out)

        return step

    zero = tuple(jnp.zeros((nb, LANES), F32) for _ in range(2 * q))
    fwd = make_step(0)
    lax.fori_loop(0, nc, fwd, zero)
    rev = make_step(2 * q)
    carry = lax.fori_loop(0, n_ctx_chunks, lambda i, cr: rev(n_ctx_chunks - 1 - i, cr), zero)
    lax.fori_loop(0, nc - n_ctx_chunks, lambda i, cr: rev(nc - 1 - i, cr), carry)

    for bi in range(nb):
        states = jnp.concatenate([bd_ref[c, pl.ds(bi, nc, stride=nb), :] for c in range(n_state_slabs)], axis=1)
        y = (jnp.dot(chunk_rows(bi), m_ref[...], preferred_element_type=F32)
             + jnp.dot(states.astype(_MXU_DTYPE), wout_ref[...], preferred_element_type=F32))
        for t in range(chunk):
            y_ref[bi, pl.ds(t, nc, stride=chunk), :] = y[:, t * LANES:(t + 1) * LANES]


def _ssm_scan(u, win, m, wout, lam_t, ctx_len, first_slab):
    b, nt, d = u.shape
    chunk = SSM_CHUNK
    nb = 4 if b % 4 == 0 else b
    n_slabs = d // LANES
    state_w = win.shape[-1]
    nc = nt // chunk
    blk = pl.BlockSpec((nb, nt, LANES), lambda j, bb: (bb, 0, j))
    table = lambda j, bb: (first_slab + j, 0, 0)
    return pl.pallas_call(
        functools.partial(_ssm_kernel, chunk=chunk, n_ctx_chunks=ctx_len // chunk),
        grid=(n_slabs, b // nb),
        in_specs=[blk,
                  pl.BlockSpec((None,) + win.shape[1:], table),
                  pl.BlockSpec((None,) + m.shape[1:], table),
                  pl.BlockSpec((None,) + wout.shape[1:], table),
                  pl.BlockSpec((None, 1, state_w), table)],
        out_specs=blk,
        out_shape=jax.ShapeDtypeStruct((b, nt, d), F32),
        scratch_shapes=[pltpu.VMEM((state_w // LANES, nc * nb, LANES), F32)],
        compiler_params=_params("parallel", "parallel"),
        name="ssm_scan",
    )(u, win, m, wout, lam_t)


def _ssm_out_kernel(y_ref, u_ref, d_ref, wa_ref, wb_ref, h_ref, mod_ref, o_ref):
    x = y_ref[...] + d_ref[...] * u_ref[...]
    z = 0.5 * x * (1.0 + jnp.tanh(math.sqrt(2.0 / math.pi) * (x + 0.044715 * (x * x * x))))
    z = z.astype(_MXU_DTYPE)
    a = jnp.dot(z, wa_ref[...], preferred_element_type=F32)
    g = jnp.dot(z, wb_ref[...], preferred_element_type=F32)
    o_ref[...] = h_ref[...] + mod_ref[2:3, :] * (a / (1.0 + jnp.exp(-g)))


def _ssm_out(y, u, d_skip, wa, wb, h, mod_tab):
    b, nt, d = h.shape
    tm = ROW_TILE
    blk = pl.BlockSpec((None, tm, d), lambda bb, i: (bb, i, 0))
    full = lambda shape: pl.BlockSpec(shape, lambda bb, i: (0,) * len(shape))
    return pl.pallas_call(
        _ssm_out_kernel,
        grid=(b, nt // tm),
        in_specs=[blk, blk, full((1, d)), full(wa.shape), full(wb.shape), blk, _mod_spec(d)],
        out_specs=blk,
        out_shape=jax.ShapeDtypeStruct((b, nt, d), F32),
        compiler_params=_params("parallel", "parallel"),
        name="ssm_out",
    )(y, u, d_skip, wa, wb, h, mod_tab)


def _split_terms(x, n):
    terms = []
    for _ in range(n - 1):
        t = x.astype(_MXU_DTYPE)
        terms.append(t)
        x = x - t.astype(F32)
    terms.append(x.astype(_MXU_DTYPE))
    return terms


def _ssm_tables_kernel(lam_ref, bt_ref, c_ref, m_ref, win_ref, wout_ref, lamt_ref, *, chunk):
    gpt = LANES // SSM_GROUP_CH
    p = SSM_STATE
    sw = gpt * p
    ci = lax.broadcasted_iota(jnp.int32, (p, sw), 0)
    oi = lax.broadcasted_iota(jnp.int32, (p, sw), 1)
    spread = (ci == oi % p).astype(_MXU_DTYPE)
    ri = lax.broadcasted_iota(jnp.int32, (LANES, sw), 0)
    oj = lax.broadcasted_iota(jnp.int32, (LANES, sw), 1)
    own_group = (ri // SSM_GROUP_CH) == (oj // p)

    def block_diag(x):
        y = jnp.zeros((LANES, sw), F32)
        for term in _split_terms(x, 3):
            y = y + jnp.dot(term, spread, preferred_element_type=F32)
        return jnp.where(own_group, y, 0.0)

    def cmul(ar, ai, br, bi):
        return ar * br - ai * bi, ar * bi + ai * br

    nt_dot = lambda a, b: lax.dot_general(a, b, (((1,), (1,)), ((), ())), precision=_HIGHEST,
                                          preferred_element_type=F32)
    taps, drive, read = [], [], []
    for x in range(2):
        lre = jnp.minimum(lam_ref[x, 0:1, :], -1e-4)
        lim = lam_ref[x, 1:2, :]
        dt = jnp.exp(lam_ref[x, 2:3, :])
        pw = []
        for j in range(chunk + 1):
            mag = jnp.exp(float(j) * (lre * dt))
            ang = float(j) * (lim * dt)
            pw.append((mag * jnp.cos(ang), mag * jnp.sin(ang)))
        nr = pw[1][0] - 1.0
        ni = pw[1][1]
        den = lre * lre + lim * lim
        coef_re = (nr * lre + ni * lim) / den
        coef_im = (ni * lre - nr * lim) / den
        bb = cmul(coef_re, coef_im, block_diag(bt_ref[x, 0]), block_diag(bt_ref[x, 1]))
        cc = (block_diag(c_ref[x, 0]), block_diag(c_ref[x, 1]))
        drive.append([cmul(pw[j][0], pw[j][1], bb[0], bb[1]) for j in range(chunk)])
        read.append([cmul(pw[j][0], pw[j][1], cc[0], cc[1]) for j in range(chunk + 1)])
        taps.append([nt_dot(u_re, cc[0]) - nt_dot(u_im, cc[1]) for u_re, u_im in drive[x]])
        lamt_ref[:, 2 * x * sw:(2 * x + 1) * sw] = pw[chunk][0]
        lamt_ref[:, (2 * x + 1) * sw:(2 * x + 2) * sw] = pw[chunk][1]

    for s in range(chunk):
        rows = slice(s * LANES, (s + 1) * LANES)
        for t in range(chunk):
            if t > s:
                blk = taps[0][t - s]
            elif t < s:
                blk = taps[1][s - t]
            else:
                blk = taps[0][0] + taps[1][0]
            m_ref[rows, t * LANES:(t + 1) * LANES] = blk.astype(m_ref.dtype)
        f_re, f_im = drive[0][chunk - 1 - s]
        r_re, r_im = drive[1][s]
        for q, part in enumerate((f_re, f_im, r_re, r_im)):
            win_ref[rows, q * sw:(q + 1) * sw] = part.astype(win_ref.dtype)
    for t in range(chunk):
        cols = slice(t * LANES, (t + 1) * LANES)
        f_re, f_im = read[0][t + 1]
        r_re, r_im = read[1][chunk - t]
        for q, part in enumerate((f_re, -f_im, r_re, -r_im)):
            wout_ref[q * sw:(q + 1) * sw, cols] = part.T.astype(wout_ref.dtype)


def _ssm_tables(a_re, a_im, log_dt, b_re, b_im, c_re, c_im, chunk):
    n_layers, _, g_total, p = a_re.shape
    gpt = LANES // SSM_GROUP_CH
    n_slabs = g_total // gpt
    n = n_layers * n_slabs
    sw = gpt * p

    def slab_lanes(x):
        return jnp.transpose(x.reshape(n_layers, 2, n_slabs, sw), (0, 2, 1, 3)).reshape(n, 2, sw)

    def slab_rows(x):
        x = x.reshape(n_layers, 2, n_slabs, LANES, p)
        return jnp.transpose(x, (0, 2, 1, 3, 4)).reshape(n, 2, LANES, p)

    lam = jnp.stack([slab_lanes(a_re), slab_lanes(a_im),
                     slab_lanes(jnp.broadcast_to(log_dt[..., None], a_re.shape))], axis=2)
    lam = jnp.pad(lam, ((0, 0), (0, 0), (0, 5), (0, 0)))
    bt = jnp.stack([slab_rows(jnp.swapaxes(b_re, -1, -2)), slab_rows(jnp.swapaxes(b_im, -1, -2))], axis=2)
    ct = jnp.stack([slab_rows(c_re), slab_rows(c_im)], axis=2)
    rows = chunk * LANES
    per_slab = lambda *shape: pl.BlockSpec((None,) + shape, lambda i: (i,) + (0,) * len(shape))
    m, win, wout, lam_t = pl.pallas_call(
        functools.partial(_ssm_tables_kernel, chunk=chunk),
        grid=(n,),
        in_specs=[per_slab(2, 8, sw), per_slab(2, 2, LANES, p), per_slab(2, 2, LANES, p)],
        out_specs=[per_slab(rows, rows), per_slab(rows, 4 * sw), per_slab(4 * sw, rows), per_slab(1, 4 * sw)],
        out_shape=[jax.ShapeDtypeStruct((n, rows, rows), _MXU_DTYPE),
                   jax.ShapeDtypeStruct((n, rows, 4 * sw), _MXU_DTYPE),
                   jax.ShapeDtypeStruct((n, 4 * sw, rows), _MXU_DTYPE),
                   jax.ShapeDtypeStruct((n, 1, 4 * sw), F32)],
        compiler_params=_params("parallel"),
        name="ssm_tables",
    )(lam, bt, ct)
    return win, m, wout, lam_t


def _router_kernel(h_ref, mod_ref, g_ref, wr_ref, br_ref, xt_ref, cmb_ref):
    xt = _norm_mod(h_ref[...], g_ref[...], mod_ref[3:4, :], mod_ref[4:5, :])
    xt_ref[...] = xt.astype(xt_ref.dtype)
    x_hi, x_lo = _split_terms(xt, 2)
    w_hi, w_lo = _split_terms(wr_ref[...], 2)
    dot = lambda a, b: jnp.dot(a, b, preferred_element_type=F32)
    logits = dot(x_hi, w_hi) + (dot(x_hi, w_lo) + dot(x_lo, w_hi)) + br_ref[...]
    lane = lax.broadcasted_iota(jnp.int32, (1, LANES), 1)
    lane_f = lane.astype(F32)
    neg = -jnp.inf
    big = 1e9
    gmask = (lane >= MOE_EXPERTS) & (lane < MOE_EXPERTS + MOE_GROUPS)
    gl = jnp.where(gmask, logits, neg)
    gmax = jnp.max(gl, axis=-1, keepdims=True)
    gidx = jnp.min(jnp.where(gl == gmax, lane_f, big), axis=-1, keepdims=True) - MOE_EXPERTS
    p_group = 1.0 / jnp.sum(jnp.where(gmask, jnp.exp(gl - gmax), 0.0), axis=-1, keepdims=True)
    in_group = (lane < MOE_EXPERTS) & ((lane // MOE_EPG).astype(F32) == gidx)
    el = jnp.where(in_group, logits, neg)
    v1 = jnp.max(el, axis=-1, keepdims=True)
    i1 = jnp.min(jnp.where(el == v1, lane_f, big), axis=-1, keepdims=True)
    el2 = jnp.where(lane_f == i1, neg, el)
    v2 = jnp.max(el2, axis=-1, keepdims=True)
    i2 = jnp.min(jnp.where(el2 == v2, lane_f, big), axis=-1, keepdims=True)
    t = jnp.exp(v2 - v1)
    w1 = p_group / (1.0 + t)
    w2 = p_group * t / (1.0 + t)
    cmb_ref[...] = jnp.where(lane_f == i1, w1, 0.0) + jnp.where(lane_f == i2, w2, 0.0)


def _router(h, mod_tab, g, wr, br):
    b, nt, d = h.shape
    tm = ROW_TILE
    full = lambda shape: pl.BlockSpec(shape, lambda bb, i: (0,) * len(shape))
    return pl.pallas_call(
        _router_kernel,
        grid=(b, nt // tm),
        in_specs=[pl.BlockSpec((None, tm, d), lambda bb, i: (bb, i, 0)), _mod_spec(d),
                  full((1, d)), full(wr.shape), full(br.shape)],
        out_specs=[pl.BlockSpec((None, tm, d), lambda bb, i: (bb, i, 0)),
                   pl.BlockSpec((None, tm, LANES), lambda bb, i: (bb, i, 0))],
        out_shape=[jax.ShapeDtypeStruct((b, nt, d), _MXU_DTYPE),
                   jax.ShapeDtypeStruct((b, nt, LANES), F32)],
        compiler_params=_params("parallel", "parallel"),
        name="moe_router",
    )(h, mod_tab, g, wr, br)


def _snake(block, step, n_steps):
    return jnp.where(block % 2 == 0, step, n_steps - 1 - step)


def _experts_kernel(xt_ref, cmb_ref, wg_ref, wu_ref, wd_ref, h_ref, mod_ref, o_ref,
                    xs_ref, cs_ref, acc_ref, pos_ref, seg_ref, *, ctx_len, nt, windows):
    i = pl.program_id(0)
    e = pl.program_id(1)
    sb = xt_ref.shape[0]
    d = xt_ref.shape[1]

    window = windows[-1]

    @pl.when(e == 0)
    def _():
        cmb = cmb_ref[...]
        lane = lax.broadcasted_iota(jnp.int32, (1, LANES), 1)
        routed = cmb != 0.0
        goh = jnp.zeros((sb, LANES), F32)
        for g in range(MOE_GROUPS):
            in_g = routed & (lane >= g * MOE_EPG) & (lane < (g + 1) * MOE_EPG)
            hit = jnp.max(jnp.where(in_g, 1.0, 0.0), axis=-1, keepdims=True)
            goh = goh + jnp.where(lane == g, hit, 0.0)
        tri = (lax.broadcasted_iota(jnp.int32, (LANES, LANES), 1)
               < lax.broadcasted_iota(jnp.int32, (LANES, LANES), 0)).astype(_MXU_DTYPE)
        cnt = jnp.zeros((1, LANES), F32)
        parts = []
        for k in range(sb // LANES):
            tile = goh[k * LANES:(k + 1) * LANES]
            parts.append(jnp.dot(tri, tile.astype(_MXU_DTYPE), preferred_element_type=F32) + cnt)
            cnt = cnt + jnp.sum(tile, axis=0, keepdims=True)
        before = jnp.concatenate(parts, axis=0)
        r_i = lax.broadcasted_iota(jnp.int32, (sb, sb), 0)
        off = jnp.zeros((1, LANES), F32)
        run = jnp.zeros((1, 1), F32)
        for g in range(MOE_GROUPS):
            off = off + jnp.where(lane == g, run, 0.0)
            run = run + jnp.sum(jnp.where(lane == g, cnt, 0.0), axis=-1, keepdims=True)
        pos = jnp.sum(goh * (off + before), axis=-1, keepdims=True)
        pos_b = jnp.broadcast_to(pos, (sb, LANES))
        pos_ref[...] = pos_b
        pos_row = pos_b.T[0:1, :].astype(jnp.int32)
        perm = (r_i == pos_row).astype(_MXU_DTYPE)
        xs_ref[0:sb, :] = jnp.dot(perm, xt_ref[...], preferred_element_type=F32).astype(xs_ref.dtype)
        cs = jnp.zeros((sb, LANES), F32)
        for term in _split_terms(cmb, 3):
            cs = cs + jnp.dot(perm, term, preferred_element_type=F32)
        cs_ref[0:sb, :] = cs
        xs_ref[sb:sb + window, :] = jnp.zeros((window, d), xs_ref.dtype)
        cs_ref[sb:sb + window, :] = jnp.zeros((window, LANES), F32)
        acc_ref[...] = jnp.zeros_like(acc_ref)
        off_i = off.astype(jnp.int32)
        cnt_i = cnt.astype(jnp.int32)
        for g in range(MOE_GROUPS):
            seg_ref[g] = off_i[0, g]
            seg_ref[MOE_GROUPS + g] = cnt_i[0, g]

    eps = wg_ref.shape[0]
    es = _snake(i, e, pl.num_programs(1))
    g = (es * eps) // MOE_EPG
    start = seg_ref[g]
    count = seg_ref[MOE_GROUPS + g]
    first = (start // ROW_ALIGN) * ROW_ALIGN
    span = start - first + count
    lane = lax.broadcasted_iota(jnp.int32, (1, LANES), 1)

    def apply_experts(r0, rows):
        r0 = pl.multiple_of(r0, ROW_ALIGN)
        x = xs_ref[pl.ds(r0, rows), :]
        cw = cs_ref[pl.ds(r0, rows), :]
        y = jnp.zeros((rows, d), F32)
        for j in range(eps):
            gate = jnp.dot(x, wg_ref[j], preferred_element_type=F32)
            up = jnp.dot(x, wu_ref[j], preferred_element_type=F32)
            w = jnp.sum(jnp.where(lane == es * eps + j, cw, 0.0), axis=-1, keepdims=True)
            hid = (gate / (1.0 + jnp.exp(-gate))) * up * w
            y = y + jnp.dot(hid.astype(_MXU_DTYPE), wd_ref[j], preferred_element_type=F32)
        acc_ref[pl.ds(r0, rows), :] += y

    smaller = 0
    for rows in windows:
        @pl.when((count > 0) & (span > smaller) & (span <= rows))
        def _(rows=rows):
            apply_experts(first, rows)
        smaller = rows

    @pl.when(span > window)
    def _():
        def window_step(k, carry):
            apply_experts(first + k * window, window)
            return carry
        lax.fori_loop(0, (span + window - 1) // window, window_step, 0)

    @pl.when(e == pl.num_programs(1) - 1)
    def _():
        c_i = lax.broadcasted_iota(jnp.int32, (sb, sb), 1)
        unperm = (c_i == pos_ref[:, 0:1].astype(jnp.int32)).astype(_MXU_DTYPE)
        y = jnp.zeros((sb, d), F32)
        for term in _split_terms(acc_ref[0:sb, :], 2):
            y = y + jnp.dot(unperm, term, preferred_element_type=F32)
        row = (i * sb) % nt + lax.broadcasted_iota(jnp.int32, (sb, 1), 0)
        gate_row = jnp.where(row < ctx_len, mod_ref[0, 5:6, :], mod_ref[1, 5:6, :])
        o_ref[...] = h_ref[...] + gate_row * y


def _experts(xt, cmb, wg, wu, wd, h, mod_tab, ctx_len):
    b, nt, d = h.shape
    sb = 1152 if nt % 1152 == 0 else ROW_TILE
    windows = (128, 192, 256, 336, 448, 576) if sb == 1152 else (48, 96)
    window = windows[-1]
    per_b = nt // sb
    n_exp, _, hid = wg.shape
    rows = b * nt
    blk = lambda w: pl.BlockSpec((sb, w), lambda i, e: (i, 0))
    n_steps = n_exp // EXPERTS_PER_STEP
    weights = lambda i, e: (_snake(i, e, n_steps), 0, 0)
    out = pl.pallas_call(
        functools.partial(_experts_kernel, ctx_len=ctx_len, nt=nt, windows=windows),
        grid=(rows // sb, n_exp // EXPERTS_PER_STEP),
        in_specs=[blk(d), blk(LANES),
                  pl.BlockSpec((EXPERTS_PER_STEP, d, hid), weights),
                  pl.BlockSpec((EXPERTS_PER_STEP, d, hid), weights),
                  pl.BlockSpec((EXPERTS_PER_STEP, hid, d), weights),
                  blk(d),
                  pl.BlockSpec((None, 2, 8, d), lambda i, e: (i // per_b, 0, 0, 0))],
        out_specs=blk(d),
        out_shape=jax.ShapeDtypeStruct((rows, d), F32),
        scratch_shapes=[pltpu.VMEM((sb + window, d), _MXU_DTYPE), pltpu.VMEM((sb + window, LANES), F32),
                        pltpu.VMEM((sb + window, d), F32), pltpu.VMEM((sb, LANES), F32),
                        pltpu.SMEM((2 * MOE_GROUPS,), jnp.int32)],
        compiler_params=_params("parallel", "arbitrary"),
        name="moe_experts",
    )(xt.reshape(rows, d), cmb.reshape(rows, LANES), wg, wu, wd, h.reshape(rows, d), mod_tab)
    return out.reshape(b, nt, d)


def _final_kernel(h_ref, g_ref, o_ref):
    h = h_ref[...]
    o_ref[...] = h * lax.rsqrt(jnp.mean(h * h, axis=-1, keepdims=True) + RMS_EPS) * g_ref[...]


def _final_norm(h, g, ctx_len):
    b, nt, d = h.shape
    tm = ROW_TILE
    skip = ctx_len // tm
    return pl.pallas_call(
        _final_kernel,
        grid=(b, (nt - ctx_len) // tm),
        in_specs=[pl.BlockSpec((None, tm, d), lambda bb, i: (bb, i + skip, 0)),
                  pl.BlockSpec((1, d), lambda bb, i: (0, 0))],
        out_specs=pl.BlockSpec((None, tm, d), lambda bb, i: (bb, i, 0)),
        out_shape=jax.ShapeDtypeStruct((b, nt - ctx_len, d), F32),
        compiler_params=_params("parallel", "parallel"),
        name="final_norm",
    )(h, g)


def _rope_tables(seq_len, ctx_len):
    n_rows = seq_len // GRID_W
    rows = jnp.repeat(jnp.arange(n_rows, dtype=F32), GRID_W)
    cols = jnp.tile(jnp.arange(GRID_W, dtype=F32), n_rows)
    half = HEAD_DIM // 2
    inv = 1.0 / (ROPE_BASE ** (jnp.arange(0, half, 2, dtype=F32) / half))
    ang_r = rows[:, None] * inv
    ang_c = cols[:, None] * inv
    ang = jnp.concatenate([ang_r, ang_r, ang_c, ang_c], axis=-1)
    ang = jnp.concatenate([jnp.zeros((ctx_len, HEAD_DIM), F32), ang], axis=0)
    ang = jnp.tile(ang, (1, LANES // HEAD_DIM))
    return jnp.cos(ang), jnp.sin(ang)


def _pad_row(v, width=LANES):
    return jnp.pad(v, (0, width - v.shape[0]))[None, :]


def kernel(x, c, ctx, c_ctx, mod_w, mod_b, norm1_g, norm2_g, final_g, attn_w_in, attn_w_out, attn_q_norm_g, attn_k_norm_g, diff_lambda_q1, diff_lambda_k1, diff_lambda_q2, diff_lambda_k2, diff_subln_g, ssm_a_re, ssm_a_im, ssm_log_dt, ssm_b_re, ssm_b_im, ssm_c_re, ssm_c_im, ssm_d, ssm_glu_w_a, ssm_glu_w_b, moe_group_w, moe_group_b, moe_router_w, moe_router_b, moe_w_gate, moe_w_up, moe_w_down):
    bsz, seq, d = x.shape
    ctx_len = ctx.shape[1]
    depth = mod_w.shape[0]
    assert ctx_len == ROW_TILE and seq % ROW_TILE == 0 and seq % GRID_W == 0

    h = jnp.concatenate([ctx, x], axis=1)

    mod_rows = 16
    c_all = jnp.concatenate([c, c_ctx[None, :], jnp.zeros((mod_rows - bsz - 1, d), F32)], axis=0)
    mods = _modulation(c_all, mod_w, mod_b).reshape(depth, mod_rows, 6, d)
    mods = jnp.pad(mods, ((0, 0), (0, 0), (0, 2), (0, 0)))
    mod_tabs = jnp.stack([jnp.broadcast_to(mods[:, bsz:bsz + 1], (depth, bsz, 8, d)), mods[:, :bsz]], axis=2)

    cos, sin = _rope_tables(seq, ctx_len)
    cast = lambda w: w.astype(_MXU_DTYPE)
    ssm_tabs = _ssm_tables(ssm_a_re, ssm_a_im, ssm_log_dt, ssm_b_re, ssm_b_im, ssm_c_re, ssm_c_im, SSM_CHUNK)

    for layer in range(depth):
        mod_tab = mod_tabs[layer]
        i = layer // 2
        if layer % 2 == 0:
            lambda_init = 0.8 - 0.6 * math.exp(-0.3 * layer)
            qkv = _attn_proj(h, mod_tab, norm1_g[layer][None, :], cast(attn_w_in[i]),
                             jnp.tile(attn_q_norm_g[i], 2)[None, :], jnp.tile(attn_k_norm_g[i], 2)[None, :],
                             cos, sin)
            lam_rows = jnp.concatenate([_pad_row(diff_lambda_q1[i]), _pad_row(diff_lambda_k1[i]),
                                        _pad_row(diff_lambda_q2[i]), _pad_row(diff_lambda_k2[i]),
                                        jnp.zeros((4, LANES), F32)], axis=0)
            h = _attention(qkv, lam_rows, diff_subln_g[i][None, :], cast(attn_w_out[i]), h, mod_tab,
                           lambda_init, ctx_len)
        else:
            u = _norm1(h, mod_tab, norm1_g[layer][None, :])
            y = _ssm_scan(u, *ssm_tabs, ctx_len, i * (d // LANES))
            h = _ssm_out(y, u, ssm_d[i][None, :], cast(ssm_glu_w_a[i]), cast(ssm_glu_w_b[i]), h, mod_tab)

        wr = jnp.concatenate([jnp.transpose(moe_router_w[layer], (1, 0, 2)).reshape(d, MOE_EXPERTS),
                              moe_group_w[layer],
                              jnp.zeros((d, LANES - MOE_EXPERTS - MOE_GROUPS), F32)], axis=1)
        br = _pad_row(jnp.concatenate([moe_router_b[layer].reshape(-1), moe_group_b[layer]]))
        xt, cmb = _router(h, mod_tab, norm2_g[layer][None, :], wr, br)
        h = _experts(xt, cmb, cast(moe_w_gate[layer]), cast(moe_w_up[layer]), cast(moe_w_down[layer]),
                     h, mod_tab, ctx_len)

    return _final_norm(h, final_g[None, :], ctx_len)
```

```python
import functools
import math

import jax
import jax.numpy as jnp
from jax import lax
from jax.experimental import pallas as pl
from jax.experimental.pallas import tpu as pltpu

F32 = jnp.float32
_MXU_DTYPE = jnp.bfloat16
_HIGHEST = lax.Precision.HIGHEST

LANES = 128
HEAD_DIM = 64
GRID_W = 64
ROPE_BASE = 10000.0
GQA_Q_HEADS = 8
GQA_GROUP = 4
DIFF_HEADS = 4
GQA_Q_W = 512
GQA_KV_W = 128
DIFF_QK_W = 512
DIFF_V_W = 512
SSM_GROUP_CH = 16
SSM_STATE = 64
MOE_GROUPS = 4
MOE_EPG = 8
MOE_EXPERTS = 32
RMS_EPS = 1e-6
SSM_CHUNK = 8
SSM_BATCH_STACK = 2
ROW_TILE = 256
WIDE_TILE = 768
ROW_ALIGN = 16
EXPERTS_PER_STEP = 4
VMEM_LIMIT = 60 * 1024 * 1024


def _params(*sem):
    return pltpu.CompilerParams(dimension_semantics=sem, vmem_limit_bytes=VMEM_LIMIT)


def _norm_mod(h, g, shift, scale):
    y = h * lax.rsqrt(jnp.mean(h * h, axis=-1, keepdims=True) + RMS_EPS) * g
    return y * (1.0 + scale) + shift


def _mm(a, b):
    return jnp.dot(a.astype(_MXU_DTYPE), b.astype(_MXU_DTYPE), preferred_element_type=F32)


def _mod_kernel(c_ref, w_ref, b_ref, o_ref):
    c = c_ref[...]
    a = c / (1.0 + jnp.exp(-c))
    o_ref[...] = jnp.dot(a, w_ref[...], preferred_element_type=F32, precision=_HIGHEST) + b_ref[...]


def _modulation(c_all, mod_w, mod_b):
    depth, d, n = mod_w.shape
    rows = c_all.shape[0]
    tn = 1536
    return pl.pallas_call(
        _mod_kernel,
        grid=(depth, n // tn),
        in_specs=[pl.BlockSpec((rows, d), lambda l, j: (0, 0)),
                  pl.BlockSpec((None, d, tn), lambda l, j: (l, 0, j)),
                  pl.BlockSpec((None, 1, tn), lambda l, j: (l, 0, j))],
        out_specs=pl.BlockSpec((None, rows, tn), lambda l, j: (l, 0, j)),
        out_shape=jax.ShapeDtypeStruct((depth, rows, n), F32),
        compiler_params=_params("parallel", "parallel"),
        name="modulation",
    )(c_all, mod_w, mod_b.reshape(depth, 1, n))


def _mod_spec(d):
    return pl.BlockSpec((None, None, 8, d), lambda b, i: (b, jnp.minimum(i, 1), 0, 0))


def _mod_pair_spec(d):
    return pl.BlockSpec((None, 2, 8, d), lambda b, i: (b, 0, 0, 0))


def _wide_tile(nt):
    return WIDE_TILE if nt % WIDE_TILE == 0 else ROW_TILE


def _mod_row(mod_ref, idx, rows, ctx_len):
    row = pl.program_id(1) * rows + lax.broadcasted_iota(jnp.int32, (rows, 1), 0)
    return jnp.where(row < ctx_len, mod_ref[0, idx:idx + 1, :], mod_ref[1, idx:idx + 1, :])


def _attn_proj_kernel(h_ref, mod_ref, g_ref, w_ref, gq_ref, gk_ref, cos_ref, sin_ref,
                      qa_ref, ka_ref, va_ref, qb_ref, kb_ref, vb_ref, *, ctx_len):
    rows = h_ref.shape[0]
    xn = _norm_mod(h_ref[...], g_ref[...], _mod_row(mod_ref, 0, rows, ctx_len),
                   _mod_row(mod_ref, 1, rows, ctx_len))
    hp = _mm(xn, w_ref[...])
    cos = cos_ref[...]
    sin = sin_ref[...]
    lane = lax.broadcasted_iota(jnp.int32, (1, LANES), 1)
    first_half = (lane % 32) < 16
    r = lax.broadcasted_iota(jnp.int32, (LANES, LANES), 0) // HEAD_DIM
    c = lax.broadcasted_iota(jnp.int32, (LANES, LANES), 1) // HEAD_DIM
    same_head = (r == c).astype(_MXU_DTYPE)

    def rope(x):
        rot = jnp.where(first_half, -pltpu.roll(x, LANES - 16, 1), pltpu.roll(x, 16, 1))
        return x * cos + rot * sin

    def head_norm(x, g):
        ss = jnp.dot((x * x).astype(_MXU_DTYPE), same_head, preferred_element_type=F32)
        return x * lax.rsqrt(ss * (1.0 / HEAD_DIM) + RMS_EPS) * g

    scale = HEAD_DIM ** -0.5 * math.log2(math.e)
    o = 0
    for s in range(GQA_Q_W // LANES):
        x = hp[:, o + s * LANES:o + (s + 1) * LANES]
        qa_ref[:, s * LANES:(s + 1) * LANES] = (rope(head_norm(x, gq_ref[...])) * scale).astype(qa_ref.dtype)
    o += GQA_Q_W
    ka_ref[...] = rope(head_norm(hp[:, o:o + LANES], gk_ref[...])).astype(ka_ref.dtype)
    o += GQA_KV_W
    v_pair = hp[:, o:o + LANES]
    low_half = lane < HEAD_DIM
    va_ref[:, 0:LANES] = jnp.where(low_half, v_pair, 1.0).astype(va_ref.dtype)
    va_ref[:, LANES:2 * LANES] = jnp.where(low_half, pltpu.roll(v_pair, HEAD_DIM, 1), 1.0).astype(va_ref.dtype)
    o += GQA_KV_W
    for s in range(DIFF_QK_W // LANES):
        x = hp[:, o + s * LANES:o + (s + 1) * LANES]
        qb_ref[:, s * LANES:(s + 1) * LANES] = (rope(x) * scale).astype(qb_ref.dtype)
    o += DIFF_QK_W
    for s in range(DIFF_QK_W // LANES):
        x = hp[:, o + s * LANES:o + (s + 1) * LANES]
        kb_ref[:, s * LANES:(s + 1) * LANES] = rope(x).astype(kb_ref.dtype)
    o += DIFF_QK_W
    for s in range(DIFF_HEADS):
        vb_ref[:, 2 * s * LANES:(2 * s + 1) * LANES] = hp[:, o + s * LANES:o + (s + 1) * LANES].astype(vb_ref.dtype)
        vb_ref[:, (2 * s + 1) * LANES:(2 * s + 2) * LANES] = jnp.ones((hp.shape[0], LANES), vb_ref.dtype)


def _attn_proj(h, mod_tab, g, w_in, gq, gk, cos, sin, ctx_len):
    b, nt, d = h.shape
    tm = _wide_tile(nt)
    widths = (GQA_Q_W, GQA_KV_W, 2 * GQA_KV_W, DIFF_QK_W, DIFF_QK_W, 2 * DIFF_V_W)
    full = lambda shape: pl.BlockSpec(shape, lambda bb, i: (0,) * len(shape))
    return pl.pallas_call(
        functools.partial(_attn_proj_kernel, ctx_len=ctx_len),
        grid=(b, nt // tm),
        in_specs=[pl.BlockSpec((None, tm, d), lambda bb, i: (bb, i, 0)),
                  _mod_pair_spec(d),
                  full((1, d)),
                  full(w_in.shape),
                  full((1, LANES)),
                  full((1, LANES)),
                  pl.BlockSpec((tm, LANES), lambda bb, i: (i, 0)),
                  pl.BlockSpec((tm, LANES), lambda bb, i: (i, 0))],
        out_specs=[pl.BlockSpec((None, tm, w), lambda bb, i: (bb, i, 0)) for w in widths],
        out_shape=[jax.ShapeDtypeStruct((b, nt, w), _MXU_DTYPE) for w in widths],
        compiler_params=_params("parallel", "parallel"),
        name="attn_proj",
    )(h, mod_tab, g, w_in, gq, gk, cos, sin)


def _attn_kernel(qa_ref, qb_ref, ka_ref, va_ref, kb_ref, vb_ref, lam_ref, sg_ref, wo_ref, h_ref, mod_ref,
                 o_ref, mrg_ref, *, lambda_init, ctx_len):
    lv = lam_ref[...]
    lam = (jnp.exp(jnp.sum(lv[0:1] * lv[1:2], axis=-1, keepdims=True))
           - jnp.exp(jnp.sum(lv[2:3] * lv[3:4], axis=-1, keepdims=True)) + lambda_init)

    def run(nk):
        tq = qa_ref.shape[0]
        qk = lambda q, k: lax.dot_general(q, k, (((1,), (1,)), ((), ())), preferred_element_type=F32)
        probs = lambda s: jnp.exp2((s - jnp.max(s, axis=-1, keepdims=True)).astype(_MXU_DTYPE))

        def normalised(o, dv):
            return o[:, 0:dv] / o[:, dv:dv + 1]

        def gqa_scores(g):
            q = jnp.concatenate([qa_ref[:, h * HEAD_DIM:(h + 1) * HEAD_DIM]
                                 for h in range(g * GQA_GROUP, (g + 1) * GQA_GROUP)], axis=0)
            return [qk(q, ka_ref[0:nk, g * HEAD_DIM:(g + 1) * HEAD_DIM])]

        def gqa_attend(g, p):
            o = jnp.dot(p[0], va_ref[0:nk, g * LANES:(g + 1) * LANES], preferred_element_type=F32)
            for j in range(GQA_GROUP):
                h = g * GQA_GROUP + j
                mrg_ref[:, h * HEAD_DIM:(h + 1) * HEAD_DIM] = normalised(
                    o[j * tq:(j + 1) * tq], HEAD_DIM).astype(mrg_ref.dtype)

        def diff_scores(h):
            c0 = h * 2 * HEAD_DIM
            return [qk(qb_ref[:, c:c + HEAD_DIM], kb_ref[0:nk, c:c + HEAD_DIM]) for c in (c0, c0 + HEAD_DIM)]

        def diff_attend(h, p):
            c0 = h * 2 * HEAD_DIM
            o = jnp.dot(jnp.concatenate(p, axis=0), vb_ref[0:nk, 2 * c0:2 * c0 + 2 * LANES],
                        preferred_element_type=F32)
            o = normalised(o[0:tq], 2 * HEAD_DIM) - lam * normalised(o[tq:2 * tq], 2 * HEAD_DIM)
            o = o * lax.rsqrt(jnp.mean(o * o, axis=-1, keepdims=True) + RMS_EPS) * sg_ref[...]
            o = o * (1.0 - lambda_init)
            mrg_ref[:, GQA_Q_W + c0:GQA_Q_W + c0 + 2 * HEAD_DIM] = o.astype(mrg_ref.dtype)

        units = ([(functools.partial(gqa_scores, g), functools.partial(gqa_attend, g))
                  for g in range(GQA_Q_HEADS // GQA_GROUP)]
                 + [(functools.partial(diff_scores, h), functools.partial(diff_attend, h))
                    for h in range(DIFF_HEADS)])
        n = len(units)
        s, p = [None] * n, [None] * n
        for step in range(n + 2):
            if step < n:
                s[step] = units[step][0]()
            if 1 <= step <= n:
                p[step - 1] = [probs(x) for x in s[step - 1]]
                s[step - 1] = None
            if 2 <= step:
                units[step - 2][1](p[step - 2])
                p[step - 2] = None

    i = pl.program_id(1)

    @pl.when(i == 0)
    def _():
        run(ctx_len)

    @pl.when(i > 0)
    def _():
        run(ka_ref.shape[0])

    y = jnp.dot(mrg_ref[...], wo_ref[...], preferred_element_type=F32)
    o_ref[...] = h_ref[...] + mod_ref[2:3, :] * y


def _attention(qkv, lam_rows, subln_g, w_out, h, mod_tab, lambda_init, ctx_len):
    qa, ka, va, qb, kb, vb = qkv
    b, nt, d = h.shape
    tq = ROW_TILE
    assert ctx_len == tq
    blk = lambda w: pl.BlockSpec((None, tq, w), lambda bb, i: (bb, i, 0))
    per_batch = lambda w: pl.BlockSpec((None, nt, w), lambda bb, i: (bb, 0, 0))
    full = lambda shape: pl.BlockSpec(shape, lambda bb, i: (0,) * len(shape))
    return pl.pallas_call(
        functools.partial(_attn_kernel, lambda_init=lambda_init, ctx_len=ctx_len),
        grid=(b, nt // tq),
        in_specs=[blk(GQA_Q_W), blk(DIFF_QK_W), per_batch(GQA_KV_W), per_batch(2 * GQA_KV_W),
                  per_batch(DIFF_QK_W), per_batch(2 * DIFF_V_W),
                  full((8, LANES)), full((1, LANES)), full(w_out.shape), blk(d), _mod_spec(d)],
        out_specs=blk(d),
        out_shape=jax.ShapeDtypeStruct((b, nt, d), F32),
        scratch_shapes=[pltpu.VMEM((tq, GQA_Q_W + DIFF_V_W), _MXU_DTYPE)],
        compiler_params=_params("parallel", "parallel"),
        name="attention",
    )(qa, qb, ka, va, kb, vb, lam_rows, subln_g, w_out, h, mod_tab)


def _norm1_kernel(h_ref, mod_ref, g_ref, o_ref, *, ctx_len):
    rows = h_ref.shape[0]
    o_ref[...] = _norm_mod(h_ref[...], g_ref[...], _mod_row(mod_ref, 0, rows, ctx_len),
                           _mod_row(mod_ref, 1, rows, ctx_len))


def _norm1(h, mod_tab, g, ctx_len):
    b, nt, d = h.shape
    tm = _wide_tile(nt)
    blk = pl.BlockSpec((None, tm, d), lambda bb, i: (bb, i, 0))
    return pl.pallas_call(
        functools.partial(_norm1_kernel, ctx_len=ctx_len),
        grid=(b, nt // tm),
        in_specs=[blk, _mod_pair_spec(d), pl.BlockSpec((1, d), lambda bb, i: (0, 0))],
        out_specs=blk,
        out_shape=jax.ShapeDtypeStruct((b, nt, d), F32),
        compiler_params=_params("parallel", "parallel"),
        name="ssm_norm",
    )(h, mod_tab, g)


def _ssm_kernel(u_ref, win_ref, m_ref, wout_ref, lam_ref, y_ref, bd_ref, *, chunk, n_ctx_chunks):
    nb, nt, _ = u_ref.shape
    nc = nt // chunk
    n_state_slabs = bd_ref.shape[0]
    q = n_state_slabs // 4

    def chunk_rows(bi):
        parts = [u_ref[bi, pl.ds(s, nc, stride=chunk), :] for s in range(chunk)]
        return jnp.concatenate(parts, axis=1).astype(_MXU_DTYPE)

    stack = SSM_BATCH_STACK if nb % SSM_BATCH_STACK == 0 else 1

    def stacked_rows(b0):
        return jnp.concatenate([chunk_rows(b0 + k) for k in range(stack)], axis=0)

    for b0 in range(0, nb, stack):
        drive = jnp.dot(stacked_rows(b0), win_ref[...], preferred_element_type=F32)
        for k in range(stack):
            for c in range(n_state_slabs):
                bd_ref[c, pl.ds(b0 + k, nc, stride=nb), :] = drive[k * nc:(k + 1) * nc, c * LANES:(c + 1) * LANES]

    lam = lam_ref[...]

    def make_step(base):
        a_re = [lam[:, (base + c) * LANES:(base + c + 1) * LANES] for c in range(q)]
        a_im = [lam[:, (base + q + c) * LANES:(base + q + c + 1) * LANES] for c in range(q)]

        def step(k, carry):
            row = pl.multiple_of(k * nb, nb)
            out = []
            for c in range(q):
                s_re, s_im = carry[2 * c], carry[2 * c + 1]
                d_re = bd_ref[base + c, pl.ds(row, nb), :]
                d_im = bd_ref[base + q + c, pl.ds(row, nb), :]
                bd_ref[base + c, pl.ds(row, nb), :] = s_re
                bd_ref[base + q + c, pl.ds(row, nb), :] = s_im
                out.append(a_re[c] * s_re - a_im[c] * s_im + d_re)
                out.append(a_re[c] * s_im + a_im[c] * s_re + d_im)
            return tuple(out)

        return step

    zero = tuple(jnp.zeros((nb, LANES), F32) for _ in range(2 * q))
    fwd = make_step(0)
    lax.fori_loop(0, nc, fwd, zero)
    rev = make_step(2 * q)
    carry = lax.fori_loop(0, n_ctx_chunks, lambda i, cr: rev(n_ctx_chunks - 1 - i, cr), zero)
    lax.fori_loop(0, nc - n_ctx_chunks, lambda i, cr: rev(nc - 1 - i, cr), carry)

    for b0 in range(0, nb, stack):
        states = jnp.concatenate(
            [jnp.concatenate([bd_ref[c, pl.ds(b0 + k, nc, stride=nb), :] for c in range(n_state_slabs)], axis=1)
             for k in range(stack)], axis=0)
        y = (jnp.dot(stacked_rows(b0), m_ref[...], preferred_element_type=F32)
             + jnp.dot(states.astype(_MXU_DTYPE), wout_ref[...], preferred_element_type=F32))
        for k in range(stack):
            for t in range(chunk):
                y_ref[b0 + k, pl.ds(t, nc, stride=chunk), :] = y[k * nc:(k + 1) * nc, t * LANES:(t + 1) * LANES]


def _ssm_scan(u, win, m, wout, lam_t, ctx_len, first_slab):
    b, nt, d = u.shape
    chunk = SSM_CHUNK
    nb = 4 if b % 4 == 0 else b
    n_slabs = d // LANES
    state_w = win.shape[-1]
    nc = nt // chunk
    blk = pl.BlockSpec((nb, nt, LANES), lambda j, bb: (bb, 0, j))
    table = lambda j, bb: (first_slab + j, 0, 0)
    return pl.pallas_call(
        functools.partial(_ssm_kernel, chunk=chunk, n_ctx_chunks=ctx_len // chunk),
        grid=(n_slabs, b // nb),
        in_specs=[blk,
                  pl.BlockSpec((None,) + win.shape[1:], table),
                  pl.BlockSpec((None,) + m.shape[1:], table),
                  pl.BlockSpec((None,) + wout.shape[1:], table),
                  pl.BlockSpec((None, 1, state_w), table)],
        out_specs=blk,
        out_shape=jax.ShapeDtypeStruct((b, nt, d), F32),
        scratch_shapes=[pltpu.VMEM((state_w // LANES, nc * nb, LANES), F32)],
        compiler_params=_params("parallel", "parallel"),
        name="ssm_scan",
    )(u, win, m, wout, lam_t)


def _ssm_out_kernel(y_ref, u_ref, d_ref, wa_ref, wb_ref, h_ref, mod_ref, o_ref, *, ctx_len):
    x = y_ref[...] + d_ref[...] * u_ref[...]
    z = 0.5 * x * (1.0 + jnp.tanh(math.sqrt(2.0 / math.pi) * (x + 0.044715 * (x * x * x))))
    z = z.astype(_MXU_DTYPE)
    a = jnp.dot(z, wa_ref[...], preferred_element_type=F32)
    g = jnp.dot(z, wb_ref[...], preferred_element_type=F32)
    gate = _mod_row(mod_ref, 2, h_ref.shape[0], ctx_len)
    o_ref[...] = h_ref[...] + gate * (a / (1.0 + jnp.exp(-g)))


def _ssm_out(y, u, d_skip, wa, wb, h, mod_tab, ctx_len):
    b, nt, d = h.shape
    tm = _wide_tile(nt)
    blk = pl.BlockSpec((None, tm, d), lambda bb, i: (bb, i, 0))
    full = lambda shape: pl.BlockSpec(shape, lambda bb, i: (0,) * len(shape))
    return pl.pallas_call(
        functools.partial(_ssm_out_kernel, ctx_len=ctx_len),
        grid=(b, nt // tm),
        in_specs=[blk, blk, full((1, d)), full(wa.shape), full(wb.shape), blk, _mod_pair_spec(d)],
        out_specs=blk,
        out_shape=jax.ShapeDtypeStruct((b, nt, d), F32),
        compiler_params=_params("parallel", "parallel"),
        name="ssm_out",
    )(y, u, d_skip, wa, wb, h, mod_tab)


def _split_terms(x, n):
    terms = []
    for _ in range(n - 1):
        t = x.astype(_MXU_DTYPE)
        terms.append(t)
        x = x - t.astype(F32)
    terms.append(x.astype(_MXU_DTYPE))
    return terms


def _ssm_tables_kernel(lam_ref, bt_ref, c_ref, m_ref, win_ref, wout_ref, lamt_ref, *, chunk):
    gpt = LANES // SSM_GROUP_CH
    p = SSM_STATE
    sw = gpt * p
    ci = lax.broadcasted_iota(jnp.int32, (p, sw), 0)
    oi = lax.broadcasted_iota(jnp.int32, (p, sw), 1)
    spread = (ci == oi % p).astype(_MXU_DTYPE)
    ri = lax.broadcasted_iota(jnp.int32, (LANES, sw), 0)
    oj = lax.broadcasted_iota(jnp.int32, (LANES, sw), 1)
    own_group = (ri // SSM_GROUP_CH) == (oj // p)

    def block_diag(x):
        y = jnp.zeros((LANES, sw), F32)
        for term in _split_terms(x, 3):
            y = y + jnp.dot(term, spread, preferred_element_type=F32)
        return jnp.where(own_group, y, 0.0)

    def cmul(ar, ai, br, bi):
        return ar * br - ai * bi, ar * bi + ai * br

    nt_dot = lambda a, b: lax.dot_general(a, b, (((1,), (1,)), ((), ())), precision=_HIGHEST,
                                          preferred_element_type=F32)
    taps, drive, read = [], [], []
    for x in range(2):
        lre = jnp.minimum(lam_ref[x, 0:1, :], -1e-4)
        lim = lam_ref[x, 1:2, :]
        dt = jnp.exp(lam_ref[x, 2:3, :])
        pw = []
        for j in range(chunk + 1):
            mag = jnp.exp(float(j) * (lre * dt))
            ang = float(j) * (lim * dt)
            pw.append((mag * jnp.cos(ang), mag * jnp.sin(ang)))
        nr = pw[1][0] - 1.0
        ni = pw[1][1]
        den = lre * lre + lim * lim
        coef_re = (nr * lre + ni * lim) / den
        coef_im = (ni * lre - nr * lim) / den
        bb = cmul(coef_re, coef_im, block_diag(bt_ref[x, 0]), block_diag(bt_ref[x, 1]))
        cc = (block_diag(c_ref[x, 0]), block_diag(c_ref[x, 1]))
        drive.append([cmul(pw[j][0], pw[j][1], bb[0], bb[1]) for j in range(chunk)])
        read.append([cmul(pw[j][0], pw[j][1], cc[0], cc[1]) for j in range(chunk + 1)])
        taps.append([nt_dot(u_re, cc[0]) - nt_dot(u_im, cc[1]) for u_re, u_im in drive[x]])
        lamt_ref[:, 2 * x * sw:(2 * x + 1) * sw] = pw[chunk][0]
        lamt_ref[:, (2 * x + 1) * sw:(2 * x + 2) * sw] = pw[chunk][1]

    for s in range(chunk):
        rows = slice(s * LANES, (s + 1) * LANES)
        for t in range(chunk):
            if t > s:
                blk = taps[0][t - s]
            elif t < s:
                blk = taps[1][s - t]
            else:
                blk = taps[0][0] + taps[1][0]
            m_ref[rows, t * LANES:(t + 1) * LANES] = blk.astype(m_ref.dtype)
        f_re, f_im = drive[0][chunk - 1 - s]
        r_re, r_im = drive[1][s]
        for q, part in enumerate((f_re, f_im, r_re, r_im)):
            win_ref[rows, q * sw:(q + 1) * sw] = part.astype(win_ref.dtype)
    for t in range(chunk):
        cols = slice(t * LANES, (t + 1) * LANES)
        f_re, f_im = read[0][t + 1]
        r_re, r_im = read[1][chunk - t]
        for q, part in enumerate((f_re, -f_im, r_re, -r_im)):
            wout_ref[q * sw:(q + 1) * sw, cols] = part.T.astype(wout_ref.dtype)


def _ssm_tables(a_re, a_im, log_dt, b_re, b_im, c_re, c_im, chunk):
    n_layers, _, g_total, p = a_re.shape
    gpt = LANES // SSM_GROUP_CH
    n_slabs = g_total // gpt
    n = n_layers * n_slabs
    sw = gpt * p

    def slab_lanes(x):
        return jnp.transpose(x.reshape(n_layers, 2, n_slabs, sw), (0, 2, 1, 3)).reshape(n, 2, sw)

    def slab_rows(x):
        x = x.reshape(n_layers, 2, n_slabs, LANES, p)
        return jnp.transpose(x, (0, 2, 1, 3, 4)).reshape(n, 2, LANES, p)

    lam = jnp.stack([slab_lanes(a_re), slab_lanes(a_im),
                     slab_lanes(jnp.broadcast_to(log_dt[..., None], a_re.shape))], axis=2)
    lam = jnp.pad(lam, ((0, 0), (0, 0), (0, 5), (0, 0)))
    bt = jnp.stack([slab_rows(jnp.swapaxes(b_re, -1, -2)), slab_rows(jnp.swapaxes(b_im, -1, -2))], axis=2)
    ct = jnp.stack([slab_rows(c_re), slab_rows(c_im)], axis=2)
    rows = chunk * LANES
    per_slab = lambda *shape: pl.BlockSpec((None,) + shape, lambda i: (i,) + (0,) * len(shape))
    m, win, wout, lam_t = pl.pallas_call(
        functools.partial(_ssm_tables_kernel, chunk=chunk),
        grid=(n,),
        in_specs=[per_slab(2, 8, sw), per_slab(2, 2, LANES, p), per_slab(2, 2, LANES, p)],
        out_specs=[per_slab(rows, rows), per_slab(rows, 4 * sw), per_slab(4 * sw, rows), per_slab(1, 4 * sw)],
        out_shape=[jax.ShapeDtypeStruct((n, rows, rows), _MXU_DTYPE),
                   jax.ShapeDtypeStruct((n, rows, 4 * sw), _MXU_DTYPE),
                   jax.ShapeDtypeStruct((n, 4 * sw, rows), _MXU_DTYPE),
                   jax.ShapeDtypeStruct((n, 1, 4 * sw), F32)],
        compiler_params=_params("parallel"),
        name="ssm_tables",
    )(lam, bt, ct)
    return win, m, wout, lam_t


def _router_kernel(h_ref, mod_ref, g_ref, wr_ref, br_ref, xt_ref, cmb_ref, *, ctx_len):
    rows = h_ref.shape[0]
    xt = _norm_mod(h_ref[...], g_ref[...], _mod_row(mod_ref, 3, rows, ctx_len),
                   _mod_row(mod_ref, 4, rows, ctx_len))
    xt_ref[...] = xt.astype(xt_ref.dtype)
    x_hi, x_lo = _split_terms(xt, 2)
    w_hi, w_lo = _split_terms(wr_ref[...], 2)
    dot = lambda a, b: jnp.dot(a, b, preferred_element_type=F32)
    logits = dot(x_hi, w_hi) + (dot(x_hi, w_lo) + dot(x_lo, w_hi)) + br_ref[...]
    lane = lax.broadcasted_iota(jnp.int32, (1, LANES), 1)
    lane_f = lane.astype(F32)
    neg = -jnp.inf
    big = 1e9
    gmask = (lane >= MOE_EXPERTS) & (lane < MOE_EXPERTS + MOE_GROUPS)
    gl = jnp.where(gmask, logits, neg)
    gmax = jnp.max(gl, axis=-1, keepdims=True)
    gidx = jnp.min(jnp.where(gl == gmax, lane_f, big), axis=-1, keepdims=True) - MOE_EXPERTS
    p_group = 1.0 / jnp.sum(jnp.where(gmask, jnp.exp(gl - gmax), 0.0), axis=-1, keepdims=True)
    in_group = (lane < MOE_EXPERTS) & ((lane // MOE_EPG).astype(F32) == gidx)
    el = jnp.where(in_group, logits, neg)
    v1 = jnp.max(el, axis=-1, keepdims=True)
    i1 = jnp.min(jnp.where(el == v1, lane_f, big), axis=-1, keepdims=True)
    el2 = jnp.where(lane_f == i1, neg, el)
    v2 = jnp.max(el2, axis=-1, keepdims=True)
    i2 = jnp.min(jnp.where(el2 == v2, lane_f, big), axis=-1, keepdims=True)
    t = jnp.exp(v2 - v1)
    w1 = p_group / (1.0 + t)
    w2 = p_group * t / (1.0 + t)
    cmb_ref[...] = jnp.where(lane_f == i1, w1, 0.0) + jnp.where(lane_f == i2, w2, 0.0)


def _router(h, mod_tab, g, wr, br, ctx_len):
    b, nt, d = h.shape
    tm = _wide_tile(nt)
    full = lambda shape: pl.BlockSpec(shape, lambda bb, i: (0,) * len(shape))
    return pl.pallas_call(
        functools.partial(_router_kernel, ctx_len=ctx_len),
        grid=(b, nt // tm),
        in_specs=[pl.BlockSpec((None, tm, d), lambda bb, i: (bb, i, 0)), _mod_pair_spec(d),
                  full((1, d)), full(wr.shape), full(br.shape)],
        out_specs=[pl.BlockSpec((None, tm, d), lambda bb, i: (bb, i, 0)),
                   pl.BlockSpec((None, tm, LANES), lambda bb, i: (bb, i, 0))],
        out_shape=[jax.ShapeDtypeStruct((b, nt, d), _MXU_DTYPE),
                   jax.ShapeDtypeStruct((b, nt, LANES), F32)],
        compiler_params=_params("parallel", "parallel"),
        name="moe_router",
    )(h, mod_tab, g, wr, br)


def _snake(block, step, n_steps):
    return jnp.where(block % 2 == 0, step, n_steps - 1 - step)


def _experts_kernel(xt_ref, cmb_ref, wg_ref, wu_ref, wd_ref, h_ref, mod_ref, o_ref,
                    xs_ref, cs_ref, acc_ref, pos_ref, seg_ref, *, ctx_len, nt, windows):
    i = pl.program_id(0)
    e = pl.program_id(1)
    sb = xt_ref.shape[0]
    d = xt_ref.shape[1]

    window = windows[-1]

    @pl.when(e == 0)
    def _():
        cmb = cmb_ref[...]
        lane = lax.broadcasted_iota(jnp.int32, (1, LANES), 1)
        routed = cmb != 0.0
        goh = jnp.zeros((sb, LANES), F32)
        for g in range(MOE_GROUPS):
            in_g = routed & (lane >= g * MOE_EPG) & (lane < (g + 1) * MOE_EPG)
            hit = jnp.max(jnp.where(in_g, 1.0, 0.0), axis=-1, keepdims=True)
            goh = goh + jnp.where(lane == g, hit, 0.0)
        tri = (lax.broadcasted_iota(jnp.int32, (LANES, LANES), 1)
               < lax.broadcasted_iota(jnp.int32, (LANES, LANES), 0)).astype(_MXU_DTYPE)
        cnt = jnp.zeros((1, LANES), F32)
        parts = []
        for k in range(sb // LANES):
            tile = goh[k * LANES:(k + 1) * LANES]
            parts.append(jnp.dot(tri, tile.astype(_MXU_DTYPE), preferred_element_type=F32) + cnt)
            cnt = cnt + jnp.sum(tile, axis=0, keepdims=True)
        before = jnp.concatenate(parts, axis=0)
        r_i = lax.broadcasted_iota(jnp.int32, (sb, sb), 0)
        off = jnp.zeros((1, LANES), F32)
        run = jnp.zeros((1, 1), F32)
        for g in range(MOE_GROUPS):
            off = off + jnp.where(lane == g, run, 0.0)
            run = run + jnp.sum(jnp.where(lane == g, cnt, 0.0), axis=-1, keepdims=True)
        pos = jnp.sum(goh * (off + before), axis=-1, keepdims=True)
        pos_b = jnp.broadcast_to(pos, (sb, LANES))
        pos_ref[...] = pos_b
        pos_row = pos_b.T[0:1, :].astype(jnp.int32)
        perm = (r_i == pos_row).astype(_MXU_DTYPE)
        xs_ref[0:sb, :] = jnp.dot(perm, xt_ref[...], preferred_element_type=F32).astype(xs_ref.dtype)
        cs = jnp.zeros((sb, LANES), F32)
        for term in _split_terms(cmb, 3):
            cs = cs + jnp.dot(perm, term, preferred_element_type=F32)
        cs_ref[0:sb, :] = cs
        xs_ref[sb:sb + window, :] = jnp.zeros((window, d), xs_ref.dtype)
        cs_ref[sb:sb + window, :] = jnp.zeros((window, LANES), F32)
        acc_ref[...] = jnp.zeros_like(acc_ref)
        off_i = off.astype(jnp.int32)
        cnt_i = cnt.astype(jnp.int32)
        for g in range(MOE_GROUPS):
            seg_ref[g] = off_i[0, g]
            seg_ref[MOE_GROUPS + g] = cnt_i[0, g]

    eps = wg_ref.shape[0]
    es = _snake(i, e, pl.num_programs(1))
    g = (es * eps) // MOE_EPG
    start = seg_ref[g]
    count = seg_ref[MOE_GROUPS + g]
    first = (start // ROW_ALIGN) * ROW_ALIGN
    span = start - first + count
    lane = lax.broadcasted_iota(jnp.int32, (1, LANES), 1)

    def apply_experts(r0, rows):
        r0 = pl.multiple_of(r0, ROW_ALIGN)
        x = xs_ref[pl.ds(r0, rows), :]
        cw = cs_ref[pl.ds(r0, rows), :]
        y = jnp.zeros((rows, d), F32)
        for j in range(eps):
            gate = jnp.dot(x, wg_ref[j], preferred_element_type=F32)
            up = jnp.dot(x, wu_ref[j], preferred_element_type=F32)
            w = jnp.sum(jnp.where(lane == es * eps + j, cw, 0.0), axis=-1, keepdims=True)
            hid = (gate / (1.0 + jnp.exp(-gate))) * up * w
            y = y + jnp.dot(hid.astype(_MXU_DTYPE), wd_ref[j], preferred_element_type=F32)
        acc_ref[pl.ds(r0, rows), :] += y

    smaller = 0
    for rows in windows:
        @pl.when((count > 0) & (span > smaller) & (span <= rows))
        def _(rows=rows):
            apply_experts(first, rows)
        smaller = rows

    @pl.when(span > window)
    def _():
        def window_step(k, carry):
            apply_experts(first + k * window, window)
            return carry
        lax.fori_loop(0, (span + window - 1) // window, window_step, 0)

    @pl.when(e == pl.num_programs(1) - 1)
    def _():
        c_i = lax.broadcasted_iota(jnp.int32, (sb, sb), 1)
        unperm = (c_i == pos_ref[:, 0:1].astype(jnp.int32)).astype(_MXU_DTYPE)
        y = jnp.zeros((sb, d), F32)
        for term in _split_terms(acc_ref[0:sb, :], 2):
            y = y + jnp.dot(unperm, term, preferred_element_type=F32)
        row = (i * sb) % nt + lax.broadcasted_iota(jnp.int32, (sb, 1), 0)
        gate_row = jnp.where(row < ctx_len, mod_ref[0, 5:6, :], mod_ref[1, 5:6, :])
        o_ref[...] = h_ref[...] + gate_row * y


def _experts(xt, cmb, wg, wu, wd, h, mod_tab, ctx_len):
    b, nt, d = h.shape
    sb = 1152 if nt % 1152 == 0 else ROW_TILE
    windows = (128, 192, 256, 336, 448, 576) if sb == 1152 else (48, 96)
    window = windows[-1]
    per_b = nt // sb
    n_exp, _, hid = wg.shape
    rows = b * nt
    blk = lambda w: pl.BlockSpec((sb, w), lambda i, e: (i, 0))
    n_steps = n_exp // EXPERTS_PER_STEP
    weights = lambda i, e: (_snake(i, e, n_steps), 0, 0)
    out = pl.pallas_call(
        functools.partial(_experts_kernel, ctx_len=ctx_len, nt=nt, windows=windows),
        grid=(rows // sb, n_exp // EXPERTS_PER_STEP),
        in_specs=[blk(d), blk(LANES),
                  pl.BlockSpec((EXPERTS_PER_STEP, d, hid), weights),
                  pl.BlockSpec((EXPERTS_PER_STEP, d, hid), weights),
                  pl.BlockSpec((EXPERTS_PER_STEP, hid, d), weights),
                  blk(d),
                  pl.BlockSpec((None, 2, 8, d), lambda i, e: (i // per_b, 0, 0, 0))],
        out_specs=blk(d),
        out_shape=jax.ShapeDtypeStruct((rows, d), F32),
        scratch_shapes=[pltpu.VMEM((sb + window, d), _MXU_DTYPE), pltpu.VMEM((sb + window, LANES), F32),
                        pltpu.VMEM((sb + window, d), F32), pltpu.VMEM((sb, LANES), F32),
                        pltpu.SMEM((2 * MOE_GROUPS,), jnp.int32)],
        compiler_params=_params("parallel", "arbitrary"),
        name="moe_experts",
    )(xt.reshape(rows, d), cmb.reshape(rows, LANES), wg, wu, wd, h.reshape(rows, d), mod_tab)
    return out.reshape(b, nt, d)


def _final_kernel(h_ref, g_ref, o_ref):
    h = h_ref[...]
    o_ref[...] = h * lax.rsqrt(jnp.mean(h * h, axis=-1, keepdims=True) + RMS_EPS) * g_ref[...]


def _final_norm(h, g, ctx_len):
    b, nt, d = h.shape
    tm = ROW_TILE
    skip = ctx_len // tm
    return pl.pallas_call(
        _final_kernel,
        grid=(b, (nt - ctx_len) // tm),
        in_specs=[pl.BlockSpec((None, tm, d), lambda bb, i: (bb, i + skip, 0)),
                  pl.BlockSpec((1, d), lambda bb, i: (0, 0))],
        out_specs=pl.BlockSpec((None, tm, d), lambda bb, i: (bb, i, 0)),
        out_shape=jax.ShapeDtypeStruct((b, nt - ctx_len, d), F32),
        compiler_params=_params("parallel", "parallel"),
        name="final_norm",
    )(h, g)


def _rope_tables(seq_len, ctx_len):
    n_rows = seq_len // GRID_W
    rows = jnp.repeat(jnp.arange(n_rows, dtype=F32), GRID_W)
    cols = jnp.tile(jnp.arange(GRID_W, dtype=F32), n_rows)
    half = HEAD_DIM // 2
    inv = 1.0 / (ROPE_BASE ** (jnp.arange(0, half, 2, dtype=F32) / half))
    ang_r = rows[:, None] * inv
    ang_c = cols[:, None] * inv
    ang = jnp.concatenate([ang_r, ang_r, ang_c, ang_c], axis=-1)
    ang = jnp.concatenate([jnp.zeros((ctx_len, HEAD_DIM), F32), ang], axis=0)
    ang = jnp.tile(ang, (1, LANES // HEAD_DIM))
    return jnp.cos(ang), jnp.sin(ang)


def _pad_row(v, width=LANES):
    return jnp.pad(v, (0, width - v.shape[0]))[None, :]


def kernel(x, c, ctx, c_ctx, mod_w, mod_b, norm1_g, norm2_g, final_g, attn_w_in, attn_w_out, attn_q_norm_g, attn_k_norm_g, diff_lambda_q1, diff_lambda_k1, diff_lambda_q2, diff_lambda_k2, diff_subln_g, ssm_a_re, ssm_a_im, ssm_log_dt, ssm_b_re, ssm_b_im, ssm_c_re, ssm_c_im, ssm_d, ssm_glu_w_a, ssm_glu_w_b, moe_group_w, moe_group_b, moe_router_w, moe_router_b, moe_w_gate, moe_w_up, moe_w_down):
    bsz, seq, d = x.shape
    ctx_len = ctx.shape[1]
    depth = mod_w.shape[0]
    assert ctx_len == ROW_TILE and seq % ROW_TILE == 0 and seq % GRID_W == 0

    h = jnp.concatenate([ctx, x], axis=1)

    mod_rows = 16
    c_all = jnp.concatenate([c, c_ctx[None, :], jnp.zeros((mod_rows - bsz - 1, d), F32)], axis=0)
    mods = _modulation(c_all, mod_w, mod_b).reshape(depth, mod_rows, 6, d)
    mods = jnp.pad(mods, ((0, 0), (0, 0), (0, 2), (0, 0)))
    mod_tabs = jnp.stack([jnp.broadcast_to(mods[:, bsz:bsz + 1], (depth, bsz, 8, d)), mods[:, :bsz]], axis=2)

    cos, sin = _rope_tables(seq, ctx_len)
    cast = lambda w: w.astype(_MXU_DTYPE)
    ssm_tabs = _ssm_tables(ssm_a_re, ssm_a_im, ssm_log_dt, ssm_b_re, ssm_b_im, ssm_c_re, ssm_c_im, SSM_CHUNK)

    for layer in range(depth):
        mod_tab = mod_tabs[layer]
        i = layer // 2
        if layer % 2 == 0:
            lambda_init = 0.8 - 0.6 * math.exp(-0.3 * layer)
            qkv = _attn_proj(h, mod_tab, norm1_g[layer][None, :], cast(attn_w_in[i]),
                             jnp.tile(attn_q_norm_g[i], 2)[None, :], jnp.tile(attn_k_norm_g[i], 2)[None, :],
                             cos, sin, ctx_len)
            lam_rows = jnp.concatenate([_pad_row(diff_lambda_q1[i]), _pad_row(diff_lambda_k1[i]),
                                        _pad_row(diff_lambda_q2[i]), _pad_row(diff_lambda_k2[i]),
                                        jnp.zeros((4, LANES), F32)], axis=0)
            h = _attention(qkv, lam_rows, diff_subln_g[i][None, :], cast(attn_w_out[i]), h, mod_tab,
                           lambda_init, ctx_len)
        else:
            u = _norm1(h, mod_tab, norm1_g[layer][None, :], ctx_len)
            y = _ssm_scan(u, *ssm_tabs, ctx_len, i * (d // LANES))
            h = _ssm_out(y, u, ssm_d[i][None, :], cast(ssm_glu_w_a[i]), cast(ssm_glu_w_b[i]), h, mod_tab,
                         ctx_len)

        wr = jnp.concatenate([jnp.transpose(moe_router_w[layer], (1, 0, 2)).reshape(d, MOE_EXPERTS),
                              moe_group_w[layer],
                              jnp.zeros((d, LANES - MOE_EXPERTS - MOE_GROUPS), F32)], axis=1)
        br = _pad_row(jnp.concatenate([moe_router_b[layer].reshape(-1), moe_group_b[layer]]))
        xt, cmb = _router(h, mod_tab, norm2_g[layer][None, :], wr, br, ctx_len)
        h = _experts(xt, cmb, cast(moe_w_gate[layer]), cast(moe_w_up[layer]), cast(moe_w_down[layer]),
                     h, mod_tab, ctx_len)

    return _final_norm(h, final_g[None, :], ctx_len)
```

```python
import functools
import math

import jax
import jax.numpy as jnp
from jax import lax
from jax.experimental import pallas as pl
from jax.experimental.pallas import tpu as pltpu

F32 = jnp.float32
_MXU_DTYPE = jnp.bfloat16
_HIGHEST = lax.Precision.HIGHEST

LANES = 128
HEAD_DIM = 64
GRID_W = 64
ROPE_BASE = 10000.0
GQA_Q_HEADS = 8
GQA_GROUP = 4
GQA_STACK = 2
DIFF_HEADS = 4
GQA_Q_W = 512
GQA_KV_W = 128
DIFF_QK_W = 512
DIFF_V_W = 512
SSM_GROUP_CH = 16
SSM_STATE = 64
MOE_GROUPS = 4
MOE_EPG = 8
MOE_EXPERTS = 32
RMS_EPS = 1e-6
SSM_CHUNK = 8
SSM_BATCH_STACK = 2
SCAN_UNROLL = 4
ROW_TILE = 256
WIDE_TILE = 768
ROW_ALIGN = 16
EXPERTS_PER_STEP = 4
MOE_BLOCK = 1152
MOE_WINDOWS = (128, 192, 256, 336, 448, 576)
MOD_COL_TILE = 1536
ROPE_SECTION = HEAD_DIM // 2
VMEM_LIMIT = 60 * 1024 * 1024


def _params(*sem):
    return pltpu.CompilerParams(dimension_semantics=sem, vmem_limit_bytes=VMEM_LIMIT)


def _norm_mod(h, g, shift, scale):
    y = h * lax.rsqrt(jnp.mean(h * h, axis=-1, keepdims=True) + RMS_EPS) * g
    return y * (1.0 + scale) + shift


def _mm(a, b):
    return jnp.dot(a.astype(_MXU_DTYPE), b.astype(_MXU_DTYPE), preferred_element_type=F32)


def _mod_kernel(c_ref, w_ref, b_ref, o_ref):
    c = c_ref[...]
    a = c / (1.0 + jnp.exp(-c))
    o_ref[...] = jnp.dot(a, w_ref[...], preferred_element_type=F32, precision=_HIGHEST) + b_ref[...]


def _modulation(c_all, mod_w, mod_b):
    depth, d, n = mod_w.shape
    rows = c_all.shape[0]
    tn = MOD_COL_TILE
    return pl.pallas_call(
        _mod_kernel,
        grid=(depth, n // tn),
        in_specs=[pl.BlockSpec((rows, d), lambda l, j: (0, 0)),
                  pl.BlockSpec((None, d, tn), lambda l, j: (l, 0, j)),
                  pl.BlockSpec((None, 1, tn), lambda l, j: (l, 0, j))],
        out_specs=pl.BlockSpec((None, rows, tn), lambda l, j: (l, 0, j)),
        out_shape=jax.ShapeDtypeStruct((depth, rows, n), F32),
        compiler_params=_params("parallel", "parallel"),
        name="modulation",
    )(c_all, mod_w, mod_b.reshape(depth, 1, n))


def _mod_spec(d):
    return pl.BlockSpec((None, None, 8, d), lambda b, i: (b, jnp.minimum(i, 1), 0, 0))


def _mod_pair_spec(d):
    return pl.BlockSpec((None, 2, 8, d), lambda b, i: (b, 0, 0, 0))


def _wide_tile(nt):
    return WIDE_TILE if nt % WIDE_TILE == 0 else ROW_TILE


def _mod_row(mod_ref, idx, rows, ctx_len):
    row = pl.program_id(1) * rows + lax.broadcasted_iota(jnp.int32, (rows, 1), 0)
    return jnp.where(row < ctx_len, mod_ref[0, idx:idx + 1, :], mod_ref[1, idx:idx + 1, :])


def _attn_proj_kernel(h_ref, mod_ref, g_ref, w_ref, gq_ref, gk_ref, cos_ref, sin_ref,
                      qa_ref, ka_ref, va_ref, qb_ref, kb_ref, vb_ref, *, ctx_len):
    rows = h_ref.shape[0]
    xn = _norm_mod(h_ref[...], g_ref[...], _mod_row(mod_ref, 0, rows, ctx_len),
                   _mod_row(mod_ref, 1, rows, ctx_len))
    hp = _mm(xn, w_ref[...])
    cos = cos_ref[...]
    sin = sin_ref[...]
    lane = lax.broadcasted_iota(jnp.int32, (1, LANES), 1)
    first_half = (lane % ROPE_SECTION) < ROPE_SECTION // 2
    r = lax.broadcasted_iota(jnp.int32, (LANES, LANES), 0) // HEAD_DIM
    c = lax.broadcasted_iota(jnp.int32, (LANES, LANES), 1) // HEAD_DIM
    same_head = (r == c).astype(_MXU_DTYPE)

    def rope(x):
        half = ROPE_SECTION // 2
        rot = jnp.where(first_half, -pltpu.roll(x, LANES - half, 1), pltpu.roll(x, half, 1))
        return x * cos + rot * sin

    def head_norm(x, g):
        ss = jnp.dot((x * x).astype(_MXU_DTYPE), same_head, preferred_element_type=F32)
        return x * lax.rsqrt(ss * (1.0 / HEAD_DIM) + RMS_EPS) * g

    scale = HEAD_DIM ** -0.5 * math.log2(math.e)
    o = 0
    for s in range(GQA_Q_W // LANES):
        x = hp[:, o + s * LANES:o + (s + 1) * LANES]
        qa_ref[:, s * LANES:(s + 1) * LANES] = (rope(head_norm(x, gq_ref[...])) * scale).astype(qa_ref.dtype)
    o += GQA_Q_W
    ka_ref[...] = rope(head_norm(hp[:, o:o + LANES], gk_ref[...])).astype(ka_ref.dtype)
    o += GQA_KV_W
    v_pair = hp[:, o:o + LANES]
    low_half = lane < HEAD_DIM
    va_ref[:, 0:LANES] = jnp.where(low_half, v_pair, 1.0).astype(va_ref.dtype)
    va_ref[:, LANES:2 * LANES] = jnp.where(low_half, pltpu.roll(v_pair, HEAD_DIM, 1), 1.0).astype(va_ref.dtype)
    o += GQA_KV_W
    for s in range(DIFF_QK_W // LANES):
        x = hp[:, o + s * LANES:o + (s + 1) * LANES]
        qb_ref[:, s * LANES:(s + 1) * LANES] = (rope(x) * scale).astype(qb_ref.dtype)
    o += DIFF_QK_W
    for s in range(DIFF_QK_W // LANES):
        x = hp[:, o + s * LANES:o + (s + 1) * LANES]
        kb_ref[:, s * LANES:(s + 1) * LANES] = rope(x).astype(kb_ref.dtype)
    o += DIFF_QK_W
    for s in range(DIFF_HEADS):
        vb_ref[:, 2 * s * LANES:(2 * s + 1) * LANES] = hp[:, o + s * LANES:o + (s + 1) * LANES].astype(vb_ref.dtype)
        vb_ref[:, (2 * s + 1) * LANES:(2 * s + 2) * LANES] = jnp.ones((hp.shape[0], LANES), vb_ref.dtype)


def _attn_proj(h, mod_tab, g, w_in, gq, gk, cos, sin, ctx_len):
    b, nt, d = h.shape
    tm = _wide_tile(nt)
    widths = (GQA_Q_W, GQA_KV_W, 2 * GQA_KV_W, DIFF_QK_W, DIFF_QK_W, 2 * DIFF_V_W)
    full = lambda shape: pl.BlockSpec(shape, lambda bb, i: (0,) * len(shape))
    return pl.pallas_call(
        functools.partial(_attn_proj_kernel, ctx_len=ctx_len),
        grid=(b, nt // tm),
        in_specs=[pl.BlockSpec((None, tm, d), lambda bb, i: (bb, i, 0)),
                  _mod_pair_spec(d),
                  full((1, d)),
                  full(w_in.shape),
                  full((1, LANES)),
                  full((1, LANES)),
                  pl.BlockSpec((tm, LANES), lambda bb, i: (i, 0)),
                  pl.BlockSpec((tm, LANES), lambda bb, i: (i, 0))],
        out_specs=[pl.BlockSpec((None, tm, w), lambda bb, i: (bb, i, 0)) for w in widths],
        out_shape=[jax.ShapeDtypeStruct((b, nt, w), _MXU_DTYPE) for w in widths],
        compiler_params=_params("parallel", "parallel"),
        name="attn_proj",
    )(h, mod_tab, g, w_in, gq, gk, cos, sin)


def _attn_kernel(qa_ref, qb_ref, ka_ref, va_ref, kb_ref, vb_ref, lam_ref, sg_ref, wo_ref, h_ref, mod_ref,
                 o_ref, mrg_ref, *, lambda_init, ctx_len):
    lv = lam_ref[...]
    lam = (jnp.exp(jnp.sum(lv[0:1] * lv[1:2], axis=-1, keepdims=True))
           - jnp.exp(jnp.sum(lv[2:3] * lv[3:4], axis=-1, keepdims=True)) + lambda_init)

    def run(nk):
        tq = qa_ref.shape[0]
        qk = lambda q, k: lax.dot_general(q, k, (((1,), (1,)), ((), ())), preferred_element_type=F32)
        probs = lambda s: jnp.exp2((s - jnp.max(s, axis=-1, keepdims=True)).astype(_MXU_DTYPE))

        def normalised(o, dv):
            return o[:, 0:dv] / o[:, dv:dv + 1]

        def gqa_scores(h0):
            g = h0 // GQA_GROUP
            q = jnp.concatenate([qa_ref[:, h * HEAD_DIM:(h + 1) * HEAD_DIM]
                                 for h in range(h0, h0 + GQA_STACK)], axis=0)
            return [qk(q, ka_ref[0:nk, g * HEAD_DIM:(g + 1) * HEAD_DIM])]

        def gqa_attend(h0, p):
            g = h0 // GQA_GROUP
            o = jnp.dot(p[0], va_ref[0:nk, g * LANES:(g + 1) * LANES], preferred_element_type=F32)
            for j in range(GQA_STACK):
                h = h0 + j
                mrg_ref[:, h * HEAD_DIM:(h + 1) * HEAD_DIM] = normalised(
                    o[j * tq:(j + 1) * tq], HEAD_DIM).astype(mrg_ref.dtype)

        def diff_scores(h):
            c0 = h * 2 * HEAD_DIM
            return [qk(qb_ref[:, c:c + HEAD_DIM], kb_ref[0:nk, c:c + HEAD_DIM]) for c in (c0, c0 + HEAD_DIM)]

        def diff_attend(h, p):
            c0 = h * 2 * HEAD_DIM
            o = jnp.dot(jnp.concatenate(p, axis=0), vb_ref[0:nk, 2 * c0:2 * c0 + 2 * LANES],
                        preferred_element_type=F32)
            o = normalised(o[0:tq], 2 * HEAD_DIM) - lam * normalised(o[tq:2 * tq], 2 * HEAD_DIM)
            o = o * lax.rsqrt(jnp.mean(o * o, axis=-1, keepdims=True) + RMS_EPS) * sg_ref[...]
            o = o * (1.0 - lambda_init)
            mrg_ref[:, GQA_Q_W + c0:GQA_Q_W + c0 + 2 * HEAD_DIM] = o.astype(mrg_ref.dtype)

        gqa_units = [(functools.partial(gqa_scores, h0), functools.partial(gqa_attend, h0))
                     for h0 in range(0, GQA_Q_HEADS, GQA_STACK)]
        diff_units = [(functools.partial(diff_scores, h), functools.partial(diff_attend, h))
                      for h in range(DIFF_HEADS)]
        units = [u for pair in zip(gqa_units, diff_units) for u in pair]
        n = len(units)
        s, p = [None] * n, [None] * n
        lag_p, lag_o = 1, 2
        for step in range(n + lag_o):
            if step < n:
                s[step] = units[step][0]()
            if lag_p <= step < n + lag_p:
                p[step - lag_p] = [probs(x) for x in s[step - lag_p]]
                s[step - lag_p] = None
            if lag_o <= step:
                units[step - lag_o][1](p[step - lag_o])
                p[step - lag_o] = None

    i = pl.program_id(1)

    @pl.when(i == 0)
    def _():
        run(ctx_len)

    @pl.when(i > 0)
    def _():
        run(ka_ref.shape[0])

    y = jnp.dot(mrg_ref[...], wo_ref[...], preferred_element_type=F32)
    o_ref[...] = h_ref[...] + mod_ref[2:3, :] * y


def _attention(qkv, lam_rows, subln_g, w_out, h, mod_tab, lambda_init, ctx_len):
    qa, ka, va, qb, kb, vb = qkv
    b, nt, d = h.shape
    tq = ROW_TILE
    assert ctx_len == tq
    blk = lambda w: pl.BlockSpec((None, tq, w), lambda bb, i: (bb, i, 0))
    per_batch = lambda w: pl.BlockSpec((None, nt, w), lambda bb, i: (bb, 0, 0))
    full = lambda shape: pl.BlockSpec(shape, lambda bb, i: (0,) * len(shape))
    return pl.pallas_call(
        functools.partial(_attn_kernel, lambda_init=lambda_init, ctx_len=ctx_len),
        grid=(b, nt // tq),
        in_specs=[blk(GQA_Q_W), blk(DIFF_QK_W), per_batch(GQA_KV_W), per_batch(2 * GQA_KV_W),
                  per_batch(DIFF_QK_W), per_batch(2 * DIFF_V_W),
                  full((8, LANES)), full((1, LANES)), full(w_out.shape), blk(d), _mod_spec(d)],
        out_specs=blk(d),
        out_shape=jax.ShapeDtypeStruct((b, nt, d), F32),
        scratch_shapes=[pltpu.VMEM((tq, GQA_Q_W + DIFF_V_W), _MXU_DTYPE)],
        compiler_params=_params("parallel", "parallel"),
        name="attention",
    )(qa, qb, ka, va, kb, vb, lam_rows, subln_g, w_out, h, mod_tab)


def _norm1_kernel(h_ref, mod_ref, g_ref, o_ref, *, ctx_len):
    rows = h_ref.shape[0]
    o_ref[...] = _norm_mod(h_ref[...], g_ref[...], _mod_row(mod_ref, 0, rows, ctx_len),
                           _mod_row(mod_ref, 1, rows, ctx_len))


def _norm1(h, mod_tab, g, ctx_len):
    b, nt, d = h.shape
    tm = _wide_tile(nt)
    blk = pl.BlockSpec((None, tm, d), lambda bb, i: (bb, i, 0))
    return pl.pallas_call(
        functools.partial(_norm1_kernel, ctx_len=ctx_len),
        grid=(b, nt // tm),
        in_specs=[blk, _mod_pair_spec(d), pl.BlockSpec((1, d), lambda bb, i: (0, 0))],
        out_specs=blk,
        out_shape=jax.ShapeDtypeStruct((b, nt, d), F32),
        compiler_params=_params("parallel", "parallel"),
        name="ssm_norm",
    )(h, mod_tab, g)


def _ssm_kernel(u_ref, win_ref, m_ref, wout_ref, lam_ref, y_ref, bd_ref, *, chunk, n_ctx_chunks):
    nb, nt, _ = u_ref.shape
    nc = nt // chunk
    n_state_slabs = bd_ref.shape[0]
    q = n_state_slabs // 4

    def chunk_rows(bi):
        parts = [u_ref[bi, pl.ds(s, nc, stride=chunk), :] for s in range(chunk)]
        return jnp.concatenate(parts, axis=1).astype(_MXU_DTYPE)

    stack = SSM_BATCH_STACK if nb % SSM_BATCH_STACK == 0 else 1

    def stacked_rows(b0):
        return jnp.concatenate([chunk_rows(b0 + k) for k in range(stack)], axis=0)

    for b0 in range(0, nb, stack):
        drive = jnp.dot(stacked_rows(b0), win_ref[...], preferred_element_type=F32)
        for k in range(stack):
            for c in range(n_state_slabs):
                bd_ref[c, pl.ds(b0 + k, nc, stride=nb), :] = drive[k * nc:(k + 1) * nc, c * LANES:(c + 1) * LANES]

    lam = lam_ref[...]

    def make_step(base):
        a_re = [lam[:, (base + c) * LANES:(base + c + 1) * LANES] for c in range(q)]
        a_im = [lam[:, (base + q + c) * LANES:(base + q + c + 1) * LANES] for c in range(q)]

        def step(k, carry):
            row = pl.multiple_of(k * nb, nb)
            out = []
            for c in range(q):
                s_re, s_im = carry[2 * c], carry[2 * c + 1]
                d_re = bd_ref[base + c, pl.ds(row, nb), :]
                d_im = bd_ref[base + q + c, pl.ds(row, nb), :]
                bd_ref[base + c, pl.ds(row, nb), :] = s_re
                bd_ref[base + q + c, pl.ds(row, nb), :] = s_im
                out.append(a_re[c] * s_re - a_im[c] * s_im + d_re)
                out.append(a_re[c] * s_im + a_im[c] * s_re + d_im)
            return tuple(out)

        return step

    zero = tuple(jnp.zeros((nb, LANES), F32) for _ in range(2 * q))
    fwd = make_step(0)
    lax.fori_loop(0, nc, fwd, zero, unroll=SCAN_UNROLL)
    rev = make_step(2 * q)
    carry = lax.fori_loop(0, n_ctx_chunks, lambda i, cr: rev(n_ctx_chunks - 1 - i, cr), zero)
    lax.fori_loop(0, nc - n_ctx_chunks, lambda i, cr: rev(nc - 1 - i, cr), carry)

    for b0 in range(0, nb, stack):
        states = jnp.concatenate(
            [jnp.concatenate([bd_ref[c, pl.ds(b0 + k, nc, stride=nb), :] for c in range(n_state_slabs)], axis=1)
             for k in range(stack)], axis=0)
        y = (jnp.dot(stacked_rows(b0), m_ref[...], preferred_element_type=F32)
             + jnp.dot(states.astype(_MXU_DTYPE), wout_ref[...], preferred_element_type=F32))
        for k in range(stack):
            for t in range(chunk):
                y_ref[b0 + k, pl.ds(t, nc, stride=chunk), :] = y[k * nc:(k + 1) * nc, t * LANES:(t + 1) * LANES]


def _ssm_scan(u, win, m, wout, lam_t, ctx_len, first_slab):
    b, nt, d = u.shape
    chunk = SSM_CHUNK
    nb = 4 if b % 4 == 0 else b
    n_slabs = d // LANES
    state_w = win.shape[-1]
    nc = nt // chunk
    blk = pl.BlockSpec((nb, nt, LANES), lambda j, bb: (bb, 0, j))
    table = lambda j, bb: (first_slab + j, 0, 0)
    return pl.pallas_call(
        functools.partial(_ssm_kernel, chunk=chunk, n_ctx_chunks=ctx_len // chunk),
        grid=(n_slabs, b // nb),
        in_specs=[blk,
                  pl.BlockSpec((None,) + win.shape[1:], table),
                  pl.BlockSpec((None,) + m.shape[1:], table),
                  pl.BlockSpec((None,) + wout.shape[1:], table),
                  pl.BlockSpec((None, 1, state_w), table)],
        out_specs=blk,
        out_shape=jax.ShapeDtypeStruct((b, nt, d), F32),
        scratch_shapes=[pltpu.VMEM((state_w // LANES, nc * nb, LANES), F32)],
        compiler_params=_params("parallel", "parallel"),
        name="ssm_scan",
    )(u, win, m, wout, lam_t)


def _ssm_out_kernel(y_ref, u_ref, d_ref, wa_ref, wb_ref, h_ref, mod_ref, o_ref, *, ctx_len):
    x = y_ref[...] + d_ref[...] * u_ref[...]
    z = 0.5 * x * (1.0 + jnp.tanh(math.sqrt(2.0 / math.pi) * (x + 0.044715 * (x * x * x))))
    z = z.astype(_MXU_DTYPE)
    a = jnp.dot(z, wa_ref[...], preferred_element_type=F32)
    g = jnp.dot(z, wb_ref[...], preferred_element_type=F32)
    gate = _mod_row(mod_ref, 2, h_ref.shape[0], ctx_len)
    o_ref[...] = h_ref[...] + gate * (a / (1.0 + jnp.exp(-g)))


def _ssm_out(y, u, d_skip, wa, wb, h, mod_tab, ctx_len):
    b, nt, d = h.shape
    tm = _wide_tile(nt)
    blk = pl.BlockSpec((None, tm, d), lambda bb, i: (bb, i, 0))
    full = lambda shape: pl.BlockSpec(shape, lambda bb, i: (0,) * len(shape))
    return pl.pallas_call(
        functools.partial(_ssm_out_kernel, ctx_len=ctx_len),
        grid=(b, nt // tm),
        in_specs=[blk, blk, full((1, d)), full(wa.shape), full(wb.shape), blk, _mod_pair_spec(d)],
        out_specs=blk,
        out_shape=jax.ShapeDtypeStruct((b, nt, d), F32),
        compiler_params=_params("parallel", "parallel"),
        name="ssm_out",
    )(y, u, d_skip, wa, wb, h, mod_tab)


def _split_terms(x, n):
    terms = []
    for _ in range(n - 1):
        t = x.astype(_MXU_DTYPE)
        terms.append(t)
        x = x - t.astype(F32)
    terms.append(x.astype(_MXU_DTYPE))
    return terms


def _ssm_tables_kernel(lam_ref, bt_ref, c_ref, m_ref, win_ref, wout_ref, lamt_ref, *, chunk):
    gpt = LANES // SSM_GROUP_CH
    p = SSM_STATE
    sw = gpt * p
    ci = lax.broadcasted_iota(jnp.int32, (p, sw), 0)
    oi = lax.broadcasted_iota(jnp.int32, (p, sw), 1)
    spread = (ci == oi % p).astype(_MXU_DTYPE)
    ri = lax.broadcasted_iota(jnp.int32, (LANES, sw), 0)
    oj = lax.broadcasted_iota(jnp.int32, (LANES, sw), 1)
    own_group = (ri // SSM_GROUP_CH) == (oj // p)

    def block_diag(x):
        y = jnp.zeros((LANES, sw), F32)
        for term in _split_terms(x, 3):
            y = y + jnp.dot(term, spread, preferred_element_type=F32)
        return jnp.where(own_group, y, 0.0)

    def cmul(ar, ai, br, bi):
        return ar * br - ai * bi, ar * bi + ai * br

    nt_dot = lambda a, b: lax.dot_general(a, b, (((1,), (1,)), ((), ())), preferred_element_type=F32)

    def tap(u, c_hi, c_lo):
        a_hi, a_lo = _split_terms(jnp.concatenate([u[0], -u[1]], axis=1), 2)
        return nt_dot(a_hi, c_hi) + (nt_dot(a_hi, c_lo) + nt_dot(a_lo, c_hi))

    taps, drive, read = [], [], []
    for x in range(2):
        lre = jnp.minimum(lam_ref[x, 0:1, :], -1e-4)
        lim = lam_ref[x, 1:2, :]
        dt = jnp.exp(lam_ref[x, 2:3, :])
        pw = []
        for j in range(chunk + 1):
            mag = jnp.exp(float(j) * (lre * dt))
            ang = float(j) * (lim * dt)
            pw.append((mag * jnp.cos(ang), mag * jnp.sin(ang)))
        nr = pw[1][0] - 1.0
        ni = pw[1][1]
        den = lre * lre + lim * lim
        coef_re = (nr * lre + ni * lim) / den
        coef_im = (ni * lre - nr * lim) / den
        bb = cmul(coef_re, coef_im, block_diag(bt_ref[x, 0]), block_diag(bt_ref[x, 1]))
        cc = (block_diag(c_ref[x, 0]), block_diag(c_ref[x, 1]))
        drive.append([cmul(pw[j][0], pw[j][1], bb[0], bb[1]) for j in range(chunk)])
        read.append([cmul(pw[j][0], pw[j][1], cc[0], cc[1]) for j in range(chunk + 1)])
        c_hi, c_lo = _split_terms(jnp.concatenate(cc, axis=1), 2)
        taps.append([tap(u, c_hi, c_lo) for u in drive[x]])
        lamt_ref[:, 2 * x * sw:(2 * x + 1) * sw] = pw[chunk][0]
        lamt_ref[:, (2 * x + 1) * sw:(2 * x + 2) * sw] = pw[chunk][1]

    for s in range(chunk):
        rows = slice(s * LANES, (s + 1) * LANES)
        for t in range(chunk):
            if t > s:
                blk = taps[0][t - s]
            elif t < s:
                blk = taps[1][s - t]
            else:
                blk = taps[0][0] + taps[1][0]
            m_ref[rows, t * LANES:(t + 1) * LANES] = blk.astype(m_ref.dtype)
        f_re, f_im = drive[0][chunk - 1 - s]
        r_re, r_im = drive[1][s]
        for q, part in enumerate((f_re, f_im, r_re, r_im)):
            win_ref[rows, q * sw:(q + 1) * sw] = part.astype(win_ref.dtype)
    for t in range(chunk):
        cols = slice(t * LANES, (t + 1) * LANES)
        f_re, f_im = read[0][t + 1]
        r_re, r_im = read[1][chunk - t]
        for q, part in enumerate((f_re, -f_im, r_re, -r_im)):
            wout_ref[q * sw:(q + 1) * sw, cols] = part.T.astype(wout_ref.dtype)


def _ssm_tables(a_re, a_im, log_dt, b_re, b_im, c_re, c_im, chunk):
    n_layers, _, g_total, p = a_re.shape
    gpt = LANES // SSM_GROUP_CH
    n_slabs = g_total // gpt
    n = n_layers * n_slabs
    sw = gpt * p

    def slab_lanes(x):
        return jnp.transpose(x.reshape(n_layers, 2, n_slabs, sw), (0, 2, 1, 3)).reshape(n, 2, sw)

    def slab_rows(x):
        x = x.reshape(n_layers, 2, n_slabs, LANES, p)
        return jnp.transpose(x, (0, 2, 1, 3, 4)).reshape(n, 2, LANES, p)

    lam = jnp.stack([slab_lanes(a_re), slab_lanes(a_im),
                     slab_lanes(jnp.broadcast_to(log_dt[..., None], a_re.shape))], axis=2)
    lam = jnp.pad(lam, ((0, 0), (0, 0), (0, 5), (0, 0)))
    bt = jnp.stack([slab_rows(jnp.swapaxes(b_re, -1, -2)), slab_rows(jnp.swapaxes(b_im, -1, -2))], axis=2)
    ct = jnp.stack([slab_rows(c_re), slab_rows(c_im)], axis=2)
    rows = chunk * LANES
    per_slab = lambda *shape: pl.BlockSpec((None,) + shape, lambda i: (i,) + (0,) * len(shape))
    m, win, wout, lam_t = pl.pallas_call(
        functools.partial(_ssm_tables_kernel, chunk=chunk),
        grid=(n,),
        in_specs=[per_slab(2, 8, sw), per_slab(2, 2, LANES, p), per_slab(2, 2, LANES, p)],
        out_specs=[per_slab(rows, rows), per_slab(rows, 4 * sw), per_slab(4 * sw, rows), per_slab(1, 4 * sw)],
        out_shape=[jax.ShapeDtypeStruct((n, rows, rows), _MXU_DTYPE),
                   jax.ShapeDtypeStruct((n, rows, 4 * sw), _MXU_DTYPE),
                   jax.ShapeDtypeStruct((n, 4 * sw, rows), _MXU_DTYPE),
                   jax.ShapeDtypeStruct((n, 1, 4 * sw), F32)],
        compiler_params=_params("parallel"),
        name="ssm_tables",
    )(lam, bt, ct)
    return win, m, wout, lam_t


def _router_kernel(h_ref, mod_ref, g_ref, wr_ref, br_ref, xt_ref, cmb_ref, *, ctx_len):
    rows = h_ref.shape[0]
    xt = _norm_mod(h_ref[...], g_ref[...], _mod_row(mod_ref, 3, rows, ctx_len),
                   _mod_row(mod_ref, 4, rows, ctx_len))
    xt_ref[...] = xt.astype(xt_ref.dtype)
    x_hi, x_lo = _split_terms(xt, 2)
    w_hi, w_lo = _split_terms(wr_ref[...], 2)
    dot = lambda a, b: jnp.dot(a, b, preferred_element_type=F32)
    logits = dot(x_hi, w_hi) + (dot(x_hi, w_lo) + dot(x_lo, w_hi)) + br_ref[...]
    lane = lax.broadcasted_iota(jnp.int32, (1, LANES), 1)
    lane_f = lane.astype(F32)
    neg = -jnp.inf
    big = 1e9
    gmask = (lane >= MOE_EXPERTS) & (lane < MOE_EXPERTS + MOE_GROUPS)
    gl = jnp.where(gmask, logits, neg)
    gmax = jnp.max(gl, axis=-1, keepdims=True)
    gidx = jnp.min(jnp.where(gl == gmax, lane_f, big), axis=-1, keepdims=True) - MOE_EXPERTS
    p_group = 1.0 / jnp.sum(jnp.where(gmask, jnp.exp(gl - gmax), 0.0), axis=-1, keepdims=True)
    in_group = (lane < MOE_EXPERTS) & ((lane // MOE_EPG).astype(F32) == gidx)
    el = jnp.where(in_group, logits, neg)
    v1 = jnp.max(el, axis=-1, keepdims=True)
    i1 = jnp.min(jnp.where(el == v1, lane_f, big), axis=-1, keepdims=True)
    el2 = jnp.where(lane_f == i1, neg, el)
    v2 = jnp.max(el2, axis=-1, keepdims=True)
    i2 = jnp.min(jnp.where(el2 == v2, lane_f, big), axis=-1, keepdims=True)
    t = jnp.exp(v2 - v1)
    w1 = p_group / (1.0 + t)
    w2 = p_group * t / (1.0 + t)
    cmb_ref[...] = jnp.where(lane_f == i1, w1, 0.0) + jnp.where(lane_f == i2, w2, 0.0)


def _router(h, mod_tab, g, wr, br, ctx_len):
    b, nt, d = h.shape
    tm = _wide_tile(nt)
    full = lambda shape: pl.BlockSpec(shape, lambda bb, i: (0,) * len(shape))
    return pl.pallas_call(
        functools.partial(_router_kernel, ctx_len=ctx_len),
        grid=(b, nt // tm),
        in_specs=[pl.BlockSpec((None, tm, d), lambda bb, i: (bb, i, 0)), _mod_pair_spec(d),
                  full((1, d)), full(wr.shape), full(br.shape)],
        out_specs=[pl.BlockSpec((None, tm, d), lambda bb, i: (bb, i, 0)),
                   pl.BlockSpec((None, tm, LANES), lambda bb, i: (bb, i, 0))],
        out_shape=[jax.ShapeDtypeStruct((b, nt, d), _MXU_DTYPE),
                   jax.ShapeDtypeStruct((b, nt, LANES), F32)],
        compiler_params=_params("parallel", "parallel"),
        name="moe_router",
    )(h, mod_tab, g, wr, br)


def _snake(block, step, n_steps):
    return jnp.where(block % 2 == 0, step, n_steps - 1 - step)


def _experts_kernel(xt_ref, cmb_ref, wg_ref, wu_ref, wd_ref, h_ref, mod_ref, o_ref,
                    xs_ref, cs_ref, acc_ref, pos_ref, seg_ref, *, ctx_len, nt, windows):
    i = pl.program_id(0)
    e = pl.program_id(1)
    sb = xt_ref.shape[0]
    d = xt_ref.shape[1]

    window = windows[-1]

    @pl.when(e == 0)
    def _():
        cmb = cmb_ref[...]
        lane = lax.broadcasted_iota(jnp.int32, (1, LANES), 1)
        routed = cmb != 0.0
        goh = jnp.zeros((sb, LANES), F32)
        for g in range(MOE_GROUPS):
            in_g = routed & (lane >= g * MOE_EPG) & (lane < (g + 1) * MOE_EPG)
            hit = jnp.max(jnp.where(in_g, 1.0, 0.0), axis=-1, keepdims=True)
            goh = goh + jnp.where(lane == g, hit, 0.0)
        tri = (lax.broadcasted_iota(jnp.int32, (LANES, LANES), 1)
               < lax.broadcasted_iota(jnp.int32, (LANES, LANES), 0)).astype(_MXU_DTYPE)
        cnt = jnp.zeros((1, LANES), F32)
        parts = []
        for k in range(sb // LANES):
            tile = goh[k * LANES:(k + 1) * LANES]
            parts.append(jnp.dot(tri, tile.astype(_MXU_DTYPE), preferred_element_type=F32) + cnt)
            cnt = cnt + jnp.sum(tile, axis=0, keepdims=True)
        before = jnp.concatenate(parts, axis=0)
        r_i = lax.broadcasted_iota(jnp.int32, (sb, sb), 0)
        off = jnp.zeros((1, LANES), F32)
        run = jnp.zeros((1, 1), F32)
        for g in range(MOE_GROUPS):
            off = off + jnp.where(lane == g, run, 0.0)
            run = run + jnp.sum(jnp.where(lane == g, cnt, 0.0), axis=-1, keepdims=True)
        pos = jnp.sum(goh * (off + before), axis=-1, keepdims=True)
        pos_b = jnp.broadcast_to(pos, (sb, LANES))
        pos_ref[...] = pos_b
        pos_row = pos_b.T[0:1, :].astype(jnp.int32)
        perm = (r_i == pos_row).astype(_MXU_DTYPE)
        xs_ref[0:sb, :] = jnp.dot(perm, xt_ref[...], preferred_element_type=F32).astype(xs_ref.dtype)
        cs = jnp.zeros((sb, LANES), F32)
        for term in _split_terms(cmb, 2):
            cs = cs + jnp.dot(perm, term, preferred_element_type=F32)
        cs_ref[0:sb, :] = cs
        xs_ref[sb:sb + window, :] = jnp.zeros((window, d), xs_ref.dtype)
        cs_ref[sb:sb + window, :] = jnp.zeros((window, LANES), F32)
        acc_ref[...] = jnp.zeros_like(acc_ref)
        off_i = off.astype(jnp.int32)
        cnt_i = cnt.astype(jnp.int32)
        for g in range(MOE_GROUPS):
            seg_ref[g] = off_i[0, g]
            seg_ref[MOE_GROUPS + g] = cnt_i[0, g]

    eps = wg_ref.shape[0]
    es = _snake(i, e, pl.num_programs(1))
    g = (es * eps) // MOE_EPG
    start = seg_ref[g]
    count = seg_ref[MOE_GROUPS + g]
    first = (start // ROW_ALIGN) * ROW_ALIGN
    span = start - first + count
    lane = lax.broadcasted_iota(jnp.int32, (1, LANES), 1)

    def apply_experts(r0, rows):
        r0 = pl.multiple_of(r0, ROW_ALIGN)
        x = xs_ref[pl.ds(r0, rows), :]
        cw = cs_ref[pl.ds(r0, rows), :]
        y = jnp.zeros((rows, d), F32)
        for j in range(eps):
            gate = jnp.dot(x, wg_ref[j], preferred_element_type=F32)
            up = jnp.dot(x, wu_ref[j], preferred_element_type=F32)
            w = jnp.sum(jnp.where(lane == es * eps + j, cw, 0.0), axis=-1, keepdims=True)
            hid = (gate / (1.0 + jnp.exp(-gate))) * up * w
            y = y + jnp.dot(hid.astype(_MXU_DTYPE), wd_ref[j], preferred_element_type=F32)
        acc_ref[pl.ds(r0, rows), :] += y

    smaller = 0
    for rows in windows:
        @pl.when((count > 0) & (span > smaller) & (span <= rows))
        def _(rows=rows):
            apply_experts(first, rows)
        smaller = rows

    @pl.when(span > window)
    def _():
        def window_step(k, carry):
            apply_experts(first + k * window, window)
            return carry
        lax.fori_loop(0, (span + window - 1) // window, window_step, 0)

    @pl.when(e == pl.num_programs(1) - 1)
    def _():
        c_i = lax.broadcasted_iota(jnp.int32, (sb, sb), 1)
        unperm = (c_i == pos_ref[:, 0:1].astype(jnp.int32)).astype(_MXU_DTYPE)
        y = jnp.dot(unperm, acc_ref[0:sb, :].astype(_MXU_DTYPE), preferred_element_type=F32)
        row = (i * sb) % nt + lax.broadcasted_iota(jnp.int32, (sb, 1), 0)
        gate_row = jnp.where(row < ctx_len, mod_ref[0, 5:6, :], mod_ref[1, 5:6, :])
        o_ref[...] = h_ref[...] + gate_row * y


def _experts(xt, cmb, wg, wu, wd, h, mod_tab, ctx_len):
    b, nt, d = h.shape
    sb = MOE_BLOCK if nt % MOE_BLOCK == 0 else ROW_TILE
    windows = MOE_WINDOWS if sb == MOE_BLOCK else (48, 96)
    window = windows[-1]
    per_b = nt // sb
    n_exp, _, hid = wg.shape
    rows = b * nt
    blk = lambda w: pl.BlockSpec((sb, w), lambda i, e: (i, 0))
    n_steps = n_exp // EXPERTS_PER_STEP
    weights = lambda i, e: (_snake(i, e, n_steps), 0, 0)
    out = pl.pallas_call(
        functools.partial(_experts_kernel, ctx_len=ctx_len, nt=nt, windows=windows),
        grid=(rows // sb, n_exp // EXPERTS_PER_STEP),
        in_specs=[blk(d), blk(LANES),
                  pl.BlockSpec((EXPERTS_PER_STEP, d, hid), weights),
                  pl.BlockSpec((EXPERTS_PER_STEP, d, hid), weights),
                  pl.BlockSpec((EXPERTS_PER_STEP, hid, d), weights),
                  blk(d),
                  pl.BlockSpec((None, 2, 8, d), lambda i, e: (i // per_b, 0, 0, 0))],
        out_specs=blk(d),
        out_shape=jax.ShapeDtypeStruct((rows, d), F32),
        scratch_shapes=[pltpu.VMEM((sb + window, d), _MXU_DTYPE), pltpu.VMEM((sb + window, LANES), F32),
                        pltpu.VMEM((sb + window, d), F32), pltpu.VMEM((sb, LANES), F32),
                        pltpu.SMEM((2 * MOE_GROUPS,), jnp.int32)],
        compiler_params=_params("parallel", "arbitrary"),
        name="moe_experts",
    )(xt.reshape(rows, d), cmb.reshape(rows, LANES), wg, wu, wd, h.reshape(rows, d), mod_tab)
    return out.reshape(b, nt, d)


def _final_kernel(h_ref, g_ref, o_ref):
    h = h_ref[...]
    o_ref[...] = h * lax.rsqrt(jnp.mean(h * h, axis=-1, keepdims=True) + RMS_EPS) * g_ref[...]


def _final_norm(h, g, ctx_len):
    b, nt, d = h.shape
    tm = ROW_TILE
    skip = ctx_len // tm
    return pl.pallas_call(
        _final_kernel,
        grid=(b, (nt - ctx_len) // tm),
        in_specs=[pl.BlockSpec((None, tm, d), lambda bb, i: (bb, i + skip, 0)),
                  pl.BlockSpec((1, d), lambda bb, i: (0, 0))],
        out_specs=pl.BlockSpec((None, tm, d), lambda bb, i: (bb, i, 0)),
        out_shape=jax.ShapeDtypeStruct((b, nt - ctx_len, d), F32),
        compiler_params=_params("parallel", "parallel"),
        name="final_norm",
    )(h, g)


def _rope_tables(seq_len, ctx_len):
    n_rows = seq_len // GRID_W
    rows = jnp.repeat(jnp.arange(n_rows, dtype=F32), GRID_W)
    cols = jnp.tile(jnp.arange(GRID_W, dtype=F32), n_rows)
    half = HEAD_DIM // 2
    inv = 1.0 / (ROPE_BASE ** (jnp.arange(0, half, 2, dtype=F32) / half))
    ang_r = rows[:, None] * inv
    ang_c = cols[:, None] * inv
    ang = jnp.concatenate([ang_r, ang_r, ang_c, ang_c], axis=-1)
    ang = jnp.concatenate([jnp.zeros((ctx_len, HEAD_DIM), F32), ang], axis=0)
    ang = jnp.tile(ang, (1, LANES // HEAD_DIM))
    return jnp.cos(ang), jnp.sin(ang)


def _pad_row(v, width=LANES):
    return jnp.pad(v, (0, width - v.shape[0]))[None, :]


def kernel(x, c, ctx, c_ctx, mod_w, mod_b, norm1_g, norm2_g, final_g, attn_w_in, attn_w_out, attn_q_norm_g, attn_k_norm_g, diff_lambda_q1, diff_lambda_k1, diff_lambda_q2, diff_lambda_k2, diff_subln_g, ssm_a_re, ssm_a_im, ssm_log_dt, ssm_b_re, ssm_b_im, ssm_c_re, ssm_c_im, ssm_d, ssm_glu_w_a, ssm_glu_w_b, moe_group_w, moe_group_b, moe_router_w, moe_router_b, moe_w_gate, moe_w_up, moe_w_down):
    bsz, seq, d = x.shape
    ctx_len = ctx.shape[1]
    depth = mod_w.shape[0]
    assert ctx_len == ROW_TILE and seq % ROW_TILE == 0 and seq % GRID_W == 0

    h = jnp.concatenate([ctx, x], axis=1)

    mod_rows = 16
    c_all = jnp.concatenate([c, c_ctx[None, :], jnp.zeros((mod_rows - bsz - 1, d), F32)], axis=0)
    mods = _modulation(c_all, mod_w, mod_b).reshape(depth, mod_rows, 6, d)
    mods = jnp.pad(mods, ((0, 0), (0, 0), (0, 2), (0, 0)))
    mod_tabs = jnp.stack([jnp.broadcast_to(mods[:, bsz:bsz + 1], (depth, bsz, 8, d)), mods[:, :bsz]], axis=2)

    cos, sin = _rope_tables(seq, ctx_len)
    cast = lambda w: w.astype(_MXU_DTYPE)
    ssm_tabs = _ssm_tables(ssm_a_re, ssm_a_im, ssm_log_dt, ssm_b_re, ssm_b_im, ssm_c_re, ssm_c_im, SSM_CHUNK)

    for layer in range(depth):
        mod_tab = mod_tabs[layer]
        i = layer // 2
        if layer % 2 == 0:
            lambda_init = 0.8 - 0.6 * math.exp(-0.3 * layer)
            qkv = _attn_proj(h, mod_tab, norm1_g[layer][None, :], cast(attn_w_in[i]),
                             jnp.tile(attn_q_norm_g[i], 2)[None, :], jnp.tile(attn_k_norm_g[i], 2)[None, :],
                             cos, sin, ctx_len)
            lam_rows = jnp.concatenate([_pad_row(diff_lambda_q1[i]), _pad_row(diff_lambda_k1[i]),
                                        _pad_row(diff_lambda_q2[i]), _pad_row(diff_lambda_k2[i]),
                                        jnp.zeros((4, LANES), F32)], axis=0)
            h = _attention(qkv, lam_rows, diff_subln_g[i][None, :], cast(attn_w_out[i]), h, mod_tab,
                           lambda_init, ctx_len)
        else:
            u = _norm1(h, mod_tab, norm1_g[layer][None, :], ctx_len)
            y = _ssm_scan(u, *ssm_tabs, ctx_len, i * (d // LANES))
            h = _ssm_out(y, u, ssm_d[i][None, :], cast(ssm_glu_w_a[i]), cast(ssm_glu_w_b[i]), h, mod_tab,
                         ctx_len)

        wr = jnp.concatenate([jnp.transpose(moe_router_w[layer], (1, 0, 2)).reshape(d, MOE_EXPERTS),
                              moe_group_w[layer],
                              jnp.zeros((d, LANES - MOE_EXPERTS - MOE_GROUPS), F32)], axis=1)
        br = _pad_row(jnp.concatenate([moe_router_b[layer].reshape(-1), moe_group_b[layer]]))
        xt, cmb = _router(h, mod_tab, norm2_g[layer][None, :], wr, br, ctx_len)
        h = _experts(xt, cmb, cast(moe_w_gate[layer]), cast(moe_w_up[layer]), cast(moe_w_down[layer]),
                     h, mod_tab, ctx_len)

    return _final_norm(h, final_g[None, :], ctx_len)
```

```python
import functools
import math

import jax
import jax.numpy as jnp
from jax import lax
from jax.experimental import pallas as pl
from jax.experimental.pallas import tpu as pltpu

F32 = jnp.float32
_MXU_DTYPE = jnp.bfloat16

LANES = 128
HEAD_DIM = 64
GRID_W = 64
ROPE_BASE = 10000.0
GQA_Q_HEADS = 8
GQA_GROUP = 4
GQA_STACK = 2
DIFF_HEADS = 4
GQA_Q_W = 512
GQA_KV_W = 128
DIFF_QK_W = 512
DIFF_V_W = 512
SSM_GROUP_CH = 16
SSM_STATE = 64
MOE_GROUPS = 4
MOE_EPG = 8
MOE_EXPERTS = 32
RMS_EPS = 1e-6
SSM_CHUNK = 8
SSM_BATCH_STACK = 2
SCAN_UNROLL = 4
ROW_TILE = 256
WIDE_TILE = 1152
ROW_ALIGN = 16
EXPERTS_PER_STEP = 4
MOE_BLOCK = 1152
MOE_WINDOWS = (128, 192, 256, 336, 448, 576)
MOD_COL_TILE = 1536
ROPE_SECTION = HEAD_DIM // 2
VMEM_LIMIT = 60 * 1024 * 1024


def _params(*sem):
    return pltpu.CompilerParams(dimension_semantics=sem, vmem_limit_bytes=VMEM_LIMIT)


def _norm_mod(h, g, shift, scale):
    y = h * lax.rsqrt(jnp.mean(h * h, axis=-1, keepdims=True) + RMS_EPS) * g
    return y * (1.0 + scale) + shift


def _mm(a, b):
    return jnp.dot(a.astype(_MXU_DTYPE), b.astype(_MXU_DTYPE), preferred_element_type=F32)


def _mod_kernel(c_ref, w_ref, b_ref, o_ref):
    c = c_ref[...]
    a = c / (1.0 + jnp.exp(-c))
    a_hi, a_lo = _split_terms(a, 2)
    w_hi, w_lo = _split_terms(w_ref[...], 2)
    dot = lambda x, y: jnp.dot(x, y, preferred_element_type=F32)
    o_ref[...] = dot(a_hi, w_hi) + (dot(a_hi, w_lo) + dot(a_lo, w_hi)) + b_ref[...]


def _modulation(c_all, mod_w, mod_b):
    depth, d, n = mod_w.shape
    rows = c_all.shape[0]
    tn = MOD_COL_TILE
    return pl.pallas_call(
        _mod_kernel,
        grid=(depth, n // tn),
        in_specs=[pl.BlockSpec((rows, d), lambda l, j: (0, 0)),
                  pl.BlockSpec((None, d, tn), lambda l, j: (l, 0, j)),
                  pl.BlockSpec((None, 1, tn), lambda l, j: (l, 0, j))],
        out_specs=pl.BlockSpec((None, rows, tn), lambda l, j: (l, 0, j)),
        out_shape=jax.ShapeDtypeStruct((depth, rows, n), F32),
        compiler_params=_params("parallel", "parallel"),
        name="modulation",
    )(c_all, mod_w, mod_b.reshape(depth, 1, n))


def _mod_spec(d):
    return pl.BlockSpec((None, None, 8, d), lambda b, i: (b, jnp.minimum(i, 1), 0, 0))


def _mod_pair_spec(d):
    return pl.BlockSpec((None, 2, 8, d), lambda b, i: (b, 0, 0, 0))


def _wide_tile(nt):
    return WIDE_TILE if nt % WIDE_TILE == 0 else ROW_TILE


def _mod_row(mod_ref, idx, rows, ctx_len):
    row = pl.program_id(1) * rows + lax.broadcasted_iota(jnp.int32, (rows, 1), 0)
    return jnp.where(row < ctx_len, mod_ref[0, idx:idx + 1, :], mod_ref[1, idx:idx + 1, :])


def _attn_proj_kernel(h_ref, mod_ref, g_ref, w_ref, gq_ref, gk_ref, cos_ref, sin_ref,
                      qa_ref, ka_ref, va_ref, qb_ref, kb_ref, vb_ref, *, ctx_len):
    rows = h_ref.shape[0]
    xn = _norm_mod(h_ref[...], g_ref[...], _mod_row(mod_ref, 0, rows, ctx_len),
                   _mod_row(mod_ref, 1, rows, ctx_len))
    hp = _mm(xn, w_ref[...])
    cos = cos_ref[...]
    sin = sin_ref[...]
    lane = lax.broadcasted_iota(jnp.int32, (1, LANES), 1)
    first_half = (lane % ROPE_SECTION) < ROPE_SECTION // 2
    r = lax.broadcasted_iota(jnp.int32, (LANES, LANES), 0) // HEAD_DIM
    c = lax.broadcasted_iota(jnp.int32, (LANES, LANES), 1) // HEAD_DIM
    same_head = (r == c).astype(_MXU_DTYPE)

    def rope(x):
        half = ROPE_SECTION // 2
        rot = jnp.where(first_half, -pltpu.roll(x, LANES - half, 1), pltpu.roll(x, half, 1))
        return x * cos + rot * sin

    def head_norm(x, g):
        ss = jnp.dot((x * x).astype(_MXU_DTYPE), same_head, preferred_element_type=F32)
        return x * lax.rsqrt(ss * (1.0 / HEAD_DIM) + RMS_EPS) * g

    scale = HEAD_DIM ** -0.5 * math.log2(math.e)
    o = 0
    for s in range(GQA_Q_W // LANES):
        x = hp[:, o + s * LANES:o + (s + 1) * LANES]
        qa_ref[:, s * LANES:(s + 1) * LANES] = (rope(head_norm(x, gq_ref[...])) * scale).astype(qa_ref.dtype)
    o += GQA_Q_W
    ka_ref[...] = rope(head_norm(hp[:, o:o + LANES], gk_ref[...])).astype(ka_ref.dtype)
    o += GQA_KV_W
    v_pair = hp[:, o:o + LANES]
    low_half = lane < HEAD_DIM
    va_ref[:, 0:LANES] = jnp.where(low_half, v_pair, 1.0).astype(va_ref.dtype)
    va_ref[:, LANES:2 * LANES] = jnp.where(low_half, pltpu.roll(v_pair, HEAD_DIM, 1), 1.0).astype(va_ref.dtype)
    o += GQA_KV_W
    for s in range(DIFF_QK_W // LANES):
        x = hp[:, o + s * LANES:o + (s + 1) * LANES]
        qb_ref[:, s * LANES:(s + 1) * LANES] = (rope(x) * scale).astype(qb_ref.dtype)
    o += DIFF_QK_W
    for s in range(DIFF_QK_W // LANES):
        x = hp[:, o + s * LANES:o + (s + 1) * LANES]
        kb_ref[:, s * LANES:(s + 1) * LANES] = rope(x).astype(kb_ref.dtype)
    o += DIFF_QK_W
    for s in range(DIFF_HEADS):
        vb_ref[:, 2 * s * LANES:(2 * s + 1) * LANES] = hp[:, o + s * LANES:o + (s + 1) * LANES].astype(vb_ref.dtype)
        vb_ref[:, (2 * s + 1) * LANES:(2 * s + 2) * LANES] = jnp.ones((hp.shape[0], LANES), vb_ref.dtype)


def _attn_proj(h, mod_tab, g, w_in, gq, gk, cos, sin, ctx_len):
    b, nt, d = h.shape
    tm = _wide_tile(nt)
    widths = (GQA_Q_W, GQA_KV_W, 2 * GQA_KV_W, DIFF_QK_W, DIFF_QK_W, 2 * DIFF_V_W)
    full = lambda shape: pl.BlockSpec(shape, lambda bb, i: (0,) * len(shape))
    return pl.pallas_call(
        functools.partial(_attn_proj_kernel, ctx_len=ctx_len),
        grid=(b, nt // tm),
        in_specs=[pl.BlockSpec((None, tm, d), lambda bb, i: (bb, i, 0)),
                  _mod_pair_spec(d),
                  full((1, d)),
                  full(w_in.shape),
                  full((1, LANES)),
                  full((1, LANES)),
                  pl.BlockSpec((tm, LANES), lambda bb, i: (i, 0)),
                  pl.BlockSpec((tm, LANES), lambda bb, i: (i, 0))],
        out_specs=[pl.BlockSpec((None, tm, w), lambda bb, i: (bb, i, 0)) for w in widths],
        out_shape=[jax.ShapeDtypeStruct((b, nt, w), _MXU_DTYPE) for w in widths],
        compiler_params=_params("parallel", "parallel"),
        name="attn_proj",
    )(h, mod_tab, g, w_in, gq, gk, cos, sin)


def _attn_kernel(qa_ref, qb_ref, ka_ref, va_ref, kb_ref, vb_ref, lam_ref, sg_ref, wo_ref, h_ref, mod_ref,
                 o_ref, mrg_ref, *, lambda_init, ctx_len):
    lv = lam_ref[...]
    lam = (jnp.exp(jnp.sum(lv[0:1] * lv[1:2], axis=-1, keepdims=True))
           - jnp.exp(jnp.sum(lv[2:3] * lv[3:4], axis=-1, keepdims=True)) + lambda_init)

    def run(nk):
        tq = qa_ref.shape[0]
        qk = lambda q, k: lax.dot_general(q, k, (((1,), (1,)), ((), ())), preferred_element_type=F32)
        probs = lambda s: jnp.exp2((s - jnp.max(s, axis=-1, keepdims=True)).astype(_MXU_DTYPE))

        def normalised(o, dv):
            return o[:, 0:dv] / o[:, dv:dv + 1]

        def gqa_scores(h0):
            g = h0 // GQA_GROUP
            q = jnp.concatenate([qa_ref[:, h * HEAD_DIM:(h + 1) * HEAD_DIM]
                                 for h in range(h0, h0 + GQA_STACK)], axis=0)
            return [qk(q, ka_ref[0:nk, g * HEAD_DIM:(g + 1) * HEAD_DIM])]

        def gqa_attend(h0, p):
            g = h0 // GQA_GROUP
            o = jnp.dot(p[0], va_ref[0:nk, g * LANES:(g + 1) * LANES], preferred_element_type=F32)
            for j in range(GQA_STACK):
                h = h0 + j
                mrg_ref[:, h * HEAD_DIM:(h + 1) * HEAD_DIM] = normalised(
                    o[j * tq:(j + 1) * tq], HEAD_DIM).astype(mrg_ref.dtype)

        def diff_scores(h):
            c0 = h * 2 * HEAD_DIM
            return [qk(qb_ref[:, c:c + HEAD_DIM], kb_ref[0:nk, c:c + HEAD_DIM]) for c in (c0, c0 + HEAD_DIM)]

        def diff_attend(h, p):
            c0 = h * 2 * HEAD_DIM
            o = jnp.dot(jnp.concatenate(p, axis=0), vb_ref[0:nk, 2 * c0:2 * c0 + 2 * LANES],
                        preferred_element_type=F32)
            o = normalised(o[0:tq], 2 * HEAD_DIM) - lam * normalised(o[tq:2 * tq], 2 * HEAD_DIM)
            o = o * lax.rsqrt(jnp.mean(o * o, axis=-1, keepdims=True) + RMS_EPS) * sg_ref[...]
            o = o * (1.0 - lambda_init)
            mrg_ref[:, GQA_Q_W + c0:GQA_Q_W + c0 + 2 * HEAD_DIM] = o.astype(mrg_ref.dtype)

        gqa_units = [(functools.partial(gqa_scores, h0), functools.partial(gqa_attend, h0))
                     for h0 in range(0, GQA_Q_HEADS, GQA_STACK)]
        diff_units = [(functools.partial(diff_scores, h), functools.partial(diff_attend, h))
                      for h in range(DIFF_HEADS)]
        units = [u for pair in zip(gqa_units, diff_units) for u in pair]
        n = len(units)
        s, p = [None] * n, [None] * n
        lag_p, lag_o = 1, 2
        for step in range(n + lag_o):
            if step < n:
                s[step] = units[step][0]()
            if lag_p <= step < n + lag_p:
                p[step - lag_p] = [probs(x) for x in s[step - lag_p]]
                s[step - lag_p] = None
            if lag_o <= step:
                units[step - lag_o][1](p[step - lag_o])
                p[step - lag_o] = None

    i = pl.program_id(1)

    @pl.when(i == 0)
    def _():
        run(ctx_len)

    @pl.when(i > 0)
    def _():
        run(ka_ref.shape[0])

    y = jnp.dot(mrg_ref[...], wo_ref[...], preferred_element_type=F32)
    o_ref[...] = h_ref[...] + mod_ref[2:3, :] * y


def _attention(qkv, lam_rows, subln_g, w_out, h, mod_tab, lambda_init, ctx_len):
    qa, ka, va, qb, kb, vb = qkv
    b, nt, d = h.shape
    tq = ROW_TILE
    assert ctx_len == tq
    blk = lambda w: pl.BlockSpec((None, tq, w), lambda bb, i: (bb, i, 0))
    per_batch = lambda w: pl.BlockSpec((None, nt, w), lambda bb, i: (bb, 0, 0))
    full = lambda shape: pl.BlockSpec(shape, lambda bb, i: (0,) * len(shape))
    return pl.pallas_call(
        functools.partial(_attn_kernel, lambda_init=lambda_init, ctx_len=ctx_len),
        grid=(b, nt // tq),
        in_specs=[blk(GQA_Q_W), blk(DIFF_QK_W), per_batch(GQA_KV_W), per_batch(2 * GQA_KV_W),
                  per_batch(DIFF_QK_W), per_batch(2 * DIFF_V_W),
                  full((8, LANES)), full((1, LANES)), full(w_out.shape), blk(d), _mod_spec(d)],
        out_specs=blk(d),
        out_shape=jax.ShapeDtypeStruct((b, nt, d), F32),
        scratch_shapes=[pltpu.VMEM((tq, GQA_Q_W + DIFF_V_W), _MXU_DTYPE)],
        compiler_params=_params("parallel", "parallel"),
        name="attention",
    )(qa, qb, ka, va, kb, vb, lam_rows, subln_g, w_out, h, mod_tab)


def _norm1_kernel(h_ref, mod_ref, g_ref, o_ref, *, ctx_len):
    rows = h_ref.shape[0]
    o_ref[...] = _norm_mod(h_ref[...], g_ref[...], _mod_row(mod_ref, 0, rows, ctx_len),
                           _mod_row(mod_ref, 1, rows, ctx_len))


def _norm1(h, mod_tab, g, ctx_len):
    b, nt, d = h.shape
    tm = _wide_tile(nt)
    blk = pl.BlockSpec((None, tm, d), lambda bb, i: (bb, i, 0))
    return pl.pallas_call(
        functools.partial(_norm1_kernel, ctx_len=ctx_len),
        grid=(b, nt // tm),
        in_specs=[blk, _mod_pair_spec(d), pl.BlockSpec((1, d), lambda bb, i: (0, 0))],
        out_specs=blk,
        out_shape=jax.ShapeDtypeStruct((b, nt, d), F32),
        compiler_params=_params("parallel", "parallel"),
        name="ssm_norm",
    )(h, mod_tab, g)


def _ssm_kernel(u_ref, win_ref, m_ref, wout_ref, lam_ref, y_ref, bd_ref, *, chunk, n_ctx_chunks):
    nb, nt, _ = u_ref.shape
    nc = nt // chunk
    n_state_slabs = bd_ref.shape[0]
    q = n_state_slabs // 4

    def chunk_rows(bi):
        parts = [u_ref[bi, pl.ds(s, nc, stride=chunk), :] for s in range(chunk)]
        return jnp.concatenate(parts, axis=1).astype(_MXU_DTYPE)

    stack = SSM_BATCH_STACK if nb % SSM_BATCH_STACK == 0 else 1

    def stacked_rows(b0):
        return jnp.concatenate([chunk_rows(b0 + k) for k in range(stack)], axis=0)

    for b0 in range(0, nb, stack):
        drive = jnp.dot(stacked_rows(b0), win_ref[...], preferred_element_type=F32)
        for k in range(stack):
            for c in range(n_state_slabs):
                bd_ref[c, pl.ds(b0 + k, nc, stride=nb), :] = drive[k * nc:(k + 1) * nc, c * LANES:(c + 1) * LANES]

    lam = lam_ref[...]

    def make_step(base):
        a_re = [lam[:, (base + c) * LANES:(base + c + 1) * LANES] for c in range(q)]
        a_im = [lam[:, (base + q + c) * LANES:(base + q + c + 1) * LANES] for c in range(q)]

        def step(k, carry):
            row = pl.multiple_of(k * nb, nb)
            out = []
            for c in range(q):
                s_re, s_im = carry[2 * c], carry[2 * c + 1]
                d_re = bd_ref[base + c, pl.ds(row, nb), :]
                d_im = bd_ref[base + q + c, pl.ds(row, nb), :]
                bd_ref[base + c, pl.ds(row, nb), :] = s_re
                bd_ref[base + q + c, pl.ds(row, nb), :] = s_im
                out.append(a_re[c] * s_re - a_im[c] * s_im + d_re)
                out.append(a_re[c] * s_im + a_im[c] * s_re + d_im)
            return tuple(out)

        return step

    zero = tuple(jnp.zeros((nb, LANES), F32) for _ in range(2 * q))
    fwd = make_step(0)
    lax.fori_loop(0, nc, fwd, zero, unroll=SCAN_UNROLL)
    rev = make_step(2 * q)
    carry = lax.fori_loop(0, n_ctx_chunks, lambda i, cr: rev(n_ctx_chunks - 1 - i, cr), zero)
    lax.fori_loop(0, nc - n_ctx_chunks, lambda i, cr: rev(nc - 1 - i, cr), carry)

    for b0 in range(0, nb, stack):
        states = jnp.concatenate(
            [jnp.concatenate([bd_ref[c, pl.ds(b0 + k, nc, stride=nb), :] for c in range(n_state_slabs)], axis=1)
             for k in range(stack)], axis=0)
        y = (jnp.dot(stacked_rows(b0), m_ref[...], preferred_element_type=F32)
             + jnp.dot(states.astype(_MXU_DTYPE), wout_ref[...], preferred_element_type=F32))
        for k in range(stack):
            for t in range(chunk):
                y_ref[b0 + k, pl.ds(t, nc, stride=chunk), :] = y[k * nc:(k + 1) * nc, t * LANES:(t + 1) * LANES]


def _ssm_scan(u, win, m, wout, lam_t, ctx_len, first_slab):
    b, nt, d = u.shape
    chunk = SSM_CHUNK
    nb = 4 if b % 4 == 0 else b
    n_slabs = d // LANES
    state_w = win.shape[-1]
    nc = nt // chunk
    blk = pl.BlockSpec((nb, nt, LANES), lambda j, bb: (bb, 0, j))
    table = lambda j, bb: (first_slab + j, 0, 0)
    return pl.pallas_call(
        functools.partial(_ssm_kernel, chunk=chunk, n_ctx_chunks=ctx_len // chunk),
        grid=(n_slabs, b // nb),
        in_specs=[blk,
                  pl.BlockSpec((None,) + win.shape[1:], table),
                  pl.BlockSpec((None,) + m.shape[1:], table),
                  pl.BlockSpec((None,) + wout.shape[1:], table),
                  pl.BlockSpec((None, 1, state_w), table)],
        out_specs=blk,
        out_shape=jax.ShapeDtypeStruct((b, nt, d), F32),
        scratch_shapes=[pltpu.VMEM((state_w // LANES, nc * nb, LANES), F32)],
        compiler_params=_params("parallel", "parallel"),
        name="ssm_scan",
    )(u, win, m, wout, lam_t)


def _ssm_out_kernel(y_ref, u_ref, d_ref, wa_ref, wb_ref, h_ref, mod_ref, o_ref, *, ctx_len):
    x = y_ref[...] + d_ref[...] * u_ref[...]
    z = 0.5 * x * (1.0 + jnp.tanh(math.sqrt(2.0 / math.pi) * (x + 0.044715 * (x * x * x))))
    z = z.astype(_MXU_DTYPE)
    a = jnp.dot(z, wa_ref[...], preferred_element_type=F32)
    g = jnp.dot(z, wb_ref[...], preferred_element_type=F32)
    gate = _mod_row(mod_ref, 2, h_ref.shape[0], ctx_len)
    o_ref[...] = h_ref[...] + gate * (a / (1.0 + jnp.exp(-g)))


def _ssm_out(y, u, d_skip, wa, wb, h, mod_tab, ctx_len):
    b, nt, d = h.shape
    tm = _wide_tile(nt)
    blk = pl.BlockSpec((None, tm, d), lambda bb, i: (bb, i, 0))
    full = lambda shape: pl.BlockSpec(shape, lambda bb, i: (0,) * len(shape))
    return pl.pallas_call(
        functools.partial(_ssm_out_kernel, ctx_len=ctx_len),
        grid=(b, nt // tm),
        in_specs=[blk, blk, full((1, d)), full(wa.shape), full(wb.shape), blk, _mod_pair_spec(d)],
        out_specs=blk,
        out_shape=jax.ShapeDtypeStruct((b, nt, d), F32),
        compiler_params=_params("parallel", "parallel"),
        name="ssm_out",
    )(y, u, d_skip, wa, wb, h, mod_tab)


def _split_terms(x, n):
    terms = []
    for _ in range(n - 1):
        t = x.astype(_MXU_DTYPE)
        terms.append(t)
        x = x - t.astype(F32)
    terms.append(x.astype(_MXU_DTYPE))
    return terms


def _ssm_tables_kernel(lam_ref, bt_ref, c_ref, m_ref, win_ref, wout_ref, lamt_ref, *, chunk):
    gpt = LANES // SSM_GROUP_CH
    p = SSM_STATE
    sw = gpt * p
    ci = lax.broadcasted_iota(jnp.int32, (p, sw), 0)
    oi = lax.broadcasted_iota(jnp.int32, (p, sw), 1)
    spread = (ci == oi % p).astype(_MXU_DTYPE)
    ri = lax.broadcasted_iota(jnp.int32, (LANES, sw), 0)
    oj = lax.broadcasted_iota(jnp.int32, (LANES, sw), 1)
    own_group = (ri // SSM_GROUP_CH) == (oj // p)

    def block_diag(x):
        y = jnp.zeros((LANES, sw), F32)
        for term in _split_terms(x, 3):
            y = y + jnp.dot(term, spread, preferred_element_type=F32)
        return jnp.where(own_group, y, 0.0)

    def cmul(ar, ai, br, bi):
        return ar * br - ai * bi, ar * bi + ai * br

    nt_dot = lambda a, b: lax.dot_general(a, b, (((1,), (1,)), ((), ())), preferred_element_type=F32)

    def tap(u, c_hi, c_lo):
        a_hi, a_lo = _split_terms(jnp.concatenate([u[0], -u[1]], axis=1), 2)
        return nt_dot(a_hi, c_hi) + (nt_dot(a_hi, c_lo) + nt_dot(a_lo, c_hi))

    taps, drive, read = [], [], []
    for x in range(2):
        lre = jnp.minimum(lam_ref[x, 0:1, :], -1e-4)
        lim = lam_ref[x, 1:2, :]
        dt = jnp.exp(lam_ref[x, 2:3, :])
        pw = []
        for j in range(chunk + 1):
            mag = jnp.exp(float(j) * (lre * dt))
            ang = float(j) * (lim * dt)
            pw.append((mag * jnp.cos(ang), mag * jnp.sin(ang)))
        nr = pw[1][0] - 1.0
        ni = pw[1][1]
        den = lre * lre + lim * lim
        coef_re = (nr * lre + ni * lim) / den
        coef_im = (ni * lre - nr * lim) / den
        bb = cmul(coef_re, coef_im, block_diag(bt_ref[x, 0]), block_diag(bt_ref[x, 1]))
        cc = (block_diag(c_ref[x, 0]), block_diag(c_ref[x, 1]))
        drive.append([cmul(pw[j][0], pw[j][1], bb[0], bb[1]) for j in range(chunk)])
        read.append([cmul(pw[j][0], pw[j][1], cc[0], cc[1]) for j in range(chunk + 1)])
        c_hi, c_lo = _split_terms(jnp.concatenate(cc, axis=1), 2)
        taps.append([tap(u, c_hi, c_lo) for u in drive[x]])
        lamt_ref[:, 2 * x * sw:(2 * x + 1) * sw] = pw[chunk][0]
        lamt_ref[:, (2 * x + 1) * sw:(2 * x + 2) * sw] = pw[chunk][1]

    for s in range(chunk):
        rows = slice(s * LANES, (s + 1) * LANES)
        for t in range(chunk):
            if t > s:
                blk = taps[0][t - s]
            elif t < s:
                blk = taps[1][s - t]
            else:
                blk = taps[0][0] + taps[1][0]
            m_ref[rows, t * LANES:(t + 1) * LANES] = blk.astype(m_ref.dtype)
        f_re, f_im = drive[0][chunk - 1 - s]
        r_re, r_im = drive[1][s]
        for q, part in enumerate((f_re, f_im, r_re, r_im)):
            win_ref[rows, q * sw:(q + 1) * sw] = part.astype(win_ref.dtype)
    for t in range(chunk):
        cols = slice(t * LANES, (t + 1) * LANES)
        f_re, f_im = read[0][t + 1]
        r_re, r_im = read[1][chunk - t]
        for q, part in enumerate((f_re, -f_im, r_re, -r_im)):
            wout_ref[q * sw:(q + 1) * sw, cols] = part.T.astype(wout_ref.dtype)


def _ssm_tables(a_re, a_im, log_dt, b_re, b_im, c_re, c_im, chunk):
    n_layers, _, g_total, p = a_re.shape
    gpt = LANES // SSM_GROUP_CH
    n_slabs = g_total // gpt
    n = n_layers * n_slabs
    sw = gpt * p

    def slab_lanes(x):
        return jnp.transpose(x.reshape(n_layers, 2, n_slabs, sw), (0, 2, 1, 3)).reshape(n, 2, sw)

    def slab_rows(x):
        x = x.reshape(n_layers, 2, n_slabs, LANES, p)
        return jnp.transpose(x, (0, 2, 1, 3, 4)).reshape(n, 2, LANES, p)

    lam = jnp.stack([slab_lanes(a_re), slab_lanes(a_im),
                     slab_lanes(jnp.broadcast_to(log_dt[..., None], a_re.shape))], axis=2)
    lam = jnp.pad(lam, ((0, 0), (0, 0), (0, 5), (0, 0)))
    bt = jnp.stack([slab_rows(jnp.swapaxes(b_re, -1, -2)), slab_rows(jnp.swapaxes(b_im, -1, -2))], axis=2)
    ct = jnp.stack([slab_rows(c_re), slab_rows(c_im)], axis=2)
    rows = chunk * LANES
    per_slab = lambda *shape: pl.BlockSpec((None,) + shape, lambda i: (i,) + (0,) * len(shape))
    m, win, wout, lam_t = pl.pallas_call(
        functools.partial(_ssm_tables_kernel, chunk=chunk),
        grid=(n,),
        in_specs=[per_slab(2, 8, sw), per_slab(2, 2, LANES, p), per_slab(2, 2, LANES, p)],
        out_specs=[per_slab(rows, rows), per_slab(rows, 4 * sw), per_slab(4 * sw, rows), per_slab(1, 4 * sw)],
        out_shape=[jax.ShapeDtypeStruct((n, rows, rows), _MXU_DTYPE),
                   jax.ShapeDtypeStruct((n, rows, 4 * sw), _MXU_DTYPE),
                   jax.ShapeDtypeStruct((n, 4 * sw, rows), _MXU_DTYPE),
                   jax.ShapeDtypeStruct((n, 1, 4 * sw), F32)],
        compiler_params=_params("parallel"),
        name="ssm_tables",
    )(lam, bt, ct)
    return win, m, wout, lam_t


def _router_kernel(h_ref, mod_ref, g_ref, wr_ref, br_ref, xt_ref, cmb_ref, *, ctx_len):
    rows = h_ref.shape[0]
    xt = _norm_mod(h_ref[...], g_ref[...], _mod_row(mod_ref, 3, rows, ctx_len),
                   _mod_row(mod_ref, 4, rows, ctx_len))
    xt_ref[...] = xt.astype(xt_ref.dtype)
    x_hi, x_lo = _split_terms(xt, 2)
    w_hi, w_lo = _split_terms(wr_ref[...], 2)
    dot = lambda a, b: jnp.dot(a, b, preferred_element_type=F32)
    logits = dot(x_hi, w_hi) + (dot(x_hi, w_lo) + dot(x_lo, w_hi)) + br_ref[...]
    lane = lax.broadcasted_iota(jnp.int32, (1, LANES), 1)
    lane_f = lane.astype(F32)
    neg = -jnp.inf
    big = 1e9
    gmask = (lane >= MOE_EXPERTS) & (lane < MOE_EXPERTS + MOE_GROUPS)
    gl = jnp.where(gmask, logits, neg)
    gmax = jnp.max(gl, axis=-1, keepdims=True)
    gidx = jnp.min(jnp.where(gl == gmax, lane_f, big), axis=-1, keepdims=True) - MOE_EXPERTS
    p_group = 1.0 / jnp.sum(jnp.where(gmask, jnp.exp(gl - gmax), 0.0), axis=-1, keepdims=True)
    in_group = (lane < MOE_EXPERTS) & ((lane // MOE_EPG).astype(F32) == gidx)
    el = jnp.where(in_group, logits, neg)
    v1 = jnp.max(el, axis=-1, keepdims=True)
    i1 = jnp.min(jnp.where(el == v1, lane_f, big), axis=-1, keepdims=True)
    el2 = jnp.where(lane_f == i1, neg, el)
    v2 = jnp.max(el2, axis=-1, keepdims=True)
    i2 = jnp.min(jnp.where(el2 == v2, lane_f, big), axis=-1, keepdims=True)
    t = jnp.exp(v2 - v1)
    w1 = p_group / (1.0 + t)
    w2 = p_group * t / (1.0 + t)
    cmb_ref[...] = jnp.where(lane_f == i1, w1, 0.0) + jnp.where(lane_f == i2, w2, 0.0)


def _router(h, mod_tab, g, wr, br, ctx_len):
    b, nt, d = h.shape
    tm = _wide_tile(nt)
    full = lambda shape: pl.BlockSpec(shape, lambda bb, i: (0,) * len(shape))
    return pl.pallas_call(
        functools.partial(_router_kernel, ctx_len=ctx_len),
        grid=(b, nt // tm),
        in_specs=[pl.BlockSpec((None, tm, d), lambda bb, i: (bb, i, 0)), _mod_pair_spec(d),
                  full((1, d)), full(wr.shape), full(br.shape)],
        out_specs=[pl.BlockSpec((None, tm, d), lambda bb, i: (bb, i, 0)),
                   pl.BlockSpec((None, tm, LANES), lambda bb, i: (bb, i, 0))],
        out_shape=[jax.ShapeDtypeStruct((b, nt, d), _MXU_DTYPE),
                   jax.ShapeDtypeStruct((b, nt, LANES), F32)],
        compiler_params=_params("parallel", "parallel"),
        name="moe_router",
    )(h, mod_tab, g, wr, br)


def _snake(block, step, n_steps):
    return jnp.where(block % 2 == 0, step, n_steps - 1 - step)


def _experts_kernel(xt_ref, cmb_ref, wg_ref, wu_ref, wd_ref, h_ref, mod_ref, o_ref,
                    xs_ref, cs_ref, acc_ref, pos_ref, seg_ref, *, ctx_len, nt, windows):
    i = pl.program_id(0)
    e = pl.program_id(1)
    sb = xt_ref.shape[0]
    d = xt_ref.shape[1]

    window = windows[-1]

    @pl.when(e == 0)
    def _():
        cmb = cmb_ref[...]
        lane = lax.broadcasted_iota(jnp.int32, (1, LANES), 1)
        routed = cmb != 0.0
        goh = jnp.zeros((sb, LANES), F32)
        for g in range(MOE_GROUPS):
            in_g = routed & (lane >= g * MOE_EPG) & (lane < (g + 1) * MOE_EPG)
            hit = jnp.max(jnp.where(in_g, 1.0, 0.0), axis=-1, keepdims=True)
            goh = goh + jnp.where(lane == g, hit, 0.0)
        tri = (lax.broadcasted_iota(jnp.int32, (LANES, LANES), 1)
               < lax.broadcasted_iota(jnp.int32, (LANES, LANES), 0)).astype(_MXU_DTYPE)
        cnt = jnp.zeros((1, LANES), F32)
        parts = []
        for k in range(sb // LANES):
            tile = goh[k * LANES:(k + 1) * LANES]
            parts.append(jnp.dot(tri, tile.astype(_MXU_DTYPE), preferred_element_type=F32) + cnt)
            cnt = cnt + jnp.sum(tile, axis=0, keepdims=True)
        before = jnp.concatenate(parts, axis=0)
        r_i = lax.broadcasted_iota(jnp.int32, (sb, sb), 0)
        off = jnp.zeros((1, LANES), F32)
        run = jnp.zeros((1, 1), F32)
        for g in range(MOE_GROUPS):
            off = off + jnp.where(lane == g, run, 0.0)
            run = run + jnp.sum(jnp.where(lane == g, cnt, 0.0), axis=-1, keepdims=True)
        pos = jnp.sum(goh * (off + before), axis=-1, keepdims=True)
        pos_b = jnp.broadcast_to(pos, (sb, LANES))
        pos_ref[...] = pos_b
        pos_row = pos_b.T[0:1, :].astype(jnp.int32)
        perm = (r_i == pos_row).astype(_MXU_DTYPE)
        xs_ref[0:sb, :] = jnp.dot(perm, xt_ref[...], preferred_element_type=F32).astype(xs_ref.dtype)
        cs = jnp.zeros((sb, LANES), F32)
        for term in _split_terms(cmb, 2):
            cs = cs + jnp.dot(perm, term, preferred_element_type=F32)
        cs_ref[0:sb, :] = cs
        xs_ref[sb:sb + window, :] = jnp.zeros((window, d), xs_ref.dtype)
        cs_ref[sb:sb + window, :] = jnp.zeros((window, LANES), F32)
        acc_ref[...] = jnp.zeros_like(acc_ref)
        off_i = off.astype(jnp.int32)
        cnt_i = cnt.astype(jnp.int32)
        for g in range(MOE_GROUPS):
            seg_ref[g] = off_i[0, g]
            seg_ref[MOE_GROUPS + g] = cnt_i[0, g]

    eps = wg_ref.shape[0]
    es = _snake(i, e, pl.num_programs(1))
    g = (es * eps) // MOE_EPG
    start = seg_ref[g]
    count = seg_ref[MOE_GROUPS + g]
    first = (start // ROW_ALIGN) * ROW_ALIGN
    span = start - first + count
    lane = lax.broadcasted_iota(jnp.int32, (1, LANES), 1)

    def apply_experts(r0, rows):
        r0 = pl.multiple_of(r0, ROW_ALIGN)
        x = xs_ref[pl.ds(r0, rows), :]
        cw = cs_ref[pl.ds(r0, rows), :]
        y = jnp.zeros((rows, d), F32)
        for j in range(eps):
            gate = jnp.dot(x, wg_ref[j], preferred_element_type=F32)
            up = jnp.dot(x, wu_ref[j], preferred_element_type=F32)
            w = jnp.sum(jnp.where(lane == es * eps + j, cw, 0.0), axis=-1, keepdims=True)
            hid = (gate / (1.0 + jnp.exp(-gate))) * up * w
            y = y + jnp.dot(hid.astype(_MXU_DTYPE), wd_ref[j], preferred_element_type=F32)
        acc_ref[pl.ds(r0, rows), :] += y

    smaller = 0
    for rows in windows:
        @pl.when((count > 0) & (span > smaller) & (span <= rows))
        def _(rows=rows):
            apply_experts(first, rows)
        smaller = rows

    @pl.when(span > window)
    def _():
        def window_step(k, carry):
            apply_experts(first + k * window, window)
            return carry
        lax.fori_loop(0, (span + window - 1) // window, window_step, 0)

    @pl.when(e == pl.num_programs(1) - 1)
    def _():
        c_i = lax.broadcasted_iota(jnp.int32, (sb, sb), 1)
        unperm = (c_i == pos_ref[:, 0:1].astype(jnp.int32)).astype(_MXU_DTYPE)
        y = jnp.dot(unperm, acc_ref[0:sb, :].astype(_MXU_DTYPE), preferred_element_type=F32)
        row = (i * sb) % nt + lax.broadcasted_iota(jnp.int32, (sb, 1), 0)
        gate_row = jnp.where(row < ctx_len, mod_ref[0, 5:6, :], mod_ref[1, 5:6, :])
        o_ref[...] = h_ref[...] + gate_row * y


def _experts(xt, cmb, wg, wu, wd, h, mod_tab, ctx_len):
    b, nt, d = h.shape
    sb = MOE_BLOCK if nt % MOE_BLOCK == 0 else ROW_TILE
    windows = MOE_WINDOWS if sb == MOE_BLOCK else (48, 96)
    window = windows[-1]
    per_b = nt // sb
    n_exp, _, hid = wg.shape
    rows = b * nt
    blk = lambda w: pl.BlockSpec((sb, w), lambda i, e: (i, 0))
    n_steps = n_exp // EXPERTS_PER_STEP
    weights = lambda i, e: (_snake(i, e, n_steps), 0, 0)
    out = pl.pallas_call(
        functools.partial(_experts_kernel, ctx_len=ctx_len, nt=nt, windows=windows),
        grid=(rows // sb, n_exp // EXPERTS_PER_STEP),
        in_specs=[blk(d), blk(LANES),
                  pl.BlockSpec((EXPERTS_PER_STEP, d, hid), weights),
                  pl.BlockSpec((EXPERTS_PER_STEP, d, hid), weights),
                  pl.BlockSpec((EXPERTS_PER_STEP, hid, d), weights),
                  blk(d),
                  pl.BlockSpec((None, 2, 8, d), lambda i, e: (i // per_b, 0, 0, 0))],
        out_specs=blk(d),
        out_shape=jax.ShapeDtypeStruct((rows, d), F32),
        scratch_shapes=[pltpu.VMEM((sb + window, d), _MXU_DTYPE), pltpu.VMEM((sb + window, LANES), F32),
                        pltpu.VMEM((sb + window, d), F32), pltpu.VMEM((sb, LANES), F32),
                        pltpu.SMEM((2 * MOE_GROUPS,), jnp.int32)],
        compiler_params=_params("parallel", "arbitrary"),
        name="moe_experts",
    )(xt.reshape(rows, d), cmb.reshape(rows, LANES), wg, wu, wd, h.reshape(rows, d), mod_tab)
    return out.reshape(b, nt, d)


def _final_kernel(h_ref, g_ref, o_ref):
    h = h_ref[...]
    o_ref[...] = h * lax.rsqrt(jnp.mean(h * h, axis=-1, keepdims=True) + RMS_EPS) * g_ref[...]


def _final_norm(h, g, ctx_len):
    b, nt, d = h.shape
    tm = ROW_TILE
    skip = ctx_len // tm
    return pl.pallas_call(
        _final_kernel,
        grid=(b, (nt - ctx_len) // tm),
        in_specs=[pl.BlockSpec((None, tm, d), lambda bb, i: (bb, i + skip, 0)),
                  pl.BlockSpec((1, d), lambda bb, i: (0, 0))],
        out_specs=pl.BlockSpec((None, tm, d), lambda bb, i: (bb, i, 0)),
        out_shape=jax.ShapeDtypeStruct((b, nt - ctx_len, d), F32),
        compiler_params=_params("parallel", "parallel"),
        name="final_norm",
    )(h, g)


def _rope_tables(seq_len, ctx_len):
    n_rows = seq_len // GRID_W
    rows = jnp.repeat(jnp.arange(n_rows, dtype=F32), GRID_W)
    cols = jnp.tile(jnp.arange(GRID_W, dtype=F32), n_rows)
    half = HEAD_DIM // 2
    inv = 1.0 / (ROPE_BASE ** (jnp.arange(0, half, 2, dtype=F32) / half))
    ang_r = rows[:, None] * inv
    ang_c = cols[:, None] * inv
    ang = jnp.concatenate([ang_r, ang_r, ang_c, ang_c], axis=-1)
    ang = jnp.concatenate([jnp.zeros((ctx_len, HEAD_DIM), F32), ang], axis=0)
    ang = jnp.tile(ang, (1, LANES // HEAD_DIM))
    return jnp.cos(ang), jnp.sin(ang)


def _pad_row(v, width=LANES):
    return jnp.pad(v, (0, width - v.shape[0]))[None, :]


def kernel(x, c, ctx, c_ctx, mod_w, mod_b, norm1_g, norm2_g, final_g, attn_w_in, attn_w_out, attn_q_norm_g, attn_k_norm_g, diff_lambda_q1, diff_lambda_k1, diff_lambda_q2, diff_lambda_k2, diff_subln_g, ssm_a_re, ssm_a_im, ssm_log_dt, ssm_b_re, ssm_b_im, ssm_c_re, ssm_c_im, ssm_d, ssm_glu_w_a, ssm_glu_w_b, moe_group_w, moe_group_b, moe_router_w, moe_router_b, moe_w_gate, moe_w_up, moe_w_down):
    bsz, seq, d = x.shape
    ctx_len = ctx.shape[1]
    depth = mod_w.shape[0]
    assert ctx_len == ROW_TILE and seq % ROW_TILE == 0 and seq % GRID_W == 0

    h = jnp.concatenate([ctx, x], axis=1)

    mod_rows = 16
    c_all = jnp.concatenate([c, c_ctx[None, :], jnp.zeros((mod_rows - bsz - 1, d), F32)], axis=0)
    mods = _modulation(c_all, mod_w, mod_b).reshape(depth, mod_rows, 6, d)
    mods = jnp.pad(mods, ((0, 0), (0, 0), (0, 2), (0, 0)))
    mod_tabs = jnp.stack([jnp.broadcast_to(mods[:, bsz:bsz + 1], (depth, bsz, 8, d)), mods[:, :bsz]], axis=2)

    cos, sin = _rope_tables(seq, ctx_len)
    cast = lambda w: w.astype(_MXU_DTYPE)
    ssm_tabs = _ssm_tables(ssm_a_re, ssm_a_im, ssm_log_dt, ssm_b_re, ssm_b_im, ssm_c_re, ssm_c_im, SSM_CHUNK)

    for layer in range(depth):
        mod_tab = mod_tabs[layer]
        i = layer // 2
        if layer % 2 == 0:
            lambda_init = 0.8 - 0.6 * math.exp(-0.3 * layer)
            qkv = _attn_proj(h, mod_tab, norm1_g[layer][None, :], cast(attn_w_in[i]),
                             jnp.tile(attn_q_norm_g[i], 2)[None, :], jnp.tile(attn_k_norm_g[i], 2)[None, :],
                             cos, sin, ctx_len)
            lam_rows = jnp.concatenate([_pad_row(diff_lambda_q1[i]), _pad_row(diff_lambda_k1[i]),
                                        _pad_row(diff_lambda_q2[i]), _pad_row(diff_lambda_k2[i]),
                                        jnp.zeros((4, LANES), F32)], axis=0)
            h = _attention(qkv, lam_rows, diff_subln_g[i][None, :], cast(attn_w_out[i]), h, mod_tab,
                           lambda_init, ctx_len)
        else:
            u = _norm1(h, mod_tab, norm1_g[layer][None, :], ctx_len)
            y = _ssm_scan(u, *ssm_tabs, ctx_len, i * (d // LANES))
            h = _ssm_out(y, u, ssm_d[i][None, :], cast(ssm_glu_w_a[i]), cast(ssm_glu_w_b[i]), h, mod_tab,
                         ctx_len)

        wr = jnp.concatenate([jnp.transpose(moe_router_w[layer], (1, 0, 2)).reshape(d, MOE_EXPERTS),
                              moe_group_w[layer],
                              jnp.zeros((d, LANES - MOE_EXPERTS - MOE_GROUPS), F32)], axis=1)
        br = _pad_row(jnp.concatenate([moe_router_b[layer].reshape(-1), moe_group_b[layer]]))
        xt, cmb = _router(h, mod_tab, norm2_g[layer][None, :], wr, br, ctx_len)
        h = _experts(xt, cmb, cast(moe_w_gate[layer]), cast(moe_w_up[layer]), cast(moe_w_down[layer]),
                     h, mod_tab, ctx_len)

    return _final_norm(h, final_g[None, :], ctx_len)
```
